```python
import math
import jax, jax.numpy as jnp
from jax import lax
import numpy as np

D_MODEL = 2048
BATCH = 4
SEQ = 2048
DEPTH = 2
DEC_BATCH = 128
DEC_SEQ = 1
PAST_LEN = 8192
PAGE_SIZE = 128

N_MIXERS = 2
N_ATTN_LAYERS = (DEPTH + 1) // 2
N_SSM_LAYERS = DEPTH // 2
HEAD_DIM = 64
N_HEADS = D_MODEL // HEAD_DIM
N_KV_HEADS = N_HEADS // 8
Q_PER_KV = N_HEADS // N_KV_HEADS
QKV_DIM = (N_HEADS + 2 * N_KV_HEADS) * HEAD_DIM
ROT_DIM = HEAD_DIM // 4
ROPE_THETA = 500000.0
WINDOW = 128
BLOCK = 128
N_META = 16
SSM_GROUP = 16
N_SSM_GROUPS = D_MODEL // SSM_GROUP
SSM_STATE = 64
N_EXPERTS = 64
TOP_K = 8
N_EXPERT_GROUPS = 8
TOPK_GROUPS = 4
D_EXPERT = 512
D_SHARED = 512
ROUTED_SCALE = 2.5
DN_ALPHA = (2 * DEPTH) ** 0.25
DN_BETA = (8 * DEPTH) ** -0.25
LN_EPS = 1e-5
F32 = jnp.float32

kernel_name = 'hybrid_swa_sink_s5_moe_deepnorm_step'


def layer_norm(x, g, b):
    xf = x.astype(F32)
    mu = xf.mean(-1, keepdims=True)
    var = jnp.square(xf - mu).mean(-1, keepdims=True)
    return ((xf - mu) * lax.rsqrt(var + LN_EPS) * g.astype(F32) + b.astype(F32)).astype(x.dtype)


def rope_partial(x, pos):
    half = ROT_DIM // 2
    inv_freq = ROPE_THETA ** (-jnp.arange(0, ROT_DIM, 2, dtype=F32) / ROT_DIM)
    ang = pos.astype(F32)[:, None] * inv_freq[None, :]
    cos = jnp.cos(ang)[:, None, :]
    sin = jnp.sin(ang)[:, None, :]
    xf = x.astype(F32)
    x1 = xf[..., :half]
    x2 = xf[..., half:ROT_DIM]
    out = jnp.concatenate([x1 * cos - x2 * sin, x2 * cos + x1 * sin, xf[..., ROT_DIM:]], -1)
    return out.astype(x.dtype)


def qkv_proj(x, pos, w_qkv, b_qkv):
    bsz, L = x.shape[0], x.shape[1]
    qkv = x @ w_qkv + b_qkv
    q, k, v = jnp.split(qkv, [N_HEADS * HEAD_DIM, (N_HEADS + N_KV_HEADS) * HEAD_DIM], axis=-1)
    q = rope_partial(q.reshape(bsz, L, N_HEADS, HEAD_DIM), pos)
    k = rope_partial(k.reshape(bsz, L, N_KV_HEADS, HEAD_DIM), pos)
    v = v.reshape(bsz, L, N_KV_HEADS, HEAD_DIM)
    return q, k, v


def sink_attend(q, k, v, mask, sinks):
    s = jnp.einsum('bnqkgd,bnskd->bnkgqs', q.astype(F32), k.astype(F32)) / math.sqrt(HEAD_DIM)
    s = jnp.where(mask[:, None, None, :, :], s, -jnp.inf)
    sink = sinks.astype(F32).reshape(N_KV_HEADS, Q_PER_KV)[:, :, None, None]
    m = jnp.maximum(s.max(-1, keepdims=True), sink)
    p = jnp.exp(s - m)
    denom = p.sum(-1, keepdims=True) + jnp.exp(sink - m)
    return jnp.einsum('bnkgqs,bnskd->bnqkgd', p / denom, v.astype(F32))


def attn_prompt(x, w_qkv, b_qkv, sinks, w_o, b_o):
    bsz, L, _ = x.shape
    pad = (-L) % BLOCK
    nb = (L + pad) // BLOCK
    q, k, v = qkv_proj(x, jnp.arange(L), w_qkv, b_qkv)
    qb = jnp.pad(q, ((0, 0), (pad, 0), (0, 0), (0, 0))).reshape(bsz, nb, BLOCK, N_KV_HEADS, Q_PER_KV, HEAD_DIM)
    kpad = jnp.pad(k, ((0, 0), (pad + BLOCK, 0), (0, 0), (0, 0))).reshape(bsz, nb + 1, BLOCK, N_KV_HEADS, HEAD_DIM)
    vpad = jnp.pad(v, ((0, 0), (pad + BLOCK, 0), (0, 0), (0, 0))).reshape(bsz, nb + 1, BLOCK, N_KV_HEADS, HEAD_DIM)
    kb = jnp.concatenate([kpad[:, :-1], kpad[:, 1:]], axis=2)
    vb = jnp.concatenate([vpad[:, :-1], vpad[:, 1:]], axis=2)
    kpos = (jnp.arange((nb + 1) * BLOCK) - (pad + BLOCK)).reshape(nb + 1, BLOCK)
    kpos = jnp.concatenate([kpos[:-1], kpos[1:]], axis=1)
    qpos = (jnp.arange(nb * BLOCK) - pad).reshape(nb, BLOCK)
    d = qpos[:, :, None] - kpos[:, None, :]
    mask = (kpos[:, None, :] >= 0) & (d >= 0) & (d <= WINDOW)
    o = sink_attend(qb, kb, vb, mask, sinks)
    o = o.reshape(bsz, nb * BLOCK, N_HEADS * HEAD_DIM)[:, pad:].astype(x.dtype)
    return o @ w_o + b_o, k[:, -WINDOW:], v[:, -WINDOW:]


def attn_sample(x, cache_k, cache_v, w_qkv, b_qkv, sinks, w_o, b_o):
    bsz, s1, _ = x.shape
    w = cache_k.shape[1]
    qpos = PAST_LEN + jnp.arange(s1)
    q, k, v = qkv_proj(x, qpos, w_qkv, b_qkv)
    keys = jnp.concatenate([cache_k.astype(k.dtype), k], axis=1)
    vals = jnp.concatenate([cache_v.astype(v.dtype), v], axis=1)
    kpos = jnp.concatenate([PAST_LEN - w + jnp.arange(w), qpos])
    d = qpos[:, None] - kpos[None, :]
    mask = ((d >= 0) & (d <= WINDOW))[None]
    o = sink_attend(q.reshape(bsz, 1, s1, N_KV_HEADS, Q_PER_KV, HEAD_DIM), keys[:, None], vals[:, None], mask, sinks)
    o = o.reshape(bsz, s1, N_HEADS * HEAD_DIM).astype(x.dtype)
    return o @ w_o + b_o, keys[:, -w:], vals[:, -w:]


def _cplx_combine(e1, e2):
    a1r, a1i, b1r, b1i = e1
    a2r, a2i, b2r, b2i = e2
    return (a1r * a2r - a1i * a2i, a1r * a2i + a1i * a2r,
            a2r * b1r - a2i * b1i + b2r, a2r * b1i + a2i * b1r + b2i)


def s5_mixer(x, s0_re, s0_im, lam_re, lam_im, log_dt, b_re, b_im, c_re, c_im, d_skip, w_glu, b_glu):
    bsz, L, _ = x.shape
    u = x.astype(F32).reshape(bsz, L, N_SSM_GROUPS, SSM_GROUP)
    dt = jnp.exp(log_dt.astype(F32))[:, None]
    lr = lam_re.astype(F32)
    li = lam_im.astype(F32)
    mag = jnp.exp(lr * dt)
    ab_re = mag * jnp.cos(li * dt)
    ab_im = mag * jnp.sin(li * dt)
    den = lr * lr + li * li
    nr = ab_re - 1.0
    ni = ab_im
    cr = ((nr * lr + ni * li) / den)[..., None]
    ci = ((ni * lr - nr * li) / den)[..., None]
    br = b_re.astype(F32)
    bi = b_im.astype(F32)
    bb_re = cr * br - ci * bi
    bb_im = cr * bi + ci * br
    bu_re = jnp.einsum('blgc,gnc->blgn', u, bb_re)
    bu_im = jnp.einsum('blgc,gnc->blgn', u, bb_im)
    s0r = s0_re.astype(F32)
    s0i = s0_im.astype(F32)
    bu_re = bu_re.at[:, 0].add(ab_re * s0r - ab_im * s0i)
    bu_im = bu_im.at[:, 0].add(ab_re * s0i + ab_im * s0r)
    a_re = jnp.broadcast_to(ab_re, bu_re.shape)
    a_im = jnp.broadcast_to(ab_im, bu_im.shape)
    _, _, s_re, s_im = lax.associative_scan(_cplx_combine, (a_re, a_im, bu_re, bu_im), axis=1)
    y = (jnp.einsum('blgn,gcn->blgc', s_re, c_re.astype(F32))
         - jnp.einsum('blgn,gcn->blgc', s_im, c_im.astype(F32))
         + d_skip.astype(F32) * u)
    z = jax.nn.gelu(y.reshape(bsz, L, D_MODEL)).astype(x.dtype)
    val, gate = jnp.split(z @ w_glu + b_glu, 2, axis=-1)
    out = val * jax.nn.sigmoid(gate)
    return out, s_re[:, -1].astype(s0_re.dtype), s_im[:, -1].astype(s0_im.dtype)


def moe_ffn(x, w_router, router_bias, w_gate, w_up, w_down, ws_gate, ws_up, ws_down):
    shp = x.shape
    t = x.reshape(-1, D_MODEL)
    n_tok = t.shape[0]
    scores = jax.nn.sigmoid(t.astype(F32) @ w_router.astype(F32))
    biased = scores + router_bias.astype(F32)
    per_grp = N_EXPERTS // N_EXPERT_GROUPS
    grp_score = lax.top_k(biased.reshape(n_tok, N_EXPERT_GROUPS, per_grp), 2)[0].sum(-1)
    _, gidx = lax.top_k(grp_score, TOPK_GROUPS)
    gmask = jax.nn.one_hot(gidx, N_EXPERT_GROUPS, dtype=F32).sum(-2)
    emask = jnp.repeat(gmask, per_grp, axis=-1) > 0
    _, eidx = lax.top_k(jnp.where(emask, biased, -jnp.inf), TOP_K)
    w_sel = jnp.take_along_axis(scores, eidx, axis=-1)
    w_sel = w_sel / w_sel.sum(-1, keepdims=True) * ROUTED_SCALE
    gates = jnp.einsum('tk,tke->te', w_sel, jax.nn.one_hot(eidx, N_EXPERTS, dtype=F32))
    hg = jnp.einsum('td,edf->tef', t, w_gate)
    hu = jnp.einsum('td,edf->tef', t, w_up)
    act = jax.nn.silu(hg) * hu * gates[..., None].astype(t.dtype)
    routed = jnp.einsum('tef,efd->td', act, w_down)
    shared = (jax.nn.silu(t @ ws_gate) * (t @ ws_up)) @ ws_down
    return (routed + shared).reshape(shp)


def setup_inputs(seed: int = 0) -> dict:
    key = jax.random.key(seed)
    ks = iter(jax.random.split(key, 40))
    nrm = lambda shape, scale: scale * jax.random.normal(next(ks), shape, F32)
    NA, NS, G, N, C = N_ATTN_LAYERS, N_SSM_LAYERS, N_SSM_GROUPS, SSM_STATE, SSM_GROUP
    cache_w = min(WINDOW, PAST_LEN)
    ds = D_MODEL ** -0.5
    w_glu = jnp.concatenate([nrm((NS, D_MODEL, D_MODEL), ds * DN_BETA), nrm((NS, D_MODEL, D_MODEL), ds)], axis=-1)
    return {
        'x_prompt': nrm((BATCH, SEQ, D_MODEL), 1.0),
        'x_sample': nrm((DEC_BATCH, DEC_SEQ, D_MODEL), 1.0),
        'cache_k': nrm((NA, DEC_BATCH, cache_w, N_KV_HEADS, HEAD_DIM), 1.0),
        'cache_v': nrm((NA, DEC_BATCH, cache_w, N_KV_HEADS, HEAD_DIM), 1.0),
        'state_ssm_re': nrm((NS, DEC_BATCH, G, N), 0.1),
        'state_ssm_im': nrm((NS, DEC_BATCH, G, N), 0.1),
        'meta_tokens': nrm((N_META, D_MODEL), 1.0),
        'w_qkv': nrm((NA, D_MODEL, QKV_DIM), ds),
        'b_qkv': nrm((NA, QKV_DIM), 0.01),
        'attn_sinks': nrm((NA, N_HEADS), 1.0),
        'w_o': nrm((NA, N_HEADS * HEAD_DIM, D_MODEL), ds * DN_BETA),
        'b_o': nrm((NA, D_MODEL), 0.01),
        'ssm_lam_re': -0.5 + nrm((NS, G, N), 0.01),
        'ssm_lam_im': jnp.pi * jnp.arange(N, dtype=F32) + nrm((NS, G, N), 0.01),
        'ssm_log_dt': jax.random.uniform(next(ks), (NS, G), F32, math.log(1e-3), math.log(1e-1)),
        'ssm_b_re': nrm((NS, G, N, C), (2 * C) ** -0.5),
        'ssm_b_im': nrm((NS, G, N, C), (2 * C) ** -0.5),
        'ssm_c_re': nrm((NS, G, C, N), N ** -0.5),
        'ssm_c_im': nrm((NS, G, C, N), N ** -0.5),
        'ssm_d': nrm((NS, G, C), 1.0),
        'w_glu': w_glu,
        'b_glu': nrm((NS, 2 * D_MODEL), 0.01),
        'ln_mix_g': 1.0 + nrm((DEPTH, D_MODEL), 0.01),
        'ln_mix_b': nrm((DEPTH, D_MODEL), 0.01),
        'w_router': nrm((DEPTH, D_MODEL, N_EXPERTS), ds),
        'router_bias': nrm((DEPTH, N_EXPERTS), 0.01),
        'w_exp_gate': nrm((DEPTH, N_EXPERTS, D_MODEL, D_EXPERT), ds),
        'w_exp_up': nrm((DEPTH, N_EXPERTS, D_MODEL, D_EXPERT), ds),
        'w_exp_down': nrm((DEPTH, N_EXPERTS, D_EXPERT, D_MODEL), D_EXPERT ** -0.5 * DN_BETA),
        'w_sh_gate': nrm((DEPTH, D_MODEL, D_SHARED), ds),
        'w_sh_up': nrm((DEPTH, D_MODEL, D_SHARED), ds),
        'w_sh_down': nrm((DEPTH, D_SHARED, D_MODEL), D_SHARED ** -0.5 * DN_BETA),
        'ln_ffn_g': 1.0 + nrm((DEPTH, D_MODEL), 0.01),
        'ln_ffn_b': nrm((DEPTH, D_MODEL), 0.01),
    }


def reference(x_prompt, x_sample, cache_k, cache_v, state_ssm_re, state_ssm_im, meta_tokens,
              w_qkv, b_qkv, attn_sinks, w_o, b_o,
              ssm_lam_re, ssm_lam_im, ssm_log_dt, ssm_b_re, ssm_b_im, ssm_c_re, ssm_c_im, ssm_d, w_glu, b_glu,
              ln_mix_g, ln_mix_b, w_router, router_bias, w_exp_gate, w_exp_up, w_exp_down,
              w_sh_gate, w_sh_up, w_sh_down, ln_ffn_g, ln_ffn_b):
    bsz = x_prompt.shape[0]
    meta = jnp.broadcast_to(meta_tokens.astype(x_prompt.dtype)[None], (bsz, N_META, D_MODEL))
    hp = jnp.concatenate([meta, x_prompt], axis=1)
    hs = x_sample
    kp_l, vp_l, ks_l, vs_l = [], [], [], []
    rp_l, ip_l, rs_l, is_l = [], [], [], []
    for i in range(DEPTH):
        j = i // N_MIXERS
        if i % N_MIXERS == 0:
            mp, kp, vp = attn_prompt(hp, w_qkv[j], b_qkv[j], attn_sinks[j], w_o[j], b_o[j])
            ms, kss, vss = attn_sample(hs, cache_k[j], cache_v[j], w_qkv[j], b_qkv[j], attn_sinks[j], w_o[j], b_o[j])
            kp_l.append(kp); vp_l.append(vp); ks_l.append(kss); vs_l.append(vss)
        else:
            z0 = jnp.zeros((bsz, N_SSM_GROUPS, SSM_STATE), state_ssm_re.dtype)
            ssm_w = (ssm_lam_re[j], ssm_lam_im[j], ssm_log_dt[j], ssm_b_re[j], ssm_b_im[j],
                     ssm_c_re[j], ssm_c_im[j], ssm_d[j], w_glu[j], b_glu[j])
            mp, rp, ip = s5_mixer(hp, z0, z0, *ssm_w)
            ms, rs, iss = s5_mixer(hs, state_ssm_re[j], state_ssm_im[j], *ssm_w)
            rp_l.append(rp); ip_l.append(ip); rs_l.append(rs); is_l.append(iss)
        hp = layer_norm(DN_ALPHA * hp + mp, ln_mix_g[i], ln_mix_b[i])
        hs = layer_norm(DN_ALPHA * hs + ms, ln_mix_g[i], ln_mix_b[i])
        moe_w = (w_router[i], router_bias[i], w_exp_gate[i], w_exp_up[i], w_exp_down[i],
                 w_sh_gate[i], w_sh_up[i], w_sh_down[i])
        hp = layer_norm(DN_ALPHA * hp + moe_ffn(hp, *moe_w), ln_ffn_g[i], ln_ffn_b[i])
        hs = layer_norm(DN_ALPHA * hs + moe_ffn(hs, *moe_w), ln_ffn_g[i], ln_ffn_b[i])
    y_prompt = hp[:, N_META:]
    y_sample = hs
    return (y_prompt, y_sample,
            jnp.stack(kp_l), jnp.stack(vp_l), jnp.stack(ks_l), jnp.stack(vs_l),
            jnp.stack(rp_l), jnp.stack(ip_l), jnp.stack(rs_l), jnp.stack(is_l))
```

```python
import functools
import math

import jax
import jax.numpy as jnp
from jax import lax
from jax.experimental import pallas as pl
from jax.experimental.pallas import tpu as pltpu

F32 = jnp.float32
BF16 = jnp.bfloat16
I32 = jnp.int32

D = 2048
NB = 4
N_META = 16
SEQ = 2048
L = N_META + SEQ
BLK = 128
PAD_FRONT = (-L) % BLK
LP = L + PAD_FRONT
NBLK = LP // BLK
T_PROMPT = NB * LP
DEC = 128
T_ALL = T_PROMPT + DEC
PAST_LEN = 8192
HD = 64
NH = 32
NKV = 4
QPK = NH // NKV
QKV = (NH + 2 * NKV) * HD
QK_COLS = (NH + NKV) * HD
ROT = HD // 4
ROPE_THETA = 500000.0
NG = 128
GC = 16
NS = 64
NCHUNK = 16
CH_STATE = 8 * NS
NE = 64
TOPK = 8
NEG = 8
PER_GRP = NE // NEG
TOPG = 4
DE = 512
ROUTED_SCALE = 2.5
DEPTH = 2
ALPHA = (2 * DEPTH) ** 0.25
LN_EPS = 1e-5

V7X_VMEM_BYTES = 64 * 1024 * 1024
VMEM_LIMIT = 56 * 1024 * 1024

TM = 384
TM_E = 256
TM_C = 128
N_PAIRS = T_ALL * TOPK
NT_E = -(-N_PAIRS // TM_E) + NE
R_ROWS = NT_E * TM_E


def _cparams(sem):
    return pltpu.CompilerParams(dimension_semantics=sem, vmem_limit_bytes=VMEM_LIMIT)


def _sigmoid(x):
    return 1.0 / (1.0 + jnp.exp(-x))


def _layer_norm(y, g, b):
    mu = jnp.mean(y, axis=-1, keepdims=True)
    yc = y - mu
    var = jnp.mean(yc * yc, axis=-1, keepdims=True)
    return yc * lax.rsqrt(var + LN_EPS) * g + b


def _qkv_kernel(x_ref, w_ref, b_ref, c_ref, s1_ref, s2_ref, q_ref, k_ref, v_ref):
    xb = x_ref[...].astype(BF16)
    acc = jnp.dot(xb, w_ref[...], preferred_element_type=F32) + b_ref[...]
    c = c_ref[...]
    s1 = s1_ref[...]
    s2 = s2_ref[...]
    for j in range(QK_COLS // 128):
        blk = acc[:, j * 128:(j + 1) * 128]
        r = blk * c + pltpu.roll(blk, 120, axis=1) * s1 + pltpu.roll(blk, 8, axis=1) * s2
        if j < D // 128:
            q_ref[:, j * 128:(j + 1) * 128] = (r * (1.0 / math.sqrt(HD))).astype(BF16)
        else:
            jj = j - D // 128
            k_ref[:, jj * 128:(jj + 1) * 128] = r
    v_ref[...] = acc[:, QK_COLS:]


def _qkv_call(x, w_bf, b, rc, rs1, rs2):
    return pl.pallas_call(
        _qkv_kernel,
        grid=(T_ALL // TM,),
        in_specs=[
            pl.BlockSpec((TM, D), lambda i: (i, 0)),
            pl.BlockSpec((D, QKV), lambda i: (0, 0)),
            pl.BlockSpec((1, QKV), lambda i: (0, 0)),
            pl.BlockSpec((TM, 128), lambda i: (i, 0)),
            pl.BlockSpec((TM, 128), lambda i: (i, 0)),
            pl.BlockSpec((TM, 128), lambda i: (i, 0)),
        ],
        out_specs=[
            pl.BlockSpec((TM, D), lambda i: (i, 0)),
            pl.BlockSpec((TM, NKV * HD), lambda i: (i, 0)),
            pl.BlockSpec((TM, NKV * HD), lambda i: (i, 0)),
        ],
        out_shape=[
            jax.ShapeDtypeStruct((T_ALL, D), BF16),
            jax.ShapeDtypeStruct((T_ALL, NKV * HD), F32),
            jax.ShapeDtypeStruct((T_ALL, NKV * HD), F32),
        ],
        compiler_params=_cparams(("parallel",)),
        name="qkv_rope",
    )(x, w_bf, b, rc, rs1, rs2)


def _rope_tables():
    pos_p = jnp.maximum(jnp.arange(LP, dtype=I32) - PAD_FRONT, 0)
    pos = jnp.concatenate([jnp.tile(pos_p, NB), jnp.full((DEC,), PAST_LEN, I32)]).astype(F32)
    half = ROT // 2
    inv_freq = ROPE_THETA ** (-jnp.arange(0, ROT, 2, dtype=F32) / ROT)
    ang = pos[:, None] * inv_freq[None, :]
    cos = jnp.cos(ang)
    sin = jnp.sin(ang)
    ones = jnp.ones((T_ALL, HD - ROT), F32)
    zer = jnp.zeros((T_ALL, HD - ROT), F32)
    zh = jnp.zeros((T_ALL, half), F32)
    c = jnp.concatenate([cos, cos, ones], axis=1)
    s1 = jnp.concatenate([-sin, zh, zer], axis=1)
    s2 = jnp.concatenate([zh, sin, zer], axis=1)
    return tuple(jnp.concatenate([t, t], axis=1) for t in (c, s1, s2))


def _attn_p_kernel(sink_ref, q_ref, kp_ref, kc_ref, vp_ref, vc_ref, o_ref):
    j = pl.program_id(0) % NBLK
    keys = jnp.concatenate([kp_ref[...], kc_ref[...]], axis=0).astype(BF16)
    vals = jnp.concatenate([vp_ref[...], vc_ref[...]], axis=0).astype(BF16)
    r = lax.broadcasted_iota(I32, (BLK, 2 * BLK), 0)
    c = lax.broadcasted_iota(I32, (BLK, 2 * BLK), 1)
    dist = BLK + r - c
    kpos = (j - 1) * BLK - PAD_FRONT + c
    mask = (dist >= 0) & (dist <= BLK) & (kpos >= 0)
    for g in range(NKV):
        kg = keys[:, g * HD:(g + 1) * HD]
        vg = vals[:, g * HD:(g + 1) * HD]
        for hh in range(QPK):
            h = g * QPK + hh
            qh = q_ref[:, h * HD:(h + 1) * HD]
            s = lax.dot_general(qh, kg, (((1,), (1,)), ((), ())), preferred_element_type=F32)
            s = jnp.where(mask, s, -jnp.inf)
            sk = sink_ref[h]
            m = jnp.maximum(jnp.max(s, axis=1, keepdims=True), sk)
            p = jnp.exp(s - m)
            den = jnp.sum(p, axis=1, keepdims=True) + jnp.exp(sk - m)
            oh = jnp.dot(p.astype(BF16), vg, preferred_element_type=F32) / den
            o_ref[:, h * HD:(h + 1) * HD] = oh.astype(BF16)


def _attn_p_call(sinks, q, k, v):
    prev = lambda i: (jnp.where(i % NBLK == 0, i, i - 1), 0)
    cur = lambda i: (i, 0)
    return pl.pallas_call(
        _attn_p_kernel,
        grid=(NB * NBLK,),
        in_specs=[
            pl.BlockSpec(memory_space=pltpu.SMEM),
            pl.BlockSpec((BLK, D), cur),
            pl.BlockSpec((BLK, NKV * HD), prev),
            pl.BlockSpec((BLK, NKV * HD), cur),
            pl.BlockSpec((BLK, NKV * HD), prev),
            pl.BlockSpec((BLK, NKV * HD), cur),
        ],
        out_specs=pl.BlockSpec((BLK, D), cur),
        out_shape=jax.ShapeDtypeStruct((T_PROMPT, D), BF16),
        compiler_params=_cparams(("parallel",)),
        name="attn_prompt",
    )(sinks, q, k, k, v, v)


SEQ_PER_STEP = 16


def _attn_s_kernel(sink_ref, q_ref, kn_ref, vn_ref, ck_ref, cv_ref, o_ref, cko_ref, cvo_ref):
    row = lax.broadcasted_iota(I32, (BLK, NKV * HD), 0)

    def body(s, carry):
        kc = ck_ref[s]
        vc = cv_ref[s]
        kn = kn_ref[pl.ds(s, 1), :]
        vn = vn_ref[pl.ds(s, 1), :]
        kb = kc.astype(BF16)
        vb = vc.astype(BF16)
        knr = kn.astype(BF16).astype(F32)
        vnr = vn.astype(BF16).astype(F32)
        for g in range(NKV):
            qg = q_ref[s, g * QPK:(g + 1) * QPK, :].astype(BF16)
            sc = lax.dot_general(qg, kb[:, g * HD:(g + 1) * HD], (((1,), (1,)), ((), ())),
                                 preferred_element_type=F32)
            sn = jnp.sum(qg.astype(F32) * knr[:, g * HD:(g + 1) * HD], axis=1, keepdims=True)
            sk = sink_ref[g * QPK:(g + 1) * QPK, :]
            m = jnp.maximum(jnp.maximum(jnp.max(sc, axis=1, keepdims=True), sn), sk)
            p = jnp.exp(sc - m)
            pn = jnp.exp(sn - m)
            den = jnp.sum(p, axis=1, keepdims=True) + pn + jnp.exp(sk - m)
            og = jnp.dot(p.astype(BF16), vb[:, g * HD:(g + 1) * HD], preferred_element_type=F32)
            og = og + pn.astype(BF16).astype(F32) * vnr[:, g * HD:(g + 1) * HD]
            o_ref[s, g * QPK:(g + 1) * QPK, :] = og / den
        cko_ref[s] = jnp.where(row == BLK - 1, kn, pltpu.roll(kc, BLK - 1, axis=0))
        cvo_ref[s] = jnp.where(row == BLK - 1, vn, pltpu.roll(vc, BLK - 1, axis=0))
        return carry

    lax.fori_loop(0, SEQ_PER_STEP, body, 0)


def _attn_s_call(sinks_col, q3, k, v, cache_k, cache_v):
    sp = SEQ_PER_STEP
    kv_off = T_PROMPT // sp
    return pl.pallas_call(
        _attn_s_kernel,
        grid=(DEC // sp,),
        in_specs=[
            pl.BlockSpec((NH, 1), lambda i: (0, 0)),
            pl.BlockSpec((sp, NH, HD), lambda i: (i, 0, 0)),
            pl.BlockSpec((sp, NKV * HD), lambda i: (kv_off + i, 0)),
            pl.BlockSpec((sp, NKV * HD), lambda i: (kv_off + i, 0)),
            pl.BlockSpec((sp, BLK, NKV * HD), lambda i: (i, 0, 0)),
            pl.BlockSpec((sp, BLK, NKV * HD), lambda i: (i, 0, 0)),
        ],
        out_specs=[
            pl.BlockSpec((sp, NH, HD), lambda i: (i, 0, 0)),
            pl.BlockSpec((sp, BLK, NKV * HD), lambda i: (i, 0, 0)),
            pl.BlockSpec((sp, BLK, NKV * HD), lambda i: (i, 0, 0)),
        ],
        out_shape=[
            jax.ShapeDtypeStruct((DEC, NH, HD), F32),
            jax.ShapeDtypeStruct((DEC, BLK, NKV * HD), F32),
            jax.ShapeDtypeStruct((DEC, BLK, NKV * HD), F32),
        ],
        compiler_params=_cparams(("parallel",)),
        name="attn_sample",
    )(sinks_col, q3, k, v, cache_k, cache_v)


def _oproj_ln_kernel(o_ref, w_ref, bo_ref, h_ref, g_ref, b_ref, out_ref):
    m = jnp.dot(o_ref[...], w_ref[...], preferred_element_type=F32) + bo_ref[...]
    out_ref[...] = _layer_norm(ALPHA * h_ref[...] + m, g_ref[...], b_ref[...])


def _oproj_ln_call(o, w_bf, bo, h, g, b):
    vec = pl.BlockSpec((1, D), lambda i: (0, 0))
    return pl.pallas_call(
        _oproj_ln_kernel,
        grid=(T_ALL // TM,),
        in_specs=[
            pl.BlockSpec((TM, D), lambda i: (i, 0)),
            pl.BlockSpec((D, D), lambda i: (0, 0)),
            vec,
            pl.BlockSpec((TM, D), lambda i: (i, 0)),
            vec,
            vec,
        ],
        out_specs=pl.BlockSpec((TM, D), lambda i: (i, 0)),
        out_shape=jax.ShapeDtypeStruct((T_ALL, D), F32),
        compiler_params=_cparams(("parallel",)),
        name="oproj_ln",
    )(o, w_bf, bo, h, g, b)


def _router_kernel(h_ref, wr_ref, rb_ref, eidx_ref, wsel_ref, rank_ref, cnt_ref, carry_ref):
    i = pl.program_id(0)

    @pl.when(i == 0)
    def _():
        carry_ref[...] = jnp.zeros_like(carry_ref)

    logits = jnp.dot(h_ref[...], wr_ref[...], preferred_element_type=F32,
                     precision=lax.Precision.HIGHEST)
    scores = _sigmoid(logits)
    biased = scores + rb_ref[...]
    lane_i = lax.broadcasted_iota(I32, (TM, NE), 1)
    lane = lane_i.astype(F32)
    grp = (lane_i // PER_GRP).astype(F32)
    ninf = -jnp.inf
    big = float(NE)

    gs = jnp.zeros((TM, NE), F32)
    for gi in range(NEG):
        seg = jnp.where(grp == gi, biased, ninf)
        m1 = jnp.max(seg, axis=1, keepdims=True)
        i1 = jnp.min(jnp.where(seg == m1, lane, big), axis=1, keepdims=True)
        m2 = jnp.max(jnp.where(lane == i1, ninf, seg), axis=1, keepdims=True)
        gs = jnp.where(grp == gi, m1 + m2, gs)

    emask = jnp.zeros((TM, NE), F32)
    for _ in range(TOPG):
        m = jnp.max(gs, axis=1, keepdims=True)
        gsel = jnp.min(jnp.where(gs == m, grp, big), axis=1, keepdims=True)
        hit = grp == gsel
        emask = jnp.where(hit, 1.0, emask)
        gs = jnp.where(hit, ninf, gs)

    work = jnp.where(emask > 0.5, biased, ninf)
    idx_cols = []
    w_cols = []
    onehot = jnp.zeros((TM, NE), F32)
    for _ in range(TOPK):
        m = jnp.max(work, axis=1, keepdims=True)
        ik = jnp.min(jnp.where(work == m, lane, big), axis=1, keepdims=True)
        hit = lane == ik
        idx_cols.append(ik)
        w_cols.append(jnp.sum(jnp.where(hit, scores, 0.0), axis=1, keepdims=True))
        onehot = jnp.where(hit, 1.0, onehot)
        work = jnp.where(hit, ninf, work)
    wsum = w_cols[0]
    for wk in w_cols[1:]:
        wsum = wsum + wk

    rr = lax.broadcasted_iota(I32, (TM, TM), 0)
    cc = lax.broadcasted_iota(I32, (TM, TM), 1)
    tri = jnp.where(cc < rr, 1.0, 0.0).astype(BF16)
    prefix = jnp.dot(tri, onehot.astype(BF16), preferred_element_type=F32) + carry_ref[...]
    carry_ref[...] = carry_ref[...] + jnp.sum(onehot, axis=0, keepdims=True)
    cnt_ref[...] = carry_ref[...]

    lane8 = lax.broadcasted_iota(I32, (TM, TOPK), 1)
    eidx = jnp.zeros((TM, TOPK), F32)
    wsel = jnp.zeros((TM, TOPK), F32)
    rank = jnp.zeros((TM, TOPK), F32)
    for k in range(TOPK):
        rk = jnp.sum(jnp.where(lane == idx_cols[k], prefix, 0.0), axis=1, keepdims=True)
        eidx = jnp.where(lane8 == k, idx_cols[k], eidx)
        wsel = jnp.where(lane8 == k, w_cols[k] / wsum * ROUTED_SCALE, wsel)
        rank = jnp.where(lane8 == k, rk, rank)
    eidx_ref[...] = eidx.astype(I32)
    wsel_ref[...] = wsel
    rank_ref[...] = rank.astype(I32)


def _router_call(h, w_router, router_bias):
    tk = pl.BlockSpec((TM, TOPK), lambda i: (i, 0))
    return pl.pallas_call(
        _router_kernel,
        grid=(T_ALL // TM,),
        in_specs=[
            pl.BlockSpec((TM, D), lambda i: (i, 0)),
            pl.BlockSpec((D, NE), lambda i: (0, 0)),
            pl.BlockSpec((1, NE), lambda i: (0, 0)),
        ],
        out_specs=[tk, tk, tk, pl.BlockSpec((1, NE), lambda i: (0, 0))],
        out_shape=[
            jax.ShapeDtypeStruct((T_ALL, TOPK), I32),
            jax.ShapeDtypeStruct((T_ALL, TOPK), F32),
            jax.ShapeDtypeStruct((T_ALL, TOPK), I32),
            jax.ShapeDtypeStruct((1, NE), F32),
        ],
        scratch_shapes=[pltpu.VMEM((1, NE), F32)],
        compiler_params=_cparams(("arbitrary",)),
        name="router",
    )(h, w_router, router_bias)


def _dispatch_kernel(cnt_ref, off_ref, pos_ref, h_ref, xs_ref, zero_ref, sem, zsem):
    i = pl.program_id(0)

    def row_copy(t, p):
        return pltpu.make_async_copy(h_ref.at[pl.ds(t, 1)], xs_ref.at[pl.ds(p, 1)], sem)

    def issue(t, carry):
        for k in range(TOPK):
            row_copy(t, pos_ref[0, 0, t * TOPK + k]).start()
        return carry

    lax.fori_loop(0, TM, issue, 0)

    @pl.when(i == 0)
    def _():
        zero_ref[...] = jnp.zeros_like(zero_ref)

        def zero_copy(r):
            return pltpu.make_async_copy(zero_ref.at[pl.ds(0, 1)], xs_ref.at[pl.ds(r, 1)], zsem)

        def per_expert(e, carry):
            lo = off_ref[e] + cnt_ref[e]
            hi = off_ref[e] + ((cnt_ref[e] + TM_E - 1) // TM_E) * TM_E

            def start(r, c2):
                zero_copy(r).start()
                return c2

            def wait(r, c2):
                zero_copy(r).wait()
                return c2

            lax.fori_loop(lo, hi, start, 0)
            lax.fori_loop(lo, hi, wait, 0)
            return carry

        lax.fori_loop(0, NE, per_expert, 0)

        last = NE - 1
        n_used = (off_ref[last] + ((cnt_ref[last] + TM_E - 1) // TM_E) * TM_E) // TM_E

        def tile_copy(tile):
            r0 = pl.multiple_of(tile * TM_E, TM_E)
            return pltpu.make_async_copy(zero_ref, xs_ref.at[pl.ds(r0, TM_E)], zsem)

        def tstart(tile, c2):
            tile_copy(tile).start()
            return c2

        def twait(tile, c2):
            tile_copy(tile).wait()
            return c2

        lax.fori_loop(n_used, NT_E, tstart, 0)
        lax.fori_loop(n_used, NT_E, twait, 0)

    for _ in range(TOPK):
        pltpu.make_async_copy(h_ref, xs_ref.at[pl.ds(0, TM)], sem).wait()


def _dispatch_call(cnt, off, pos_blocks, h):
    return pl.pallas_call(
        _dispatch_kernel,
        grid_spec=pltpu.PrefetchScalarGridSpec(
            num_scalar_prefetch=2,
            grid=(T_ALL // TM,),
            in_specs=[
                pl.BlockSpec((1, 1, TM * TOPK), lambda i, c, o: (i, 0, 0), memory_space=pltpu.SMEM),
                pl.BlockSpec((TM, D), lambda i, c, o: (i, 0)),
            ],
            out_specs=pl.BlockSpec(memory_space=pl.ANY),
            scratch_shapes=[
                pltpu.VMEM((TM_E, D), F32),
                pltpu.SemaphoreType.DMA(()),
                pltpu.SemaphoreType.DMA(()),
            ],
        ),
        out_shape=jax.ShapeDtypeStruct((R_ROWS, D), F32),
        compiler_params=_cparams(("arbitrary",)),
        name="moe_dispatch",
    )(cnt, off, pos_blocks, h)


def _expert_kernel(te_ref, na_ref, xs_ref, wg_ref, wu_ref, wd_ref, ys_ref, wg_s, wu_s, wd_s):
    i = pl.program_id(0)
    na = na_ref[0]
    ic = jnp.minimum(i, na - 1)
    first = jnp.logical_or(i == 0, te_ref[ic] != te_ref[jnp.maximum(ic - 1, 0)])

    @pl.when(jnp.logical_and(i < na, first))
    def _():
        wg_s[...] = wg_ref[0, 0].astype(BF16)
        wu_s[...] = wu_ref[0, 0].astype(BF16)
        wd_s[...] = wd_ref[0, 0].astype(BF16)

    @pl.when(i < na)
    def _():
        x = xs_ref[...].astype(BF16)
        hg = jnp.dot(x, wg_s[...], preferred_element_type=F32)
        hu = jnp.dot(x, wu_s[...], preferred_element_type=F32)
        act = (hg * _sigmoid(hg) * hu).astype(BF16)
        ys_ref[...] = jnp.dot(act, wd_s[...], preferred_element_type=F32)

    @pl.when(i >= na)
    def _():
        ys_ref[...] = jnp.zeros_like(ys_ref)


def _expert_call(layer, te, nact, xs, w_gate, w_up, w_down):
    def xmap(i, te_r, na_r):
        return (jnp.minimum(i, na_r[0] - 1), 0)

    def wmap(i, te_r, na_r):
        return (layer, te_r[jnp.minimum(i, na_r[0] - 1)], 0, 0)

    return pl.pallas_call(
        _expert_kernel,
        grid_spec=pltpu.PrefetchScalarGridSpec(
            num_scalar_prefetch=2,
            grid=(NT_E,),
            in_specs=[
                pl.BlockSpec((TM_E, D), xmap),
                pl.BlockSpec((1, 1, D, DE), wmap),
                pl.BlockSpec((1, 1, D, DE), wmap),
                pl.BlockSpec((1, 1, DE, D), wmap),
            ],
            out_specs=pl.BlockSpec((TM_E, D), lambda i, te_r, na_r: (i, 0)),
            scratch_shapes=[
                pltpu.VMEM((D, DE), BF16),
                pltpu.VMEM((D, DE), BF16),
                pltpu.VMEM((DE, D), BF16),
            ],
        ),
        out_shape=jax.ShapeDtypeStruct((R_ROWS, D), F32),
        compiler_params=_cparams(("arbitrary",)),
        name="moe_experts",
    )(te, nact, xs, w_gate, w_up, w_down)


def _combine_kernel(pos_ref, w_ref, h_ref, ys_ref, wsg_ref, wsu_ref, wsd_ref, g_ref, b_ref,
                    out_ref, gbuf, sem):
    def issue(t, carry):
        for k in range(TOPK):
            p = pos_ref[0, 0, t * TOPK + k]
            pltpu.make_async_copy(ys_ref.at[pl.ds(p, 1)], gbuf.at[k, pl.ds(t, 1)], sem).start()
        return carry

    lax.fori_loop(0, TM_C, issue, 0)

    h = h_ref[...]
    hb = h.astype(BF16)
    sg = jnp.dot(hb, wsg_ref[...], preferred_element_type=F32)
    su = jnp.dot(hb, wsu_ref[...], preferred_element_type=F32)
    act = (sg * _sigmoid(sg) * su).astype(BF16)
    y = ALPHA * h + jnp.dot(act, wsd_ref[...], preferred_element_type=F32)

    for k in range(TOPK):
        pltpu.make_async_copy(ys_ref.at[pl.ds(0, TM_C)], gbuf.at[k], sem).wait()
    w = w_ref[...]
    for k in range(TOPK):
        y = y + w[:, k:k + 1] * gbuf[k]
    out_ref[...] = _layer_norm(y, g_ref[...], b_ref[...])


def _combine_call(pos_blocks, wsel, h, ys, wsg, wsu, wsd, g, b):
    vec = pl.BlockSpec((1, D), lambda i: (0, 0))
    return pl.pallas_call(
        _combine_kernel,
        grid=(T_ALL // TM_C,),
        in_specs=[
            pl.BlockSpec((1, 1, TM_C * TOPK), lambda i: (i, 0, 0), memory_space=pltpu.SMEM),
            pl.BlockSpec((TM_C, TOPK), lambda i: (i, 0)),
            pl.BlockSpec((TM_C, D), lambda i: (i, 0)),
            pl.BlockSpec(memory_space=pl.ANY),
            pl.BlockSpec((D, DE), lambda i: (0, 0)),
            pl.BlockSpec((D, DE), lambda i: (0, 0)),
            pl.BlockSpec((DE, D), lambda i: (0, 0)),
            vec,
            vec,
        ],
        out_specs=pl.BlockSpec((TM_C, D), lambda i: (i, 0)),
        out_shape=jax.ShapeDtypeStruct((T_ALL, D), F32),
        scratch_shapes=[pltpu.VMEM((TOPK, TM_C, D), F32), pltpu.SemaphoreType.DMA(())],
        compiler_params=_cparams(("arbitrary",)),
        name="moe_combine",
    )(pos_blocks, wsel, h, ys, wsg, wsu, wsd, g, b)


def _moe_layer(layer, h, w_router, router_bias, w_exp_gate, w_exp_up, w_exp_down,
               w_sh_gate, w_sh_up, w_sh_down, ln_g, ln_b):
    eidx, wsel, rank, counts = _router_call(h, w_router[layer], router_bias[layer][None, :])
    cnt = counts[0].astype(I32)
    padded = ((cnt + TM_E - 1) // TM_E) * TM_E
    off_end = jnp.cumsum(padded)
    off = off_end - padded
    pos = off[eidx] + rank
    nact = (off_end[-1] // TM_E).reshape(1)
    tile_start = jnp.arange(NT_E, dtype=I32) * TM_E
    te = jnp.minimum(jnp.sum(tile_start[:, None] >= off_end[None, :], axis=1), NE - 1).astype(I32)

    xs = _dispatch_call(cnt, off, pos.reshape(T_ALL // TM, 1, TM * TOPK), h)
    ys = _expert_call(layer, te, nact, xs, w_exp_gate, w_exp_up, w_exp_down)
    return _combine_call(
        pos.reshape(T_ALL // TM_C, 1, TM_C * TOPK), wsel, h, ys,
        w_sh_gate[layer].astype(BF16), w_sh_up[layer].astype(BF16), w_sh_down[layer].astype(BF16),
        ln_g[layer][None, :], ln_b[layer][None, :])


def _ssm_prep_kernel(lr_ref, li_ref, ldt_ref, br_ref, bi_ref, abr_ref, abi_ref, bbr_ref, bbi_ref):
    lr = lr_ref[...]
    li = li_ref[...]
    dt = jnp.exp(ldt_ref[...])
    mag = jnp.exp(lr * dt)
    ab_re = mag * jnp.cos(li * dt)
    ab_im = mag * jnp.sin(li * dt)
    den = lr * lr + li * li
    nr = ab_re - 1.0
    ni = ab_im
    cr = (nr * lr + ni * li) / den
    ci = (ni * lr - nr * li) / den
    br = br_ref[...]
    bi = bi_ref[...]
    abr_ref[...] = ab_re
    abi_ref[...] = ab_im
    bbr_ref[...] = cr * br - ci * bi
    bbi_ref[...] = cr * bi + ci * br


def _ssm_prep_call(lam_re, lam_im, log_dt, b_re, b_im):
    wide = (NG, GC * NS)
    lr = jnp.tile(lam_re, (1, GC))
    li = jnp.tile(lam_im, (1, GC))
    ldt = jnp.broadcast_to(log_dt[:, None], wide)
    br = jnp.transpose(b_re, (0, 2, 1)).reshape(wide)
    bi = jnp.transpose(b_im, (0, 2, 1)).reshape(wide)
    sds = jax.ShapeDtypeStruct(wide, F32)
    return pl.pallas_call(
        _ssm_prep_kernel, out_shape=[sds, sds, sds, sds], name="ssm_prep",
        compiler_params=pltpu.CompilerParams(vmem_limit_bytes=VMEM_LIMIT),
    )(lr, li, ldt, br, bi)


def _cmul(ar, ai, xr, xi):
    return ar * xr - ai * xi, ar * xi + ai * xr


GROUPS_PER_ITER = 4


def _ssm_p_kernel(u_ref, wb_ref, wc_ref, a_ref, d_ref, z_ref, st_ref, s_scr):
    row = lax.broadcasted_iota(I32, (LP, 128), 0)
    u = jnp.where(row >= PAD_FRONT, u_ref[...], 0.0)
    s_scr[...] = jnp.dot(u.astype(BF16), wb_ref[0], preferred_element_type=F32)

    a1r = a_ref[0, 0:1, :]
    a1i = a_ref[0, 1:2, :]
    pows = [(a1r, a1i)]
    for _ in range(7):
        pows.append(_cmul(a1r, a1i, *pows[-1]))
    apr = jnp.concatenate([p[0] for p in pows], axis=0)
    api = jnp.concatenate([p[1] for p in pows], axis=0)
    sub = lax.broadcasted_iota(I32, (8, CH_STATE), 0)

    def group_scan(xr, xi):
        for s in (1, 2, 4):
            ar, ai = pows[s - 1]
            sr = jnp.where(sub >= s, pltpu.roll(xr, s, axis=0), 0.0)
            si = jnp.where(sub >= s, pltpu.roll(xi, s, axis=0), 0.0)
            tr, ti = _cmul(ar, ai, sr, si)
            xr = xr + tr
            xi = xi + ti
        return xr, xi

    def body(it, carry):
        cr, ci = carry
        for gq in range(GROUPS_PER_ITER):
            r0 = pl.multiple_of((it * GROUPS_PER_ITER + gq) * 8, 8)
            xr, xi = group_scan(s_scr[pl.ds(r0, 8), 0:CH_STATE], s_scr[pl.ds(r0, 8), CH_STATE:])
            tr, ti = _cmul(apr, api, cr, ci)
            xr = xr + tr
            xi = xi + ti
            s_scr[pl.ds(r0, 8), 0:CH_STATE] = xr
            s_scr[pl.ds(r0, 8), CH_STATE:] = xi
            cr = xr[7:8, :]
            ci = xi[7:8, :]
        return cr, ci

    zero = jnp.zeros((1, CH_STATE), F32)
    cr, ci = lax.fori_loop(0, LP // (8 * GROUPS_PER_ITER), body, (zero, zero))
    st_ref[0, 0, :, 0:CH_STATE] = cr
    st_ref[0, 0, :, CH_STATE:] = ci

    y = jnp.dot(s_scr[...].astype(BF16), wc_ref[0], preferred_element_type=F32) + d_ref[0] * u
    z_ref[...] = jax.nn.gelu(y).astype(BF16)


def _ssm_p_call(h, wb_bf, wc_bf, a_tab, d_tab):
    return pl.pallas_call(
        _ssm_p_kernel,
        grid=(NB, NCHUNK),
        in_specs=[
            pl.BlockSpec((LP, 128), lambda b, k: (b, k)),
            pl.BlockSpec((1, 128, 2 * CH_STATE), lambda b, k: (k, 0, 0)),
            pl.BlockSpec((1, 2 * CH_STATE, 128), lambda b, k: (k, 0, 0)),
            pl.BlockSpec((1, 2, CH_STATE), lambda b, k: (k, 0, 0)),
            pl.BlockSpec((1, 1, 128), lambda b, k: (k, 0, 0)),
        ],
        out_specs=[
            pl.BlockSpec((LP, 128), lambda b, k: (b, k)),
            pl.BlockSpec((1, 1, 1, 2 * CH_STATE), lambda b, k: (b, k, 0, 0)),
        ],
        out_shape=[
            jax.ShapeDtypeStruct((T_PROMPT, D), BF16),
            jax.ShapeDtypeStruct((NB, NCHUNK, 1, 2 * CH_STATE), F32),
        ],
        scratch_shapes=[pltpu.VMEM((LP, 2 * CH_STATE), F32)],
        compiler_params=_cparams(("parallel", "parallel")),
        name="ssm_prompt",
    )(h, wb_bf, wc_bf, a_tab, d_tab)


def _ssm_s_kernel(u_ref, sr_ref, si_ref, wb_ref, wc_ref, a_ref, d_ref, z_ref, nr_ref, ni_ref):
    u = u_ref[...]
    bu = jnp.dot(u, wb_ref[0], preferred_element_type=F32, precision=lax.Precision.HIGHEST)
    ar = a_ref[0, 0:1, :]
    ai = a_ref[0, 1:2, :]
    tr, ti = _cmul(ar, ai, sr_ref[...], si_ref[...])
    nr = tr + bu[:, 0:CH_STATE]
    ni = ti + bu[:, CH_STATE:]
    nr_ref[...] = nr
    ni_ref[...] = ni
    s = jnp.concatenate([nr, ni], axis=1).astype(BF16)
    y = jnp.dot(s, wc_ref[0], preferred_element_type=F32) + d_ref[0] * u
    z_ref[...] = jax.nn.gelu(y).astype(BF16)


def _ssm_s_call(h, s0r, s0i, wb_f32, wc_bf, a_tab, d_tab):
    st = pl.BlockSpec((DEC, CH_STATE), lambda k: (0, k))
    return pl.pallas_call(
        _ssm_s_kernel,
        grid=(NCHUNK,),
        in_specs=[
            pl.BlockSpec((DEC, 128), lambda k: (T_PROMPT // DEC, k)),
            st,
            st,
            pl.BlockSpec((1, 128, 2 * CH_STATE), lambda k: (k, 0, 0)),
            pl.BlockSpec((1, 2 * CH_STATE, 128), lambda k: (k, 0, 0)),
            pl.BlockSpec((1, 2, CH_STATE), lambda k: (k, 0, 0)),
            pl.BlockSpec((1, 1, 128), lambda k: (k, 0, 0)),
        ],
        out_specs=[pl.BlockSpec((DEC, 128), lambda k: (0, k)), st, st],
        out_shape=[
            jax.ShapeDtypeStruct((DEC, D), BF16),
            jax.ShapeDtypeStruct((DEC, NG * NS), F32),
            jax.ShapeDtypeStruct((DEC, NG * NS), F32),
        ],
        compiler_params=_cparams(("parallel",)),
        name="ssm_sample",
    )(h, s0r, s0i, wb_f32, wc_bf, a_tab, d_tab)


def _glu_ln_kernel(z_ref, w_ref, bg_ref, h_ref, g_ref, b_ref, out_ref):
    acc = jnp.dot(z_ref[...], w_ref[...], preferred_element_type=F32) + bg_ref[...]
    m = acc[:, :D] * _sigmoid(acc[:, D:])
    out_ref[...] = _layer_norm(ALPHA * h_ref[...] + m, g_ref[...], b_ref[...])


def _glu_ln_call(z, w_bf, bg, h, g, b):
    vec = pl.BlockSpec((1, D), lambda i: (0, 0))
    return pl.pallas_call(
        _glu_ln_kernel,
        grid=(T_ALL // TM,),
        in_specs=[
            pl.BlockSpec((TM, D), lambda i: (i, 0)),
            pl.BlockSpec((D, 2 * D), lambda i: (0, 0), pipeline_mode=pl.Buffered(1)),
            pl.BlockSpec((1, 2 * D), lambda i: (0, 0)),
            pl.BlockSpec((TM, D), lambda i: (i, 0)),
            vec,
            vec,
        ],
        out_specs=pl.BlockSpec((TM, D), lambda i: (i, 0)),
        out_shape=jax.ShapeDtypeStruct((T_ALL, D), F32),
        compiler_params=_cparams(("parallel",)),
        name="glu_ln",
    )(z, w_bf, bg, h, g, b)


def _block_diag_in(t):
    t4 = t.reshape(NCHUNK, 8, GC, NS)
    eye = jnp.eye(8, dtype=t.dtype)
    return jnp.einsum("kgcn,gh->kgchn", t4, eye).reshape(NCHUNK, 128, CH_STATE)


def _block_diag_out(t):
    t4 = t.reshape(NCHUNK, 8, GC, NS)
    eye = jnp.eye(8, dtype=t.dtype)
    return jnp.einsum("kgcn,gh->kgnhc", t4, eye).reshape(NCHUNK, CH_STATE, 128)


def kernel(x_prompt, x_sample, cache_k, cache_v, state_ssm_re, state_ssm_im, meta_tokens, w_qkv, b_qkv, attn_sinks, w_o, b_o, ssm_lam_re, ssm_lam_im, ssm_log_dt, ssm_b_re, ssm_b_im, ssm_c_re, ssm_c_im, ssm_d, w_glu, b_glu, ln_mix_g, ln_mix_b, w_router, router_bias, w_exp_gate, w_exp_up, w_exp_down, w_sh_gate, w_sh_up, w_sh_down, ln_ffn_g, ln_ffn_b):
    moe_w = (w_router, router_bias, w_exp_gate, w_exp_up, w_exp_down, w_sh_gate, w_sh_up, w_sh_down,
             ln_ffn_g, ln_ffn_b)

    meta = jnp.broadcast_to(meta_tokens[None], (NB, N_META, D))
    hp = jnp.concatenate([jnp.zeros((NB, PAD_FRONT, D), F32), meta, x_prompt], axis=1)
    h = jnp.concatenate([hp.reshape(T_PROMPT, D), x_sample.reshape(DEC, D)], axis=0)

    rc, rs1, rs2 = _rope_tables()
    q, k, v = _qkv_call(h, w_qkv[0].astype(BF16), b_qkv[0][None, :], rc, rs1, rs2)
    o_p = _attn_p_call(attn_sinks[0], q, k, v)
    q3 = q[T_PROMPT:].astype(F32).reshape(DEC, NH, HD)
    o_s, ck_new, cv_new = _attn_s_call(
        attn_sinks[0][:, None], q3, k, v,
        cache_k[0].reshape(DEC, BLK, NKV * HD), cache_v[0].reshape(DEC, BLK, NKV * HD))
    o = jnp.concatenate([o_p, o_s.reshape(DEC, D).astype(BF16)], axis=0)
    h = _oproj_ln_call(o, w_o[0].astype(BF16), b_o[0][None, :], h,
                       ln_mix_g[0][None, :], ln_mix_b[0][None, :])
    h = _moe_layer(0, h, *moe_w)

    kp = k[:T_PROMPT].reshape(NB, LP, NKV, HD)[:, LP - BLK:]
    vp = v[:T_PROMPT].reshape(NB, LP, NKV, HD)[:, LP - BLK:]

    ab_re, ab_im, bb_re, bb_im = _ssm_prep_call(
        ssm_lam_re[0], ssm_lam_im[0], ssm_log_dt[0], ssm_b_re[0], ssm_b_im[0])
    wb = jnp.concatenate([_block_diag_in(bb_re), _block_diag_in(bb_im)], axis=2)
    wc = jnp.concatenate([_block_diag_out(ssm_c_re[0]), -_block_diag_out(ssm_c_im[0])], axis=1)
    wc_bf = wc.astype(BF16)
    a_tab = jnp.stack([ab_re[:, :NS].reshape(NCHUNK, CH_STATE),
                       ab_im[:, :NS].reshape(NCHUNK, CH_STATE)], axis=1)
    d_tab = ssm_d[0].reshape(NCHUNK, 1, 128)
    z_p, st_p = _ssm_p_call(h, wb.astype(BF16), wc_bf, a_tab, d_tab)
    z_s, sr_new, si_new = _ssm_s_call(
        h, state_ssm_re[0].reshape(DEC, NG * NS), state_ssm_im[0].reshape(DEC, NG * NS),
        wb, wc_bf, a_tab, d_tab)
    z = jnp.concatenate([z_p, z_s], axis=0)
    h = _glu_ln_call(z, w_glu[0].astype(BF16), b_glu[0][None, :], h,
                     ln_mix_g[1][None, :], ln_mix_b[1][None, :])
    h = _moe_layer(1, h, *moe_w)

    y_prompt = h[:T_PROMPT].reshape(NB, LP, D)[:, PAD_FRONT + N_META:]
    y_sample = h[T_PROMPT:].reshape(DEC, 1, D)
    st_p = st_p.reshape(NB, NCHUNK, 2, 8, NS)
    rp = st_p[:, :, 0].reshape(NB, NG, NS)
    ip = st_p[:, :, 1].reshape(NB, NG, NS)
    return (y_prompt, y_sample,
            kp[None], vp[None],
            ck_new.reshape(1, DEC, BLK, NKV, HD), cv_new.reshape(1, DEC, BLK, NKV, HD),
            rp[None], ip[None],
            sr_new.reshape(1, DEC, NG, NS), si_new.reshape(1, DEC, NG, NS))
```

```python
import functools
import math

import jax
import jax.numpy as jnp
from jax import lax
from jax.experimental import pallas as pl
from jax.experimental.pallas import tpu as pltpu

F32 = jnp.float32
BF16 = jnp.bfloat16
I32 = jnp.int32
U32 = jnp.uint32

D = 2048
HALF = D // 2
NB = 4
N_META = 16
SEQ = 2048
L = N_META + SEQ
BLK = 128
PAD_FRONT = (-L) % BLK
LP = L + PAD_FRONT
NBLK = LP // BLK
T_PROMPT = NB * LP
DEC = 128
T_ALL = T_PROMPT + DEC
PAST_LEN = 8192
HD = 64
NH = 32
NKV = 4
QPK = NH // NKV
QKV = (NH + 2 * NKV) * HD
QK_COLS = (NH + NKV) * HD
ROT = HD // 4
ROT_HALF = ROT // 2
ROPE_THETA = 500000.0
NG = 128
GC = 16
NS = 64
NCHUNK = 16
CH_STATE = 8 * NS
NE = 64
TOPK = 8
NEG = 8
PER_GRP = NE // NEG
TOPG = 4
DE = 512
ROUTED_SCALE = 2.5
DEPTH = 2
ALPHA = (2 * DEPTH) ** 0.25
LN_EPS = 1e-5

V7X_VMEM_BYTES = 64 * 1024 * 1024
VMEM_LIMIT = 56 * 1024 * 1024
LANES = 128

TM = 384
TM_E = 256
N_PAIRS = T_ALL * TOPK
NT_E = -(-N_PAIRS // TM_E) + NE
R_ROWS = NT_E * TM_E
TOK_BITS = 14
TOK_MASK = (1 << TOK_BITS) - 1
DUMMY_BASE = TOPK * T_ALL
YK_ROWS = DUMMY_BASE + TM_E
INV_STEPS = 8
INV_CH = N_PAIRS // INV_STEPS
assert T_ALL <= TOK_MASK and INV_CH % TOPK == 0 and T_ALL % TM == 0


def _cparams(sem):
    return pltpu.CompilerParams(dimension_semantics=sem, vmem_limit_bytes=VMEM_LIMIT)


def _sigmoid(x):
    return 1.0 / (1.0 + jnp.exp(-x))


def _layer_norm(y, g, b):
    mu = jnp.mean(y, axis=-1, keepdims=True)
    yc = y - mu
    var = jnp.mean(yc * yc, axis=-1, keepdims=True)
    return yc * lax.rsqrt(var + LN_EPS) * g + b


def _pack_pair(lo, hi):
    lo_b = lax.bitcast_convert_type(lo.astype(BF16).astype(F32), U32) >> 16
    hi_b = lax.bitcast_convert_type(hi.astype(BF16).astype(F32), U32) & jnp.uint32(0xFFFF0000)
    return lo_b | hi_b


def _unpack_pair(w):
    lo = lax.bitcast_convert_type(w << 16, F32)
    hi = lax.bitcast_convert_type(w & jnp.uint32(0xFFFF0000), F32)
    return lo, hi


def _qkv_kernel(x_ref, w_ref, b_ref, c_ref, s1_ref, s2_ref, q_ref, k_ref, v_ref):
    xb = x_ref[...].astype(BF16)
    acc = jnp.dot(xb, w_ref[...], preferred_element_type=F32) + b_ref[...]
    c = c_ref[...]
    s1 = s1_ref[...]
    s2 = s2_ref[...]
    for j in range(QK_COLS // LANES):
        blk = acc[:, j * LANES:(j + 1) * LANES]
        r = (blk * c + pltpu.roll(blk, LANES - ROT_HALF, axis=1) * s1
             + pltpu.roll(blk, ROT_HALF, axis=1) * s2)
        if j < D // LANES:
            q_ref[:, j * LANES:(j + 1) * LANES] = (r * (1.0 / math.sqrt(HD))).astype(BF16)
        else:
            jj = j - D // LANES
            k_ref[:, jj * LANES:(jj + 1) * LANES] = r
    v_ref[...] = acc[:, QK_COLS:]


def _qkv_call(x, w_bf, b, rc, rs1, rs2):
    return pl.pallas_call(
        _qkv_kernel,
        grid=(T_ALL // TM,),
        in_specs=[
            pl.BlockSpec((TM, D), lambda i: (i, 0)),
            pl.BlockSpec((D, QKV), lambda i: (0, 0)),
            pl.BlockSpec((1, QKV), lambda i: (0, 0)),
            pl.BlockSpec((TM, LANES), lambda i: (i, 0)),
            pl.BlockSpec((TM, LANES), lambda i: (i, 0)),
            pl.BlockSpec((TM, LANES), lambda i: (i, 0)),
        ],
        out_specs=[
            pl.BlockSpec((TM, D), lambda i: (i, 0)),
            pl.BlockSpec((TM, NKV * HD), lambda i: (i, 0)),
            pl.BlockSpec((TM, NKV * HD), lambda i: (i, 0)),
        ],
        out_shape=[
            jax.ShapeDtypeStruct((T_ALL, D), BF16),
            jax.ShapeDtypeStruct((T_ALL, NKV * HD), F32),
            jax.ShapeDtypeStruct((T_ALL, NKV * HD), F32),
        ],
        compiler_params=_cparams(("parallel",)),
        name="qkv_rope",
    )(x, w_bf, b, rc, rs1, rs2)


def _rope_tables():
    pos_p = jnp.maximum(jnp.arange(LP, dtype=I32) - PAD_FRONT, 0)
    pos = jnp.concatenate([jnp.tile(pos_p, NB), jnp.full((DEC,), PAST_LEN, I32)]).astype(F32)
    inv_freq = ROPE_THETA ** (-jnp.arange(0, ROT, 2, dtype=F32) / ROT)
    ang = pos[:, None] * inv_freq[None, :]
    cos = jnp.cos(ang)
    sin = jnp.sin(ang)
    ones = jnp.ones((T_ALL, HD - ROT), F32)
    zer = jnp.zeros((T_ALL, HD - ROT), F32)
    zh = jnp.zeros((T_ALL, ROT_HALF), F32)
    c = jnp.concatenate([cos, cos, ones], axis=1)
    s1 = jnp.concatenate([-sin, zh, zer], axis=1)
    s2 = jnp.concatenate([zh, sin, zer], axis=1)
    return tuple(jnp.concatenate([t, t], axis=1) for t in (c, s1, s2))


def _attn_p_kernel(sink_ref, q_ref, kp_ref, kc_ref, vp_ref, vc_ref, o_ref):
    j = pl.program_id(0) % NBLK
    keys = jnp.concatenate([kp_ref[...], kc_ref[...]], axis=0).astype(BF16)
    vals = jnp.concatenate([vp_ref[...], vc_ref[...]], axis=0).astype(BF16)
    r = lax.broadcasted_iota(I32, (BLK, 2 * BLK), 0)
    c = lax.broadcasted_iota(I32, (BLK, 2 * BLK), 1)
    dist = BLK + r - c
    kpos = (j - 1) * BLK - PAD_FRONT + c
    mask = (dist >= 0) & (dist <= BLK) & (kpos >= 0)
    for g in range(NKV):
        kg = keys[:, g * HD:(g + 1) * HD]
        vg = vals[:, g * HD:(g + 1) * HD]
        for hh in range(QPK):
            h = g * QPK + hh
            qh = q_ref[:, h * HD:(h + 1) * HD]
            s = lax.dot_general(qh, kg, (((1,), (1,)), ((), ())), preferred_element_type=F32)
            s = jnp.where(mask, s, -jnp.inf)
            sk = sink_ref[h]
            m = jnp.maximum(jnp.max(s, axis=1, keepdims=True), sk)
            p = jnp.exp(s - m)
            den = jnp.sum(p, axis=1, keepdims=True) + jnp.exp(sk - m)
            oh = jnp.dot(p.astype(BF16), vg, preferred_element_type=F32) / den
            o_ref[:, h * HD:(h + 1) * HD] = oh.astype(BF16)


def _attn_p_call(sinks, q, k, v):
    prev = lambda i: (jnp.where(i % NBLK == 0, i, i - 1), 0)
    cur = lambda i: (i, 0)
    return pl.pallas_call(
        _attn_p_kernel,
        grid=(NB * NBLK,),
        in_specs=[
            pl.BlockSpec(memory_space=pltpu.SMEM),
            pl.BlockSpec((BLK, D), cur),
            pl.BlockSpec((BLK, NKV * HD), prev),
            pl.BlockSpec((BLK, NKV * HD), cur),
            pl.BlockSpec((BLK, NKV * HD), prev),
            pl.BlockSpec((BLK, NKV * HD), cur),
        ],
        out_specs=pl.BlockSpec((BLK, D), cur),
        out_shape=jax.ShapeDtypeStruct((T_PROMPT, D), BF16),
        compiler_params=_cparams(("parallel",)),
        name="attn_prompt",
    )(sinks, q, k, k, v, v)


SEQ_PER_STEP = 16


def _attn_s_kernel(sink_ref, q_ref, kn_ref, vn_ref, ck_ref, cv_ref, o_ref, cko_ref, cvo_ref):
    row = lax.broadcasted_iota(I32, (BLK, NKV * HD), 0)

    def body(s, carry):
        kc = ck_ref[s]
        vc = cv_ref[s]
        kn = kn_ref[pl.ds(s, 1), :]
        vn = vn_ref[pl.ds(s, 1), :]
        kb = kc.astype(BF16)
        vb = vc.astype(BF16)
        knr = kn.astype(BF16).astype(F32)
        vnr = vn.astype(BF16).astype(F32)
        for g in range(NKV):
            qg = q_ref[s, g * QPK:(g + 1) * QPK, :].astype(BF16)
            sc = lax.dot_general(qg, kb[:, g * HD:(g + 1) * HD], (((1,), (1,)), ((), ())),
                                 preferred_element_type=F32)
            sn = jnp.sum(qg.astype(F32) * knr[:, g * HD:(g + 1) * HD], axis=1, keepdims=True)
            sk = sink_ref[g * QPK:(g + 1) * QPK, :]
            m = jnp.maximum(jnp.maximum(jnp.max(sc, axis=1, keepdims=True), sn), sk)
            p = jnp.exp(sc - m)
            pn = jnp.exp(sn - m)
            den = jnp.sum(p, axis=1, keepdims=True) + pn + jnp.exp(sk - m)
            og = jnp.dot(p.astype(BF16), vb[:, g * HD:(g + 1) * HD], preferred_element_type=F32)
            og = og + pn.astype(BF16).astype(F32) * vnr[:, g * HD:(g + 1) * HD]
            o_ref[s, g * QPK:(g + 1) * QPK, :] = og / den
        cko_ref[s] = jnp.where(row == BLK - 1, kn, pltpu.roll(kc, BLK - 1, axis=0))
        cvo_ref[s] = jnp.where(row == BLK - 1, vn, pltpu.roll(vc, BLK - 1, axis=0))
        return carry

    lax.fori_loop(0, SEQ_PER_STEP, body, 0)


def _attn_s_call(sinks_col, q3, k, v, cache_k, cache_v):
    sp = SEQ_PER_STEP
    kv_off = T_PROMPT // sp
    return pl.pallas_call(
        _attn_s_kernel,
        grid=(DEC // sp,),
        in_specs=[
            pl.BlockSpec((NH, 1), lambda i: (0, 0)),
            pl.BlockSpec((sp, NH, HD), lambda i: (i, 0, 0)),
            pl.BlockSpec((sp, NKV * HD), lambda i: (kv_off + i, 0)),
            pl.BlockSpec((sp, NKV * HD), lambda i: (kv_off + i, 0)),
            pl.BlockSpec((sp, BLK, NKV * HD), lambda i: (i, 0, 0)),
            pl.BlockSpec((sp, BLK, NKV * HD), lambda i: (i, 0, 0)),
        ],
        out_specs=[
            pl.BlockSpec((sp, NH, HD), lambda i: (i, 0, 0)),
            pl.BlockSpec((sp, BLK, NKV * HD), lambda i: (i, 0, 0)),
            pl.BlockSpec((sp, BLK, NKV * HD), lambda i: (i, 0, 0)),
        ],
        out_shape=[
            jax.ShapeDtypeStruct((DEC, NH, HD), F32),
            jax.ShapeDtypeStruct((DEC, BLK, NKV * HD), F32),
            jax.ShapeDtypeStruct((DEC, BLK, NKV * HD), F32),
        ],
        compiler_params=_cparams(("parallel",)),
        name="attn_sample",
    )(sinks_col, q3, k, v, cache_k, cache_v)


def _oproj_ln_kernel(o_ref, w_ref, bo_ref, h_ref, g_ref, b_ref, out_ref, hp_ref):
    m = jnp.dot(o_ref[...], w_ref[...], preferred_element_type=F32) + bo_ref[...]
    out = _layer_norm(ALPHA * h_ref[...] + m, g_ref[...], b_ref[...])
    out_ref[...] = out
    hp_ref[...] = _pack_pair(out[:, :HALF], out[:, HALF:])


def _oproj_ln_call(o, w_bf, bo, h, g, b):
    vec = pl.BlockSpec((1, D), lambda i: (0, 0))
    return pl.pallas_call(
        _oproj_ln_kernel,
        grid=(T_ALL // TM,),
        in_specs=[
            pl.BlockSpec((TM, D), lambda i: (i, 0)),
            pl.BlockSpec((D, D), lambda i: (0, 0)),
            vec,
            pl.BlockSpec((TM, D), lambda i: (i, 0)),
            vec,
            vec,
        ],
        out_specs=[pl.BlockSpec((TM, D), lambda i: (i, 0)), pl.BlockSpec((TM, HALF), lambda i: (i, 0))],
        out_shape=[jax.ShapeDtypeStruct((T_ALL, D), F32), jax.ShapeDtypeStruct((T_ALL, HALF), U32)],
        compiler_params=_cparams(("parallel",)),
        name="oproj_ln",
    )(o, w_bf, bo, h, g, b)


def _router_kernel(h_ref, wr_ref, rb_ref, eidx_ref, wsel_ref, rank_ref, cnt_ref, carry_ref):
    i = pl.program_id(0)

    @pl.when(i == 0)
    def _():
        carry_ref[...] = jnp.zeros_like(carry_ref)

    logits = jnp.dot(h_ref[...], wr_ref[...], preferred_element_type=F32,
                     precision=lax.Precision.HIGHEST)
    scores = _sigmoid(logits)
    biased = scores + rb_ref[...]
    lane_i = lax.broadcasted_iota(I32, (TM, NE), 1)
    lane = lane_i.astype(F32)
    grp = (lane_i // PER_GRP).astype(F32)
    ninf = -jnp.inf
    big = float(NE)

    gs = jnp.zeros((TM, NE), F32)
    for gi in range(NEG):
        seg = jnp.where(grp == gi, biased, ninf)
        m1 = jnp.max(seg, axis=1, keepdims=True)
        i1 = jnp.min(jnp.where(seg == m1, lane, big), axis=1, keepdims=True)
        m2 = jnp.max(jnp.where(lane == i1, ninf, seg), axis=1, keepdims=True)
        gs = jnp.where(grp == gi, m1 + m2, gs)

    emask = jnp.zeros((TM, NE), F32)
    for _ in range(TOPG):
        m = jnp.max(gs, axis=1, keepdims=True)
        gsel = jnp.min(jnp.where(gs == m, grp, big), axis=1, keepdims=True)
        hit = grp == gsel
        emask = jnp.where(hit, 1.0, emask)
        gs = jnp.where(hit, ninf, gs)

    work = jnp.where(emask > 0.5, biased, ninf)
    idx_cols = []
    w_cols = []
    onehot = jnp.zeros((TM, NE), F32)
    for _ in range(TOPK):
        m = jnp.max(work, axis=1, keepdims=True)
        ik = jnp.min(jnp.where(work == m, lane, big), axis=1, keepdims=True)
        hit = lane == ik
        idx_cols.append(ik)
        w_cols.append(jnp.sum(jnp.where(hit, scores, 0.0), axis=1, keepdims=True))
        onehot = jnp.where(hit, 1.0, onehot)
        work = jnp.where(hit, ninf, work)
    wsum = w_cols[0]
    for wk in w_cols[1:]:
        wsum = wsum + wk

    rr = lax.broadcasted_iota(I32, (TM, TM), 0)
    cc = lax.broadcasted_iota(I32, (TM, TM), 1)
    tri = jnp.where(cc < rr, 1.0, 0.0).astype(BF16)
    prefix = jnp.dot(tri, onehot.astype(BF16), preferred_element_type=F32) + carry_ref[...]
    carry_ref[...] = carry_ref[...] + jnp.sum(onehot, axis=0, keepdims=True)
    cnt_ref[...] = carry_ref[...]

    lane8 = lax.broadcasted_iota(I32, (TM, TOPK), 1)
    eidx = jnp.zeros((TM, TOPK), F32)
    wsel = jnp.zeros((TM, TOPK), F32)
    rank = jnp.zeros((TM, TOPK), F32)
    for k in range(TOPK):
        rk = jnp.sum(jnp.where(lane == idx_cols[k], prefix, 0.0), axis=1, keepdims=True)
        eidx = jnp.where(lane8 == k, idx_cols[k], eidx)
        wsel = jnp.where(lane8 == k, w_cols[k] / wsum * ROUTED_SCALE, wsel)
        rank = jnp.where(lane8 == k, rk, rank)
    eidx_ref[...] = eidx.astype(I32)
    wsel_ref[...] = wsel
    rank_ref[...] = rank.astype(I32)


def _router_call(h, w_router, router_bias):
    tk = pl.BlockSpec((TM, TOPK), lambda i: (i, 0))
    return pl.pallas_call(
        _router_kernel,
        grid=(T_ALL // TM,),
        in_specs=[
            pl.BlockSpec((TM, D), lambda i: (i, 0)),
            pl.BlockSpec((D, NE), lambda i: (0, 0)),
            pl.BlockSpec((1, NE), lambda i: (0, 0)),
        ],
        out_specs=[tk, tk, tk, pl.BlockSpec((1, NE), lambda i: (0, 0))],
        out_shape=[
            jax.ShapeDtypeStruct((T_ALL, TOPK), I32),
            jax.ShapeDtypeStruct((T_ALL, TOPK), F32),
            jax.ShapeDtypeStruct((T_ALL, TOPK), I32),
            jax.ShapeDtypeStruct((1, NE), F32),
        ],
        scratch_shapes=[pltpu.VMEM((1, NE), F32)],
        compiler_params=_cparams(("arbitrary",)),
        name="router",
    )(h, w_router, router_bias)


def _rowinfo_kernel(cnt_ref, off_ref, pos_ref, info_ref):
    i = pl.program_id(0)

    def pad_word(r):
        return (DUMMY_BASE + (r & (TM_E - 1))) << TOK_BITS

    @pl.when(i == 0)
    def _():
        def fill(r, carry):
            info_ref[r] = pad_word(r)
            return carry

        def per_expert(e, carry):
            lo = off_ref[e] + cnt_ref[e]
            hi = off_ref[e] + ((cnt_ref[e] + TM_E - 1) // TM_E) * TM_E
            lax.fori_loop(lo, hi, fill, 0)
            return carry

        lax.fori_loop(0, NE, per_expert, 0)
        last = NE - 1
        used = off_ref[last] + ((cnt_ref[last] + TM_E - 1) // TM_E) * TM_E
        lax.fori_loop(used, R_ROWS, fill, 0)

    tok0 = i * (INV_CH // TOPK)

    def body(j, carry):
        tok = tok0 + j
        for k in range(TOPK):
            info_ref[pos_ref[0, 0, j * TOPK + k]] = ((k * T_ALL + tok) << TOK_BITS) | tok
        return carry

    lax.fori_loop(0, INV_CH // TOPK, body, 0)


def _rowinfo_call(cnt, off, pos):
    return pl.pallas_call(
        _rowinfo_kernel,
        grid_spec=pltpu.PrefetchScalarGridSpec(
            num_scalar_prefetch=2,
            grid=(INV_STEPS,),
            in_specs=[pl.BlockSpec((1, 1, INV_CH), lambda i, c, o: (i, 0, 0), memory_space=pltpu.SMEM)],
            out_specs=pl.BlockSpec(memory_space=pltpu.SMEM),
        ),
        out_shape=jax.ShapeDtypeStruct((R_ROWS,), I32),
        compiler_params=pltpu.CompilerParams(dimension_semantics=("arbitrary",)),
        name="moe_rowinfo",
    )(cnt, off, pos.reshape(INV_STEPS, 1, INV_CH))


def _expert_kernel(layer, te_ref, gi_ref, ne_ref, na_ref, ginfo_ref, sinfo_ref, hp_ref,
                   wg_hbm, wu_hbm, wd_hbm, yk_ref,
                   xb0, xb1, yb0, yb1, wgf, wuf, wdf, wg_s, wu_s, wd_s, gsem, ssem, wsem, zsem):
    s = pl.program_id(0)
    na = na_ref[0]
    xb = (xb0, xb1)
    yb = (yb0, yb1)

    def weight_copies(e, slot):
        return (pltpu.make_async_copy(wg_hbm.at[layer, e], wgf.at[slot], wsem.at[0]),
                pltpu.make_async_copy(wu_hbm.at[layer, e], wuf.at[slot], wsem.at[1]),
                pltpu.make_async_copy(wd_hbm.at[layer, e], wdf.at[slot], wsem.at[2]))

    def gather_row(par, r):
        tok = ginfo_ref[0, 0, r] & TOK_MASK
        pltpu.make_async_copy(hp_ref.at[pl.ds(tok, 1)], xb[par].at[pl.ds(r, 1)], gsem.at[par]).start()

    def scatter_row(par, r):
        dst = sinfo_ref[0, 0, r] >> TOK_BITS
        pltpu.make_async_copy(yb[par].at[pl.ds(r, 1)], yk_ref.at[pl.ds(dst, 1)], ssem.at[par]).start()

    def compute(par, between):
        lo, hi = _unpack_pair(xb[1 - par][...])
        lo = lo.astype(BF16)
        hi = hi.astype(BF16)
        between(0)
        hg = (jnp.dot(lo, wg_s[0:HALF, :], preferred_element_type=F32)
              + jnp.dot(hi, wg_s[HALF:D, :], preferred_element_type=F32))
        between(1)
        hu = (jnp.dot(lo, wu_s[0:HALF, :], preferred_element_type=F32)
              + jnp.dot(hi, wu_s[HALF:D, :], preferred_element_type=F32))
        between(2)
        act = (hg * _sigmoid(hg) * hu).astype(BF16)
        y = jnp.dot(act, wd_s[...], preferred_element_type=F32)
        between(3)
        yb[1 - par][...] = _pack_pair(y[:, :HALF], y[:, HALF:])

    @pl.when(jnp.logical_and(s >= 1, s <= na))
    def _():
        pltpu.make_async_copy(hp_ref.at[pl.ds(0, TM_E)], xb0, gsem.at[(s + 1) % 2]).wait()

    @pl.when(jnp.logical_and(s >= 3, s <= na + 2))
    def _():
        pltpu.make_async_copy(yb0, yk_ref.at[pl.ds(0, TM_E)], ssem.at[(s + 1) % 2]).wait()

    @pl.when(s == 0)
    def _():
        for cp in weight_copies(te_ref[0], 0):
            cp.start()
        yb0[...] = jnp.zeros_like(yb0)
        zc = pltpu.make_async_copy(yb0, yk_ref.at[pl.ds(DUMMY_BASE, TM_E)], zsem)
        zc.start()
        zc.wait()

    c = s - 1
    cc = jnp.clip(c, 0, na - 1)
    first = jnp.logical_or(c == 0, te_ref[cc] != te_ref[jnp.maximum(cc - 1, 0)])

    @pl.when(jnp.logical_and(jnp.logical_and(c >= 0, c < na), first))
    def _():
        slot = gi_ref[cc] % 2
        for cp in weight_copies(te_ref[cc], slot):
            cp.wait()
        wg_s[...] = wgf[slot].astype(BF16)
        wu_s[...] = wuf[slot].astype(BF16)
        wd_s[...] = wdf[slot].astype(BF16)
        nxt = ne_ref[cc]

        @pl.when(nxt >= 0)
        def _():
            for cp in weight_copies(nxt, 1 - slot):
                cp.start()

    steady = jnp.logical_and(s >= 2, s < na)
    quarter = TM_E // 4
    for par in (0, 1):
        mine = (s % 2) == par

        @pl.when(jnp.logical_and(steady, mine))
        def _():
            def between(q):
                for r in range(q * quarter, (q + 1) * quarter):
                    gather_row(par, r)
                    scatter_row(par, r)

            compute(par, between)

        @pl.when(jnp.logical_and(jnp.logical_not(steady), mine))
        def _():
            @pl.when(s < na)
            def _():
                def g(r, carry):
                    gather_row(par, r)
                    return carry

                lax.fori_loop(0, TM_E, g, 0)

            @pl.when(jnp.logical_and(s >= 1, s <= na))
            def _():
                compute(par, lambda q: None)

            @pl.when(jnp.logical_and(s >= 2, s <= na + 1))
            def _():
                def sc(r, carry):
                    scatter_row(par, r)
                    return carry

                lax.fori_loop(0, TM_E, sc, 0)


def _expert_call(layer, te, gi, ne, nact, rowinfo, hp, w_gate, w_up, w_down):
    def gmap(s, te_r, gi_r, ne_r, na_r):
        return (jnp.minimum(s, na_r[0] - 1), 0, 0)

    def smap(s, te_r, gi_r, ne_r, na_r):
        return (jnp.clip(s - 2, 0, na_r[0] - 1), 0, 0)

    info = rowinfo.reshape(NT_E, 1, TM_E)
    anyspec = pl.BlockSpec(memory_space=pl.ANY)
    return pl.pallas_call(
        functools.partial(_expert_kernel, layer),
        grid_spec=pltpu.PrefetchScalarGridSpec(
            num_scalar_prefetch=4,
            grid=(NT_E + 3,),
            in_specs=[
                pl.BlockSpec((1, 1, TM_E), gmap, memory_space=pltpu.SMEM),
                pl.BlockSpec((1, 1, TM_E), smap, memory_space=pltpu.SMEM),
                anyspec, anyspec, anyspec, anyspec,
            ],
            out_specs=anyspec,
            scratch_shapes=[
                pltpu.VMEM((TM_E, HALF), U32),
                pltpu.VMEM((TM_E, HALF), U32),
                pltpu.VMEM((TM_E, HALF), U32),
                pltpu.VMEM((TM_E, HALF), U32),
                pltpu.VMEM((2, D, DE), F32),
                pltpu.VMEM((2, D, DE), F32),
                pltpu.VMEM((2, DE, D), F32),
                pltpu.VMEM((D, DE), BF16),
                pltpu.VMEM((D, DE), BF16),
                pltpu.VMEM((DE, D), BF16),
                pltpu.SemaphoreType.DMA((2,)),
                pltpu.SemaphoreType.DMA((2,)),
                pltpu.SemaphoreType.DMA((3,)),
                pltpu.SemaphoreType.DMA(()),
            ],
        ),
        out_shape=jax.ShapeDtypeStruct((YK_ROWS, HALF), U32),
        compiler_params=_cparams(("arbitrary",)),
        name="moe_experts",
    )(te, gi, ne, nact, info, info, hp, w_gate, w_up, w_down)


def _combine_kernel(w_ref, h_ref, *rest):
    yk_refs = rest[:TOPK]
    wsg_ref, wsu_ref, wsd_ref, g_ref, b_ref, out_ref = rest[TOPK:]
    h = h_ref[...]
    hb = h.astype(BF16)
    sg = jnp.dot(hb, wsg_ref[...], preferred_element_type=F32)
    su = jnp.dot(hb, wsu_ref[...], preferred_element_type=F32)
    act = (sg * _sigmoid(sg) * su).astype(BF16)
    y = ALPHA * h + jnp.dot(act, wsd_ref[...], preferred_element_type=F32)
    ylo = y[:, :HALF]
    yhi = y[:, HALF:]
    w = w_ref[...]
    for k in range(TOPK):
        lo, hi = _unpack_pair(yk_refs[k][...])
        wk = w[:, k:k + 1]
        ylo = ylo + wk * lo
        yhi = yhi + wk * hi
    out_ref[...] = _layer_norm(jnp.concatenate([ylo, yhi], axis=1), g_ref[...], b_ref[...])


def _combine_call(wsel, h, yk, wsg, wsu, wsd, g, b):
    vec = pl.BlockSpec((1, D), lambda i: (0, 0))
    nblk = T_ALL // TM
    once = dict(pipeline_mode=pl.Buffered(1))
    yk_specs = [pl.BlockSpec((TM, HALF), lambda i, k=k: (k * nblk + i, 0)) for k in range(TOPK)]
    return pl.pallas_call(
        _combine_kernel,
        grid=(nblk,),
        in_specs=[
            pl.BlockSpec((TM, TOPK), lambda i: (i, 0)),
            pl.BlockSpec((TM, D), lambda i: (i, 0)),
            *yk_specs,
            pl.BlockSpec((D, DE), lambda i: (0, 0), **once),
            pl.BlockSpec((D, DE), lambda i: (0, 0), **once),
            pl.BlockSpec((DE, D), lambda i: (0, 0), **once),
            vec,
            vec,
        ],
        out_specs=pl.BlockSpec((TM, D), lambda i: (i, 0)),
        out_shape=jax.ShapeDtypeStruct((T_ALL, D), F32),
        compiler_params=_cparams(("parallel",)),
        name="moe_combine",
    )(wsel, h, *([yk] * TOPK), wsg, wsu, wsd, g, b)


def _moe_layer(layer, h, hp, w_router, router_bias, w_exp_gate, w_exp_up, w_exp_down,
               w_sh_gate, w_sh_up, w_sh_down, ln_g, ln_b):
    eidx, wsel, rank, counts = _router_call(h, w_router[layer], router_bias[layer][None, :])
    cnt = counts[0].astype(I32)
    padded = ((cnt + TM_E - 1) // TM_E) * TM_E
    off_end = jnp.cumsum(padded)
    off = off_end - padded
    pos = off[eidx] + rank
    nact_s = off_end[-1] // TM_E
    tile_start = jnp.arange(NT_E, dtype=I32) * TM_E
    te = jnp.minimum(jnp.sum(tile_start[:, None] >= off_end[None, :], axis=1), NE - 1).astype(I32)
    is_first = jnp.concatenate([jnp.ones((1,), jnp.bool_), te[1:] != te[:-1]])
    gi = (jnp.cumsum(is_first.astype(I32)) - 1).astype(I32)
    nxt_tile = off_end[te] // TM_E
    ne = jnp.where(nxt_tile < nact_s, te[jnp.minimum(nxt_tile, NT_E - 1)], -1).astype(I32)

    rowinfo = _rowinfo_call(cnt, off, pos)
    yk = _expert_call(layer, te, gi, ne, nact_s.reshape(1).astype(I32), rowinfo, hp,
                      w_exp_gate, w_exp_up, w_exp_down)
    return _combine_call(
        wsel, h, yk,
        w_sh_gate[layer].astype(BF16), w_sh_up[layer].astype(BF16), w_sh_down[layer].astype(BF16),
        ln_g[layer][None, :], ln_b[layer][None, :])


def _ssm_prep_kernel(lr_ref, li_ref, ldt_ref, br_ref, bi_ref, abr_ref, abi_ref, bbr_ref, bbi_ref):
    lr = lr_ref[...]
    li = li_ref[...]
    dt = jnp.exp(ldt_ref[...])
    mag = jnp.exp(lr * dt)
    ab_re = mag * jnp.cos(li * dt)
    ab_im = mag * jnp.sin(li * dt)
    den = lr * lr + li * li
    nr = ab_re - 1.0
    ni = ab_im
    cr = (nr * lr + ni * li) / den
    ci = (ni * lr - nr * li) / den
    br = br_ref[...]
    bi = bi_ref[...]
    abr_ref[...] = ab_re
    abi_ref[...] = ab_im
    bbr_ref[...] = cr * br - ci * bi
    bbi_ref[...] = cr * bi + ci * br


def _ssm_prep_call(lam_re, lam_im, log_dt, b_re, b_im):
    wide = (NG, GC * NS)
    lr = jnp.tile(lam_re, (1, GC))
    li = jnp.tile(lam_im, (1, GC))
    ldt = jnp.broadcast_to(log_dt[:, None], wide)
    br = jnp.transpose(b_re, (0, 2, 1)).reshape(wide)
    bi = jnp.transpose(b_im, (0, 2, 1)).reshape(wide)
    sds = jax.ShapeDtypeStruct(wide, F32)
    return pl.pallas_call(
        _ssm_prep_kernel, out_shape=[sds, sds, sds, sds], name="ssm_prep",
        compiler_params=pltpu.CompilerParams(vmem_limit_bytes=VMEM_LIMIT),
    )(lr, li, ldt, br, bi)


def _cmul(ar, ai, xr, xi):
    return ar * xr - ai * xi, ar * xi + ai * xr


GROUPS_PER_ITER = 4


def _ssm_p_kernel(u_ref, wb_ref, wc_ref, a_ref, d_ref, z_ref, st_ref, s_scr):
    row = lax.broadcasted_iota(I32, (LP, LANES), 0)
    u = jnp.where(row >= PAD_FRONT, u_ref[...], 0.0)
    s_scr[...] = jnp.dot(u.astype(BF16), wb_ref[0], preferred_element_type=F32)

    a1r = a_ref[0, 0:1, :]
    a1i = a_ref[0, 1:2, :]
    pows = [(a1r, a1i)]
    for _ in range(7):
        pows.append(_cmul(a1r, a1i, *pows[-1]))
    apr = jnp.concatenate([p[0] for p in pows], axis=0)
    api = jnp.concatenate([p[1] for p in pows], axis=0)
    sub = lax.broadcasted_iota(I32, (8, CH_STATE), 0)

    def group_scan(xr, xi):
        for s in (1, 2, 4):
            ar, ai = pows[s - 1]
            sr = jnp.where(sub >= s, pltpu.roll(xr, s, axis=0), 0.0)
            si = jnp.where(sub >= s, pltpu.roll(xi, s, axis=0), 0.0)
            tr, ti = _cmul(ar, ai, sr, si)
            xr = xr + tr
            xi = xi + ti
        return xr, xi

    def body(it, carry):
        cr, ci = carry
        for gq in range(GROUPS_PER_ITER):
            r0 = pl.multiple_of((it * GROUPS_PER_ITER + gq) * 8, 8)
            xr, xi = group_scan(s_scr[pl.ds(r0, 8), 0:CH_STATE], s_scr[pl.ds(r0, 8), CH_STATE:])
            tr, ti = _cmul(apr, api, cr, ci)
            xr = xr + tr
            xi = xi + ti
            s_scr[pl.ds(r0, 8), 0:CH_STATE] = xr
            s_scr[pl.ds(r0, 8), CH_STATE:] = xi
            cr = xr[7:8, :]
            ci = xi[7:8, :]
        return cr, ci

    zero = jnp.zeros((1, CH_STATE), F32)
    cr, ci = lax.fori_loop(0, LP // (8 * GROUPS_PER_ITER), body, (zero, zero))
    st_ref[0, 0, :, 0:CH_STATE] = cr
    st_ref[0, 0, :, CH_STATE:] = ci

    y = jnp.dot(s_scr[...].astype(BF16), wc_ref[0], preferred_element_type=F32) + d_ref[0] * u
    z_ref[...] = jax.nn.gelu(y).astype(BF16)


def _ssm_p_call(h, wb_bf, wc_bf, a_tab, d_tab):
    return pl.pallas_call(
        _ssm_p_kernel,
        grid=(NB, NCHUNK),
        in_specs=[
            pl.BlockSpec((LP, LANES), lambda b, k: (b, k)),
            pl.BlockSpec((1, LANES, 2 * CH_STATE), lambda b, k: (k, 0, 0)),
            pl.BlockSpec((1, 2 * CH_STATE, LANES), lambda b, k: (k, 0, 0)),
            pl.BlockSpec((1, 2, CH_STATE), lambda b, k: (k, 0, 0)),
            pl.BlockSpec((1, 1, LANES), lambda b, k: (k, 0, 0)),
        ],
        out_specs=[
            pl.BlockSpec((LP, LANES), lambda b, k: (b, k)),
            pl.BlockSpec((1, 1, 1, 2 * CH_STATE), lambda b, k: (b, k, 0, 0)),
        ],
        out_shape=[
            jax.ShapeDtypeStruct((T_PROMPT, D), BF16),
            jax.ShapeDtypeStruct((NB, NCHUNK, 1, 2 * CH_STATE), F32),
        ],
        scratch_shapes=[pltpu.VMEM((LP, 2 * CH_STATE), F32)],
        compiler_params=_cparams(("parallel", "parallel")),
        name="ssm_prompt",
    )(h, wb_bf, wc_bf, a_tab, d_tab)


def _ssm_s_kernel(u_ref, sr_ref, si_ref, wb_ref, wc_ref, a_ref, d_ref, z_ref, nr_ref, ni_ref):
    u = u_ref[...]
    bu = jnp.dot(u, wb_ref[0], preferred_element_type=F32, precision=lax.Precision.HIGHEST)
    ar = a_ref[0, 0:1, :]
    ai = a_ref[0, 1:2, :]
    tr, ti = _cmul(ar, ai, sr_ref[...], si_ref[...])
    nr = tr + bu[:, 0:CH_STATE]
    ni = ti + bu[:, CH_STATE:]
    nr_ref[...] = nr
    ni_ref[...] = ni
    s = jnp.concatenate([nr, ni], axis=1).astype(BF16)
    y = jnp.dot(s, wc_ref[0], preferred_element_type=F32) + d_ref[0] * u
    z_ref[...] = jax.nn.gelu(y).astype(BF16)


def _ssm_s_call(h, s0r, s0i, wb_f32, wc_bf, a_tab, d_tab):
    st = pl.BlockSpec((DEC, CH_STATE), lambda k: (0, k))
    return pl.pallas_call(
        _ssm_s_kernel,
        grid=(NCHUNK,),
        in_specs=[
            pl.BlockSpec((DEC, LANES), lambda k: (T_PROMPT // DEC, k)),
            st,
            st,
            pl.BlockSpec((1, LANES, 2 * CH_STATE), lambda k: (k, 0, 0)),
            pl.BlockSpec((1, 2 * CH_STATE, LANES), lambda k: (k, 0, 0)),
            pl.BlockSpec((1, 2, CH_STATE), lambda k: (k, 0, 0)),
            pl.BlockSpec((1, 1, LANES), lambda k: (k, 0, 0)),
        ],
        out_specs=[pl.BlockSpec((DEC, LANES), lambda k: (0, k)), st, st],
        out_shape=[
            jax.ShapeDtypeStruct((DEC, D), BF16),
            jax.ShapeDtypeStruct((DEC, NG * NS), F32),
            jax.ShapeDtypeStruct((DEC, NG * NS), F32),
        ],
        compiler_params=_cparams(("parallel",)),
        name="ssm_sample",
    )(h, s0r, s0i, wb_f32, wc_bf, a_tab, d_tab)


def _glu_ln_kernel(z_ref, w_ref, bg_ref, h_ref, g_ref, b_ref, out_ref, hp_ref):
    acc = jnp.dot(z_ref[...], w_ref[...], preferred_element_type=F32) + bg_ref[...]
    m = acc[:, :D] * _sigmoid(acc[:, D:])
    out = _layer_norm(ALPHA * h_ref[...] + m, g_ref[...], b_ref[...])
    out_ref[...] = out
    hp_ref[...] = _pack_pair(out[:, :HALF], out[:, HALF:])


def _glu_ln_call(z, w_bf, bg, h, g, b):
    vec = pl.BlockSpec((1, D), lambda i: (0, 0))
    return pl.pallas_call(
        _glu_ln_kernel,
        grid=(T_ALL // TM,),
        in_specs=[
            pl.BlockSpec((TM, D), lambda i: (i, 0)),
            pl.BlockSpec((D, 2 * D), lambda i: (0, 0), pipeline_mode=pl.Buffered(1)),
            pl.BlockSpec((1, 2 * D), lambda i: (0, 0)),
            pl.BlockSpec((TM, D), lambda i: (i, 0)),
            vec,
            vec,
        ],
        out_specs=[pl.BlockSpec((TM, D), lambda i: (i, 0)), pl.BlockSpec((TM, HALF), lambda i: (i, 0))],
        out_shape=[jax.ShapeDtypeStruct((T_ALL, D), F32), jax.ShapeDtypeStruct((T_ALL, HALF), U32)],
        compiler_params=_cparams(("parallel",)),
        name="glu_ln",
    )(z, w_bf, bg, h, g, b)


def _block_diag_in(t):
    t4 = t.reshape(NCHUNK, 8, GC, NS)
    eye = jnp.eye(8, dtype=t.dtype)
    return jnp.einsum("kgcn,gh->kgchn", t4, eye).reshape(NCHUNK, LANES, CH_STATE)


def _block_diag_out(t):
    t4 = t.reshape(NCHUNK, 8, GC, NS)
    eye = jnp.eye(8, dtype=t.dtype)
    return jnp.einsum("kgcn,gh->kgnhc", t4, eye).reshape(NCHUNK, CH_STATE, LANES)


def kernel(x_prompt, x_sample, cache_k, cache_v, state_ssm_re, state_ssm_im, meta_tokens, w_qkv, b_qkv, attn_sinks, w_o, b_o, ssm_lam_re, ssm_lam_im, ssm_log_dt, ssm_b_re, ssm_b_im, ssm_c_re, ssm_c_im, ssm_d, w_glu, b_glu, ln_mix_g, ln_mix_b, w_router, router_bias, w_exp_gate, w_exp_up, w_exp_down, w_sh_gate, w_sh_up, w_sh_down, ln_ffn_g, ln_ffn_b):
    moe_w = (w_router, router_bias, w_exp_gate, w_exp_up, w_exp_down, w_sh_gate, w_sh_up, w_sh_down,
             ln_ffn_g, ln_ffn_b)

    meta = jnp.broadcast_to(meta_tokens[None], (NB, N_META, D))
    hp0 = jnp.concatenate([jnp.zeros((NB, PAD_FRONT, D), F32), meta, x_prompt], axis=1)
    h = jnp.concatenate([hp0.reshape(T_PROMPT, D), x_sample.reshape(DEC, D)], axis=0)

    rc, rs1, rs2 = _rope_tables()
    q, k, v = _qkv_call(h, w_qkv[0].astype(BF16), b_qkv[0][None, :], rc, rs1, rs2)
    o_p = _attn_p_call(attn_sinks[0], q, k, v)
    q3 = q[T_PROMPT:].astype(F32).reshape(DEC, NH, HD)
    o_s, ck_new, cv_new = _attn_s_call(
        attn_sinks[0][:, None], q3, k, v,
        cache_k[0].reshape(DEC, BLK, NKV * HD), cache_v[0].reshape(DEC, BLK, NKV * HD))
    o = jnp.concatenate([o_p, o_s.reshape(DEC, D).astype(BF16)], axis=0)
    h, hp = _oproj_ln_call(o, w_o[0].astype(BF16), b_o[0][None, :], h,
                           ln_mix_g[0][None, :], ln_mix_b[0][None, :])
    h = _moe_layer(0, h, hp, *moe_w)

    kp = k[:T_PROMPT].reshape(NB, LP, NKV, HD)[:, LP - BLK:]
    vp = v[:T_PROMPT].reshape(NB, LP, NKV, HD)[:, LP - BLK:]

    ab_re, ab_im, bb_re, bb_im = _ssm_prep_call(
        ssm_lam_re[0], ssm_lam_im[0], ssm_log_dt[0], ssm_b_re[0], ssm_b_im[0])
    wb = jnp.concatenate([_block_diag_in(bb_re), _block_diag_in(bb_im)], axis=2)
    wc = jnp.concatenate([_block_diag_out(ssm_c_re[0]), -_block_diag_out(ssm_c_im[0])], axis=1)
    wc_bf = wc.astype(BF16)
    a_tab = jnp.stack([ab_re[:, :NS].reshape(NCHUNK, CH_STATE),
                       ab_im[:, :NS].reshape(NCHUNK, CH_STATE)], axis=1)
    d_tab = ssm_d[0].reshape(NCHUNK, 1, LANES)
    z_p, st_p = _ssm_p_call(h, wb.astype(BF16), wc_bf, a_tab, d_tab)
    z_s, sr_new, si_new = _ssm_s_call(
        h, state_ssm_re[0].reshape(DEC, NG * NS), state_ssm_im[0].reshape(DEC, NG * NS),
        wb, wc_bf, a_tab, d_tab)
    z = jnp.concatenate([z_p, z_s], axis=0)
    h, hp = _glu_ln_call(z, w_glu[0].astype(BF16), b_glu[0][None, :], h,
                         ln_mix_g[1][None, :], ln_mix_b[1][None, :])
    h = _moe_layer(1, h, hp, *moe_w)

    y_prompt = h[:T_PROMPT].reshape(NB, LP, D)[:, PAD_FRONT + N_META:]
    y_sample = h[T_PROMPT:].reshape(DEC, 1, D)
    st_p = st_p.reshape(NB, NCHUNK, 2, 8, NS)
    rp = st_p[:, :, 0].reshape(NB, NG, NS)
    ip = st_p[:, :, 1].reshape(NB, NG, NS)
    return (y_prompt, y_sample,
            kp[None], vp[None],
            ck_new.reshape(1, DEC, BLK, NKV, HD), cv_new.reshape(1, DEC, BLK, NKV, HD),
            rp[None], ip[None],
            sr_new.reshape(1, DEC, NG, NS), si_new.reshape(1, DEC, NG, NS))
```

```python
import functools
import math

import jax
import jax.numpy as jnp
import numpy as np
from jax import lax
from jax.experimental import pallas as pl
from jax.experimental.pallas import tpu as pltpu

F32 = jnp.float32
BF16 = jnp.bfloat16
I32 = jnp.int32
U32 = jnp.uint32

D = 2048
HALF = D // 2
NB = 4
N_META = 16
SEQ = 2048
L = N_META + SEQ
BLK = 128
PAD_FRONT = (-L) % BLK
LP = L + PAD_FRONT
NBLK = LP // BLK
T_PROMPT = NB * LP
DEC = 128
T_ALL = T_PROMPT + DEC
PAST_LEN = 8192
HD = 64
NH = 32
NKV = 4
QPK = NH // NKV
QKV = (NH + 2 * NKV) * HD
QK_COLS = (NH + NKV) * HD
ROT = HD // 4
ROT_HALF = ROT // 2
ROPE_THETA = 500000.0
NG = 128
GC = 16
NS = 64
NCHUNK = 16
CH_STATE = 8 * NS
NE = 64
TOPK = 8
NEG = 8
PER_GRP = NE // NEG
TOPG = 4
DE = 512
ROUTED_SCALE = 2.5
DEPTH = 2
ALPHA = (2 * DEPTH) ** 0.25
LN_EPS = 1e-5

V7X_VMEM_BYTES = 64 * 1024 * 1024
VMEM_LIMIT = 56 * 1024 * 1024
LANES = 128

TM = 384
N_ROW_TILES = T_ALL // TM
TAIL_START = (N_ROW_TILES - 1) * TM
TM_E = 256
N_PAIRS = T_ALL * TOPK
NT_E = -(-N_PAIRS // TM_E) + NE
R_ROWS = NT_E * TM_E
TOK_BITS = 14
TOK_MASK = (1 << TOK_BITS) - 1
DUMMY_BASE = TOPK * T_ALL
YK_ROWS = DUMMY_BASE + TM_E
assert T_ALL <= TOK_MASK and T_ALL % TOPK == 0 and T_ALL % TM == 0 and TAIL_START <= T_PROMPT


def _cparams(sem):
    return pltpu.CompilerParams(dimension_semantics=sem, vmem_limit_bytes=VMEM_LIMIT)


def _sigmoid(x):
    return 1.0 / (1.0 + jnp.exp(-x))


def _layer_norm(y, g, b):
    mu = jnp.mean(y, axis=-1, keepdims=True)
    yc = y - mu
    var = jnp.mean(yc * yc, axis=-1, keepdims=True)
    return yc * lax.rsqrt(var + LN_EPS) * g + b


def _pack_pair(lo, hi):
    lo_b = lax.bitcast_convert_type(lo.astype(BF16).astype(F32), U32) >> 16
    hi_b = lax.bitcast_convert_type(hi.astype(BF16).astype(F32), U32) & jnp.uint32(0xFFFF0000)
    return lo_b | hi_b


def _unpack_pair(w):
    lo = lax.bitcast_convert_type(w << 16, F32)
    hi = lax.bitcast_convert_type(w & jnp.uint32(0xFFFF0000), F32)
    return lo, hi


def _qkv_kernel(x_ref, w_ref, b_ref, c_ref, s1_ref, s2_ref, q_ref, k_ref, v_ref):
    xb = x_ref[...].astype(BF16)
    acc = jnp.dot(xb, w_ref[...], preferred_element_type=F32) + b_ref[...]
    c = c_ref[...]
    s1 = s1_ref[...]
    s2 = s2_ref[...]
    for j in range(QK_COLS // LANES):
        blk = acc[:, j * LANES:(j + 1) * LANES]
        r = (blk * c + pltpu.roll(blk, LANES - ROT_HALF, axis=1) * s1
             + pltpu.roll(blk, ROT_HALF, axis=1) * s2)
        if j < D // LANES:
            q_ref[:, j * LANES:(j + 1) * LANES] = (r * (1.0 / math.sqrt(HD))).astype(BF16)
        else:
            jj = j - D // LANES
            k_ref[:, jj * LANES:(jj + 1) * LANES] = r
    v_ref[...] = acc[:, QK_COLS:]


def _qkv_call(x, w_bf, b, rc, rs1, rs2):
    return pl.pallas_call(
        _qkv_kernel,
        grid=(T_ALL // TM,),
        in_specs=[
            pl.BlockSpec((TM, D), lambda i: (i, 0)),
            pl.BlockSpec((D, QKV), lambda i: (0, 0)),
            pl.BlockSpec((1, QKV), lambda i: (0, 0)),
            pl.BlockSpec((TM, LANES), lambda i: (i, 0)),
            pl.BlockSpec((TM, LANES), lambda i: (i, 0)),
            pl.BlockSpec((TM, LANES), lambda i: (i, 0)),
        ],
        out_specs=[
            pl.BlockSpec((TM, D), lambda i: (i, 0)),
            pl.BlockSpec((TM, NKV * HD), lambda i: (i, 0)),
            pl.BlockSpec((TM, NKV * HD), lambda i: (i, 0)),
        ],
        out_shape=[
            jax.ShapeDtypeStruct((T_ALL, D), BF16),
            jax.ShapeDtypeStruct((T_ALL, NKV * HD), F32),
            jax.ShapeDtypeStruct((T_ALL, NKV * HD), F32),
        ],
        compiler_params=_cparams(("parallel",)),
        name="qkv_rope",
    )(x, w_bf, b, rc, rs1, rs2)


def _rope_tables():
    pos_p = jnp.maximum(jnp.arange(LP, dtype=I32) - PAD_FRONT, 0)
    pos = jnp.concatenate([jnp.tile(pos_p, NB), jnp.full((DEC,), PAST_LEN, I32)]).astype(F32)
    inv_freq = ROPE_THETA ** (-jnp.arange(0, ROT, 2, dtype=F32) / ROT)
    ang = pos[:, None] * inv_freq[None, :]
    cos = jnp.cos(ang)
    sin = jnp.sin(ang)
    lane = np.arange(LANES) % HD
    freq = np.arange(ROT_HALF)[:, None]
    first = ((lane[None, :] == freq) & (lane[None, :] < ROT_HALF)).astype(np.float32)
    second = ((lane[None, :] - ROT_HALF == freq) & (lane[None, :] < ROT)).astype(np.float32)
    rest = (lane >= ROT).astype(np.float32)[None, :]
    place = functools.partial(jnp.dot, precision=lax.Precision.HIGHEST)
    c = place(cos, jnp.asarray(first + second)) + jnp.asarray(rest)
    s1 = place(sin, jnp.asarray(-first))
    s2 = place(sin, jnp.asarray(second))
    return c, s1, s2


def _attn_p_kernel(sink_ref, q_ref, kp_ref, kc_ref, vp_ref, vc_ref, o_ref):
    j = pl.program_id(0) % NBLK
    keys = jnp.concatenate([kp_ref[...], kc_ref[...]], axis=0).astype(BF16)
    vals = jnp.concatenate([vp_ref[...], vc_ref[...]], axis=0).astype(BF16)
    r = lax.broadcasted_iota(I32, (BLK, 2 * BLK), 0)
    c = lax.broadcasted_iota(I32, (BLK, 2 * BLK), 1)
    dist = BLK + r - c
    kpos = (j - 1) * BLK - PAD_FRONT + c
    mask = (dist >= 0) & (dist <= BLK) & (kpos >= 0)
    for g in range(NKV):
        kg = keys[:, g * HD:(g + 1) * HD]
        vg = vals[:, g * HD:(g + 1) * HD]
        for hh in range(QPK):
            h = g * QPK + hh
            qh = q_ref[:, h * HD:(h + 1) * HD]
            s = lax.dot_general(qh, kg, (((1,), (1,)), ((), ())), preferred_element_type=F32)
            s = jnp.where(mask, s, -jnp.inf)
            sk = sink_ref[h]
            m = jnp.maximum(jnp.max(s, axis=1, keepdims=True), sk)
            p = jnp.exp(s - m)
            den = jnp.sum(p, axis=1, keepdims=True) + jnp.exp(sk - m)
            oh = jnp.dot(p.astype(BF16), vg, preferred_element_type=F32) / den
            o_ref[:, h * HD:(h + 1) * HD] = oh.astype(BF16)


def _attn_p_call(sinks, q, k, v):
    prev = lambda i: (jnp.where(i % NBLK == 0, i, i - 1), 0)
    cur = lambda i: (i, 0)
    return pl.pallas_call(
        _attn_p_kernel,
        grid=(NB * NBLK,),
        in_specs=[
            pl.BlockSpec(memory_space=pltpu.SMEM),
            pl.BlockSpec((BLK, D), cur),
            pl.BlockSpec((BLK, NKV * HD), prev),
            pl.BlockSpec((BLK, NKV * HD), cur),
            pl.BlockSpec((BLK, NKV * HD), prev),
            pl.BlockSpec((BLK, NKV * HD), cur),
        ],
        out_specs=pl.BlockSpec((BLK, D), cur),
        out_shape=jax.ShapeDtypeStruct((T_PROMPT, D), BF16),
        compiler_params=_cparams(("parallel",)),
        name="attn_prompt",
    )(sinks, q, k, k, v, v)


SEQ_PER_STEP = 16
SEQ_UNROLL = 1


def _attn_s_kernel(sink_ref, q_ref, kn_ref, vn_ref, ck_ref, cv_ref, o_ref, cko_ref, cvo_ref):
    row = lax.broadcasted_iota(I32, (BLK, NKV * HD), 0)

    def one_seq(s):
        kc = ck_ref[s]
        vc = cv_ref[s]
        kn = kn_ref[pl.ds(s, 1), :]
        vn = vn_ref[pl.ds(s, 1), :]
        kb = kc.astype(BF16)
        vb = vc.astype(BF16)
        knr = kn.astype(BF16).astype(F32)
        vnr = vn.astype(BF16).astype(F32)
        for g in range(NKV):
            qg = q_ref[s, g * QPK:(g + 1) * QPK, :].astype(BF16)
            sc = lax.dot_general(qg, kb[:, g * HD:(g + 1) * HD], (((1,), (1,)), ((), ())),
                                 preferred_element_type=F32)
            sn = jnp.sum(qg.astype(F32) * knr[:, g * HD:(g + 1) * HD], axis=1, keepdims=True)
            sk = sink_ref[g * QPK:(g + 1) * QPK, :]
            m = jnp.maximum(jnp.maximum(jnp.max(sc, axis=1, keepdims=True), sn), sk)
            p = jnp.exp(sc - m)
            pn = jnp.exp(sn - m)
            den = jnp.sum(p, axis=1, keepdims=True) + pn + jnp.exp(sk - m)
            og = jnp.dot(p.astype(BF16), vb[:, g * HD:(g + 1) * HD], preferred_element_type=F32)
            og = og + pn.astype(BF16).astype(F32) * vnr[:, g * HD:(g + 1) * HD]
            o_ref[s, g * QPK:(g + 1) * QPK, :] = og / den
        cko_ref[s] = jnp.where(row == BLK - 1, kn, pltpu.roll(kc, BLK - 1, axis=0))
        cvo_ref[s] = jnp.where(row == BLK - 1, vn, pltpu.roll(vc, BLK - 1, axis=0))

    def body(it, carry):
        for u in range(SEQ_UNROLL):
            one_seq(it * SEQ_UNROLL + u)
        return carry

    lax.fori_loop(0, SEQ_PER_STEP // SEQ_UNROLL, body, 0)


def _attn_s_call(sinks_col, q3, k, v, cache_k, cache_v):
    sp = SEQ_PER_STEP
    kv_off = T_PROMPT // sp
    return pl.pallas_call(
        _attn_s_kernel,
        grid=(DEC // sp,),
        in_specs=[
            pl.BlockSpec((NH, 1), lambda i: (0, 0)),
            pl.BlockSpec((sp, NH, HD), lambda i: (i, 0, 0)),
            pl.BlockSpec((sp, NKV * HD), lambda i: (kv_off + i, 0)),
            pl.BlockSpec((sp, NKV * HD), lambda i: (kv_off + i, 0)),
            pl.BlockSpec((sp, BLK, NKV * HD), lambda i: (i, 0, 0)),
            pl.BlockSpec((sp, BLK, NKV * HD), lambda i: (i, 0, 0)),
        ],
        out_specs=[
            pl.BlockSpec((sp, NH, HD), lambda i: (i, 0, 0)),
            pl.BlockSpec((sp, BLK, NKV * HD), lambda i: (i, 0, 0)),
            pl.BlockSpec((sp, BLK, NKV * HD), lambda i: (i, 0, 0)),
        ],
        out_shape=[
            jax.ShapeDtypeStruct((DEC, NH, HD), F32),
            jax.ShapeDtypeStruct((DEC, BLK, NKV * HD), F32),
            jax.ShapeDtypeStruct((DEC, BLK, NKV * HD), F32),
        ],
        compiler_params=_cparams(("parallel",)),
        name="attn_sample",
    )(sinks_col, q3, k, v, cache_k, cache_v)


def _prompt_or_tail(x_ref, tail_ref):
    return jnp.where(pl.program_id(0) == N_ROW_TILES - 1, tail_ref[...], x_ref[...])


def _tail_tile(x_prompt_rows, x_sample_rows):
    return jnp.concatenate([x_prompt_rows[TAIL_START:], x_sample_rows], axis=0)


_PROMPT_TILE = lambda i: (jnp.minimum(i, N_ROW_TILES - 2), 0)


def _oproj_ln_kernel(o_ref, ot_ref, w_ref, bo_ref, h_ref, g_ref, b_ref, out_ref, hp_ref):
    o = _prompt_or_tail(o_ref, ot_ref)
    m = jnp.dot(o, w_ref[...], preferred_element_type=F32) + bo_ref[...]
    out = _layer_norm(ALPHA * h_ref[...] + m, g_ref[...], b_ref[...])
    out_ref[...] = out
    hp_ref[...] = _pack_pair(out[:, :HALF], out[:, HALF:])


def _oproj_ln_call(o_p, o_tail, w_bf, bo, h, g, b):
    vec = pl.BlockSpec((1, D), lambda i: (0, 0))
    return pl.pallas_call(
        _oproj_ln_kernel,
        grid=(N_ROW_TILES,),
        in_specs=[
            pl.BlockSpec((TM, D), _PROMPT_TILE),
            pl.BlockSpec((TM, D), lambda i: (0, 0)),
            pl.BlockSpec((D, D), lambda i: (0, 0)),
            vec,
            pl.BlockSpec((TM, D), lambda i: (i, 0)),
            vec,
            vec,
        ],
        out_specs=[pl.BlockSpec((TM, D), lambda i: (i, 0)), pl.BlockSpec((TM, HALF), lambda i: (i, 0))],
        out_shape=[jax.ShapeDtypeStruct((T_ALL, D), F32), jax.ShapeDtypeStruct((T_ALL, HALF), U32)],
        compiler_params=_cparams(("parallel",)),
        name="oproj_ln",
    )(o_p, o_tail, w_bf, bo, h, g, b)


def _router_kernel(h_ref, wr_ref, rb_ref, eidx_ref, wsel_ref, rank_ref, cnt_ref, carry_ref):
    i = pl.program_id(0)

    @pl.when(i == 0)
    def _():
        carry_ref[...] = jnp.zeros_like(carry_ref)

    logits = jnp.dot(h_ref[...], wr_ref[...], preferred_element_type=F32,
                     precision=lax.Precision.HIGHEST)
    scores = _sigmoid(logits)
    biased = scores + rb_ref[...]
    lane_i = lax.broadcasted_iota(I32, (TM, NE), 1)
    lane = lane_i.astype(F32)
    grp = (lane_i // PER_GRP).astype(F32)
    ninf = -jnp.inf
    big = float(NE)

    gs = jnp.zeros((TM, NE), F32)
    for gi in range(NEG):
        seg = jnp.where(grp == gi, biased, ninf)
        m1 = jnp.max(seg, axis=1, keepdims=True)
        i1 = jnp.min(jnp.where(seg == m1, lane, big), axis=1, keepdims=True)
        m2 = jnp.max(jnp.where(lane == i1, ninf, seg), axis=1, keepdims=True)
        gs = jnp.where(grp == gi, m1 + m2, gs)

    emask = jnp.zeros((TM, NE), F32)
    for _ in range(TOPG):
        m = jnp.max(gs, axis=1, keepdims=True)
        gsel = jnp.min(jnp.where(gs == m, grp, big), axis=1, keepdims=True)
        hit = grp == gsel
        emask = jnp.where(hit, 1.0, emask)
        gs = jnp.where(hit, ninf, gs)

    work = jnp.where(emask > 0.5, biased, ninf)
    idx_cols = []
    w_cols = []
    onehot = jnp.zeros((TM, NE), F32)
    for _ in range(TOPK):
        m = jnp.max(work, axis=1, keepdims=True)
        ik = jnp.min(jnp.where(work == m, lane, big), axis=1, keepdims=True)
        hit = lane == ik
        idx_cols.append(ik)
        w_cols.append(jnp.sum(jnp.where(hit, scores, 0.0), axis=1, keepdims=True))
        onehot = jnp.where(hit, 1.0, onehot)
        work = jnp.where(hit, ninf, work)
    wsum = w_cols[0]
    for wk in w_cols[1:]:
        wsum = wsum + wk

    rr = lax.broadcasted_iota(I32, (TM, TM), 0)
    cc = lax.broadcasted_iota(I32, (TM, TM), 1)
    tri = jnp.where(cc < rr, 1.0, 0.0).astype(BF16)
    prefix = jnp.dot(tri, onehot.astype(BF16), preferred_element_type=F32) + carry_ref[...]
    carry_ref[...] = carry_ref[...] + jnp.sum(onehot, axis=0, keepdims=True)
    cnt_ref[...] = carry_ref[...]

    lane8 = lax.broadcasted_iota(I32, (TM, TOPK), 1)
    lanew = lax.broadcasted_iota(I32, (TM, LANES), 1)
    eidx = jnp.zeros((TM, LANES), F32)
    rank = jnp.zeros((TM, LANES), F32)
    wsel = jnp.zeros((TM, TOPK), F32)
    for k in range(TOPK):
        rk = jnp.sum(jnp.where(lane == idx_cols[k], prefix, 0.0), axis=1, keepdims=True)
        eidx = jnp.where(lanew == k, idx_cols[k], eidx)
        rank = jnp.where(lanew == k, rk, rank)
        wsel = jnp.where(lane8 == k, w_cols[k] / wsum * ROUTED_SCALE, wsel)
    wsel_ref[...] = wsel
    eidx_ref[...] = eidx.T[0:TOPK, :].astype(I32)
    rank_ref[...] = rank.T[0:TOPK, :].astype(I32)


def _router_call(h, w_router, router_bias):
    tk = pl.BlockSpec((TM, TOPK), lambda i: (i, 0))
    kt = pl.BlockSpec((TOPK, TM), lambda i: (0, i))
    return pl.pallas_call(
        _router_kernel,
        grid=(T_ALL // TM,),
        in_specs=[
            pl.BlockSpec((TM, D), lambda i: (i, 0)),
            pl.BlockSpec((D, NE), lambda i: (0, 0)),
            pl.BlockSpec((1, NE), lambda i: (0, 0)),
        ],
        out_specs=[kt, tk, kt, pl.BlockSpec((1, NE), lambda i: (0, 0))],
        out_shape=[
            jax.ShapeDtypeStruct((TOPK, T_ALL), I32),
            jax.ShapeDtypeStruct((T_ALL, TOPK), F32),
            jax.ShapeDtypeStruct((TOPK, T_ALL), I32),
            jax.ShapeDtypeStruct((1, NE), F32),
        ],
        scratch_shapes=[pltpu.VMEM((1, NE), F32)],
        compiler_params=_cparams(("arbitrary",)),
        name="router",
    )(h, w_router, router_bias)


def _tiles_of(cnt):
    return (cnt + TM_E - 1) // TM_E


def _plan_kernel(cnt_ref, eidx_ref, rank_ref, pos_ref, te_ref, gi_ref, ne_ref, na_ref, off_s):
    def offsets(e, run):
        off_s[e] = run
        return run + _tiles_of(cnt_ref[e]) * TM_E

    total = lax.fori_loop(0, NE, offsets, 0)
    na = total // TM_E
    na_ref[0] = na

    def idle(t, carry):
        te_ref[t] = NE - 1
        gi_ref[t] = 0
        ne_ref[t] = -1
        return carry

    lax.fori_loop(na, NT_E, idle, 0)

    def forward(e, ordinal):
        t0 = off_s[e] // TM_E
        nt = _tiles_of(cnt_ref[e])

        def mark(t, carry):
            te_ref[t] = e
            gi_ref[t] = ordinal
            return carry

        lax.fori_loop(t0, t0 + nt, mark, 0)
        return ordinal + jnp.where(nt > 0, 1, 0)

    lax.fori_loop(0, NE, forward, 0)

    def backward(i, nxt):
        e = NE - 1 - i
        t0 = off_s[e] // TM_E
        nt = _tiles_of(cnt_ref[e])

        def mark(t, carry):
            ne_ref[t] = nxt
            return carry

        lax.fori_loop(t0, t0 + nt, mark, 0)
        return jnp.where(nt > 0, e, nxt)

    lax.fori_loop(0, NE, backward, -1)

    eidx = eidx_ref[...]
    pos = rank_ref[...]
    for e in range(NE):
        pos = pos + jnp.where(eidx == e, off_s[e], 0)
    pos_ref[...] = pos


def _plan_call(cnt, eidx_t, rank_t):
    smem = pl.BlockSpec(memory_space=pltpu.SMEM)
    vmem = pl.BlockSpec(memory_space=pltpu.VMEM)
    tiles = jax.ShapeDtypeStruct((NT_E,), I32)
    return pl.pallas_call(
        _plan_kernel,
        in_specs=[smem, vmem, vmem],
        out_specs=[vmem, smem, smem, smem, smem],
        out_shape=[jax.ShapeDtypeStruct((TOPK, T_ALL), I32), tiles, tiles, tiles,
                   jax.ShapeDtypeStruct((1,), I32)],
        scratch_shapes=[pltpu.SMEM((NE,), I32)],
        name="moe_plan",
    )(cnt, eidx_t, rank_t)


def _rowinfo_kernel(pos_ref, word_ref, pad_ref, info_ref, sem):
    @pl.when(pl.program_id(0) == 0)
    def _():
        cp = pltpu.make_async_copy(pad_ref, info_ref, sem)
        cp.start()
        cp.wait()

    def body(j, carry):
        for u in range(TOPK):
            jj = j * TOPK + u
            info_ref[pos_ref[0, 0, jj]] = word_ref[0, 0, jj]
        return carry

    lax.fori_loop(0, T_ALL // TOPK, body, 0)


def _rowinfo_words():
    tok = np.arange(T_ALL, dtype=np.int64)[None, :]
    slot = np.arange(TOPK, dtype=np.int64)[:, None]
    words = ((slot * T_ALL + tok) << TOK_BITS) | tok
    rows = np.arange(R_ROWS, dtype=np.int64)
    pads = (DUMMY_BASE + (rows & (TM_E - 1))) << TOK_BITS
    return (jnp.asarray(words.astype(np.int32).reshape(TOPK, 1, T_ALL)),
            jnp.asarray(pads.astype(np.int32)))


def _rowinfo_call(pos_t):
    words, pads = _rowinfo_words()
    blk = pl.BlockSpec((1, 1, T_ALL), lambda i: (i, 0, 0), memory_space=pltpu.SMEM)
    return pl.pallas_call(
        _rowinfo_kernel,
        grid=(TOPK,),
        in_specs=[blk, blk, pl.BlockSpec(memory_space=pl.ANY)],
        out_specs=pl.BlockSpec(memory_space=pltpu.SMEM),
        out_shape=jax.ShapeDtypeStruct((R_ROWS,), I32),
        scratch_shapes=[pltpu.SemaphoreType.DMA(())],
        compiler_params=pltpu.CompilerParams(dimension_semantics=("arbitrary",)),
        name="moe_rowinfo",
    )(pos_t.reshape(TOPK, 1, T_ALL), words, pads)


def _expert_kernel(layer, te_ref, gi_ref, ne_ref, na_ref, ginfo_ref, sinfo_ref, hp_ref,
                   wg_hbm, wu_hbm, wd_hbm, yk_ref,
                   xb0, xb1, yb0, yb1, wgf, wuf, wdf, wg_s, wu_s, wd_s, gsem, ssem, wsem, zsem):
    s = pl.program_id(0)
    na = na_ref[0]
    xb = (xb0, xb1)
    yb = (yb0, yb1)

    def weight_copies(e, slot):
        return (pltpu.make_async_copy(wg_hbm.at[layer, e], wgf.at[slot], wsem.at[0]),
                pltpu.make_async_copy(wu_hbm.at[layer, e], wuf.at[slot], wsem.at[1]),
                pltpu.make_async_copy(wd_hbm.at[layer, e], wdf.at[slot], wsem.at[2]))

    def gather_row(par, r):
        tok = ginfo_ref[0, 0, r] & TOK_MASK
        pltpu.make_async_copy(hp_ref.at[pl.ds(tok, 1)], xb[par].at[pl.ds(r, 1)], gsem.at[par]).start(
            priority=0)

    def scatter_row(par, r, thread=0):
        dst = sinfo_ref[0, 0, r] >> TOK_BITS
        pltpu.make_async_copy(yb[par].at[pl.ds(r, 1)], yk_ref.at[pl.ds(dst, 1)], ssem.at[par]).start(
            priority=thread)

    def compute(par, between):
        lo, hi = _unpack_pair(xb[1 - par][...])
        lo = lo.astype(BF16)
        hi = hi.astype(BF16)
        between(0)
        hg = (jnp.dot(lo, wg_s[0:HALF, :], preferred_element_type=F32)
              + jnp.dot(hi, wg_s[HALF:D, :], preferred_element_type=F32))
        between(1)
        hu = (jnp.dot(lo, wu_s[0:HALF, :], preferred_element_type=F32)
              + jnp.dot(hi, wu_s[HALF:D, :], preferred_element_type=F32))
        between(2)
        act = (hg * _sigmoid(hg) * hu).astype(BF16)
        y = jnp.dot(act, wd_s[...], preferred_element_type=F32)
        between(3)
        yb[1 - par][...] = _pack_pair(y[:, :HALF], y[:, HALF:])

    @pl.when(jnp.logical_and(s >= 1, s <= na))
    def _():
        pltpu.make_async_copy(hp_ref.at[pl.ds(0, TM_E)], xb0, gsem.at[(s + 1) % 2]).wait()

    @pl.when(jnp.logical_and(s >= 3, s <= na + 2))
    def _():
        pltpu.make_async_copy(yb0, yk_ref.at[pl.ds(0, TM_E)], ssem.at[(s + 1) % 2]).wait()

    @pl.when(s == 0)
    def _():
        for cp in weight_copies(te_ref[0], 0):
            cp.start(priority=1)
        yb0[...] = jnp.zeros_like(yb0)
        zc = pltpu.make_async_copy(yb0, yk_ref.at[pl.ds(DUMMY_BASE, TM_E)], zsem)
        zc.start()
        zc.wait()

    c = s - 1
    cc = jnp.clip(c, 0, na - 1)
    first = jnp.logical_or(c == 0, te_ref[cc] != te_ref[jnp.maximum(cc - 1, 0)])

    @pl.when(jnp.logical_and(jnp.logical_and(c >= 0, c < na), first))
    def _():
        slot = gi_ref[cc] % 2
        for cp in weight_copies(te_ref[cc], slot):
            cp.wait()
        wg_s[...] = wgf[slot].astype(BF16)
        wu_s[...] = wuf[slot].astype(BF16)
        wd_s[...] = wdf[slot].astype(BF16)
        nxt = ne_ref[cc]

        @pl.when(nxt >= 0)
        def _():
            for cp in weight_copies(nxt, 1 - slot):
                cp.start(priority=1)

    steady = jnp.logical_and(s >= 2, s < na)
    quarter = TM_E // 4
    for par in (0, 1):
        mine = (s % 2) == par

        @pl.when(jnp.logical_and(steady, mine))
        def _():
            def between(q):
                for r in range(q * quarter, (q + 1) * quarter):
                    gather_row(par, r)
                    scatter_row(par, r, thread=r % 2)

            compute(par, between)

        @pl.when(jnp.logical_and(jnp.logical_not(steady), mine))
        def _():
            @pl.when(s < na)
            def _():
                def g(r, carry):
                    gather_row(par, r)
                    return carry

                lax.fori_loop(0, TM_E, g, 0)

            @pl.when(jnp.logical_and(s >= 1, s <= na))
            def _():
                compute(par, lambda q: None)

            @pl.when(jnp.logical_and(s >= 2, s <= na + 1))
            def _():
                def sc(r, carry):
                    scatter_row(par, r)
                    return carry

                lax.fori_loop(0, TM_E, sc, 0)


def _expert_call(layer, te, gi, ne, nact, rowinfo, hp, w_gate, w_up, w_down):
    def gmap(s, te_r, gi_r, ne_r, na_r):
        return (jnp.minimum(s, na_r[0] - 1), 0, 0)

    def smap(s, te_r, gi_r, ne_r, na_r):
        return (jnp.clip(s - 2, 0, na_r[0] - 1), 0, 0)

    info = rowinfo.reshape(NT_E, 1, TM_E)
    anyspec = pl.BlockSpec(memory_space=pl.ANY)
    return pl.pallas_call(
        functools.partial(_expert_kernel, layer),
        grid_spec=pltpu.PrefetchScalarGridSpec(
            num_scalar_prefetch=4,
            grid=(NT_E + 3,),
            in_specs=[
                pl.BlockSpec((1, 1, TM_E), gmap, memory_space=pltpu.SMEM),
                pl.BlockSpec((1, 1, TM_E), smap, memory_space=pltpu.SMEM),
                anyspec, anyspec, anyspec, anyspec,
            ],
            out_specs=anyspec,
            scratch_shapes=[
                pltpu.VMEM((TM_E, HALF), U32),
                pltpu.VMEM((TM_E, HALF), U32),
                pltpu.VMEM((TM_E, HALF), U32),
                pltpu.VMEM((TM_E, HALF), U32),
                pltpu.VMEM((2, D, DE), F32),
                pltpu.VMEM((2, D, DE), F32),
                pltpu.VMEM((2, DE, D), F32),
                pltpu.VMEM((D, DE), BF16),
                pltpu.VMEM((D, DE), BF16),
                pltpu.VMEM((DE, D), BF16),
                pltpu.SemaphoreType.DMA((2,)),
                pltpu.SemaphoreType.DMA((2,)),
                pltpu.SemaphoreType.DMA((3,)),
                pltpu.SemaphoreType.DMA(()),
            ],
        ),
        out_shape=jax.ShapeDtypeStruct((YK_ROWS, HALF), U32),
        compiler_params=_cparams(("arbitrary",)),
        name="moe_experts",
    )(te, gi, ne, nact, info, info, hp, w_gate, w_up, w_down)


BLK_PER_TILE = TM // BLK


def _final_copies(step, obuf, yp_ref, ys_ref, sem):
    out = []
    slot = step % 2
    for m in range(BLK_PER_TILE):
        g = step * BLK_PER_TILE + m
        b = g // NBLK
        j = g % NBLK
        src = obuf.at[slot, pl.ds(m * BLK, BLK)]
        r0 = pl.multiple_of(jnp.maximum(b * SEQ + (j - 1) * BLK, 0), BLK)
        out.append((jnp.logical_and(g < NB * NBLK, j >= 1),
                    pltpu.make_async_copy(src, yp_ref.at[pl.ds(r0, BLK)], sem.at[slot])))
        out.append((g == NB * NBLK, pltpu.make_async_copy(src, ys_ref, sem.at[slot])))
    return out


def _combine_kernel(final, w_ref, h_ref, *rest):
    yk_refs = rest[:TOPK]
    wsg_ref, wsu_ref, wsd_ref, g_ref, b_ref = rest[TOPK:TOPK + 5]
    outs = rest[TOPK + 5:]
    i = pl.program_id(0)
    if final:
        yp_ref, ys_ref, obuf, osem = outs

        def wait_step(step):
            for cond, cp in _final_copies(step, obuf, yp_ref, ys_ref, osem):
                @pl.when(cond)
                def _():
                    cp.wait()

        @pl.when(i >= 2)
        def _():
            wait_step(i - 2)

    h = h_ref[...]
    hb = h.astype(BF16)
    sg = jnp.dot(hb, wsg_ref[...], preferred_element_type=F32)
    su = jnp.dot(hb, wsu_ref[...], preferred_element_type=F32)
    act = (sg * _sigmoid(sg) * su).astype(BF16)
    y = ALPHA * h + jnp.dot(act, wsd_ref[...], preferred_element_type=F32)
    ylo = y[:, :HALF]
    yhi = y[:, HALF:]
    w = w_ref[...]
    for k in range(TOPK):
        lo, hi = _unpack_pair(yk_refs[k][...])
        wk = w[:, k:k + 1]
        ylo = ylo + wk * lo
        yhi = yhi + wk * hi
    out = _layer_norm(jnp.concatenate([ylo, yhi], axis=1), g_ref[...], b_ref[...])
    if not final:
        outs[0][...] = out
        return
    obuf[i % 2] = out
    for cond, cp in _final_copies(i, obuf, yp_ref, ys_ref, osem):
        @pl.when(cond)
        def _():
            cp.start()

    @pl.when(i == N_ROW_TILES - 1)
    def _():
        wait_step(i - 1)
        wait_step(i)


def _combine_call(final, wsel, h, yk, wsg, wsu, wsd, g, b):
    vec = pl.BlockSpec((1, D), lambda i: (0, 0))
    once = dict(pipeline_mode=pl.Buffered(1))
    yk_specs = [pl.BlockSpec((TM, HALF), lambda i, k=k: (k * N_ROW_TILES + i, 0)) for k in range(TOPK)]
    if final:
        anyspec = pl.BlockSpec(memory_space=pl.ANY)
        out_specs = [anyspec, anyspec]
        out_shape = [jax.ShapeDtypeStruct((NB * SEQ, D), F32), jax.ShapeDtypeStruct((DEC, D), F32)]
        scratch = [pltpu.VMEM((2, TM, D), F32), pltpu.SemaphoreType.DMA((2,))]
    else:
        out_specs = pl.BlockSpec((TM, D), lambda i: (i, 0))
        out_shape = jax.ShapeDtypeStruct((T_ALL, D), F32)
        scratch = []
    return pl.pallas_call(
        functools.partial(_combine_kernel, final),
        grid=(N_ROW_TILES,),
        in_specs=[
            pl.BlockSpec((TM, TOPK), lambda i: (i, 0)),
            pl.BlockSpec((TM, D), lambda i: (i, 0)),
            *yk_specs,
            pl.BlockSpec((D, DE), lambda i: (0, 0), **once),
            pl.BlockSpec((D, DE), lambda i: (0, 0), **once),
            pl.BlockSpec((DE, D), lambda i: (0, 0), **once),
            vec,
            vec,
        ],
        out_specs=out_specs,
        out_shape=out_shape,
        scratch_shapes=scratch,
        compiler_params=_cparams(("arbitrary",)),
        name="moe_combine_final" if final else "moe_combine",
    )(wsel, h, *([yk] * TOPK), wsg, wsu, wsd, g, b)


def _moe_layer(layer, final, h, hp, w_router, router_bias, w_exp_gate, w_exp_up, w_exp_down,
               w_sh_gate, w_sh_up, w_sh_down, ln_g, ln_b):
    eidx_t, wsel, rank_t, counts = _router_call(h, w_router[layer], router_bias[layer][None, :])
    pos_t, te, gi, ne, nact = _plan_call(counts[0].astype(I32), eidx_t, rank_t)
    rowinfo = _rowinfo_call(pos_t)
    yk = _expert_call(layer, te, gi, ne, nact, rowinfo, hp, w_exp_gate, w_exp_up, w_exp_down)
    return _combine_call(
        final, wsel, h, yk,
        w_sh_gate[layer].astype(BF16), w_sh_up[layer].astype(BF16), w_sh_down[layer].astype(BF16),
        ln_g[layer][None, :], ln_b[layer][None, :])


def _ssm_prep_kernel(lr_ref, li_ref, ldt_ref, br_ref, bi_ref, abr_ref, abi_ref, bbr_ref, bbi_ref):
    lr = lr_ref[...]
    li = li_ref[...]
    dt = jnp.exp(ldt_ref[...])
    mag = jnp.exp(lr * dt)
    ab_re = mag * jnp.cos(li * dt)
    ab_im = mag * jnp.sin(li * dt)
    den = lr * lr + li * li
    nr = ab_re - 1.0
    ni = ab_im
    cr = (nr * lr + ni * li) / den
    ci = (ni * lr - nr * li) / den
    br = br_ref[...]
    bi = bi_ref[...]
    abr_ref[...] = ab_re
    abi_ref[...] = ab_im
    bbr_ref[...] = cr * br - ci * bi
    bbi_ref[...] = cr * bi + ci * br


def _ssm_prep_call(lam_re, lam_im, log_dt, b_re, b_im):
    wide = (NG, GC * NS)
    lr = jnp.tile(lam_re, (1, GC))
    li = jnp.tile(lam_im, (1, GC))
    ldt = jnp.broadcast_to(log_dt[:, None], wide)
    br = jnp.transpose(b_re, (0, 2, 1)).reshape(wide)
    bi = jnp.transpose(b_im, (0, 2, 1)).reshape(wide)
    sds = jax.ShapeDtypeStruct(wide, F32)
    return pl.pallas_call(
        _ssm_prep_kernel, out_shape=[sds, sds, sds, sds], name="ssm_prep",
        compiler_params=pltpu.CompilerParams(vmem_limit_bytes=VMEM_LIMIT),
    )(lr, li, ldt, br, bi)


def _cmul(ar, ai, xr, xi):
    return ar * xr - ai * xi, ar * xi + ai * xr


GROUPS_PER_ITER = 4


def _ssm_p_kernel(u_ref, wb_ref, wc_ref, a_ref, d_ref, z_ref, st_ref, s_scr):
    row = lax.broadcasted_iota(I32, (LP, LANES), 0)
    u = jnp.where(row >= PAD_FRONT, u_ref[...], 0.0)
    s_scr[...] = jnp.dot(u.astype(BF16), wb_ref[0], preferred_element_type=F32)

    a1r = a_ref[0, 0:1, :]
    a1i = a_ref[0, 1:2, :]
    pows = [(a1r, a1i)]
    for _ in range(7):
        pows.append(_cmul(a1r, a1i, *pows[-1]))
    apr = jnp.concatenate([p[0] for p in pows], axis=0)
    api = jnp.concatenate([p[1] for p in pows], axis=0)
    sub = lax.broadcasted_iota(I32, (8, CH_STATE), 0)

    def group_scan(xr, xi):
        for s in (1, 2, 4):
            ar, ai = pows[s - 1]
            sr = jnp.where(sub >= s, pltpu.roll(xr, s, axis=0), 0.0)
            si = jnp.where(sub >= s, pltpu.roll(xi, s, axis=0), 0.0)
            tr, ti = _cmul(ar, ai, sr, si)
            xr = xr + tr
            xi = xi + ti
        return xr, xi

    def body(it, carry):
        cr, ci = carry
        for gq in range(GROUPS_PER_ITER):
            r0 = pl.multiple_of((it * GROUPS_PER_ITER + gq) * 8, 8)
            xr, xi = group_scan(s_scr[pl.ds(r0, 8), 0:CH_STATE], s_scr[pl.ds(r0, 8), CH_STATE:])
            tr, ti = _cmul(apr, api, cr, ci)
            xr = xr + tr
            xi = xi + ti
            s_scr[pl.ds(r0, 8), 0:CH_STATE] = xr
            s_scr[pl.ds(r0, 8), CH_STATE:] = xi
            cr = xr[7:8, :]
            ci = xi[7:8, :]
        return cr, ci

    zero = jnp.zeros((1, CH_STATE), F32)
    cr, ci = lax.fori_loop(0, LP // (8 * GROUPS_PER_ITER), body, (zero, zero))
    st_ref[0, 0, :, 0:CH_STATE] = cr
    st_ref[0, 0, :, CH_STATE:] = ci

    y = jnp.dot(s_scr[...].astype(BF16), wc_ref[0], preferred_element_type=F32) + d_ref[0] * u
    z_ref[...] = jax.nn.gelu(y).astype(BF16)


def _ssm_p_call(h, wb_bf, wc_bf, a_tab, d_tab):
    return pl.pallas_call(
        _ssm_p_kernel,
        grid=(NB, NCHUNK),
        in_specs=[
            pl.BlockSpec((LP, LANES), lambda b, k: (b, k)),
            pl.BlockSpec((1, LANES, 2 * CH_STATE), lambda b, k: (k, 0, 0)),
            pl.BlockSpec((1, 2 * CH_STATE, LANES), lambda b, k: (k, 0, 0)),
            pl.BlockSpec((1, 2, CH_STATE), lambda b, k: (k, 0, 0)),
            pl.BlockSpec((1, 1, LANES), lambda b, k: (k, 0, 0)),
        ],
        out_specs=[
            pl.BlockSpec((LP, LANES), lambda b, k: (b, k)),
            pl.BlockSpec((1, 1, 1, 2 * CH_STATE), lambda b, k: (b, k, 0, 0)),
        ],
        out_shape=[
            jax.ShapeDtypeStruct((T_PROMPT, D), BF16),
            jax.ShapeDtypeStruct((NB, NCHUNK, 1, 2 * CH_STATE), F32),
        ],
        scratch_shapes=[pltpu.VMEM((LP, 2 * CH_STATE), F32)],
        compiler_params=_cparams(("parallel", "parallel")),
        name="ssm_prompt",
    )(h, wb_bf, wc_bf, a_tab, d_tab)


def _ssm_s_kernel(u_ref, sr_ref, si_ref, wb_ref, wc_ref, a_ref, d_ref, z_ref, nr_ref, ni_ref):
    u = u_ref[...]
    bu = jnp.dot(u, wb_ref[0], preferred_element_type=F32, precision=lax.Precision.HIGHEST)
    ar = a_ref[0, 0:1, :]
    ai = a_ref[0, 1:2, :]
    tr, ti = _cmul(ar, ai, sr_ref[...], si_ref[...])
    nr = tr + bu[:, 0:CH_STATE]
    ni = ti + bu[:, CH_STATE:]
    nr_ref[...] = nr
    ni_ref[...] = ni
    s = jnp.concatenate([nr, ni], axis=1).astype(BF16)
    y = jnp.dot(s, wc_ref[0], preferred_element_type=F32) + d_ref[0] * u
    z_ref[...] = jax.nn.gelu(y).astype(BF16)


def _ssm_s_call(h, s0r, s0i, wb_f32, wc_bf, a_tab, d_tab):
    st = pl.BlockSpec((DEC, CH_STATE), lambda k: (0, k))
    return pl.pallas_call(
        _ssm_s_kernel,
        grid=(NCHUNK,),
        in_specs=[
            pl.BlockSpec((DEC, LANES), lambda k: (T_PROMPT // DEC, k)),
            st,
            st,
            pl.BlockSpec((1, LANES, 2 * CH_STATE), lambda k: (k, 0, 0)),
            pl.BlockSpec((1, 2 * CH_STATE, LANES), lambda k: (k, 0, 0)),
            pl.BlockSpec((1, 2, CH_STATE), lambda k: (k, 0, 0)),
            pl.BlockSpec((1, 1, LANES), lambda k: (k, 0, 0)),
        ],
        out_specs=[pl.BlockSpec((DEC, LANES), lambda k: (0, k)), st, st],
        out_shape=[
            jax.ShapeDtypeStruct((DEC, D), BF16),
            jax.ShapeDtypeStruct((DEC, NG * NS), F32),
            jax.ShapeDtypeStruct((DEC, NG * NS), F32),
        ],
        compiler_params=_cparams(("parallel",)),
        name="ssm_sample",
    )(h, s0r, s0i, wb_f32, wc_bf, a_tab, d_tab)


def _glu_ln_kernel(z_ref, zt_ref, w_ref, bg_ref, h_ref, g_ref, b_ref, out_ref, hp_ref):
    z = _prompt_or_tail(z_ref, zt_ref)
    acc = jnp.dot(z, w_ref[...], preferred_element_type=F32) + bg_ref[...]
    m = acc[:, :D] * _sigmoid(acc[:, D:])
    out = _layer_norm(ALPHA * h_ref[...] + m, g_ref[...], b_ref[...])
    out_ref[...] = out
    hp_ref[...] = _pack_pair(out[:, :HALF], out[:, HALF:])


def _glu_ln_call(z_p, z_tail, w_bf, bg, h, g, b):
    vec = pl.BlockSpec((1, D), lambda i: (0, 0))
    return pl.pallas_call(
        _glu_ln_kernel,
        grid=(N_ROW_TILES,),
        in_specs=[
            pl.BlockSpec((TM, D), _PROMPT_TILE),
            pl.BlockSpec((TM, D), lambda i: (0, 0)),
            pl.BlockSpec((D, 2 * D), lambda i: (0, 0), pipeline_mode=pl.Buffered(1)),
            pl.BlockSpec((1, 2 * D), lambda i: (0, 0)),
            pl.BlockSpec((TM, D), lambda i: (i, 0)),
            vec,
            vec,
        ],
        out_specs=[pl.BlockSpec((TM, D), lambda i: (i, 0)), pl.BlockSpec((TM, HALF), lambda i: (i, 0))],
        out_shape=[jax.ShapeDtypeStruct((T_ALL, D), F32), jax.ShapeDtypeStruct((T_ALL, HALF), U32)],
        compiler_params=_cparams(("parallel",)),
        name="glu_ln",
    )(z_p, z_tail, w_bf, bg, h, g, b)


def _block_diag_in(t):
    t4 = t.reshape(NCHUNK, 8, GC, NS)
    eye = jnp.eye(8, dtype=t.dtype)
    return jnp.einsum("kgcn,gh->kgchn", t4, eye).reshape(NCHUNK, LANES, CH_STATE)


def _block_diag_out(t):
    t4 = t.reshape(NCHUNK, 8, GC, NS)
    eye = jnp.eye(8, dtype=t.dtype)
    return jnp.einsum("kgcn,gh->kgnhc", t4, eye).reshape(NCHUNK, CH_STATE, LANES)


def kernel(x_prompt, x_sample, cache_k, cache_v, state_ssm_re, state_ssm_im, meta_tokens, w_qkv, b_qkv, attn_sinks, w_o, b_o, ssm_lam_re, ssm_lam_im, ssm_log_dt, ssm_b_re, ssm_b_im, ssm_c_re, ssm_c_im, ssm_d, w_glu, b_glu, ln_mix_g, ln_mix_b, w_router, router_bias, w_exp_gate, w_exp_up, w_exp_down, w_sh_gate, w_sh_up, w_sh_down, ln_ffn_g, ln_ffn_b):
    moe_w = (w_router, router_bias, w_exp_gate, w_exp_up, w_exp_down, w_sh_gate, w_sh_up, w_sh_down,
             ln_ffn_g, ln_ffn_b)

    front = jnp.concatenate([jnp.zeros((PAD_FRONT, D), F32), meta_tokens], axis=0)
    pieces = []
    for b in range(NB):
        pieces += [front, x_prompt[b]]
    h = jnp.concatenate(pieces + [x_sample.reshape(DEC, D)], axis=0)

    rc, rs1, rs2 = _rope_tables()
    q, k, v = _qkv_call(h, w_qkv[0].astype(BF16), b_qkv[0][None, :], rc, rs1, rs2)
    o_p = _attn_p_call(attn_sinks[0], q, k, v)
    q3 = q[T_PROMPT:].astype(F32).reshape(DEC, NH, HD)
    o_s, ck_new, cv_new = _attn_s_call(
        attn_sinks[0][:, None], q3, k, v,
        cache_k[0].reshape(DEC, BLK, NKV * HD), cache_v[0].reshape(DEC, BLK, NKV * HD))
    o_tail = _tail_tile(o_p, o_s.reshape(DEC, D).astype(BF16))
    h, hp = _oproj_ln_call(o_p, o_tail, w_o[0].astype(BF16), b_o[0][None, :], h,
                           ln_mix_g[0][None, :], ln_mix_b[0][None, :])
    h = _moe_layer(0, False, h, hp, *moe_w)

    kp = k[:T_PROMPT].reshape(NB, LP, NKV, HD)[:, LP - BLK:]
    vp = v[:T_PROMPT].reshape(NB, LP, NKV, HD)[:, LP - BLK:]

    ab_re, ab_im, bb_re, bb_im = _ssm_prep_call(
        ssm_lam_re[0], ssm_lam_im[0], ssm_log_dt[0], ssm_b_re[0], ssm_b_im[0])
    wb = jnp.concatenate([_block_diag_in(bb_re), _block_diag_in(bb_im)], axis=2)
    wc = jnp.concatenate([_block_diag_out(ssm_c_re[0]), -_block_diag_out(ssm_c_im[0])], axis=1)
    wc_bf = wc.astype(BF16)
    a_tab = jnp.stack([ab_re[:, :NS].reshape(NCHUNK, CH_STATE),
                       ab_im[:, :NS].reshape(NCHUNK, CH_STATE)], axis=1)
    d_tab = ssm_d[0].reshape(NCHUNK, 1, LANES)
    z_p, st_p = _ssm_p_call(h, wb.astype(BF16), wc_bf, a_tab, d_tab)
    z_s, sr_new, si_new = _ssm_s_call(
        h, state_ssm_re[0].reshape(DEC, NG * NS), state_ssm_im[0].reshape(DEC, NG * NS),
        wb, wc_bf, a_tab, d_tab)
    h, hp = _glu_ln_call(z_p, _tail_tile(z_p, z_s), w_glu[0].astype(BF16), b_glu[0][None, :], h,
                         ln_mix_g[1][None, :], ln_mix_b[1][None, :])
    y_prompt, y_sample = _moe_layer(1, True, h, hp, *moe_w)
    y_prompt = y_prompt.reshape(NB, SEQ, D)
    y_sample = y_sample.reshape(DEC, 1, D)
    st_p = st_p.reshape(NB, NCHUNK, 2, 8, NS)
    rp = st_p[:, :, 0].reshape(NB, NG, NS)
    ip = st_p[:, :, 1].reshape(NB, NG, NS)
    return (y_prompt, y_sample,
            kp[None], vp[None],
            ck_new.reshape(1, DEC, BLK, NKV, HD), cv_new.reshape(1, DEC, BLK, NKV, HD),
            rp[None], ip[None],
            sr_new.reshape(1, DEC, NG, NS), si_new.reshape(1, DEC, NG, NS))
```

```python
import functools
import math

import jax
import jax.numpy as jnp
import numpy as np
from jax import lax
from jax.experimental import pallas as pl
from jax.experimental.pallas import tpu as pltpu

F32 = jnp.float32
BF16 = jnp.bfloat16
I32 = jnp.int32
U32 = jnp.uint32

D = 2048
HALF = D // 2
NB = 4
N_META = 16
SEQ = 2048
L = N_META + SEQ
BLK = 128
PAD_FRONT = (-L) % BLK
LP = L + PAD_FRONT
NBLK = LP // BLK
T_PROMPT = NB * LP
DEC = 128
T_ALL = T_PROMPT + DEC
PAST_LEN = 8192
HD = 64
NH = 32
NKV = 4
QPK = NH // NKV
QKV = (NH + 2 * NKV) * HD
QK_COLS = (NH + NKV) * HD
ROT = HD // 4
ROT_HALF = ROT // 2
ROPE_THETA = 500000.0
NG = 128
GC = 16
NS = 64
NCHUNK = 16
CH_STATE = 8 * NS
NE = 64
TOPK = 8
NEG = 8
PER_GRP = NE // NEG
TOPG = 4
DE = 512
ROUTED_SCALE = 2.5
DEPTH = 2
ALPHA = (2 * DEPTH) ** 0.25
LN_EPS = 1e-5

V7X_VMEM_BYTES = 64 * 1024 * 1024
VMEM_LIMIT = 56 * 1024 * 1024
LANES = 128

TM = 384
N_ROW_TILES = T_ALL // TM
TAIL_START = (N_ROW_TILES - 1) * TM
TM_E = 256
N_PAIRS = T_ALL * TOPK
NT_E = -(-N_PAIRS // TM_E) + NE
R_ROWS = NT_E * TM_E
TOK_BITS = 14
TOK_MASK = (1 << TOK_BITS) - 1
DUMMY_BASE = TOPK * T_ALL
YK_ROWS = DUMMY_BASE + TM_E
assert T_ALL <= TOK_MASK and T_ALL % TOPK == 0 and T_ALL % TM == 0 and TAIL_START <= T_PROMPT


def _cparams(sem):
    return pltpu.CompilerParams(dimension_semantics=sem, vmem_limit_bytes=VMEM_LIMIT)


def _sigmoid(x):
    return 1.0 / (1.0 + jnp.exp(-x))


def _layer_norm(y, g, b):
    mu = jnp.mean(y, axis=-1, keepdims=True)
    yc = y - mu
    var = jnp.mean(yc * yc, axis=-1, keepdims=True)
    return yc * lax.rsqrt(var + LN_EPS) * g + b


def _pack_pair(lo, hi):
    lo_b = lax.bitcast_convert_type(lo.astype(BF16).astype(F32), U32) >> 16
    hi_b = lax.bitcast_convert_type(hi.astype(BF16).astype(F32), U32) & jnp.uint32(0xFFFF0000)
    return lo_b | hi_b


def _unpack_pair(w):
    lo = lax.bitcast_convert_type(w << 16, F32)
    hi = lax.bitcast_convert_type(w & jnp.uint32(0xFFFF0000), F32)
    return lo, hi


def _qkv_kernel(x_ref, w_ref, b_ref, c_ref, s1_ref, s2_ref, q_ref, k_ref, v_ref):
    xb = x_ref[...].astype(BF16)
    acc = jnp.dot(xb, w_ref[...], preferred_element_type=F32) + b_ref[...]
    c = c_ref[...]
    s1 = s1_ref[...]
    s2 = s2_ref[...]
    for j in range(QK_COLS // LANES):
        blk = acc[:, j * LANES:(j + 1) * LANES]
        r = (blk * c + pltpu.roll(blk, LANES - ROT_HALF, axis=1) * s1
             + pltpu.roll(blk, ROT_HALF, axis=1) * s2)
        if j < D // LANES:
            q_ref[:, j * LANES:(j + 1) * LANES] = (r * (1.0 / math.sqrt(HD))).astype(BF16)
        else:
            jj = j - D // LANES
            k_ref[:, jj * LANES:(jj + 1) * LANES] = r
    v_ref[...] = acc[:, QK_COLS:]


def _qkv_call(x, w_bf, b, rc, rs1, rs2):
    return pl.pallas_call(
        _qkv_kernel,
        grid=(T_ALL // TM,),
        in_specs=[
            pl.BlockSpec((TM, D), lambda i: (i, 0)),
            pl.BlockSpec((D, QKV), lambda i: (0, 0)),
            pl.BlockSpec((1, QKV), lambda i: (0, 0)),
            pl.BlockSpec((TM, LANES), lambda i: (i, 0)),
            pl.BlockSpec((TM, LANES), lambda i: (i, 0)),
            pl.BlockSpec((TM, LANES), lambda i: (i, 0)),
        ],
        out_specs=[
            pl.BlockSpec((TM, D), lambda i: (i, 0)),
            pl.BlockSpec((TM, NKV * HD), lambda i: (i, 0)),
            pl.BlockSpec((TM, NKV * HD), lambda i: (i, 0)),
        ],
        out_shape=[
            jax.ShapeDtypeStruct((T_ALL, D), BF16),
            jax.ShapeDtypeStruct((T_ALL, NKV * HD), F32),
            jax.ShapeDtypeStruct((T_ALL, NKV * HD), F32),
        ],
        compiler_params=_cparams(("parallel",)),
        name="qkv_rope",
    )(x, w_bf, b, rc, rs1, rs2)


def _rope_tables():
    pos_p = jnp.maximum(jnp.arange(LP, dtype=I32) - PAD_FRONT, 0)
    pos = jnp.concatenate([jnp.tile(pos_p, NB), jnp.full((DEC,), PAST_LEN, I32)]).astype(F32)
    inv_freq = ROPE_THETA ** (-jnp.arange(0, ROT, 2, dtype=F32) / ROT)
    ang = pos[:, None] * inv_freq[None, :]
    cos = jnp.cos(ang)
    sin = jnp.sin(ang)
    lane = np.arange(LANES) % HD
    freq = np.arange(ROT_HALF)[:, None]
    first = ((lane[None, :] == freq) & (lane[None, :] < ROT_HALF)).astype(np.float32)
    second = ((lane[None, :] - ROT_HALF == freq) & (lane[None, :] < ROT)).astype(np.float32)
    rest = (lane >= ROT).astype(np.float32)[None, :]
    place = functools.partial(jnp.dot, precision=lax.Precision.HIGHEST)
    c = place(cos, jnp.asarray(first + second)) + jnp.asarray(rest)
    s1 = place(sin, jnp.asarray(-first))
    s2 = place(sin, jnp.asarray(second))
    return c, s1, s2


def _attn_p_kernel(sink_ref, q_ref, kp_ref, kc_ref, vp_ref, vc_ref, o_ref):
    j = pl.program_id(0) % NBLK
    keys = jnp.concatenate([kp_ref[...], kc_ref[...]], axis=0).astype(BF16)
    vals = jnp.concatenate([vp_ref[...], vc_ref[...]], axis=0).astype(BF16)
    r = lax.broadcasted_iota(I32, (BLK, 2 * BLK), 0)
    c = lax.broadcasted_iota(I32, (BLK, 2 * BLK), 1)
    dist = BLK + r - c
    kpos = (j - 1) * BLK - PAD_FRONT + c
    mask = (dist >= 0) & (dist <= BLK) & (kpos >= 0)
    for g in range(NKV):
        kg = keys[:, g * HD:(g + 1) * HD]
        vg = vals[:, g * HD:(g + 1) * HD]
        for hh in range(QPK):
            h = g * QPK + hh
            qh = q_ref[:, h * HD:(h + 1) * HD]
            s = lax.dot_general(qh, kg, (((1,), (1,)), ((), ())), preferred_element_type=F32)
            s = jnp.where(mask, s, -jnp.inf)
            sk = sink_ref[h]
            m = jnp.maximum(jnp.max(s, axis=1, keepdims=True), sk)
            p = jnp.exp(s - m)
            den = jnp.sum(p, axis=1, keepdims=True) + jnp.exp(sk - m)
            oh = jnp.dot(p.astype(BF16), vg, preferred_element_type=F32) / den
            o_ref[:, h * HD:(h + 1) * HD] = oh.astype(BF16)


def _attn_p_call(sinks, q, k, v):
    prev = lambda i: (jnp.where(i % NBLK == 0, i, i - 1), 0)
    cur = lambda i: (i, 0)
    return pl.pallas_call(
        _attn_p_kernel,
        grid=(NB * NBLK,),
        in_specs=[
            pl.BlockSpec(memory_space=pltpu.SMEM),
            pl.BlockSpec((BLK, D), cur),
            pl.BlockSpec((BLK, NKV * HD), prev),
            pl.BlockSpec((BLK, NKV * HD), cur),
            pl.BlockSpec((BLK, NKV * HD), prev),
            pl.BlockSpec((BLK, NKV * HD), cur),
        ],
        out_specs=pl.BlockSpec((BLK, D), cur),
        out_shape=jax.ShapeDtypeStruct((T_PROMPT, D), BF16),
        compiler_params=_cparams(("parallel",)),
        name="attn_prompt",
    )(sinks, q, k, k, v, v)


SEQ_PER_STEP = 16
SEQ_UNROLL = 1


def _attn_s_kernel(sink_ref, q_ref, kn_ref, vn_ref, ck_ref, cv_ref, o_ref, cko_ref, cvo_ref):
    row = lax.broadcasted_iota(I32, (BLK, NKV * HD), 0)
    hrow = lax.broadcasted_iota(I32, (NH, NKV * HD), 0) // QPK
    hlane = lax.broadcasted_iota(I32, (NH, NKV * HD), 1) // HD
    own = hrow == hlane
    sk = sink_ref[...]

    def one_seq(s):
        kc = ck_ref[s]
        vc = cv_ref[s]
        kn = kn_ref[pl.ds(s, 1), :]
        vn = vn_ref[pl.ds(s, 1), :]
        knr = kn.astype(BF16).astype(F32)
        vnr = vn.astype(BF16).astype(F32)
        q = q_ref[s].astype(BF16)
        qe = jnp.where(own, jnp.concatenate([q] * NKV, axis=1), jnp.zeros((), BF16))
        sc = lax.dot_general(qe, kc.astype(BF16), (((1,), (1,)), ((), ())),
                             preferred_element_type=F32)
        sn = jnp.sum(qe.astype(F32) * knr, axis=1, keepdims=True)
        m = jnp.maximum(jnp.maximum(jnp.max(sc, axis=1, keepdims=True), sn), sk)
        p = jnp.exp(sc - m)
        pn = jnp.exp(sn - m)
        den = jnp.sum(p, axis=1, keepdims=True) + pn + jnp.exp(sk - m)
        of = jnp.dot(p.astype(BF16), vc.astype(BF16), preferred_element_type=F32)
        of = jnp.where(own, of + pn.astype(BF16).astype(F32) * vnr, 0.0)
        og = of[:, 0:HD]
        for g in range(1, NKV):
            og = og + of[:, g * HD:(g + 1) * HD]
        o_ref[s] = og / den
        cko_ref[s] = jnp.where(row == BLK - 1, kn, pltpu.roll(kc, BLK - 1, axis=0))
        cvo_ref[s] = jnp.where(row == BLK - 1, vn, pltpu.roll(vc, BLK - 1, axis=0))

    def body(it, carry):
        for u in range(SEQ_UNROLL):
            one_seq(it * SEQ_UNROLL + u)
        return carry

    lax.fori_loop(0, SEQ_PER_STEP // SEQ_UNROLL, body, 0)


def _attn_s_call(sinks_col, q3, k, v, cache_k, cache_v):
    sp = SEQ_PER_STEP
    kv_off = T_PROMPT // sp
    return pl.pallas_call(
        _attn_s_kernel,
        grid=(DEC // sp,),
        in_specs=[
            pl.BlockSpec((NH, 1), lambda i: (0, 0)),
            pl.BlockSpec((sp, NH, HD), lambda i: (i, 0, 0)),
            pl.BlockSpec((sp, NKV * HD), lambda i: (kv_off + i, 0)),
            pl.BlockSpec((sp, NKV * HD), lambda i: (kv_off + i, 0)),
            pl.BlockSpec((sp, BLK, NKV * HD), lambda i: (i, 0, 0)),
            pl.BlockSpec((sp, BLK, NKV * HD), lambda i: (i, 0, 0)),
        ],
        out_specs=[
            pl.BlockSpec((sp, NH, HD), lambda i: (i, 0, 0)),
            pl.BlockSpec((sp, BLK, NKV * HD), lambda i: (i, 0, 0)),
            pl.BlockSpec((sp, BLK, NKV * HD), lambda i: (i, 0, 0)),
        ],
        out_shape=[
            jax.ShapeDtypeStruct((DEC, NH, HD), F32),
            jax.ShapeDtypeStruct((DEC, BLK, NKV * HD), F32),
            jax.ShapeDtypeStruct((DEC, BLK, NKV * HD), F32),
        ],
        compiler_params=_cparams(("parallel",)),
        name="attn_sample",
    )(sinks_col, q3, k, v, cache_k, cache_v)


def _prompt_or_tail(x_ref, tail_ref):
    return jnp.where(pl.program_id(0) == N_ROW_TILES - 1, tail_ref[...], x_ref[...])


def _tail_tile(x_prompt_rows, x_sample_rows):
    return jnp.concatenate([x_prompt_rows[TAIL_START:], x_sample_rows], axis=0)


_PROMPT_TILE = lambda i: (jnp.minimum(i, N_ROW_TILES - 2), 0)


def _oproj_ln_kernel(o_ref, ot_ref, w_ref, bo_ref, h_ref, g_ref, b_ref, out_ref, hp_ref):
    o = _prompt_or_tail(o_ref, ot_ref)
    m = jnp.dot(o, w_ref[...], preferred_element_type=F32) + bo_ref[...]
    out = _layer_norm(ALPHA * h_ref[...] + m, g_ref[...], b_ref[...])
    out_ref[...] = out
    hp_ref[...] = _pack_pair(out[:, :HALF], out[:, HALF:])


def _oproj_ln_call(o_p, o_tail, w_bf, bo, h, g, b):
    vec = pl.BlockSpec((1, D), lambda i: (0, 0))
    return pl.pallas_call(
        _oproj_ln_kernel,
        grid=(N_ROW_TILES,),
        in_specs=[
            pl.BlockSpec((TM, D), _PROMPT_TILE),
            pl.BlockSpec((TM, D), lambda i: (0, 0)),
            pl.BlockSpec((D, D), lambda i: (0, 0)),
            vec,
            pl.BlockSpec((TM, D), lambda i: (i, 0)),
            vec,
            vec,
        ],
        out_specs=[pl.BlockSpec((TM, D), lambda i: (i, 0)), pl.BlockSpec((TM, HALF), lambda i: (i, 0))],
        out_shape=[jax.ShapeDtypeStruct((T_ALL, D), F32), jax.ShapeDtypeStruct((T_ALL, HALF), U32)],
        compiler_params=_cparams(("parallel",)),
        name="oproj_ln",
    )(o_p, o_tail, w_bf, bo, h, g, b)


def _router_kernel(h_ref, wr_ref, rb_ref, eidx_ref, wsel_ref, rank_ref, cnt_ref, carry_ref):
    i = pl.program_id(0)

    @pl.when(i == 0)
    def _():
        carry_ref[...] = jnp.zeros_like(carry_ref)

    logits = jnp.dot(h_ref[...], wr_ref[...], preferred_element_type=F32,
                     precision=lax.Precision.HIGHEST)
    scores = _sigmoid(logits)
    biased = scores + rb_ref[...]
    lane_i = lax.broadcasted_iota(I32, (TM, NE), 1)
    lane = lane_i.astype(F32)
    grp = (lane_i // PER_GRP).astype(F32)
    ninf = -jnp.inf
    big = float(NE)

    gs = jnp.zeros((TM, NE), F32)
    for gi in range(NEG):
        seg = jnp.where(grp == gi, biased, ninf)
        m1 = jnp.max(seg, axis=1, keepdims=True)
        i1 = jnp.min(jnp.where(seg == m1, lane, big), axis=1, keepdims=True)
        m2 = jnp.max(jnp.where(lane == i1, ninf, seg), axis=1, keepdims=True)
        gs = jnp.where(grp == gi, m1 + m2, gs)

    emask = jnp.zeros((TM, NE), F32)
    for _ in range(TOPG):
        m = jnp.max(gs, axis=1, keepdims=True)
        gsel = jnp.min(jnp.where(gs == m, grp, big), axis=1, keepdims=True)
        hit = grp == gsel
        emask = jnp.where(hit, 1.0, emask)
        gs = jnp.where(hit, ninf, gs)

    work = jnp.where(emask > 0.5, biased, ninf)
    idx_cols = []
    w_cols = []
    onehot = jnp.zeros((TM, NE), F32)
    for _ in range(TOPK):
        m = jnp.max(work, axis=1, keepdims=True)
        ik = jnp.min(jnp.where(work == m, lane, big), axis=1, keepdims=True)
        hit = lane == ik
        idx_cols.append(ik)
        w_cols.append(jnp.sum(jnp.where(hit, scores, 0.0), axis=1, keepdims=True))
        onehot = jnp.where(hit, 1.0, onehot)
        work = jnp.where(hit, ninf, work)
    wsum = w_cols[0]
    for wk in w_cols[1:]:
        wsum = wsum + wk

    rr = lax.broadcasted_iota(I32, (TM, TM), 0)
    cc = lax.broadcasted_iota(I32, (TM, TM), 1)
    tri = jnp.where(cc < rr, 1.0, 0.0).astype(BF16)
    prefix = jnp.dot(tri, onehot.astype(BF16), preferred_element_type=F32) + carry_ref[...]
    carry_ref[...] = carry_ref[...] + jnp.sum(onehot, axis=0, keepdims=True)
    cnt_ref[...] = carry_ref[...]

    lane8 = lax.broadcasted_iota(I32, (TM, TOPK), 1)
    lanew = lax.broadcasted_iota(I32, (TM, LANES), 1)
    eidx = jnp.zeros((TM, LANES), F32)
    rank = jnp.zeros((TM, LANES), F32)
    wsel = jnp.zeros((TM, TOPK), F32)
    for k in range(TOPK):
        rk = jnp.sum(jnp.where(lane == idx_cols[k], prefix, 0.0), axis=1, keepdims=True)
        eidx = jnp.where(lanew == k, idx_cols[k], eidx)
        rank = jnp.where(lanew == k, rk, rank)
        wsel = jnp.where(lane8 == k, w_cols[k] / wsum * ROUTED_SCALE, wsel)
    wsel_ref[...] = wsel
    eidx_ref[...] = eidx.T[0:TOPK, :].astype(I32)
    rank_ref[...] = rank.T[0:TOPK, :].astype(I32)


def _router_call(h, w_router, router_bias):
    tk = pl.BlockSpec((TM, TOPK), lambda i: (i, 0))
    kt = pl.BlockSpec((TOPK, TM), lambda i: (0, i))
    return pl.pallas_call(
        _router_kernel,
        grid=(T_ALL // TM,),
        in_specs=[
            pl.BlockSpec((TM, D), lambda i: (i, 0)),
            pl.BlockSpec((D, NE), lambda i: (0, 0)),
            pl.BlockSpec((1, NE), lambda i: (0, 0)),
        ],
        out_specs=[kt, tk, kt, pl.BlockSpec((1, NE), lambda i: (0, 0))],
        out_shape=[
            jax.ShapeDtypeStruct((TOPK, T_ALL), I32),
            jax.ShapeDtypeStruct((T_ALL, TOPK), F32),
            jax.ShapeDtypeStruct((TOPK, T_ALL), I32),
            jax.ShapeDtypeStruct((1, NE), F32),
        ],
        scratch_shapes=[pltpu.VMEM((1, NE), F32)],
        compiler_params=_cparams(("arbitrary",)),
        name="router",
    )(h, w_router, router_bias)


def _tiles_of(cnt):
    return (cnt + TM_E - 1) // TM_E


def _plan_kernel(cnt_ref, eidx_ref, rank_ref, pos_ref, te_ref, gi_ref, ne_ref, na_ref, off_s):
    def offsets(e, run):
        off_s[e] = run
        return run + _tiles_of(cnt_ref[e]) * TM_E

    total = lax.fori_loop(0, NE, offsets, 0)
    na = total // TM_E
    na_ref[0] = na

    def idle(t, carry):
        te_ref[t] = NE - 1
        gi_ref[t] = 0
        ne_ref[t] = -1
        return carry

    lax.fori_loop(na, NT_E, idle, 0)

    def forward(e, ordinal):
        t0 = off_s[e] // TM_E
        nt = _tiles_of(cnt_ref[e])

        def mark(t, carry):
            te_ref[t] = e
            gi_ref[t] = ordinal
            return carry

        lax.fori_loop(t0, t0 + nt, mark, 0)
        return ordinal + jnp.where(nt > 0, 1, 0)

    lax.fori_loop(0, NE, forward, 0)

    def backward(i, nxt):
        e = NE - 1 - i
        t0 = off_s[e] // TM_E
        nt = _tiles_of(cnt_ref[e])

        def mark(t, carry):
            ne_ref[t] = nxt
            return carry

        lax.fori_loop(t0, t0 + nt, mark, 0)
        return jnp.where(nt > 0, e, nxt)

    lax.fori_loop(0, NE, backward, -1)

    eidx = eidx_ref[...]
    pos = rank_ref[...]
    for e in range(NE):
        pos = pos + jnp.where(eidx == e, off_s[e], 0)
    pos_ref[...] = pos


def _plan_call(cnt, eidx_t, rank_t):
    smem = pl.BlockSpec(memory_space=pltpu.SMEM)
    vmem = pl.BlockSpec(memory_space=pltpu.VMEM)
    tiles = jax.ShapeDtypeStruct((NT_E,), I32)
    return pl.pallas_call(
        _plan_kernel,
        in_specs=[smem, vmem, vmem],
        out_specs=[vmem, smem, smem, smem, smem],
        out_shape=[jax.ShapeDtypeStruct((TOPK, T_ALL), I32), tiles, tiles, tiles,
                   jax.ShapeDtypeStruct((1,), I32)],
        scratch_shapes=[pltpu.SMEM((NE,), I32)],
        name="moe_plan",
    )(cnt, eidx_t, rank_t)


def _rowinfo_kernel(pos_ref, word_ref, pad_ref, info_ref, sem):
    @pl.when(pl.program_id(0) == 0)
    def _():
        cp = pltpu.make_async_copy(pad_ref, info_ref, sem)
        cp.start()
        cp.wait()

    def body(j, carry):
        for u in range(TOPK):
            jj = j * TOPK + u
            info_ref[pos_ref[0, 0, jj]] = word_ref[0, 0, jj]
        return carry

    lax.fori_loop(0, T_ALL // TOPK, body, 0)


def _rowinfo_words():
    tok = np.arange(T_ALL, dtype=np.int64)[None, :]
    slot = np.arange(TOPK, dtype=np.int64)[:, None]
    words = ((slot * T_ALL + tok) << TOK_BITS) | tok
    rows = np.arange(R_ROWS, dtype=np.int64)
    pads = (DUMMY_BASE + (rows & (TM_E - 1))) << TOK_BITS
    return (jnp.asarray(words.astype(np.int32).reshape(TOPK, 1, T_ALL)),
            jnp.asarray(pads.astype(np.int32)))


def _rowinfo_call(pos_t):
    words, pads = _rowinfo_words()
    blk = pl.BlockSpec((1, 1, T_ALL), lambda i: (i, 0, 0), memory_space=pltpu.SMEM)
    return pl.pallas_call(
        _rowinfo_kernel,
        grid=(TOPK,),
        in_specs=[blk, blk, pl.BlockSpec(memory_space=pl.ANY)],
        out_specs=pl.BlockSpec(memory_space=pltpu.SMEM),
        out_shape=jax.ShapeDtypeStruct((R_ROWS,), I32),
        scratch_shapes=[pltpu.SemaphoreType.DMA(())],
        compiler_params=pltpu.CompilerParams(dimension_semantics=("arbitrary",)),
        name="moe_rowinfo",
    )(pos_t.reshape(TOPK, 1, T_ALL), words, pads)


def _expert_kernel(layer, te_ref, gi_ref, ne_ref, na_ref, ginfo_ref, sinfo_ref, hp_ref,
                   wg_hbm, wu_hbm, wd_hbm, yk_ref,
                   xb0, xb1, yb0, yb1, wgf, wuf, wdf, wg_s, wu_s, wd_s, gsem, ssem, wsem, zsem):
    s = pl.program_id(0)
    na = na_ref[0]
    xb = (xb0, xb1)
    yb = (yb0, yb1)

    def weight_copies(e, slot):
        return (pltpu.make_async_copy(wg_hbm.at[layer, e], wgf.at[slot], wsem.at[0]),
                pltpu.make_async_copy(wu_hbm.at[layer, e], wuf.at[slot], wsem.at[1]),
                pltpu.make_async_copy(wd_hbm.at[layer, e], wdf.at[slot], wsem.at[2]))

    def gather_row(par, r):
        tok = ginfo_ref[0, 0, r] & TOK_MASK
        pltpu.make_async_copy(hp_ref.at[pl.ds(tok, 1)], xb[par].at[pl.ds(r, 1)], gsem.at[par]).start(
            priority=0)

    def scatter_row(par, r, thread=0):
        dst = sinfo_ref[0, 0, r] >> TOK_BITS
        pltpu.make_async_copy(yb[par].at[pl.ds(r, 1)], yk_ref.at[pl.ds(dst, 1)], ssem.at[par]).start(
            priority=thread)

    def compute(par, between):
        lo, hi = _unpack_pair(xb[1 - par][...])
        lo = lo.astype(BF16)
        hi = hi.astype(BF16)
        between(0)
        hg = (jnp.dot(lo, wg_s[0:HALF, :], preferred_element_type=F32)
              + jnp.dot(hi, wg_s[HALF:D, :], preferred_element_type=F32))
        between(1)
        hu = (jnp.dot(lo, wu_s[0:HALF, :], preferred_element_type=F32)
              + jnp.dot(hi, wu_s[HALF:D, :], preferred_element_type=F32))
        between(2)
        act = (hg * _sigmoid(hg) * hu).astype(BF16)
        y = jnp.dot(act, wd_s[...], preferred_element_type=F32)
        between(3)
        yb[1 - par][...] = _pack_pair(y[:, :HALF], y[:, HALF:])

    @pl.when(jnp.logical_and(s >= 1, s <= na))
    def _():
        pltpu.make_async_copy(hp_ref.at[pl.ds(0, TM_E)], xb0, gsem.at[(s + 1) % 2]).wait()

    @pl.when(jnp.logical_and(s >= 3, s <= na + 2))
    def _():
        pltpu.make_async_copy(yb0, yk_ref.at[pl.ds(0, TM_E)], ssem.at[(s + 1) % 2]).wait()

    @pl.when(s == 0)
    def _():
        for cp in weight_copies(te_ref[0], 0):
            cp.start(priority=1)
        yb0[...] = jnp.zeros_like(yb0)
        zc = pltpu.make_async_copy(yb0, yk_ref.at[pl.ds(DUMMY_BASE, TM_E)], zsem)
        zc.start()
        zc.wait()

    c = s - 1
    cc = jnp.clip(c, 0, na - 1)
    first = jnp.logical_or(c == 0, te_ref[cc] != te_ref[jnp.maximum(cc - 1, 0)])

    @pl.when(jnp.logical_and(jnp.logical_and(c >= 0, c < na), first))
    def _():
        slot = gi_ref[cc] % 2
        for cp in weight_copies(te_ref[cc], slot):
            cp.wait()
        wg_s[...] = wgf[slot].astype(BF16)
        wu_s[...] = wuf[slot].astype(BF16)
        wd_s[...] = wdf[slot].astype(BF16)
        nxt = ne_ref[cc]

        @pl.when(nxt >= 0)
        def _():
            for cp in weight_copies(nxt, 1 - slot):
                cp.start(priority=1)

    steady = jnp.logical_and(s >= 2, s < na)
    for par in (0, 1):
        mine = (s % 2) == par

        @pl.when(jnp.logical_and(steady, mine))
        def _():
            def between(q):
                for r in range(q * (TM_E // 4), (q + 1) * (TM_E // 4)):
                    gather_row(par, r)
                    scatter_row(par, r, thread=r % 2)

            compute(par, between)

        @pl.when(jnp.logical_and(jnp.logical_not(steady), mine))
        def _():
            @pl.when(s < na)
            def _():
                def g(r, carry):
                    gather_row(par, r)
                    return carry

                lax.fori_loop(0, TM_E, g, 0)

            @pl.when(jnp.logical_and(s >= 1, s <= na))
            def _():
                compute(par, lambda q: None)

            @pl.when(jnp.logical_and(s >= 2, s <= na + 1))
            def _():
                def sc(r, carry):
                    scatter_row(par, r)
                    return carry

                lax.fori_loop(0, TM_E, sc, 0)


def _expert_call(layer, te, gi, ne, nact, rowinfo, hp, w_gate, w_up, w_down):
    def gmap(s, te_r, gi_r, ne_r, na_r):
        return (jnp.minimum(s, na_r[0] - 1), 0, 0)

    def smap(s, te_r, gi_r, ne_r, na_r):
        return (jnp.clip(s - 2, 0, na_r[0] - 1), 0, 0)

    info = rowinfo.reshape(NT_E, 1, TM_E)
    anyspec = pl.BlockSpec(memory_space=pl.ANY)
    return pl.pallas_call(
        functools.partial(_expert_kernel, layer),
        grid_spec=pltpu.PrefetchScalarGridSpec(
            num_scalar_prefetch=4,
            grid=(NT_E + 3,),
            in_specs=[
                pl.BlockSpec((1, 1, TM_E), gmap, memory_space=pltpu.SMEM),
                pl.BlockSpec((1, 1, TM_E), smap, memory_space=pltpu.SMEM),
                anyspec, anyspec, anyspec, anyspec,
            ],
            out_specs=anyspec,
            scratch_shapes=[
                pltpu.VMEM((TM_E, HALF), U32),
                pltpu.VMEM((TM_E, HALF), U32),
                pltpu.VMEM((TM_E, HALF), U32),
                pltpu.VMEM((TM_E, HALF), U32),
                pltpu.VMEM((2, D, DE), F32),
                pltpu.VMEM((2, D, DE), F32),
                pltpu.VMEM((2, DE, D), F32),
                pltpu.VMEM((D, DE), BF16),
                pltpu.VMEM((D, DE), BF16),
                pltpu.VMEM((DE, D), BF16),
                pltpu.SemaphoreType.DMA((2,)),
                pltpu.SemaphoreType.DMA((2,)),
                pltpu.SemaphoreType.DMA((3,)),
                pltpu.SemaphoreType.DMA(()),
            ],
        ),
        out_shape=jax.ShapeDtypeStruct((YK_ROWS, HALF), U32),
        compiler_params=_cparams(("arbitrary",)),
        name="moe_experts",
    )(te, gi, ne, nact, info, info, hp, w_gate, w_up, w_down)


BLK_PER_TILE = TM // BLK


def _final_copies(step, obuf, yp_ref, ys_ref, sem):
    out = []
    slot = step % 2
    for m in range(BLK_PER_TILE):
        g = step * BLK_PER_TILE + m
        b = g // NBLK
        j = g % NBLK
        src = obuf.at[slot, pl.ds(m * BLK, BLK)]
        r0 = pl.multiple_of(jnp.maximum(b * SEQ + (j - 1) * BLK, 0), BLK)
        out.append((jnp.logical_and(g < NB * NBLK, j >= 1),
                    pltpu.make_async_copy(src, yp_ref.at[pl.ds(r0, BLK)], sem.at[slot])))
        out.append((g == NB * NBLK, pltpu.make_async_copy(src, ys_ref, sem.at[slot])))
    return out


def _combine_kernel(final, w_ref, h_ref, *rest):
    yk_refs = rest[:TOPK]
    wsg_ref, wsu_ref, wsd_ref, g_ref, b_ref = rest[TOPK:TOPK + 5]
    outs = rest[TOPK + 5:]
    i = pl.program_id(0)
    if final:
        yp_ref, ys_ref, obuf, osem = outs

        def wait_step(step):
            for cond, cp in _final_copies(step, obuf, yp_ref, ys_ref, osem):
                @pl.when(cond)
                def _():
                    cp.wait()

        @pl.when(i >= 2)
        def _():
            wait_step(i - 2)

    h = h_ref[...]
    hb = h.astype(BF16)
    sg = jnp.dot(hb, wsg_ref[...], preferred_element_type=F32)
    su = jnp.dot(hb, wsu_ref[...], preferred_element_type=F32)
    act = (sg * _sigmoid(sg) * su).astype(BF16)
    y = ALPHA * h + jnp.dot(act, wsd_ref[...], preferred_element_type=F32)
    ylo = y[:, :HALF]
    yhi = y[:, HALF:]
    w = w_ref[...]
    for k in range(TOPK):
        lo, hi = _unpack_pair(yk_refs[k][...])
        wk = w[:, k:k + 1]
        ylo = ylo + wk * lo
        yhi = yhi + wk * hi
    out = _layer_norm(jnp.concatenate([ylo, yhi], axis=1), g_ref[...], b_ref[...])
    if not final:
        outs[0][...] = out
        return
    obuf[i % 2] = out
    for cond, cp in _final_copies(i, obuf, yp_ref, ys_ref, osem):
        @pl.when(cond)
        def _():
            cp.start()

    @pl.when(i == N_ROW_TILES - 1)
    def _():
        wait_step(i - 1)
        wait_step(i)


def _combine_call(final, wsel, h, yk, wsg, wsu, wsd, g, b):
    vec = pl.BlockSpec((1, D), lambda i: (0, 0))
    once = dict(pipeline_mode=pl.Buffered(1))
    yk_specs = [pl.BlockSpec((TM, HALF), lambda i, k=k: (k * N_ROW_TILES + i, 0)) for k in range(TOPK)]
    if final:
        anyspec = pl.BlockSpec(memory_space=pl.ANY)
        out_specs = [anyspec, anyspec]
        out_shape = [jax.ShapeDtypeStruct((NB * SEQ, D), F32), jax.ShapeDtypeStruct((DEC, D), F32)]
        scratch = [pltpu.VMEM((2, TM, D), F32), pltpu.SemaphoreType.DMA((2,))]
    else:
        out_specs = pl.BlockSpec((TM, D), lambda i: (i, 0))
        out_shape = jax.ShapeDtypeStruct((T_ALL, D), F32)
        scratch = []
    return pl.pallas_call(
        functools.partial(_combine_kernel, final),
        grid=(N_ROW_TILES,),
        in_specs=[
            pl.BlockSpec((TM, TOPK), lambda i: (i, 0)),
            pl.BlockSpec((TM, D), lambda i: (i, 0)),
            *yk_specs,
            pl.BlockSpec((D, DE), lambda i: (0, 0), **once),
            pl.BlockSpec((D, DE), lambda i: (0, 0), **once),
            pl.BlockSpec((DE, D), lambda i: (0, 0), **once),
            vec,
            vec,
        ],
        out_specs=out_specs,
        out_shape=out_shape,
        scratch_shapes=scratch,
        compiler_params=_cparams(("arbitrary",)),
        name="moe_combine_final" if final else "moe_combine",
    )(wsel, h, *([yk] * TOPK), wsg, wsu, wsd, g, b)


def _moe_layer(layer, final, h, hp, w_router, router_bias, w_exp_gate, w_exp_up, w_exp_down,
               w_sh_gate, w_sh_up, w_sh_down, ln_g, ln_b):
    eidx_t, wsel, rank_t, counts = _router_call(h, w_router[layer], router_bias[layer][None, :])
    pos_t, te, gi, ne, nact = _plan_call(counts[0].astype(I32), eidx_t, rank_t)
    rowinfo = _rowinfo_call(pos_t)
    yk = _expert_call(layer, te, gi, ne, nact, rowinfo, hp, w_exp_gate, w_exp_up, w_exp_down)
    return _combine_call(
        final, wsel, h, yk,
        w_sh_gate[layer].astype(BF16), w_sh_up[layer].astype(BF16), w_sh_down[layer].astype(BF16),
        ln_g[layer][None, :], ln_b[layer][None, :])


def _ssm_prep_kernel(lr_ref, li_ref, ldt_ref, br_ref, bi_ref, abr_ref, abi_ref, bbr_ref, bbi_ref):
    lr = lr_ref[...]
    li = li_ref[...]
    dt = jnp.exp(ldt_ref[...])
    mag = jnp.exp(lr * dt)
    ab_re = mag * jnp.cos(li * dt)
    ab_im = mag * jnp.sin(li * dt)
    den = lr * lr + li * li
    nr = ab_re - 1.0
    ni = ab_im
    cr = (nr * lr + ni * li) / den
    ci = (ni * lr - nr * li) / den
    br = br_ref[...]
    bi = bi_ref[...]
    abr_ref[...] = ab_re
    abi_ref[...] = ab_im
    bbr_ref[...] = cr * br - ci * bi
    bbi_ref[...] = cr * bi + ci * br


def _ssm_prep_call(lam_re, lam_im, log_dt, b_re, b_im):
    wide = (NG, GC * NS)
    lr = jnp.tile(lam_re, (1, GC))
    li = jnp.tile(lam_im, (1, GC))
    ldt = jnp.broadcast_to(log_dt[:, None], wide)
    br = jnp.transpose(b_re, (0, 2, 1)).reshape(wide)
    bi = jnp.transpose(b_im, (0, 2, 1)).reshape(wide)
    sds = jax.ShapeDtypeStruct(wide, F32)
    return pl.pallas_call(
        _ssm_prep_kernel, out_shape=[sds, sds, sds, sds], name="ssm_prep",
        compiler_params=pltpu.CompilerParams(vmem_limit_bytes=VMEM_LIMIT),
    )(lr, li, ldt, br, bi)


def _cmul(ar, ai, xr, xi):
    return ar * xr - ai * xi, ar * xi + ai * xr


NSEG = 8
SEGL = LP // NSEG
STEP_UNROLL = 4
MOVE_UNROLL = 8
assert LP == NSEG * SEGL and SEGL % STEP_UNROLL == 0 and SEGL % MOVE_UNROLL == 0 and PAD_FRONT < SEGL


def _ssm_p_kernel(u_ref, wb_ref, wc_ref, a_ref, d_ref, z_ref, st_ref, u_scr, s_scr, y_scr):
    def interleave(it, carry):
        for q in range(MOVE_UNROLL):
            t = it * MOVE_UNROLL + q
            u_scr[pl.ds(pl.multiple_of(t * NSEG, NSEG), NSEG), :] = u_ref[pl.ds(t, NSEG, stride=SEGL), :]
        return carry

    lax.fori_loop(0, SEGL // MOVE_UNROLL, interleave, 0)
    row = lax.broadcasted_iota(I32, (LP, LANES), 0)
    is_pad = jnp.logical_and(row % NSEG == 0, row // NSEG < PAD_FRONT)
    u = jnp.where(is_pad, 0.0, u_scr[...])
    s_scr[...] = jnp.dot(u.astype(BF16), wb_ref[0], preferred_element_type=F32)

    ar = a_ref[0, 0:1, :]
    ai = a_ref[0, 1:2, :]

    def group(t):
        r0 = pl.multiple_of(t * NSEG, NSEG)
        return pl.ds(r0, NSEG)

    def local_scan(it, carry):
        sr, si = carry
        for q in range(STEP_UNROLL):
            g = group(it * STEP_UNROLL + q)
            tr, ti = _cmul(ar, ai, sr, si)
            sr = tr + s_scr[g, 0:CH_STATE]
            si = ti + s_scr[g, CH_STATE:]
            s_scr[g, 0:CH_STATE] = sr
            s_scr[g, CH_STATE:] = si
        return sr, si

    zero8 = jnp.zeros((NSEG, CH_STATE), F32)
    er, ei = lax.fori_loop(0, SEGL // STEP_UNROLL, local_scan, (zero8, zero8))

    pr, pi = ar, ai
    acc = None
    bits = SEGL
    while bits:
        if bits & 1:
            acc = (pr, pi) if acc is None else _cmul(pr, pi, *acc)
        bits >>= 1
        if bits:
            pr, pi = _cmul(pr, pi, pr, pi)
    alr, ali = acc
    cr = jnp.zeros((1, CH_STATE), F32)
    ci = jnp.zeros((1, CH_STATE), F32)
    ins_r, ins_i = [], []
    for j in range(NSEG):
        ins_r.append(cr)
        ins_i.append(ci)
        tr, ti = _cmul(alr, ali, cr, ci)
        cr = tr + er[j:j + 1, :]
        ci = ti + ei[j:j + 1, :]
    st_ref[0, 0, :, 0:CH_STATE] = cr
    st_ref[0, 0, :, CH_STATE:] = ci

    def fixup(it, carry):
        dr, di = carry
        for q in range(STEP_UNROLL):
            g = group(it * STEP_UNROLL + q)
            dr, di = _cmul(ar, ai, dr, di)
            s_scr[g, 0:CH_STATE] = s_scr[g, 0:CH_STATE] + dr
            s_scr[g, CH_STATE:] = s_scr[g, CH_STATE:] + di
        return dr, di

    lax.fori_loop(0, SEGL // STEP_UNROLL, fixup,
                  (jnp.concatenate(ins_r, axis=0), jnp.concatenate(ins_i, axis=0)))

    y_scr[...] = jnp.dot(s_scr[...].astype(BF16), wc_ref[0], preferred_element_type=F32) + d_ref[0] * u

    def deinterleave(it, carry):
        for q in range(MOVE_UNROLL):
            t = it * MOVE_UNROLL + q
            u_scr[pl.ds(t, NSEG, stride=SEGL), :] = y_scr[pl.ds(pl.multiple_of(t * NSEG, NSEG), NSEG), :]
        return carry

    lax.fori_loop(0, SEGL // MOVE_UNROLL, deinterleave, 0)
    z_ref[...] = jax.nn.gelu(u_scr[...]).astype(BF16)


def _ssm_p_call(h, wb_bf, wc_bf, a_tab, d_tab):
    return pl.pallas_call(
        _ssm_p_kernel,
        grid=(NB, NCHUNK),
        in_specs=[
            pl.BlockSpec((LP, LANES), lambda b, k: (b, k)),
            pl.BlockSpec((1, LANES, 2 * CH_STATE), lambda b, k: (k, 0, 0)),
            pl.BlockSpec((1, 2 * CH_STATE, LANES), lambda b, k: (k, 0, 0)),
            pl.BlockSpec((1, 2, CH_STATE), lambda b, k: (k, 0, 0)),
            pl.BlockSpec((1, 1, LANES), lambda b, k: (k, 0, 0)),
        ],
        out_specs=[
            pl.BlockSpec((LP, LANES), lambda b, k: (b, k)),
            pl.BlockSpec((1, 1, 1, 2 * CH_STATE), lambda b, k: (b, k, 0, 0)),
        ],
        out_shape=[
            jax.ShapeDtypeStruct((T_PROMPT, D), BF16),
            jax.ShapeDtypeStruct((NB, NCHUNK, 1, 2 * CH_STATE), F32),
        ],
        scratch_shapes=[pltpu.VMEM((LP, LANES), F32), pltpu.VMEM((LP, 2 * CH_STATE), F32),
                        pltpu.VMEM((LP, LANES), F32)],
        compiler_params=_cparams(("parallel", "parallel")),
        name="ssm_prompt",
    )(h, wb_bf, wc_bf, a_tab, d_tab)


def _ssm_s_kernel(u_ref, sr_ref, si_ref, wb_ref, wc_ref, a_ref, d_ref, z_ref, nr_ref, ni_ref):
    u = u_ref[...]
    bu = jnp.dot(u, wb_ref[0], preferred_element_type=F32, precision=lax.Precision.HIGHEST)
    ar = a_ref[0, 0:1, :]
    ai = a_ref[0, 1:2, :]
    tr, ti = _cmul(ar, ai, sr_ref[...], si_ref[...])
    nr = tr + bu[:, 0:CH_STATE]
    ni = ti + bu[:, CH_STATE:]
    nr_ref[...] = nr
    ni_ref[...] = ni
    s = jnp.concatenate([nr, ni], axis=1).astype(BF16)
    y = jnp.dot(s, wc_ref[0], preferred_element_type=F32) + d_ref[0] * u
    z_ref[...] = jax.nn.gelu(y).astype(BF16)


def _ssm_s_call(h, s0r, s0i, wb_f32, wc_bf, a_tab, d_tab):
    st = pl.BlockSpec((DEC, CH_STATE), lambda k: (0, k))
    return pl.pallas_call(
        _ssm_s_kernel,
        grid=(NCHUNK,),
        in_specs=[
            pl.BlockSpec((DEC, LANES), lambda k: (T_PROMPT // DEC, k)),
            st,
            st,
            pl.BlockSpec((1, LANES, 2 * CH_STATE), lambda k: (k, 0, 0)),
            pl.BlockSpec((1, 2 * CH_STATE, LANES), lambda k: (k, 0, 0)),
            pl.BlockSpec((1, 2, CH_STATE), lambda k: (k, 0, 0)),
            pl.BlockSpec((1, 1, LANES), lambda k: (k, 0, 0)),
        ],
        out_specs=[pl.BlockSpec((DEC, LANES), lambda k: (0, k)), st, st],
        out_shape=[
            jax.ShapeDtypeStruct((DEC, D), BF16),
            jax.ShapeDtypeStruct((DEC, NG * NS), F32),
            jax.ShapeDtypeStruct((DEC, NG * NS), F32),
        ],
        compiler_params=_cparams(("parallel",)),
        name="ssm_sample",
    )(h, s0r, s0i, wb_f32, wc_bf, a_tab, d_tab)


def _glu_ln_kernel(z_ref, zt_ref, w_ref, bg_ref, h_ref, g_ref, b_ref, out_ref, hp_ref):
    z = _prompt_or_tail(z_ref, zt_ref)
    acc = jnp.dot(z, w_ref[...], preferred_element_type=F32) + bg_ref[...]
    m = acc[:, :D] * _sigmoid(acc[:, D:])
    out = _layer_norm(ALPHA * h_ref[...] + m, g_ref[...], b_ref[...])
    out_ref[...] = out
    hp_ref[...] = _pack_pair(out[:, :HALF], out[:, HALF:])


def _glu_ln_call(z_p, z_tail, w_bf, bg, h, g, b):
    vec = pl.BlockSpec((1, D), lambda i: (0, 0))
    return pl.pallas_call(
        _glu_ln_kernel,
        grid=(N_ROW_TILES,),
        in_specs=[
            pl.BlockSpec((TM, D), _PROMPT_TILE),
            pl.BlockSpec((TM, D), lambda i: (0, 0)),
            pl.BlockSpec((D, 2 * D), lambda i: (0, 0), pipeline_mode=pl.Buffered(1)),
            pl.BlockSpec((1, 2 * D), lambda i: (0, 0)),
            pl.BlockSpec((TM, D), lambda i: (i, 0)),
            vec,
            vec,
        ],
        out_specs=[pl.BlockSpec((TM, D), lambda i: (i, 0)), pl.BlockSpec((TM, HALF), lambda i: (i, 0))],
        out_shape=[jax.ShapeDtypeStruct((T_ALL, D), F32), jax.ShapeDtypeStruct((T_ALL, HALF), U32)],
        compiler_params=_cparams(("parallel",)),
        name="glu_ln",
    )(z_p, z_tail, w_bf, bg, h, g, b)


def _block_diag_in(t):
    t4 = t.reshape(NCHUNK, 8, GC, NS)
    eye = jnp.eye(8, dtype=t.dtype)
    return jnp.einsum("kgcn,gh->kgchn", t4, eye).reshape(NCHUNK, LANES, CH_STATE)


def _block_diag_out(t):
    t4 = t.reshape(NCHUNK, 8, GC, NS)
    eye = jnp.eye(8, dtype=t.dtype)
    return jnp.einsum("kgcn,gh->kgnhc", t4, eye).reshape(NCHUNK, CH_STATE, LANES)


def kernel(x_prompt, x_sample, cache_k, cache_v, state_ssm_re, state_ssm_im, meta_tokens, w_qkv, b_qkv, attn_sinks, w_o, b_o, ssm_lam_re, ssm_lam_im, ssm_log_dt, ssm_b_re, ssm_b_im, ssm_c_re, ssm_c_im, ssm_d, w_glu, b_glu, ln_mix_g, ln_mix_b, w_router, router_bias, w_exp_gate, w_exp_up, w_exp_down, w_sh_gate, w_sh_up, w_sh_down, ln_ffn_g, ln_ffn_b):
    moe_w = (w_router, router_bias, w_exp_gate, w_exp_up, w_exp_down, w_sh_gate, w_sh_up, w_sh_down,
             ln_ffn_g, ln_ffn_b)

    front = jnp.concatenate([jnp.zeros((PAD_FRONT, D), F32), meta_tokens], axis=0)
    pieces = []
    for b in range(NB):
        pieces += [front, x_prompt[b]]
    h = jnp.concatenate(pieces + [x_sample.reshape(DEC, D)], axis=0)

    rc, rs1, rs2 = _rope_tables()
    q, k, v = _qkv_call(h, w_qkv[0].astype(BF16), b_qkv[0][None, :], rc, rs1, rs2)
    o_p = _attn_p_call(attn_sinks[0], q, k, v)
    q3 = q[T_PROMPT:].astype(F32).reshape(DEC, NH, HD)
    o_s, ck_new, cv_new = _attn_s_call(
        attn_sinks[0][:, None], q3, k, v,
        cache_k[0].reshape(DEC, BLK, NKV * HD), cache_v[0].reshape(DEC, BLK, NKV * HD))
    o_tail = _tail_tile(o_p, o_s.reshape(DEC, D).astype(BF16))
    h, hp = _oproj_ln_call(o_p, o_tail, w_o[0].astype(BF16), b_o[0][None, :], h,
                           ln_mix_g[0][None, :], ln_mix_b[0][None, :])
    h = _moe_layer(0, False, h, hp, *moe_w)

    kp = k[:T_PROMPT].reshape(NB, LP, NKV, HD)[:, LP - BLK:]
    vp = v[:T_PROMPT].reshape(NB, LP, NKV, HD)[:, LP - BLK:]

    ab_re, ab_im, bb_re, bb_im = _ssm_prep_call(
        ssm_lam_re[0], ssm_lam_im[0], ssm_log_dt[0], ssm_b_re[0], ssm_b_im[0])
    wb = jnp.concatenate([_block_diag_in(bb_re), _block_diag_in(bb_im)], axis=2)
    wc = jnp.concatenate([_block_diag_out(ssm_c_re[0]), -_block_diag_out(ssm_c_im[0])], axis=1)
    wc_bf = wc.astype(BF16)
    a_tab = jnp.stack([ab_re[:, :NS].reshape(NCHUNK, CH_STATE),
                       ab_im[:, :NS].reshape(NCHUNK, CH_STATE)], axis=1)
    d_tab = ssm_d[0].reshape(NCHUNK, 1, LANES)
    z_p, st_p = _ssm_p_call(h, wb.astype(BF16), wc_bf, a_tab, d_tab)
    z_s, sr_new, si_new = _ssm_s_call(
        h, state_ssm_re[0].reshape(DEC, NG * NS), state_ssm_im[0].reshape(DEC, NG * NS),
        wb, wc_bf, a_tab, d_tab)
    h, hp = _glu_ln_call(z_p, _tail_tile(z_p, z_s), w_glu[0].astype(BF16), b_glu[0][None, :], h,
                         ln_mix_g[1][None, :], ln_mix_b[1][None, :])
    y_prompt, y_sample = _moe_layer(1, True, h, hp, *moe_w)
    y_prompt = y_prompt.reshape(NB, SEQ, D)
    y_sample = y_sample.reshape(DEC, 1, D)
    st_p = st_p.reshape(NB, NCHUNK, 2, 8, NS)
    rp = st_p[:, :, 0].reshape(NB, NG, NS)
    ip = st_p[:, :, 1].reshape(NB, NG, NS)
    return (y_prompt, y_sample,
            kp[None], vp[None],
            ck_new.reshape(1, DEC, BLK, NKV, HD), cv_new.reshape(1, DEC, BLK, NKV, HD),
            rp[None], ip[None],
            sr_new.reshape(1, DEC, NG, NS), si_new.reshape(1, DEC, NG, NS))
```

```python
import functools
import math

import jax
import jax.numpy as jnp
import numpy as np
from jax import lax
from jax.experimental import pallas as pl
from jax.experimental.pallas import tpu as pltpu

F32 = jnp.float32
BF16 = jnp.bfloat16
I32 = jnp.int32
U32 = jnp.uint32

D = 2048
HALF = D // 2
NB = 4
N_META = 16
SEQ = 2048
L = N_META + SEQ
BLK = 128
PAD_FRONT = (-L) % BLK
LP = L + PAD_FRONT
NBLK = LP // BLK
T_PROMPT = NB * LP
DEC = 128
T_ALL = T_PROMPT + DEC
PAST_LEN = 8192
HD = 64
NH = 32
NKV = 4
QPK = NH // NKV
QKV = (NH + 2 * NKV) * HD
QK_COLS = (NH + NKV) * HD
ROT = HD // 4
ROT_HALF = ROT // 2
ROPE_THETA = 500000.0
NG = 128
GC = 16
NS = 64
NCHUNK = 16
CH_STATE = 8 * NS
NE = 64
TOPK = 8
NEG = 8
PER_GRP = NE // NEG
TOPG = 4
DE = 512
ROUTED_SCALE = 2.5
DEPTH = 2
ALPHA = (2 * DEPTH) ** 0.25
LN_EPS = 1e-5

V7X_VMEM_BYTES = 64 * 1024 * 1024
VMEM_LIMIT = 56 * 1024 * 1024
LANES = 128

TM = 384
N_ROW_TILES = T_ALL // TM
TAIL_START = (N_ROW_TILES - 1) * TM
TM_E = 256
N_PAIRS = T_ALL * TOPK
NT_E = -(-N_PAIRS // TM_E) + NE
R_ROWS = NT_E * TM_E
TOK_BITS = 14
TOK_MASK = (1 << TOK_BITS) - 1
DUMMY_BASE = TOPK * T_ALL
YK_ROWS = DUMMY_BASE + TM_E
assert T_ALL <= TOK_MASK and T_ALL % TOPK == 0 and T_ALL % TM == 0 and TAIL_START <= T_PROMPT


def _cparams(sem):
    return pltpu.CompilerParams(dimension_semantics=sem, vmem_limit_bytes=VMEM_LIMIT)


def _sigmoid(x):
    return 1.0 / (1.0 + jnp.exp(-x))


def _layer_norm(y, g, b):
    mu = jnp.mean(y, axis=-1, keepdims=True)
    yc = y - mu
    var = jnp.mean(yc * yc, axis=-1, keepdims=True)
    return yc * lax.rsqrt(var + LN_EPS) * g + b


def _pack_pair(lo, hi):
    lo_b = lax.bitcast_convert_type(lo.astype(BF16).astype(F32), U32) >> 16
    hi_b = lax.bitcast_convert_type(hi.astype(BF16).astype(F32), U32) & jnp.uint32(0xFFFF0000)
    return lo_b | hi_b


def _unpack_pair(w):
    lo = lax.bitcast_convert_type(w << 16, F32)
    hi = lax.bitcast_convert_type(w & jnp.uint32(0xFFFF0000), F32)
    return lo, hi


def _qkv_kernel(x_ref, w_ref, b_ref, c_ref, s1_ref, s2_ref, q_ref, k_ref, v_ref):
    xb = x_ref[...].astype(BF16)
    acc = jnp.dot(xb, w_ref[...], preferred_element_type=F32) + b_ref[...]
    c = c_ref[...]
    s1 = s1_ref[...]
    s2 = s2_ref[...]
    for j in range(QK_COLS // LANES):
        blk = acc[:, j * LANES:(j + 1) * LANES]
        r = (blk * c + pltpu.roll(blk, LANES - ROT_HALF, axis=1) * s1
             + pltpu.roll(blk, ROT_HALF, axis=1) * s2)
        if j < D // LANES:
            q_ref[:, j * LANES:(j + 1) * LANES] = (r * (1.0 / math.sqrt(HD))).astype(BF16)
        else:
            jj = j - D // LANES
            k_ref[:, jj * LANES:(jj + 1) * LANES] = r
    v_ref[...] = acc[:, QK_COLS:]


def _qkv_call(x, w_bf, b, rc, rs1, rs2):
    return pl.pallas_call(
        _qkv_kernel,
        grid=(T_ALL // TM,),
        in_specs=[
            pl.BlockSpec((TM, D), lambda i: (i, 0)),
            pl.BlockSpec((D, QKV), lambda i: (0, 0)),
            pl.BlockSpec((1, QKV), lambda i: (0, 0)),
            pl.BlockSpec((TM, LANES), lambda i: (i, 0)),
            pl.BlockSpec((TM, LANES), lambda i: (i, 0)),
            pl.BlockSpec((TM, LANES), lambda i: (i, 0)),
        ],
        out_specs=[
            pl.BlockSpec((TM, D), lambda i: (i, 0)),
            pl.BlockSpec((TM, NKV * HD), lambda i: (i, 0)),
            pl.BlockSpec((TM, NKV * HD), lambda i: (i, 0)),
        ],
        out_shape=[
            jax.ShapeDtypeStruct((T_ALL, D), BF16),
            jax.ShapeDtypeStruct((T_ALL, NKV * HD), F32),
            jax.ShapeDtypeStruct((T_ALL, NKV * HD), F32),
        ],
        compiler_params=_cparams(("parallel",)),
        name="qkv_rope",
    )(x, w_bf, b, rc, rs1, rs2)


def _rope_tables():
    pos_p = jnp.maximum(jnp.arange(LP, dtype=I32) - PAD_FRONT, 0)
    pos = jnp.concatenate([jnp.tile(pos_p, NB), jnp.full((DEC,), PAST_LEN, I32)]).astype(F32)
    inv_freq = ROPE_THETA ** (-jnp.arange(0, ROT, 2, dtype=F32) / ROT)
    ang = pos[:, None] * inv_freq[None, :]
    cos = jnp.cos(ang)
    sin = jnp.sin(ang)
    lane = np.arange(LANES) % HD
    freq = np.arange(ROT_HALF)[:, None]
    first = ((lane[None, :] == freq) & (lane[None, :] < ROT_HALF)).astype(np.float32)
    second = ((lane[None, :] - ROT_HALF == freq) & (lane[None, :] < ROT)).astype(np.float32)
    rest = (lane >= ROT).astype(np.float32)[None, :]
    place = functools.partial(jnp.dot, precision=lax.Precision.HIGHEST)
    c = place(cos, jnp.asarray(first + second)) + jnp.asarray(rest)
    s1 = place(sin, jnp.asarray(-first))
    s2 = place(sin, jnp.asarray(second))
    return c, s1, s2


def _attn_p_kernel(sink_ref, q_ref, kp_ref, kc_ref, vp_ref, vc_ref, o_ref):
    j = pl.program_id(0) % NBLK
    keys = jnp.concatenate([kp_ref[...], kc_ref[...]], axis=0).astype(BF16)
    vals = jnp.concatenate([vp_ref[...], vc_ref[...]], axis=0).astype(BF16)
    r = lax.broadcasted_iota(I32, (BLK, 2 * BLK), 0)
    c = lax.broadcasted_iota(I32, (BLK, 2 * BLK), 1)
    dist = BLK + r - c
    kpos = (j - 1) * BLK - PAD_FRONT + c
    mask = (dist >= 0) & (dist <= BLK) & (kpos >= 0)
    for g in range(NKV):
        kg = keys[:, g * HD:(g + 1) * HD]
        vg = vals[:, g * HD:(g + 1) * HD]
        heads = range(g * QPK, (g + 1) * QPK)
        scores = [lax.dot_general(q_ref[:, h * HD:(h + 1) * HD], kg, (((1,), (1,)), ((), ())),
                                  preferred_element_type=F32) for h in heads]
        probs, rdens = [], []
        for h, s in zip(heads, scores):
            s = jnp.where(mask, s, -jnp.inf)
            sk = sink_ref[h]
            m = jnp.maximum(jnp.max(s, axis=1, keepdims=True), sk)
            p = jnp.exp(s - m)
            rdens.append(1.0 / (jnp.sum(p, axis=1, keepdims=True) + jnp.exp(sk - m)))
            probs.append(p.astype(BF16))
        for h, p, rden in zip(heads, probs, rdens):
            oh = jnp.dot(p, vg, preferred_element_type=F32) * rden
            o_ref[:, h * HD:(h + 1) * HD] = oh.astype(BF16)


def _attn_p_call(sinks, q, k, v):
    prev = lambda i: (jnp.where(i % NBLK == 0, i, i - 1), 0)
    cur = lambda i: (i, 0)
    return pl.pallas_call(
        _attn_p_kernel,
        grid=(NB * NBLK,),
        in_specs=[
            pl.BlockSpec(memory_space=pltpu.SMEM),
            pl.BlockSpec((BLK, D), cur),
            pl.BlockSpec((BLK, NKV * HD), prev),
            pl.BlockSpec((BLK, NKV * HD), cur),
            pl.BlockSpec((BLK, NKV * HD), prev),
            pl.BlockSpec((BLK, NKV * HD), cur),
        ],
        out_specs=pl.BlockSpec((BLK, D), cur),
        out_shape=jax.ShapeDtypeStruct((T_PROMPT, D), BF16),
        compiler_params=_cparams(("parallel",)),
        name="attn_prompt",
    )(sinks, q, k, k, v, v)


SEQ_PER_STEP = 16
SEQ_UNROLL = 4


def _attn_s_kernel(sink_ref, q_ref, kn_ref, vn_ref, ck_ref, cv_ref, o_ref, cko_ref, cvo_ref):
    row = lax.broadcasted_iota(I32, (BLK, NKV * HD), 0)
    hrow = lax.broadcasted_iota(I32, (NH, NKV * HD), 0) // QPK
    hlane = lax.broadcasted_iota(I32, (NH, NKV * HD), 1) // HD
    own = hrow == hlane
    sk = sink_ref[...]

    def score_stage(s):
        kn = kn_ref[pl.ds(s, 1), :]
        knr = kn.astype(BF16).astype(F32)
        q = q_ref[s].astype(BF16)
        qe = jnp.where(own, jnp.concatenate([q] * NKV, axis=1), jnp.zeros((), BF16))
        sc = lax.dot_general(qe, ck_ref[s].astype(BF16), (((1,), (1,)), ((), ())),
                             preferred_element_type=F32)
        sn = jnp.sum(qe.astype(F32) * knr, axis=1, keepdims=True)
        return sc, sn

    def softmax_stage(sc, sn):
        m = jnp.maximum(jnp.maximum(jnp.max(sc, axis=1, keepdims=True), sn), sk)
        p = jnp.exp(sc - m)
        pn = jnp.exp(sn - m)
        rden = 1.0 / (jnp.sum(p, axis=1, keepdims=True) + pn + jnp.exp(sk - m))
        return p.astype(BF16), pn.astype(BF16).astype(F32), rden

    def value_stage(s, p, pn, rden):
        vn = vn_ref[pl.ds(s, 1), :]
        vnr = vn.astype(BF16).astype(F32)
        of = jnp.dot(p, cv_ref[s].astype(BF16), preferred_element_type=F32)
        of = jnp.where(own, of + pn * vnr, 0.0)
        og = of[:, 0:HD]
        for g in range(1, NKV):
            og = og + of[:, g * HD:(g + 1) * HD]
        o_ref[s] = og * rden
        cko_ref[s] = jnp.where(row == BLK - 1, kn_ref[pl.ds(s, 1), :], pltpu.roll(ck_ref[s], BLK - 1, axis=0))
        cvo_ref[s] = jnp.where(row == BLK - 1, vn, pltpu.roll(cv_ref[s], BLK - 1, axis=0))

    def body(it, carry):
        seqs = [it * SEQ_UNROLL + u for u in range(SEQ_UNROLL)]
        staged = [score_stage(s) for s in seqs]
        soft = [softmax_stage(*st) for st in staged]
        for s, sm in zip(seqs, soft):
            value_stage(s, *sm)
        return carry

    lax.fori_loop(0, SEQ_PER_STEP // SEQ_UNROLL, body, 0)


def _attn_s_call(sinks_col, q3, k, v, cache_k, cache_v):
    sp = SEQ_PER_STEP
    kv_off = T_PROMPT // sp
    return pl.pallas_call(
        _attn_s_kernel,
        grid=(DEC // sp,),
        in_specs=[
            pl.BlockSpec((NH, 1), lambda i: (0, 0)),
            pl.BlockSpec((sp, NH, HD), lambda i: (i, 0, 0)),
            pl.BlockSpec((sp, NKV * HD), lambda i: (kv_off + i, 0)),
            pl.BlockSpec((sp, NKV * HD), lambda i: (kv_off + i, 0)),
            pl.BlockSpec((sp, BLK, NKV * HD), lambda i: (i, 0, 0)),
            pl.BlockSpec((sp, BLK, NKV * HD), lambda i: (i, 0, 0)),
        ],
        out_specs=[
            pl.BlockSpec((sp, NH, HD), lambda i: (i, 0, 0)),
            pl.BlockSpec((sp, BLK, NKV * HD), lambda i: (i, 0, 0)),
            pl.BlockSpec((sp, BLK, NKV * HD), lambda i: (i, 0, 0)),
        ],
        out_shape=[
            jax.ShapeDtypeStruct((DEC, NH, HD), F32),
            jax.ShapeDtypeStruct((DEC, BLK, NKV * HD), F32),
            jax.ShapeDtypeStruct((DEC, BLK, NKV * HD), F32),
        ],
        compiler_params=_cparams(("parallel",)),
        name="attn_sample",
    )(sinks_col, q3, k, v, cache_k, cache_v)


def _prompt_or_tail(x_ref, tail_ref):
    return jnp.where(pl.program_id(0) == N_ROW_TILES - 1, tail_ref[...], x_ref[...])


def _tail_tile(x_prompt_rows, x_sample_rows):
    return jnp.concatenate([x_prompt_rows[TAIL_START:], x_sample_rows], axis=0)


_PROMPT_TILE = lambda i: (jnp.minimum(i, N_ROW_TILES - 2), 0)


def _oproj_ln_kernel(o_ref, ot_ref, w_ref, bo_ref, h_ref, g_ref, b_ref, out_ref, hp_ref):
    o = _prompt_or_tail(o_ref, ot_ref)
    m = jnp.dot(o, w_ref[...], preferred_element_type=F32) + bo_ref[...]
    out = _layer_norm(ALPHA * h_ref[...] + m, g_ref[...], b_ref[...])
    out_ref[...] = out
    hp_ref[...] = _pack_pair(out[:, :HALF], out[:, HALF:])


def _oproj_ln_call(o_p, o_tail, w_bf, bo, h, g, b):
    vec = pl.BlockSpec((1, D), lambda i: (0, 0))
    return pl.pallas_call(
        _oproj_ln_kernel,
        grid=(N_ROW_TILES,),
        in_specs=[
            pl.BlockSpec((TM, D), _PROMPT_TILE),
            pl.BlockSpec((TM, D), lambda i: (0, 0)),
            pl.BlockSpec((D, D), lambda i: (0, 0)),
            vec,
            pl.BlockSpec((TM, D), lambda i: (i, 0)),
            vec,
            vec,
        ],
        out_specs=[pl.BlockSpec((TM, D), lambda i: (i, 0)), pl.BlockSpec((TM, HALF), lambda i: (i, 0))],
        out_shape=[jax.ShapeDtypeStruct((T_ALL, D), F32), jax.ShapeDtypeStruct((T_ALL, HALF), U32)],
        compiler_params=_cparams(("parallel",)),
        name="oproj_ln",
    )(o_p, o_tail, w_bf, bo, h, g, b)


def _router_kernel(h_ref, wr_ref, rb_ref, eidx_ref, wsel_ref, rank_ref, cnt_ref, carry_ref):
    i = pl.program_id(0)

    @pl.when(i == 0)
    def _():
        carry_ref[...] = jnp.zeros_like(carry_ref)

    h = h_ref[...]
    w = wr_ref[...]
    h_hi = h.astype(BF16)
    h_lo = (h - h_hi.astype(F32)).astype(BF16)
    w_hi = w.astype(BF16)
    w_lo = (w - w_hi.astype(F32)).astype(BF16)
    logits = (jnp.dot(h_hi, w_hi, preferred_element_type=F32)
              + (jnp.dot(h_hi, w_lo, preferred_element_type=F32)
                 + jnp.dot(h_lo, w_hi, preferred_element_type=F32)))
    scores = _sigmoid(logits.T[0:NE, :])
    biased = scores + rb_ref[...]
    ninf = -jnp.inf
    sub = lax.broadcasted_iota(I32, (PER_GRP, TM), 0).astype(F32)
    sc_g = [scores[g * PER_GRP:(g + 1) * PER_GRP, :] for g in range(NEG)]
    b_g = [biased[g * PER_GRP:(g + 1) * PER_GRP, :] for g in range(NEG)]
    e_g = [sub + float(g * PER_GRP) for g in range(NEG)]

    def smax(x):
        return jnp.max(x, axis=0, keepdims=True)

    def smin(x):
        return jnp.min(x, axis=0, keepdims=True)

    gs = []
    for g in range(NEG):
        m1 = smax(b_g[g])
        i1 = smin(jnp.where(b_g[g] == m1, sub, float(PER_GRP)))
        m2 = smax(jnp.where(sub == i1, ninf, b_g[g]))
        gs.append(m1 + m2)

    work = []
    for g in range(NEG):
        beaten = jnp.zeros((1, TM), F32)
        for o in range(NEG):
            if o != g:
                wins = (gs[o] >= gs[g]) if o < g else (gs[o] > gs[g])
                beaten = beaten + jnp.where(wins, 1.0, 0.0)
        work.append(jnp.where(beaten < float(TOPG), b_g[g], ninf))

    idx_rows, w_rows = [], []
    onehot = [jnp.zeros((PER_GRP, TM), F32) for _ in range(NEG)]
    for _ in range(TOPK):
        m = smax(work[0])
        for g in range(1, NEG):
            m = jnp.maximum(m, smax(work[g]))
        ik = smin(jnp.where(work[0] == m, e_g[0], float(NE)))
        for g in range(1, NEG):
            ik = jnp.minimum(ik, smin(jnp.where(work[g] == m, e_g[g], float(NE))))
        wk = jnp.zeros((1, TM), F32)
        for g in range(NEG):
            hit = e_g[g] == ik
            wk = wk + jnp.sum(jnp.where(hit, sc_g[g], 0.0), axis=0, keepdims=True)
            onehot[g] = jnp.where(hit, 1.0, onehot[g])
            work[g] = jnp.where(hit, ninf, work[g])
        idx_rows.append(ik)
        w_rows.append(wk)
    wsum = w_rows[0]
    for wk in w_rows[1:]:
        wsum = wsum + wk

    rr = lax.broadcasted_iota(I32, (TM, TM), 0)
    cc = lax.broadcasted_iota(I32, (TM, TM), 1)
    tri = jnp.where(rr < cc, 1.0, 0.0).astype(BF16)
    oh = jnp.concatenate(onehot, axis=0)
    prefix = jnp.dot(oh.astype(BF16), tri, preferred_element_type=F32) + carry_ref[...]
    carry_ref[...] = carry_ref[...] + jnp.sum(oh, axis=1, keepdims=True)
    cnt_ref[...] = carry_ref[...]

    rank_rows = []
    for k in range(TOPK):
        rk = jnp.zeros((1, TM), F32)
        for g in range(NEG):
            pg = prefix[g * PER_GRP:(g + 1) * PER_GRP, :]
            rk = rk + jnp.sum(jnp.where(e_g[g] == idx_rows[k], pg, 0.0), axis=0, keepdims=True)
        rank_rows.append(rk)
    eidx_ref[...] = jnp.concatenate(idx_rows, axis=0).astype(I32)
    rank_ref[...] = jnp.concatenate(rank_rows, axis=0).astype(I32)
    w_t = jnp.concatenate([wk / wsum * ROUTED_SCALE for wk in w_rows]
                          + [jnp.zeros((LANES - TOPK, TM), F32)], axis=0)
    wsel_ref[...] = w_t.T[:, 0:TOPK]


def _router_call(h, w_router, router_bias):
    tk = pl.BlockSpec((TM, TOPK), lambda i: (i, 0))
    kt = pl.BlockSpec((TOPK, TM), lambda i: (0, i))
    return pl.pallas_call(
        _router_kernel,
        grid=(T_ALL // TM,),
        in_specs=[
            pl.BlockSpec((TM, D), lambda i: (i, 0)),
            pl.BlockSpec((D, LANES), lambda i: (0, 0)),
            pl.BlockSpec((NE, 1), lambda i: (0, 0)),
        ],
        out_specs=[kt, tk, kt, pl.BlockSpec((NE, 1), lambda i: (0, 0))],
        out_shape=[
            jax.ShapeDtypeStruct((TOPK, T_ALL), I32),
            jax.ShapeDtypeStruct((T_ALL, TOPK), F32),
            jax.ShapeDtypeStruct((TOPK, T_ALL), I32),
            jax.ShapeDtypeStruct((NE, 1), F32),
        ],
        scratch_shapes=[pltpu.VMEM((NE, 1), F32)],
        compiler_params=_cparams(("arbitrary",)),
        name="router",
    )(h, jnp.pad(w_router, ((0, 0), (0, LANES - NE))), router_bias[:, None])


def _tiles_of(cnt):
    return (cnt + TM_E - 1) // TM_E


def _plan_kernel(cnt_ref, eidx_ref, rank_ref, pos_ref, te_ref, gi_ref, ne_ref, na_ref, off_s):
    def offsets(e, run):
        off_s[e] = run
        return run + _tiles_of(cnt_ref[e]) * TM_E

    total = lax.fori_loop(0, NE, offsets, 0)
    na = total // TM_E
    na_ref[0] = na

    def idle(t, carry):
        te_ref[t] = NE - 1
        gi_ref[t] = 0
        ne_ref[t] = -1
        return carry

    lax.fori_loop(na, NT_E, idle, 0)

    def forward(e, ordinal):
        t0 = off_s[e] // TM_E
        nt = _tiles_of(cnt_ref[e])

        def mark(t, carry):
            te_ref[t] = e
            gi_ref[t] = ordinal
            return carry

        lax.fori_loop(t0, t0 + nt, mark, 0)
        return ordinal + jnp.where(nt > 0, 1, 0)

    lax.fori_loop(0, NE, forward, 0)

    def backward(i, nxt):
        e = NE - 1 - i
        t0 = off_s[e] // TM_E
        nt = _tiles_of(cnt_ref[e])

        def mark(t, carry):
            ne_ref[t] = nxt
            return carry

        lax.fori_loop(t0, t0 + nt, mark, 0)
        return jnp.where(nt > 0, e, nxt)

    lax.fori_loop(0, NE, backward, -1)

    eidx = eidx_ref[...]
    pos = rank_ref[...]
    for e in range(NE):
        pos = pos + jnp.where(eidx == e, off_s[e], 0)
    pos_ref[...] = pos


def _plan_call(cnt, eidx_t, rank_t):
    smem = pl.BlockSpec(memory_space=pltpu.SMEM)
    vmem = pl.BlockSpec(memory_space=pltpu.VMEM)
    tiles = jax.ShapeDtypeStruct((NT_E,), I32)
    return pl.pallas_call(
        _plan_kernel,
        in_specs=[smem, vmem, vmem],
        out_specs=[vmem, smem, smem, smem, smem],
        out_shape=[jax.ShapeDtypeStruct((TOPK, T_ALL), I32), tiles, tiles, tiles,
                   jax.ShapeDtypeStruct((1,), I32)],
        scratch_shapes=[pltpu.SMEM((NE,), I32)],
        name="moe_plan",
    )(cnt, eidx_t, rank_t)


def _rowinfo_kernel(pos_ref, word_ref, pad_ref, info_ref, sem):
    @pl.when(pl.program_id(0) == 0)
    def _():
        cp = pltpu.make_async_copy(pad_ref, info_ref, sem)
        cp.start()
        cp.wait()

    def body(j, carry):
        for u in range(TOPK):
            jj = j * TOPK + u
            info_ref[pos_ref[0, 0, jj]] = word_ref[0, 0, jj]
        return carry

    lax.fori_loop(0, T_ALL // TOPK, body, 0)


def _rowinfo_words():
    tok = np.arange(T_ALL, dtype=np.int64)[None, :]
    slot = np.arange(TOPK, dtype=np.int64)[:, None]
    words = ((slot * T_ALL + tok) << TOK_BITS) | tok
    rows = np.arange(R_ROWS, dtype=np.int64)
    pads = (DUMMY_BASE + (rows & (TM_E - 1))) << TOK_BITS
    return (jnp.asarray(words.astype(np.int32).reshape(TOPK, 1, T_ALL)),
            jnp.asarray(pads.astype(np.int32)))


def _rowinfo_call(pos_t):
    words, pads = _rowinfo_words()
    blk = pl.BlockSpec((1, 1, T_ALL), lambda i: (i, 0, 0), memory_space=pltpu.SMEM)
    return pl.pallas_call(
        _rowinfo_kernel,
        grid=(TOPK,),
        in_specs=[blk, blk, pl.BlockSpec(memory_space=pl.ANY)],
        out_specs=pl.BlockSpec(memory_space=pltpu.SMEM),
        out_shape=jax.ShapeDtypeStruct((R_ROWS,), I32),
        scratch_shapes=[pltpu.SemaphoreType.DMA(())],
        compiler_params=pltpu.CompilerParams(dimension_semantics=("arbitrary",)),
        name="moe_rowinfo",
    )(pos_t.reshape(TOPK, 1, T_ALL), words, pads)


def _expert_kernel(layer, te_ref, gi_ref, ne_ref, na_ref, ginfo_ref, sinfo_ref, hp_ref,
                   wg_hbm, wu_hbm, wd_hbm, yk_ref,
                   xb0, xb1, yb0, yb1, wgf, wuf, wdf, wg_s, wu_s, wd_s, gsem, ssem, wsem, zsem):
    s = pl.program_id(0)
    na = na_ref[0]
    xb = (xb0, xb1)
    yb = (yb0, yb1)

    def weight_copies(e, slot):
        return (pltpu.make_async_copy(wg_hbm.at[layer, e], wgf.at[slot], wsem.at[0]),
                pltpu.make_async_copy(wu_hbm.at[layer, e], wuf.at[slot], wsem.at[1]),
                pltpu.make_async_copy(wd_hbm.at[layer, e], wdf.at[slot], wsem.at[2]))

    def gather_row(par, r, thread=0):
        tok = ginfo_ref[0, 0, r] & TOK_MASK
        pltpu.make_async_copy(hp_ref.at[pl.ds(tok, 1)], xb[par].at[pl.ds(r, 1)], gsem.at[par]).start(
            priority=thread)

    def scatter_row(par, r, thread=0):
        dst = sinfo_ref[0, 0, r] >> TOK_BITS
        pltpu.make_async_copy(yb[par].at[pl.ds(r, 1)], yk_ref.at[pl.ds(dst, 1)], ssem.at[par]).start(
            priority=thread)

    def compute(par, between):
        lo, hi = _unpack_pair(xb[1 - par][...])
        lo = lo.astype(BF16)
        hi = hi.astype(BF16)
        between(0)
        hg = (jnp.dot(lo, wg_s[0:HALF, :], preferred_element_type=F32)
              + jnp.dot(hi, wg_s[HALF:D, :], preferred_element_type=F32))
        between(1)
        hu = (jnp.dot(lo, wu_s[0:HALF, :], preferred_element_type=F32)
              + jnp.dot(hi, wu_s[HALF:D, :], preferred_element_type=F32))
        between(2)
        act = (hg * _sigmoid(hg) * hu).astype(BF16)
        y = jnp.dot(act, wd_s[...], preferred_element_type=F32)
        between(3)
        yb[1 - par][...] = _pack_pair(y[:, :HALF], y[:, HALF:])

    @pl.when(jnp.logical_and(s >= 1, s <= na))
    def _():
        pltpu.make_async_copy(hp_ref.at[pl.ds(0, TM_E)], xb0, gsem.at[(s + 1) % 2]).wait()

    @pl.when(jnp.logical_and(s >= 3, s <= na + 2))
    def _():
        pltpu.make_async_copy(yb0, yk_ref.at[pl.ds(0, TM_E)], ssem.at[(s + 1) % 2]).wait()

    @pl.when(s == 0)
    def _():
        for cp in weight_copies(te_ref[0], 0):
            cp.start(priority=1)
        yb0[...] = jnp.zeros_like(yb0)
        zc = pltpu.make_async_copy(yb0, yk_ref.at[pl.ds(DUMMY_BASE, TM_E)], zsem)
        zc.start()
        zc.wait()

    c = s - 1
    cc = jnp.clip(c, 0, na - 1)
    first = jnp.logical_or(c == 0, te_ref[cc] != te_ref[jnp.maximum(cc - 1, 0)])

    @pl.when(jnp.logical_and(jnp.logical_and(c >= 0, c < na), first))
    def _():
        slot = gi_ref[cc] % 2
        for cp in weight_copies(te_ref[cc], slot):
            cp.wait()
        wg_s[...] = wgf[slot].astype(BF16)
        wu_s[...] = wuf[slot].astype(BF16)
        wd_s[...] = wdf[slot].astype(BF16)
        nxt = ne_ref[cc]

        @pl.when(nxt >= 0)
        def _():
            for cp in weight_copies(nxt, 1 - slot):
                cp.start(priority=1)

    steady = jnp.logical_and(s >= 2, s < na)
    for par in (0, 1):
        mine = (s % 2) == par

        @pl.when(jnp.logical_and(steady, mine))
        def _():
            def between(q):
                for r in range(q * (TM_E // 4), (q + 1) * (TM_E // 4)):
                    gather_row(par, r, thread=(r + 1) % 2)
                    scatter_row(par, r, thread=r % 2)

            compute(par, between)

        @pl.when(jnp.logical_and(jnp.logical_not(steady), mine))
        def _():
            @pl.when(s < na)
            def _():
                def g(r, carry):
                    gather_row(par, r)
                    return carry

                lax.fori_loop(0, TM_E, g, 0)

            @pl.when(jnp.logical_and(s >= 1, s <= na))
            def _():
                compute(par, lambda q: None)

            @pl.when(jnp.logical_and(s >= 2, s <= na + 1))
            def _():
                def sc(r, carry):
                    scatter_row(par, r)
                    return carry

                lax.fori_loop(0, TM_E, sc, 0)


def _expert_call(layer, te, gi, ne, nact, rowinfo, hp, w_gate, w_up, w_down):
    def gmap(s, te_r, gi_r, ne_r, na_r):
        return (jnp.minimum(s, na_r[0] - 1), 0, 0)

    def smap(s, te_r, gi_r, ne_r, na_r):
        return (jnp.clip(s - 2, 0, na_r[0] - 1), 0, 0)

    info = rowinfo.reshape(NT_E, 1, TM_E)
    anyspec = pl.BlockSpec(memory_space=pl.ANY)
    return pl.pallas_call(
        functools.partial(_expert_kernel, layer),
        grid_spec=pltpu.PrefetchScalarGridSpec(
            num_scalar_prefetch=4,
            grid=(NT_E + 3,),
            in_specs=[
                pl.BlockSpec((1, 1, TM_E), gmap, memory_space=pltpu.SMEM),
                pl.BlockSpec((1, 1, TM_E), smap, memory_space=pltpu.SMEM),
                anyspec, anyspec, anyspec, anyspec,
            ],
            out_specs=anyspec,
            scratch_shapes=[
                pltpu.VMEM((TM_E, HALF), U32),
                pltpu.VMEM((TM_E, HALF), U32),
                pltpu.VMEM((TM_E, HALF), U32),
                pltpu.VMEM((TM_E, HALF), U32),
                pltpu.VMEM((2, D, DE), F32),
                pltpu.VMEM((2, D, DE), F32),
                pltpu.VMEM((2, DE, D), F32),
                pltpu.VMEM((D, DE), BF16),
                pltpu.VMEM((D, DE), BF16),
                pltpu.VMEM((DE, D), BF16),
                pltpu.SemaphoreType.DMA((2,)),
                pltpu.SemaphoreType.DMA((2,)),
                pltpu.SemaphoreType.DMA((3,)),
                pltpu.SemaphoreType.DMA(()),
            ],
        ),
        out_shape=jax.ShapeDtypeStruct((YK_ROWS, HALF), U32),
        compiler_params=_cparams(("arbitrary",)),
        name="moe_experts",
    )(te, gi, ne, nact, info, info, hp, w_gate, w_up, w_down)


BLK_PER_TILE = TM // BLK


def _final_copies(step, obuf, yp_ref, ys_ref, sem):
    out = []
    slot = step % 2
    for m in range(BLK_PER_TILE):
        g = step * BLK_PER_TILE + m
        b = g // NBLK
        j = g % NBLK
        src = obuf.at[slot, pl.ds(m * BLK, BLK)]
        r0 = pl.multiple_of(jnp.maximum(b * SEQ + (j - 1) * BLK, 0), BLK)
        out.append((jnp.logical_and(g < NB * NBLK, j >= 1),
                    pltpu.make_async_copy(src, yp_ref.at[pl.ds(r0, BLK)], sem.at[slot])))
        out.append((g == NB * NBLK, pltpu.make_async_copy(src, ys_ref, sem.at[slot])))
    return out


def _combine_kernel(final, w_ref, h_ref, *rest):
    yk_refs = rest[:TOPK]
    wsg_ref, wsu_ref, wsd_ref, g_ref, b_ref = rest[TOPK:TOPK + 5]
    outs = rest[TOPK + 5:]
    i = pl.program_id(0)
    if final:
        yp_ref, ys_ref, obuf, osem = outs

        def wait_step(step):
            for cond, cp in _final_copies(step, obuf, yp_ref, ys_ref, osem):
                @pl.when(cond)
                def _():
                    cp.wait()

        @pl.when(i >= 2)
        def _():
            wait_step(i - 2)

    h = h_ref[...]
    hb = h.astype(BF16)
    sg = jnp.dot(hb, wsg_ref[...], preferred_element_type=F32)
    su = jnp.dot(hb, wsu_ref[...], preferred_element_type=F32)
    act = (sg * _sigmoid(sg) * su).astype(BF16)
    y = ALPHA * h + jnp.dot(act, wsd_ref[...], preferred_element_type=F32)
    ylo = y[:, :HALF]
    yhi = y[:, HALF:]
    w = w_ref[...]
    for k in range(TOPK):
        lo, hi = _unpack_pair(yk_refs[k][...])
        wk = w[:, k:k + 1]
        ylo = ylo + wk * lo
        yhi = yhi + wk * hi
    out = _layer_norm(jnp.concatenate([ylo, yhi], axis=1), g_ref[...], b_ref[...])
    if not final:
        outs[0][...] = out
        return
    obuf[i % 2] = out
    for cond, cp in _final_copies(i, obuf, yp_ref, ys_ref, osem):
        @pl.when(cond)
        def _():
            cp.start()

    @pl.when(i == N_ROW_TILES - 1)
    def _():
        wait_step(i - 1)
        wait_step(i)


def _combine_call(final, wsel, h, yk, wsg, wsu, wsd, g, b):
    vec = pl.BlockSpec((1, D), lambda i: (0, 0))
    once = dict(pipeline_mode=pl.Buffered(1))
    yk_specs = [pl.BlockSpec((TM, HALF), lambda i, k=k: (k * N_ROW_TILES + i, 0)) for k in range(TOPK)]
    if final:
        anyspec = pl.BlockSpec(memory_space=pl.ANY)
        out_specs = [anyspec, anyspec]
        out_shape = [jax.ShapeDtypeStruct((NB * SEQ, D), F32), jax.ShapeDtypeStruct((DEC, D), F32)]
        scratch = [pltpu.VMEM((2, TM, D), F32), pltpu.SemaphoreType.DMA((2,))]
    else:
        out_specs = pl.BlockSpec((TM, D), lambda i: (i, 0))
        out_shape = jax.ShapeDtypeStruct((T_ALL, D), F32)
        scratch = []
    return pl.pallas_call(
        functools.partial(_combine_kernel, final),
        grid=(N_ROW_TILES,),
        in_specs=[
            pl.BlockSpec((TM, TOPK), lambda i: (i, 0)),
            pl.BlockSpec((TM, D), lambda i: (i, 0)),
            *yk_specs,
            pl.BlockSpec((D, DE), lambda i: (0, 0), **once),
            pl.BlockSpec((D, DE), lambda i: (0, 0), **once),
            pl.BlockSpec((DE, D), lambda i: (0, 0), **once),
            vec,
            vec,
        ],
        out_specs=out_specs,
        out_shape=out_shape,
        scratch_shapes=scratch,
        compiler_params=_cparams(("arbitrary",)),
        name="moe_combine_final" if final else "moe_combine",
    )(wsel, h, *([yk] * TOPK), wsg, wsu, wsd, g, b)


def _moe_layer(layer, final, h, hp, w_router, router_bias, w_exp_gate, w_exp_up, w_exp_down,
               w_sh_gate, w_sh_up, w_sh_down, ln_g, ln_b):
    eidx_t, wsel, rank_t, counts = _router_call(h, w_router[layer], router_bias[layer])
    pos_t, te, gi, ne, nact = _plan_call(counts[:, 0].astype(I32), eidx_t, rank_t)
    rowinfo = _rowinfo_call(pos_t)
    yk = _expert_call(layer, te, gi, ne, nact, rowinfo, hp, w_exp_gate, w_exp_up, w_exp_down)
    return _combine_call(
        final, wsel, h, yk,
        w_sh_gate[layer].astype(BF16), w_sh_up[layer].astype(BF16), w_sh_down[layer].astype(BF16),
        ln_g[layer][None, :], ln_b[layer][None, :])


def _ssm_prep_kernel(lr_ref, li_ref, ldt_ref, br_ref, bi_ref, abr_ref, abi_ref, bbr_ref, bbi_ref):
    lr = lr_ref[...]
    li = li_ref[...]
    dt = jnp.exp(ldt_ref[...])
    mag = jnp.exp(lr * dt)
    ab_re = mag * jnp.cos(li * dt)
    ab_im = mag * jnp.sin(li * dt)
    den = lr * lr + li * li
    nr = ab_re - 1.0
    ni = ab_im
    cr = (nr * lr + ni * li) / den
    ci = (ni * lr - nr * li) / den
    br = br_ref[...]
    bi = bi_ref[...]
    abr_ref[...] = ab_re
    abi_ref[...] = ab_im
    bbr_ref[...] = cr * br - ci * bi
    bbi_ref[...] = cr * bi + ci * br


def _ssm_prep_call(lam_re, lam_im, log_dt, b_re, b_im):
    wide = (NG, GC * NS)
    lr = jnp.tile(lam_re, (1, GC))
    li = jnp.tile(lam_im, (1, GC))
    ldt = jnp.broadcast_to(log_dt[:, None], wide)
    br = jnp.transpose(b_re, (0, 2, 1)).reshape(wide)
    bi = jnp.transpose(b_im, (0, 2, 1)).reshape(wide)
    sds = jax.ShapeDtypeStruct(wide, F32)
    return pl.pallas_call(
        _ssm_prep_kernel, out_shape=[sds, sds, sds, sds], name="ssm_prep",
        compiler_params=pltpu.CompilerParams(vmem_limit_bytes=VMEM_LIMIT),
    )(lr, li, ldt, br, bi)


def _cmul(ar, ai, xr, xi):
    return ar * xr - ai * xi, ar * xi + ai * xr


NSEG = 8
SEGL = LP // NSEG
STEP_UNROLL = 4
MOVE_UNROLL = 8
assert LP == NSEG * SEGL and SEGL % STEP_UNROLL == 0 and SEGL % MOVE_UNROLL == 0 and PAD_FRONT < SEGL


def _ssm_p_kernel(u_ref, wb_ref, wc_ref, a_ref, d_ref, z_ref, st_ref, u_scr, s_scr, y_scr):
    def interleave(it, carry):
        for q in range(MOVE_UNROLL):
            t = it * MOVE_UNROLL + q
            u_scr[pl.ds(pl.multiple_of(t * NSEG, NSEG), NSEG), :] = u_ref[pl.ds(t, NSEG, stride=SEGL), :]
        return carry

    lax.fori_loop(0, SEGL // MOVE_UNROLL, interleave, 0)
    row = lax.broadcasted_iota(I32, (LP, LANES), 0)
    is_pad = jnp.logical_and(row % NSEG == 0, row // NSEG < PAD_FRONT)
    u = jnp.where(is_pad, 0.0, u_scr[...])
    s_scr[...] = jnp.dot(u.astype(BF16), wb_ref[0], preferred_element_type=F32)

    ar = a_ref[0, 0:1, :]
    ai = a_ref[0, 1:2, :]

    def group(t):
        r0 = pl.multiple_of(t * NSEG, NSEG)
        return pl.ds(r0, NSEG)

    def local_scan(it, carry):
        sr, si = carry
        for q in range(STEP_UNROLL):
            g = group(it * STEP_UNROLL + q)
            tr, ti = _cmul(ar, ai, sr, si)
            sr = tr + s_scr[g, 0:CH_STATE]
            si = ti + s_scr[g, CH_STATE:]
            s_scr[g, 0:CH_STATE] = sr
            s_scr[g, CH_STATE:] = si
        return sr, si

    zero8 = jnp.zeros((NSEG, CH_STATE), F32)
    er, ei = lax.fori_loop(0, SEGL // STEP_UNROLL, local_scan, (zero8, zero8))

    pr, pi = ar, ai
    acc = None
    bits = SEGL
    while bits:
        if bits & 1:
            acc = (pr, pi) if acc is None else _cmul(pr, pi, *acc)
        bits >>= 1
        if bits:
            pr, pi = _cmul(pr, pi, pr, pi)
    alr, ali = acc
    cr = jnp.zeros((1, CH_STATE), F32)
    ci = jnp.zeros((1, CH_STATE), F32)
    ins_r, ins_i = [], []
    for j in range(NSEG):
        ins_r.append(cr)
        ins_i.append(ci)
        tr, ti = _cmul(alr, ali, cr, ci)
        cr = tr + er[j:j + 1, :]
        ci = ti + ei[j:j + 1, :]
    st_ref[0, 0, :, 0:CH_STATE] = cr
    st_ref[0, 0, :, CH_STATE:] = ci

    def fixup(it, carry):
        dr, di = carry
        for q in range(STEP_UNROLL):
            g = group(it * STEP_UNROLL + q)
            dr, di = _cmul(ar, ai, dr, di)
            s_scr[g, 0:CH_STATE] = s_scr[g, 0:CH_STATE] + dr
            s_scr[g, CH_STATE:] = s_scr[g, CH_STATE:] + di
        return dr, di

    lax.fori_loop(0, SEGL // STEP_UNROLL, fixup,
                  (jnp.concatenate(ins_r, axis=0), jnp.concatenate(ins_i, axis=0)))

    y_scr[...] = jnp.dot(s_scr[...].astype(BF16), wc_ref[0], preferred_element_type=F32) + d_ref[0] * u

    def deinterleave(it, carry):
        for q in range(MOVE_UNROLL):
            t = it * MOVE_UNROLL + q
            u_scr[pl.ds(t, NSEG, stride=SEGL), :] = y_scr[pl.ds(pl.multiple_of(t * NSEG, NSEG), NSEG), :]
        return carry

    lax.fori_loop(0, SEGL // MOVE_UNROLL, deinterleave, 0)
    z_ref[...] = jax.nn.gelu(u_scr[...]).astype(BF16)


def _ssm_p_call(h, wb_bf, wc_bf, a_tab, d_tab):
    return pl.pallas_call(
        _ssm_p_kernel,
        grid=(NB, NCHUNK),
        in_specs=[
            pl.BlockSpec((LP, LANES), lambda b, k: (b, k)),
            pl.BlockSpec((1, LANES, 2 * CH_STATE), lambda b, k: (k, 0, 0)),
            pl.BlockSpec((1, 2 * CH_STATE, LANES), lambda b, k: (k, 0, 0)),
            pl.BlockSpec((1, 2, CH_STATE), lambda b, k: (k, 0, 0)),
            pl.BlockSpec((1, 1, LANES), lambda b, k: (k, 0, 0)),
        ],
        out_specs=[
            pl.BlockSpec((LP, LANES), lambda b, k: (b, k)),
            pl.BlockSpec((1, 1, 1, 2 * CH_STATE), lambda b, k: (b, k, 0, 0)),
        ],
        out_shape=[
            jax.ShapeDtypeStruct((T_PROMPT, D), BF16),
            jax.ShapeDtypeStruct((NB, NCHUNK, 1, 2 * CH_STATE), F32),
        ],
        scratch_shapes=[pltpu.VMEM((LP, LANES), F32), pltpu.VMEM((LP, 2 * CH_STATE), F32),
                        pltpu.VMEM((LP, LANES), F32)],
        compiler_params=_cparams(("parallel", "parallel")),
        name="ssm_prompt",
    )(h, wb_bf, wc_bf, a_tab, d_tab)


def _ssm_s_kernel(u_ref, sr_ref, si_ref, wb_ref, wc_ref, a_ref, d_ref, z_ref, nr_ref, ni_ref):
    u = u_ref[...]
    bu = jnp.dot(u, wb_ref[0], preferred_element_type=F32, precision=lax.Precision.HIGHEST)
    ar = a_ref[0, 0:1, :]
    ai = a_ref[0, 1:2, :]
    tr, ti = _cmul(ar, ai, sr_ref[...], si_ref[...])
    nr = tr + bu[:, 0:CH_STATE]
    ni = ti + bu[:, CH_STATE:]
    nr_ref[...] = nr
    ni_ref[...] = ni
    s = jnp.concatenate([nr, ni], axis=1).astype(BF16)
    y = jnp.dot(s, wc_ref[0], preferred_element_type=F32) + d_ref[0] * u
    z_ref[...] = jax.nn.gelu(y).astype(BF16)


def _ssm_s_call(h, s0r, s0i, wb_f32, wc_bf, a_tab, d_tab):
    st = pl.BlockSpec((DEC, CH_STATE), lambda k: (0, k))
    return pl.pallas_call(
        _ssm_s_kernel,
        grid=(NCHUNK,),
        in_specs=[
            pl.BlockSpec((DEC, LANES), lambda k: (T_PROMPT // DEC, k)),
            st,
            st,
            pl.BlockSpec((1, LANES, 2 * CH_STATE), lambda k: (k, 0, 0)),
            pl.BlockSpec((1, 2 * CH_STATE, LANES), lambda k: (k, 0, 0)),
            pl.BlockSpec((1, 2, CH_STATE), lambda k: (k, 0, 0)),
            pl.BlockSpec((1, 1, LANES), lambda k: (k, 0, 0)),
        ],
        out_specs=[pl.BlockSpec((DEC, LANES), lambda k: (0, k)), st, st],
        out_shape=[
            jax.ShapeDtypeStruct((DEC, D), BF16),
            jax.ShapeDtypeStruct((DEC, NG * NS), F32),
            jax.ShapeDtypeStruct((DEC, NG * NS), F32),
        ],
        compiler_params=_cparams(("parallel",)),
        name="ssm_sample",
    )(h, s0r, s0i, wb_f32, wc_bf, a_tab, d_tab)


def _glu_ln_kernel(z_ref, zt_ref, w_ref, bg_ref, h_ref, g_ref, b_ref, out_ref, hp_ref):
    z = _prompt_or_tail(z_ref, zt_ref)
    acc = jnp.dot(z, w_ref[...], preferred_element_type=F32) + bg_ref[...]
    m = acc[:, :D] * _sigmoid(acc[:, D:])
    out = _layer_norm(ALPHA * h_ref[...] + m, g_ref[...], b_ref[...])
    out_ref[...] = out
    hp_ref[...] = _pack_pair(out[:, :HALF], out[:, HALF:])


def _glu_ln_call(z_p, z_tail, w_bf, bg, h, g, b):
    vec = pl.BlockSpec((1, D), lambda i: (0, 0))
    return pl.pallas_call(
        _glu_ln_kernel,
        grid=(N_ROW_TILES,),
        in_specs=[
            pl.BlockSpec((TM, D), _PROMPT_TILE),
            pl.BlockSpec((TM, D), lambda i: (0, 0)),
            pl.BlockSpec((D, 2 * D), lambda i: (0, 0), pipeline_mode=pl.Buffered(1)),
            pl.BlockSpec((1, 2 * D), lambda i: (0, 0)),
            pl.BlockSpec((TM, D), lambda i: (i, 0)),
            vec,
            vec,
        ],
        out_specs=[pl.BlockSpec((TM, D), lambda i: (i, 0)), pl.BlockSpec((TM, HALF), lambda i: (i, 0))],
        out_shape=[jax.ShapeDtypeStruct((T_ALL, D), F32), jax.ShapeDtypeStruct((T_ALL, HALF), U32)],
        compiler_params=_cparams(("parallel",)),
        name="glu_ln",
    )(z_p, z_tail, w_bf, bg, h, g, b)


def _block_diag_in(t):
    t4 = t.reshape(NCHUNK, 8, GC, NS)
    eye = jnp.eye(8, dtype=t.dtype)
    return jnp.einsum("kgcn,gh->kgchn", t4, eye).reshape(NCHUNK, LANES, CH_STATE)


def _block_diag_out(t):
    t4 = t.reshape(NCHUNK, 8, GC, NS)
    eye = jnp.eye(8, dtype=t.dtype)
    return jnp.einsum("kgcn,gh->kgnhc", t4, eye).reshape(NCHUNK, CH_STATE, LANES)


def kernel(x_prompt, x_sample, cache_k, cache_v, state_ssm_re, state_ssm_im, meta_tokens, w_qkv, b_qkv, attn_sinks, w_o, b_o, ssm_lam_re, ssm_lam_im, ssm_log_dt, ssm_b_re, ssm_b_im, ssm_c_re, ssm_c_im, ssm_d, w_glu, b_glu, ln_mix_g, ln_mix_b, w_router, router_bias, w_exp_gate, w_exp_up, w_exp_down, w_sh_gate, w_sh_up, w_sh_down, ln_ffn_g, ln_ffn_b):
    moe_w = (w_router, router_bias, w_exp_gate, w_exp_up, w_exp_down, w_sh_gate, w_sh_up, w_sh_down,
             ln_ffn_g, ln_ffn_b)

    front = jnp.concatenate([jnp.zeros((PAD_FRONT, D), F32), meta_tokens], axis=0)
    pieces = []
    for b in range(NB):
        pieces += [front, x_prompt[b]]
    h = jnp.concatenate(pieces + [x_sample.reshape(DEC, D)], axis=0)

    rc, rs1, rs2 = _rope_tables()
    q, k, v = _qkv_call(h, w_qkv[0].astype(BF16), b_qkv[0][None, :], rc, rs1, rs2)
    o_p = _attn_p_call(attn_sinks[0], q, k, v)
    q3 = q[T_PROMPT:].astype(F32).reshape(DEC, NH, HD)
    o_s, ck_new, cv_new = _attn_s_call(
        attn_sinks[0][:, None], q3, k, v,
        cache_k[0].reshape(DEC, BLK, NKV * HD), cache_v[0].reshape(DEC, BLK, NKV * HD))
    o_tail = _tail_tile(o_p, o_s.reshape(DEC, D).astype(BF16))
    h, hp = _oproj_ln_call(o_p, o_tail, w_o[0].astype(BF16), b_o[0][None, :], h,
                           ln_mix_g[0][None, :], ln_mix_b[0][None, :])
    h = _moe_layer(0, False, h, hp, *moe_w)

    kp = k[:T_PROMPT].reshape(NB, LP, NKV, HD)[:, LP - BLK:]
    vp = v[:T_PROMPT].reshape(NB, LP, NKV, HD)[:, LP - BLK:]

    ab_re, ab_im, bb_re, bb_im = _ssm_prep_call(
        ssm_lam_re[0], ssm_lam_im[0], ssm_log_dt[0], ssm_b_re[0], ssm_b_im[0])
    wb = jnp.concatenate([_block_diag_in(bb_re), _block_diag_in(bb_im)], axis=2)
    wc = jnp.concatenate([_block_diag_out(ssm_c_re[0]), -_block_diag_out(ssm_c_im[0])], axis=1)
    wc_bf = wc.astype(BF16)
    a_tab = jnp.stack([ab_re[:, :NS].reshape(NCHUNK, CH_STATE),
                       ab_im[:, :NS].reshape(NCHUNK, CH_STATE)], axis=1)
    d_tab = ssm_d[0].reshape(NCHUNK, 1, LANES)
    z_p, st_p = _ssm_p_call(h, wb.astype(BF16), wc_bf, a_tab, d_tab)
    z_s, sr_new, si_new = _ssm_s_call(
        h, state_ssm_re[0].reshape(DEC, NG * NS), state_ssm_im[0].reshape(DEC, NG * NS),
        wb, wc_bf, a_tab, d_tab)
    h, hp = _glu_ln_call(z_p, _tail_tile(z_p, z_s), w_glu[0].astype(BF16), b_glu[0][None, :], h,
                         ln_mix_g[1][None, :], ln_mix_b[1][None, :])
    y_prompt, y_sample = _moe_layer(1, True, h, hp, *moe_w)
    y_prompt = y_prompt.reshape(NB, SEQ, D)
    y_sample = y_sample.reshape(DEC, 1, D)
    st_p = st_p.reshape(NB, NCHUNK, 2, 8, NS)
    rp = st_p[:, :, 0].reshape(NB, NG, NS)
    ip = st_p[:, :, 1].reshape(NB, NG, NS)
    return (y_prompt, y_sample,
            kp[None], vp[None],
            ck_new.reshape(1, DEC, BLK, NKV, HD), cv_new.reshape(1, DEC, BLK, NKV, HD),
            rp[None], ip[None],
            sr_new.reshape(1, DEC, NG, NS), si_new.reshape(1, DEC, NG, NS))
```

```python
import functools
import math

import jax
import jax.numpy as jnp
import numpy as np
from jax import lax
from jax.experimental import pallas as pl
from jax.experimental.pallas import tpu as pltpu

F32 = jnp.float32
BF16 = jnp.bfloat16
I32 = jnp.int32
U32 = jnp.uint32

D = 2048
HALF = D // 2
NB = 4
N_META = 16
SEQ = 2048
L = N_META + SEQ
BLK = 128
PAD_FRONT = (-L) % BLK
LP = L + PAD_FRONT
NBLK = LP // BLK
T_PROMPT = NB * LP
DEC = 128
T_ALL = T_PROMPT + DEC
PAST_LEN = 8192
HD = 64
NH = 32
NKV = 4
QPK = NH // NKV
QKV = (NH + 2 * NKV) * HD
QK_COLS = (NH + NKV) * HD
ROT = HD // 4
ROT_HALF = ROT // 2
ROPE_THETA = 500000.0
NG = 128
GC = 16
NS = 64
NCHUNK = 16
CH_STATE = 8 * NS
NE = 64
TOPK = 8
NEG = 8
PER_GRP = NE // NEG
TOPG = 4
DE = 512
ROUTED_SCALE = 2.5
DEPTH = 2
ALPHA = (2 * DEPTH) ** 0.25
LN_EPS = 1e-5

V7X_VMEM_BYTES = 64 * 1024 * 1024
VMEM_LIMIT = 56 * 1024 * 1024
LANES = 128

TM = 384
N_ROW_TILES = T_ALL // TM
TAIL_START = (N_ROW_TILES - 1) * TM
TM_E = 256
N_PAIRS = T_ALL * TOPK
NT_E = -(-N_PAIRS // TM_E) + NE
R_ROWS = NT_E * TM_E
TOK_BITS = 14
TOK_MASK = (1 << TOK_BITS) - 1
DUMMY_BASE = TOPK * T_ALL
YK_ROWS = DUMMY_BASE + TM_E
assert T_ALL <= TOK_MASK and T_ALL % TOPK == 0 and T_ALL % TM == 0 and TAIL_START <= T_PROMPT


def _cparams(sem):
    return pltpu.CompilerParams(dimension_semantics=sem, vmem_limit_bytes=VMEM_LIMIT)


def _sigmoid(x):
    return 1.0 / (1.0 + jnp.exp(-x))


def _layer_norm(y, g, b):
    mu = jnp.mean(y, axis=-1, keepdims=True)
    yc = y - mu
    var = jnp.mean(yc * yc, axis=-1, keepdims=True)
    return yc * lax.rsqrt(var + LN_EPS) * g + b


def _pack_pair(lo, hi):
    lo_b = lax.bitcast_convert_type(lo.astype(BF16).astype(F32), U32) >> 16
    hi_b = lax.bitcast_convert_type(hi.astype(BF16).astype(F32), U32) & jnp.uint32(0xFFFF0000)
    return lo_b | hi_b


def _unpack_pair(w):
    lo = lax.bitcast_convert_type(w << 16, F32)
    hi = lax.bitcast_convert_type(w & jnp.uint32(0xFFFF0000), F32)
    return lo, hi


def _qkv_kernel(x_ref, w_ref, b_ref, c_ref, s1_ref, s2_ref, q_ref, k_ref, v_ref):
    xb = x_ref[...].astype(BF16)
    acc = jnp.dot(xb, w_ref[...], preferred_element_type=F32) + b_ref[...]
    c = c_ref[...]
    s1 = s1_ref[...]
    s2 = s2_ref[...]
    for j in range(QK_COLS // LANES):
        blk = acc[:, j * LANES:(j + 1) * LANES]
        r = (blk * c + pltpu.roll(blk, LANES - ROT_HALF, axis=1) * s1
             + pltpu.roll(blk, ROT_HALF, axis=1) * s2)
        if j < D // LANES:
            q_ref[:, j * LANES:(j + 1) * LANES] = (r * (1.0 / math.sqrt(HD))).astype(BF16)
        else:
            jj = j - D // LANES
            k_ref[:, jj * LANES:(jj + 1) * LANES] = r
    v_ref[...] = acc[:, QK_COLS:]


def _qkv_call(x, w_bf, b, rc, rs1, rs2):
    return pl.pallas_call(
        _qkv_kernel,
        grid=(T_ALL // TM,),
        in_specs=[
            pl.BlockSpec((TM, D), lambda i: (i, 0)),
            pl.BlockSpec((D, QKV), lambda i: (0, 0)),
            pl.BlockSpec((1, QKV), lambda i: (0, 0)),
            pl.BlockSpec((TM, LANES), lambda i: (i, 0)),
            pl.BlockSpec((TM, LANES), lambda i: (i, 0)),
            pl.BlockSpec((TM, LANES), lambda i: (i, 0)),
        ],
        out_specs=[
            pl.BlockSpec((TM, D), lambda i: (i, 0)),
            pl.BlockSpec((TM, NKV * HD), lambda i: (i, 0)),
            pl.BlockSpec((TM, NKV * HD), lambda i: (i, 0)),
        ],
        out_shape=[
            jax.ShapeDtypeStruct((T_ALL, D), BF16),
            jax.ShapeDtypeStruct((T_ALL, NKV * HD), F32),
            jax.ShapeDtypeStruct((T_ALL, NKV * HD), F32),
        ],
        compiler_params=_cparams(("parallel",)),
        name="qkv_rope",
    )(x, w_bf, b, rc, rs1, rs2)


def _rope_tables():
    pos_p = jnp.maximum(jnp.arange(LP, dtype=I32) - PAD_FRONT, 0)
    pos = jnp.concatenate([jnp.tile(pos_p, NB), jnp.full((DEC,), PAST_LEN, I32)]).astype(F32)
    inv_freq = ROPE_THETA ** (-jnp.arange(0, ROT, 2, dtype=F32) / ROT)
    ang = pos[:, None] * inv_freq[None, :]
    cos = jnp.cos(ang)
    sin = jnp.sin(ang)
    lane = np.arange(LANES) % HD
    freq = np.arange(ROT_HALF)[:, None]
    first = ((lane[None, :] == freq) & (lane[None, :] < ROT_HALF)).astype(np.float32)
    second = ((lane[None, :] - ROT_HALF == freq) & (lane[None, :] < ROT)).astype(np.float32)
    rest = (lane >= ROT).astype(np.float32)[None, :]
    place = functools.partial(jnp.dot, precision=lax.Precision.HIGHEST)
    c = place(cos, jnp.asarray(first + second)) + jnp.asarray(rest)
    s1 = place(sin, jnp.asarray(-first))
    s2 = place(sin, jnp.asarray(second))
    return c, s1, s2


def _attn_p_kernel(sink_ref, q_ref, kp_ref, kc_ref, vp_ref, vc_ref, o_ref):
    j = pl.program_id(0) % NBLK
    keys = jnp.concatenate([kp_ref[...], kc_ref[...]], axis=0).astype(BF16)
    vals = jnp.concatenate([vp_ref[...], vc_ref[...]], axis=0).astype(BF16)
    r = lax.broadcasted_iota(I32, (BLK, 2 * BLK), 0)
    c = lax.broadcasted_iota(I32, (BLK, 2 * BLK), 1)
    dist = BLK + r - c
    kpos = (j - 1) * BLK - PAD_FRONT + c
    mask = (dist >= 0) & (dist <= BLK) & (kpos >= 0)
    for g in range(NKV):
        kg = keys[:, g * HD:(g + 1) * HD]
        vg = vals[:, g * HD:(g + 1) * HD]
        heads = range(g * QPK, (g + 1) * QPK)
        scores = [lax.dot_general(q_ref[:, h * HD:(h + 1) * HD], kg, (((1,), (1,)), ((), ())),
                                  preferred_element_type=F32) for h in heads]
        probs, rdens = [], []
        for h, s in zip(heads, scores):
            s = jnp.where(mask, s, -jnp.inf)
            sk = sink_ref[h]
            m = jnp.maximum(jnp.max(s, axis=1, keepdims=True), sk)
            p = jnp.exp(s - m)
            rdens.append(1.0 / (jnp.sum(p, axis=1, keepdims=True) + jnp.exp(sk - m)))
            probs.append(p.astype(BF16))
        for h, p, rden in zip(heads, probs, rdens):
            oh = jnp.dot(p, vg, preferred_element_type=F32) * rden
            o_ref[:, h * HD:(h + 1) * HD] = oh.astype(BF16)


def _attn_p_call(sinks, q, k, v):
    prev = lambda i: (jnp.where(i % NBLK == 0, i, i - 1), 0)
    cur = lambda i: (i, 0)
    return pl.pallas_call(
        _attn_p_kernel,
        grid=(NB * NBLK,),
        in_specs=[
            pl.BlockSpec(memory_space=pltpu.SMEM),
            pl.BlockSpec((BLK, D), cur),
            pl.BlockSpec((BLK, NKV * HD), prev),
            pl.BlockSpec((BLK, NKV * HD), cur),
            pl.BlockSpec((BLK, NKV * HD), prev),
            pl.BlockSpec((BLK, NKV * HD), cur),
        ],
        out_specs=pl.BlockSpec((BLK, D), cur),
        out_shape=jax.ShapeDtypeStruct((T_PROMPT, D), BF16),
        compiler_params=_cparams(("parallel",)),
        name="attn_prompt",
    )(sinks, q, k, k, v, v)


SEQ_PER_STEP = 16
SEQ_UNROLL = 4


def _attn_s_kernel(sink_ref, q_ref, kn_ref, vn_ref, ck_ref, cv_ref, o_ref, cko_ref, cvo_ref):
    row = lax.broadcasted_iota(I32, (BLK, NKV * HD), 0)
    hrow = lax.broadcasted_iota(I32, (NH, NKV * HD), 0) // QPK
    hlane = lax.broadcasted_iota(I32, (NH, NKV * HD), 1) // HD
    own = hrow == hlane
    sk = sink_ref[...]

    def score_stage(s):
        kn = kn_ref[pl.ds(s, 1), :]
        knr = kn.astype(BF16).astype(F32)
        q = q_ref[s].astype(BF16)
        qe = jnp.where(own, jnp.concatenate([q] * NKV, axis=1), jnp.zeros((), BF16))
        sc = lax.dot_general(qe, ck_ref[s].astype(BF16), (((1,), (1,)), ((), ())),
                             preferred_element_type=F32)
        sn = jnp.sum(qe.astype(F32) * knr, axis=1, keepdims=True)
        return sc, sn

    def softmax_stage(sc, sn):
        m = jnp.maximum(jnp.maximum(jnp.max(sc, axis=1, keepdims=True), sn), sk)
        p = jnp.exp(sc - m)
        pn = jnp.exp(sn - m)
        rden = 1.0 / (jnp.sum(p, axis=1, keepdims=True) + pn + jnp.exp(sk - m))
        return p.astype(BF16), pn.astype(BF16).astype(F32), rden

    def value_stage(s, p, pn, rden):
        vn = vn_ref[pl.ds(s, 1), :]
        vnr = vn.astype(BF16).astype(F32)
        of = jnp.dot(p, cv_ref[s].astype(BF16), preferred_element_type=F32)
        of = jnp.where(own, of + pn * vnr, 0.0)
        og = of[:, 0:HD]
        for g in range(1, NKV):
            og = og + of[:, g * HD:(g + 1) * HD]
        o_ref[s] = og * rden
        cko_ref[s] = jnp.where(row == BLK - 1, kn_ref[pl.ds(s, 1), :], pltpu.roll(ck_ref[s], BLK - 1, axis=0))
        cvo_ref[s] = jnp.where(row == BLK - 1, vn, pltpu.roll(cv_ref[s], BLK - 1, axis=0))

    def body(it, carry):
        seqs = [it * SEQ_UNROLL + u for u in range(SEQ_UNROLL)]
        staged = [score_stage(s) for s in seqs]
        soft = [softmax_stage(*st) for st in staged]
        for s, sm in zip(seqs, soft):
            value_stage(s, *sm)
        return carry

    lax.fori_loop(0, SEQ_PER_STEP // SEQ_UNROLL, body, 0)


def _attn_s_call(sinks_col, q3, k, v, cache_k, cache_v):
    sp = SEQ_PER_STEP
    kv_off = T_PROMPT // sp
    return pl.pallas_call(
        _attn_s_kernel,
        grid=(DEC // sp,),
        in_specs=[
            pl.BlockSpec((NH, 1), lambda i: (0, 0)),
            pl.BlockSpec((sp, NH, HD), lambda i: (i, 0, 0)),
            pl.BlockSpec((sp, NKV * HD), lambda i: (kv_off + i, 0)),
            pl.BlockSpec((sp, NKV * HD), lambda i: (kv_off + i, 0)),
            pl.BlockSpec((sp, BLK, NKV * HD), lambda i: (i, 0, 0)),
            pl.BlockSpec((sp, BLK, NKV * HD), lambda i: (i, 0, 0)),
        ],
        out_specs=[
            pl.BlockSpec((sp, NH, HD), lambda i: (i, 0, 0)),
            pl.BlockSpec((sp, BLK, NKV * HD), lambda i: (i, 0, 0)),
            pl.BlockSpec((sp, BLK, NKV * HD), lambda i: (i, 0, 0)),
        ],
        out_shape=[
            jax.ShapeDtypeStruct((DEC, NH, HD), F32),
            jax.ShapeDtypeStruct((DEC, BLK, NKV * HD), F32),
            jax.ShapeDtypeStruct((DEC, BLK, NKV * HD), F32),
        ],
        compiler_params=_cparams(("parallel",)),
        name="attn_sample",
    )(sinks_col, q3, k, v, cache_k, cache_v)


def _prompt_or_tail(x_ref, tail_ref):
    return jnp.where(pl.program_id(0) == N_ROW_TILES - 1, tail_ref[...], x_ref[...])


def _tail_tile(x_prompt_rows, x_sample_rows):
    return jnp.concatenate([x_prompt_rows[TAIL_START:], x_sample_rows], axis=0)


_PROMPT_TILE = lambda i: (jnp.minimum(i, N_ROW_TILES - 2), 0)


def _oproj_ln_kernel(o_ref, ot_ref, w_ref, bo_ref, h_ref, g_ref, b_ref, out_ref, hp_ref):
    o = _prompt_or_tail(o_ref, ot_ref)
    m = jnp.dot(o, w_ref[...], preferred_element_type=F32) + bo_ref[...]
    out = _layer_norm(ALPHA * h_ref[...] + m, g_ref[...], b_ref[...])
    out_ref[...] = out
    hp_ref[...] = _pack_pair(out[:, :HALF], out[:, HALF:])


def _oproj_ln_call(o_p, o_tail, w_bf, bo, h, g, b):
    vec = pl.BlockSpec((1, D), lambda i: (0, 0))
    return pl.pallas_call(
        _oproj_ln_kernel,
        grid=(N_ROW_TILES,),
        in_specs=[
            pl.BlockSpec((TM, D), _PROMPT_TILE),
            pl.BlockSpec((TM, D), lambda i: (0, 0)),
            pl.BlockSpec((D, D), lambda i: (0, 0)),
            vec,
            pl.BlockSpec((TM, D), lambda i: (i, 0)),
            vec,
            vec,
        ],
        out_specs=[pl.BlockSpec((TM, D), lambda i: (i, 0)), pl.BlockSpec((TM, HALF), lambda i: (i, 0))],
        out_shape=[jax.ShapeDtypeStruct((T_ALL, D), F32), jax.ShapeDtypeStruct((T_ALL, HALF), U32)],
        compiler_params=_cparams(("parallel",)),
        name="oproj_ln",
    )(o_p, o_tail, w_bf, bo, h, g, b)


def _router_kernel(h_ref, wr_ref, rb_ref, eidx_ref, wsel_ref, rank_ref, cnt_ref, carry_ref):
    i = pl.program_id(0)

    @pl.when(i == 0)
    def _():
        carry_ref[...] = jnp.zeros_like(carry_ref)

    h = h_ref[...]
    w = wr_ref[...]
    h_hi = h.astype(BF16)
    h_lo = (h - h_hi.astype(F32)).astype(BF16)
    w_hi = w.astype(BF16)
    w_lo = (w - w_hi.astype(F32)).astype(BF16)
    logits = (jnp.dot(h_hi, w_hi, preferred_element_type=F32)
              + (jnp.dot(h_hi, w_lo, preferred_element_type=F32)
                 + jnp.dot(h_lo, w_hi, preferred_element_type=F32)))
    scores = _sigmoid(logits.T[0:NE, :])
    biased = scores + rb_ref[...]
    ninf = -jnp.inf
    sub = lax.broadcasted_iota(I32, (PER_GRP, TM), 0).astype(F32)
    sc_g = [scores[g * PER_GRP:(g + 1) * PER_GRP, :] for g in range(NEG)]
    b_g = [biased[g * PER_GRP:(g + 1) * PER_GRP, :] for g in range(NEG)]
    e_g = [sub + float(g * PER_GRP) for g in range(NEG)]

    def smax(x):
        return jnp.max(x, axis=0, keepdims=True)

    def smin(x):
        return jnp.min(x, axis=0, keepdims=True)

    gs = []
    for g in range(NEG):
        m1 = smax(b_g[g])
        i1 = smin(jnp.where(b_g[g] == m1, sub, float(PER_GRP)))
        m2 = smax(jnp.where(sub == i1, ninf, b_g[g]))
        gs.append(m1 + m2)

    work = []
    for g in range(NEG):
        beaten = jnp.zeros((1, TM), F32)
        for o in range(NEG):
            if o != g:
                wins = (gs[o] >= gs[g]) if o < g else (gs[o] > gs[g])
                beaten = beaten + jnp.where(wins, 1.0, 0.0)
        work.append(jnp.where(beaten < float(TOPG), b_g[g], ninf))

    idx_rows, w_rows = [], []
    onehot = [jnp.zeros((PER_GRP, TM), F32) for _ in range(NEG)]
    for _ in range(TOPK):
        m = smax(work[0])
        for g in range(1, NEG):
            m = jnp.maximum(m, smax(work[g]))
        ik = smin(jnp.where(work[0] == m, e_g[0], float(NE)))
        for g in range(1, NEG):
            ik = jnp.minimum(ik, smin(jnp.where(work[g] == m, e_g[g], float(NE))))
        wk = jnp.zeros((1, TM), F32)
        for g in range(NEG):
            hit = e_g[g] == ik
            wk = wk + jnp.sum(jnp.where(hit, sc_g[g], 0.0), axis=0, keepdims=True)
            onehot[g] = jnp.where(hit, 1.0, onehot[g])
            work[g] = jnp.where(hit, ninf, work[g])
        idx_rows.append(ik)
        w_rows.append(wk)
    wsum = w_rows[0]
    for wk in w_rows[1:]:
        wsum = wsum + wk

    rr = lax.broadcasted_iota(I32, (TM, TM), 0)
    cc = lax.broadcasted_iota(I32, (TM, TM), 1)
    tri = jnp.where(rr < cc, 1.0, 0.0).astype(BF16)
    oh = jnp.concatenate(onehot, axis=0)
    prefix = jnp.dot(oh.astype(BF16), tri, preferred_element_type=F32) + carry_ref[...]
    carry_ref[...] = carry_ref[...] + jnp.sum(oh, axis=1, keepdims=True)
    cnt_ref[...] = carry_ref[...]

    rank_rows = []
    for k in range(TOPK):
        rk = jnp.zeros((1, TM), F32)
        for g in range(NEG):
            pg = prefix[g * PER_GRP:(g + 1) * PER_GRP, :]
            rk = rk + jnp.sum(jnp.where(e_g[g] == idx_rows[k], pg, 0.0), axis=0, keepdims=True)
        rank_rows.append(rk)
    eidx_ref[...] = jnp.concatenate(idx_rows, axis=0).astype(I32)
    rank_ref[...] = jnp.concatenate(rank_rows, axis=0).astype(I32)
    w_t = jnp.concatenate([wk / wsum * ROUTED_SCALE for wk in w_rows]
                          + [jnp.zeros((LANES - TOPK, TM), F32)], axis=0)
    wsel_ref[...] = w_t.T[:, 0:TOPK]


def _router_call(h, w_router, router_bias):
    tk = pl.BlockSpec((TM, TOPK), lambda i: (i, 0))
    kt = pl.BlockSpec((TOPK, TM), lambda i: (0, i))
    return pl.pallas_call(
        _router_kernel,
        grid=(T_ALL // TM,),
        in_specs=[
            pl.BlockSpec((TM, D), lambda i: (i, 0)),
            pl.BlockSpec((D, LANES), lambda i: (0, 0)),
            pl.BlockSpec((NE, 1), lambda i: (0, 0)),
        ],
        out_specs=[kt, tk, kt, pl.BlockSpec((NE, 1), lambda i: (0, 0))],
        out_shape=[
            jax.ShapeDtypeStruct((TOPK, T_ALL), I32),
            jax.ShapeDtypeStruct((T_ALL, TOPK), F32),
            jax.ShapeDtypeStruct((TOPK, T_ALL), I32),
            jax.ShapeDtypeStruct((NE, 1), F32),
        ],
        scratch_shapes=[pltpu.VMEM((NE, 1), F32)],
        compiler_params=_cparams(("arbitrary",)),
        name="router",
    )(h, jnp.pad(w_router, ((0, 0), (0, LANES - NE))), router_bias[:, None])


def _tiles_of(cnt):
    return (cnt + TM_E - 1) // TM_E


def _plan_kernel(cnt_ref, eidx_ref, rank_ref, pos_ref, te_ref, gi_ref, ne_ref, na_ref, off_s):
    def offsets(e, run):
        off_s[e] = run
        return run + _tiles_of(cnt_ref[e]) * TM_E

    total = lax.fori_loop(0, NE, offsets, 0)
    na = total // TM_E
    na_ref[0] = na

    def idle(t, carry):
        te_ref[t] = NE - 1
        gi_ref[t] = 0
        ne_ref[t] = -1
        return carry

    lax.fori_loop(na, NT_E, idle, 0)

    def forward(e, ordinal):
        t0 = off_s[e] // TM_E
        nt = _tiles_of(cnt_ref[e])

        def mark(t, carry):
            te_ref[t] = e
            gi_ref[t] = ordinal
            return carry

        lax.fori_loop(t0, t0 + nt, mark, 0)
        return ordinal + jnp.where(nt > 0, 1, 0)

    lax.fori_loop(0, NE, forward, 0)

    def backward(i, nxt):
        e = NE - 1 - i
        t0 = off_s[e] // TM_E
        nt = _tiles_of(cnt_ref[e])

        def mark(t, carry):
            ne_ref[t] = nxt
            return carry

        lax.fori_loop(t0, t0 + nt, mark, 0)
        return jnp.where(nt > 0, e, nxt)

    lax.fori_loop(0, NE, backward, -1)

    eidx = eidx_ref[...]
    pos = rank_ref[...]
    for e in range(NE):
        pos = pos + jnp.where(eidx == e, off_s[e], 0)
    pos_ref[...] = pos


def _plan_call(cnt, eidx_t, rank_t):
    smem = pl.BlockSpec(memory_space=pltpu.SMEM)
    vmem = pl.BlockSpec(memory_space=pltpu.VMEM)
    tiles = jax.ShapeDtypeStruct((NT_E,), I32)
    return pl.pallas_call(
        _plan_kernel,
        in_specs=[smem, vmem, vmem],
        out_specs=[vmem, smem, smem, smem, smem],
        out_shape=[jax.ShapeDtypeStruct((TOPK, T_ALL), I32), tiles, tiles, tiles,
                   jax.ShapeDtypeStruct((1,), I32)],
        scratch_shapes=[pltpu.SMEM((NE,), I32)],
        name="moe_plan",
    )(cnt, eidx_t, rank_t)


INFO_UNROLL = 16
WORD_STEP = (1 << TOK_BITS) + 1
assert T_ALL % INFO_UNROLL == 0


def _rowinfo_kernel(pos_ref, pad_ref, info_ref, sem):
    slot = pl.program_id(0)

    @pl.when(slot == 0)
    def _():
        cp = pltpu.make_async_copy(pad_ref, info_ref, sem)
        cp.start()
        cp.wait()

    word0 = (slot * T_ALL) << TOK_BITS

    def body(j, carry):
        base = j * INFO_UNROLL
        wj = word0 + base * WORD_STEP
        for u in range(INFO_UNROLL):
            info_ref[pos_ref[0, 0, base + u]] = wj + u * WORD_STEP
        return carry

    lax.fori_loop(0, T_ALL // INFO_UNROLL, body, 0)


def _rowinfo_call(pos_t):
    rows = np.arange(R_ROWS, dtype=np.int64)
    pads = jnp.asarray(((DUMMY_BASE + (rows & (TM_E - 1))) << TOK_BITS).astype(np.int32))
    return pl.pallas_call(
        _rowinfo_kernel,
        grid=(TOPK,),
        in_specs=[pl.BlockSpec((1, 1, T_ALL), lambda i: (i, 0, 0), memory_space=pltpu.SMEM),
                  pl.BlockSpec(memory_space=pl.ANY)],
        out_specs=pl.BlockSpec(memory_space=pltpu.SMEM),
        out_shape=jax.ShapeDtypeStruct((R_ROWS,), I32),
        scratch_shapes=[pltpu.SemaphoreType.DMA(())],
        compiler_params=pltpu.CompilerParams(dimension_semantics=("arbitrary",)),
        name="moe_rowinfo",
    )(pos_t.reshape(TOPK, 1, T_ALL), pads)


def _expert_kernel(layer, te_ref, gi_ref, ne_ref, na_ref, ginfo_ref, sinfo_ref, hp_ref,
                   wg_hbm, wu_hbm, wd_hbm, yk_ref,
                   xb0, xb1, yb0, yb1, wgf, wuf, wdf, wg_s, wu_s, wd_s, gsem, ssem, wsem, zsem):
    s = pl.program_id(0)
    na = na_ref[0]
    xb = (xb0, xb1)
    yb = (yb0, yb1)

    def weight_copies(e, slot):
        return (pltpu.make_async_copy(wg_hbm.at[layer, e], wgf.at[slot], wsem.at[0]),
                pltpu.make_async_copy(wu_hbm.at[layer, e], wuf.at[slot], wsem.at[1]),
                pltpu.make_async_copy(wd_hbm.at[layer, e], wdf.at[slot], wsem.at[2]))

    def gather_row(par, r, thread=0):
        tok = ginfo_ref[0, 0, r] & TOK_MASK
        pltpu.make_async_copy(hp_ref.at[pl.ds(tok, 1)], xb[par].at[pl.ds(r, 1)], gsem.at[par]).start(
            priority=thread)

    def scatter_row(par, r, thread=0):
        dst = sinfo_ref[0, 0, r] >> TOK_BITS
        pltpu.make_async_copy(yb[par].at[pl.ds(r, 1)], yk_ref.at[pl.ds(dst, 1)], ssem.at[par]).start(
            priority=thread)

    def compute(par, between):
        lo, hi = _unpack_pair(xb[1 - par][...])
        lo = lo.astype(BF16)
        hi = hi.astype(BF16)
        between(0)
        hg = (jnp.dot(lo, wg_s[0:HALF, :], preferred_element_type=F32)
              + jnp.dot(hi, wg_s[HALF:D, :], preferred_element_type=F32))
        between(1)
        hu = (jnp.dot(lo, wu_s[0:HALF, :], preferred_element_type=F32)
              + jnp.dot(hi, wu_s[HALF:D, :], preferred_element_type=F32))
        between(2)
        act = (hg * _sigmoid(hg) * hu).astype(BF16)
        y = jnp.dot(act, wd_s[...], preferred_element_type=F32)
        between(3)
        yb[1 - par][...] = _pack_pair(y[:, :HALF], y[:, HALF:])

    @pl.when(jnp.logical_and(s >= 1, s <= na))
    def _():
        pltpu.make_async_copy(hp_ref.at[pl.ds(0, TM_E)], xb0, gsem.at[(s + 1) % 2]).wait()

    @pl.when(jnp.logical_and(s >= 3, s <= na + 2))
    def _():
        pltpu.make_async_copy(yb0, yk_ref.at[pl.ds(0, TM_E)], ssem.at[(s + 1) % 2]).wait()

    @pl.when(s == 0)
    def _():
        for cp in weight_copies(te_ref[0], 0):
            cp.start(priority=1)
        yb0[...] = jnp.zeros_like(yb0)
        zc = pltpu.make_async_copy(yb0, yk_ref.at[pl.ds(DUMMY_BASE, TM_E)], zsem)
        zc.start()
        zc.wait()

    c = s - 1
    cc = jnp.clip(c, 0, na - 1)
    first = jnp.logical_or(c == 0, te_ref[cc] != te_ref[jnp.maximum(cc - 1, 0)])

    @pl.when(jnp.logical_and(jnp.logical_and(c >= 0, c < na), first))
    def _():
        slot = gi_ref[cc] % 2
        for cp in weight_copies(te_ref[cc], slot):
            cp.wait()
        wg_s[...] = wgf[slot].astype(BF16)
        wu_s[...] = wuf[slot].astype(BF16)
        wd_s[...] = wdf[slot].astype(BF16)
        nxt = ne_ref[cc]

        @pl.when(nxt >= 0)
        def _():
            for cp in weight_copies(nxt, 1 - slot):
                cp.start(priority=1)

    steady = jnp.logical_and(s >= 2, s < na)
    for par in (0, 1):
        mine = (s % 2) == par

        @pl.when(jnp.logical_and(steady, mine))
        def _():
            def between(q):
                for r in range(q * (TM_E // 4), (q + 1) * (TM_E // 4)):
                    gather_row(par, r)
                    scatter_row(par, r, thread=r % 2)

            compute(par, between)

        @pl.when(jnp.logical_and(jnp.logical_not(steady), mine))
        def _():
            @pl.when(s < na)
            def _():
                def g(r, carry):
                    gather_row(par, r)
                    return carry

                lax.fori_loop(0, TM_E, g, 0)

            @pl.when(jnp.logical_and(s >= 1, s <= na))
            def _():
                compute(par, lambda q: None)

            @pl.when(jnp.logical_and(s >= 2, s <= na + 1))
            def _():
                def sc(r, carry):
                    scatter_row(par, r)
                    return carry

                lax.fori_loop(0, TM_E, sc, 0)


def _expert_call(layer, te, gi, ne, nact, rowinfo, hp, w_gate, w_up, w_down):
    def gmap(s, te_r, gi_r, ne_r, na_r):
        return (jnp.minimum(s, na_r[0] - 1), 0, 0)

    def smap(s, te_r, gi_r, ne_r, na_r):
        return (jnp.clip(s - 2, 0, na_r[0] - 1), 0, 0)

    info = rowinfo.reshape(NT_E, 1, TM_E)
    anyspec = pl.BlockSpec(memory_space=pl.ANY)
    return pl.pallas_call(
        functools.partial(_expert_kernel, layer),
        grid_spec=pltpu.PrefetchScalarGridSpec(
            num_scalar_prefetch=4,
            grid=(NT_E + 3,),
            in_specs=[
                pl.BlockSpec((1, 1, TM_E), gmap, memory_space=pltpu.SMEM),
                pl.BlockSpec((1, 1, TM_E), smap, memory_space=pltpu.SMEM),
                anyspec, anyspec, anyspec, anyspec,
            ],
            out_specs=anyspec,
            scratch_shapes=[
                pltpu.VMEM((TM_E, HALF), U32),
                pltpu.VMEM((TM_E, HALF), U32),
                pltpu.VMEM((TM_E, HALF), U32),
                pltpu.VMEM((TM_E, HALF), U32),
                pltpu.VMEM((2, D, DE), F32),
                pltpu.VMEM((2, D, DE), F32),
                pltpu.VMEM((2, DE, D), F32),
                pltpu.VMEM((D, DE), BF16),
                pltpu.VMEM((D, DE), BF16),
                pltpu.VMEM((DE, D), BF16),
                pltpu.SemaphoreType.DMA((2,)),
                pltpu.SemaphoreType.DMA((2,)),
                pltpu.SemaphoreType.DMA((3,)),
                pltpu.SemaphoreType.DMA(()),
            ],
        ),
        out_shape=jax.ShapeDtypeStruct((YK_ROWS, HALF), U32),
        compiler_params=_cparams(("arbitrary",)),
        name="moe_experts",
    )(te, gi, ne, nact, info, info, hp, w_gate, w_up, w_down)


BLK_PER_TILE = TM // BLK


def _final_copies(step, obuf, yp_ref, ys_ref, sem):
    out = []
    slot = step % 2
    for m in range(BLK_PER_TILE):
        g = step * BLK_PER_TILE + m
        b = g // NBLK
        j = g % NBLK
        src = obuf.at[slot, pl.ds(m * BLK, BLK)]
        r0 = pl.multiple_of(jnp.maximum(b * SEQ + (j - 1) * BLK, 0), BLK)
        out.append((jnp.logical_and(g < NB * NBLK, j >= 1),
                    pltpu.make_async_copy(src, yp_ref.at[pl.ds(r0, BLK)], sem.at[slot])))
        out.append((g == NB * NBLK, pltpu.make_async_copy(src, ys_ref, sem.at[slot])))
    return out


def _combine_kernel(final, w_ref, h_ref, *rest):
    yk_refs = rest[:TOPK]
    wsg_ref, wsu_ref, wsd_ref, g_ref, b_ref = rest[TOPK:TOPK + 5]
    outs = rest[TOPK + 5:]
    i = pl.program_id(0)
    if final:
        yp_ref, ys_ref, obuf, osem = outs

        def wait_step(step):
            for cond, cp in _final_copies(step, obuf, yp_ref, ys_ref, osem):
                @pl.when(cond)
                def _():
                    cp.wait()

        @pl.when(i >= 2)
        def _():
            wait_step(i - 2)

    h = h_ref[...]
    hb = h.astype(BF16)
    sg = jnp.dot(hb, wsg_ref[...], preferred_element_type=F32)
    su = jnp.dot(hb, wsu_ref[...], preferred_element_type=F32)
    act = (sg * _sigmoid(sg) * su).astype(BF16)
    y = ALPHA * h + jnp.dot(act, wsd_ref[...], preferred_element_type=F32)
    ylo = y[:, :HALF]
    yhi = y[:, HALF:]
    w = w_ref[...]
    for k in range(TOPK):
        lo, hi = _unpack_pair(yk_refs[k][...])
        wk = w[:, k:k + 1]
        ylo = ylo + wk * lo
        yhi = yhi + wk * hi
    out = _layer_norm(jnp.concatenate([ylo, yhi], axis=1), g_ref[...], b_ref[...])
    if not final:
        outs[0][...] = out
        return
    obuf[i % 2] = out
    for cond, cp in _final_copies(i, obuf, yp_ref, ys_ref, osem):
        @pl.when(cond)
        def _():
            cp.start()

    @pl.when(i == N_ROW_TILES - 1)
    def _():
        wait_step(i - 1)
        wait_step(i)


def _combine_call(final, wsel, h, yk, wsg, wsu, wsd, g, b):
    vec = pl.BlockSpec((1, D), lambda i: (0, 0))
    once = dict(pipeline_mode=pl.Buffered(1))
    yk_specs = [pl.BlockSpec((TM, HALF), lambda i, k=k: (k * N_ROW_TILES + i, 0)) for k in range(TOPK)]
    if final:
        anyspec = pl.BlockSpec(memory_space=pl.ANY)
        out_specs = [anyspec, anyspec]
        out_shape = [jax.ShapeDtypeStruct((NB * SEQ, D), F32), jax.ShapeDtypeStruct((DEC, D), F32)]
        scratch = [pltpu.VMEM((2, TM, D), F32), pltpu.SemaphoreType.DMA((2,))]
    else:
        out_specs = pl.BlockSpec((TM, D), lambda i: (i, 0))
        out_shape = jax.ShapeDtypeStruct((T_ALL, D), F32)
        scratch = []
    return pl.pallas_call(
        functools.partial(_combine_kernel, final),
        grid=(N_ROW_TILES,),
        in_specs=[
            pl.BlockSpec((TM, TOPK), lambda i: (i, 0)),
            pl.BlockSpec((TM, D), lambda i: (i, 0)),
            *yk_specs,
            pl.BlockSpec((D, DE), lambda i: (0, 0), **once),
            pl.BlockSpec((D, DE), lambda i: (0, 0), **once),
            pl.BlockSpec((DE, D), lambda i: (0, 0), **once),
            vec,
            vec,
        ],
        out_specs=out_specs,
        out_shape=out_shape,
        scratch_shapes=scratch,
        compiler_params=_cparams(("arbitrary",)),
        name="moe_combine_final" if final else "moe_combine",
    )(wsel, h, *([yk] * TOPK), wsg, wsu, wsd, g, b)


def _moe_layer(layer, final, h, hp, w_router, router_bias, w_exp_gate, w_exp_up, w_exp_down,
               w_sh_gate, w_sh_up, w_sh_down, ln_g, ln_b):
    eidx_t, wsel, rank_t, counts = _router_call(h, w_router[layer], router_bias[layer])
    pos_t, te, gi, ne, nact = _plan_call(counts[:, 0].astype(I32), eidx_t, rank_t)
    rowinfo = _rowinfo_call(pos_t)
    yk = _expert_call(layer, te, gi, ne, nact, rowinfo, hp, w_exp_gate, w_exp_up, w_exp_down)
    return _combine_call(
        final, wsel, h, yk,
        w_sh_gate[layer].astype(BF16), w_sh_up[layer].astype(BF16), w_sh_down[layer].astype(BF16),
        ln_g[layer][None, :], ln_b[layer][None, :])


def _ssm_prep_kernel(lr_ref, li_ref, ldt_ref, br_ref, bi_ref, abr_ref, abi_ref, bbr_ref, bbi_ref):
    lr = lr_ref[...]
    li = li_ref[...]
    dt = jnp.exp(ldt_ref[...])
    mag = jnp.exp(lr * dt)
    ab_re = mag * jnp.cos(li * dt)
    ab_im = mag * jnp.sin(li * dt)
    den = lr * lr + li * li
    nr = ab_re - 1.0
    ni = ab_im
    cr = (nr * lr + ni * li) / den
    ci = (ni * lr - nr * li) / den
    br = br_ref[...]
    bi = bi_ref[...]
    abr_ref[...] = ab_re
    abi_ref[...] = ab_im
    bbr_ref[...] = cr * br - ci * bi
    bbi_ref[...] = cr * bi + ci * br


def _ssm_prep_call(lam_re, lam_im, log_dt, b_re, b_im):
    wide = (NG, GC * NS)
    lr = jnp.tile(lam_re, (1, GC))
    li = jnp.tile(lam_im, (1, GC))
    ldt = jnp.broadcast_to(log_dt[:, None], wide)
    br = jnp.transpose(b_re, (0, 2, 1)).reshape(wide)
    bi = jnp.transpose(b_im, (0, 2, 1)).reshape(wide)
    sds = jax.ShapeDtypeStruct(wide, F32)
    return pl.pallas_call(
        _ssm_prep_kernel, out_shape=[sds, sds, sds, sds], name="ssm_prep",
        compiler_params=pltpu.CompilerParams(vmem_limit_bytes=VMEM_LIMIT),
    )(lr, li, ldt, br, bi)


def _cmul(ar, ai, xr, xi):
    return ar * xr - ai * xi, ar * xi + ai * xr


NSEG = 8
SEGL = LP // NSEG
STEP_UNROLL = 4
MOVE_UNROLL = 8
assert LP == NSEG * SEGL and SEGL % STEP_UNROLL == 0 and SEGL % MOVE_UNROLL == 0 and PAD_FRONT < SEGL


def _ssm_p_kernel(u_ref, wb_ref, wc_ref, a_ref, d_ref, z_ref, st_ref, u_scr, s_scr, y_scr):
    def interleave(it, carry):
        for q in range(MOVE_UNROLL):
            t = it * MOVE_UNROLL + q
            u_scr[pl.ds(pl.multiple_of(t * NSEG, NSEG), NSEG), :] = u_ref[pl.ds(t, NSEG, stride=SEGL), :]
        return carry

    lax.fori_loop(0, SEGL // MOVE_UNROLL, interleave, 0)
    row = lax.broadcasted_iota(I32, (LP, LANES), 0)
    is_pad = jnp.logical_and(row % NSEG == 0, row // NSEG < PAD_FRONT)
    u = jnp.where(is_pad, 0.0, u_scr[...])
    s_scr[...] = jnp.dot(u.astype(BF16), wb_ref[0], preferred_element_type=F32)

    ar = a_ref[0, 0:1, :]
    ai = a_ref[0, 1:2, :]

    def group(t):
        r0 = pl.multiple_of(t * NSEG, NSEG)
        return pl.ds(r0, NSEG)

    def local_scan(it, carry):
        sr, si = carry
        for q in range(STEP_UNROLL):
            g = group(it * STEP_UNROLL + q)
            tr, ti = _cmul(ar, ai, sr, si)
            sr = tr + s_scr[g, 0:CH_STATE]
            si = ti + s_scr[g, CH_STATE:]
            s_scr[g, 0:CH_STATE] = sr
            s_scr[g, CH_STATE:] = si
        return sr, si

    zero8 = jnp.zeros((NSEG, CH_STATE), F32)
    er, ei = lax.fori_loop(0, SEGL // STEP_UNROLL, local_scan, (zero8, zero8))

    pr, pi = ar, ai
    acc = None
    bits = SEGL
    while bits:
        if bits & 1:
            acc = (pr, pi) if acc is None else _cmul(pr, pi, *acc)
        bits >>= 1
        if bits:
            pr, pi = _cmul(pr, pi, pr, pi)
    alr, ali = acc
    cr = jnp.zeros((1, CH_STATE), F32)
    ci = jnp.zeros((1, CH_STATE), F32)
    ins_r, ins_i = [], []
    for j in range(NSEG):
        ins_r.append(cr)
        ins_i.append(ci)
        tr, ti = _cmul(alr, ali, cr, ci)
        cr = tr + er[j:j + 1, :]
        ci = ti + ei[j:j + 1, :]
    st_ref[0, 0, :, 0:CH_STATE] = cr
    st_ref[0, 0, :, CH_STATE:] = ci

    def fixup(it, carry):
        dr, di = carry
        for q in range(STEP_UNROLL):
            g = group(it * STEP_UNROLL + q)
            dr, di = _cmul(ar, ai, dr, di)
            s_scr[g, 0:CH_STATE] = s_scr[g, 0:CH_STATE] + dr
            s_scr[g, CH_STATE:] = s_scr[g, CH_STATE:] + di
        return dr, di

    lax.fori_loop(0, SEGL // STEP_UNROLL, fixup,
                  (jnp.concatenate(ins_r, axis=0), jnp.concatenate(ins_i, axis=0)))

    y_scr[...] = jnp.dot(s_scr[...].astype(BF16), wc_ref[0], preferred_element_type=F32) + d_ref[0] * u

    def deinterleave(it, carry):
        for q in range(MOVE_UNROLL):
            t = it * MOVE_UNROLL + q
            u_scr[pl.ds(t, NSEG, stride=SEGL), :] = y_scr[pl.ds(pl.multiple_of(t * NSEG, NSEG), NSEG), :]
        return carry

    lax.fori_loop(0, SEGL // MOVE_UNROLL, deinterleave, 0)
    z_ref[...] = jax.nn.gelu(u_scr[...]).astype(BF16)


def _ssm_p_call(h, wb_bf, wc_bf, a_tab, d_tab):
    return pl.pallas_call(
        _ssm_p_kernel,
        grid=(NB, NCHUNK),
        in_specs=[
            pl.BlockSpec((LP, LANES), lambda b, k: (b, k)),
            pl.BlockSpec((1, LANES, 2 * CH_STATE), lambda b, k: (k, 0, 0)),
            pl.BlockSpec((1, 2 * CH_STATE, LANES), lambda b, k: (k, 0, 0)),
            pl.BlockSpec((1, 2, CH_STATE), lambda b, k: (k, 0, 0)),
            pl.BlockSpec((1, 1, LANES), lambda b, k: (k, 0, 0)),
        ],
        out_specs=[
            pl.BlockSpec((LP, LANES), lambda b, k: (b, k)),
            pl.BlockSpec((1, 1, 1, 2 * CH_STATE), lambda b, k: (b, k, 0, 0)),
        ],
        out_shape=[
            jax.ShapeDtypeStruct((T_PROMPT, D), BF16),
            jax.ShapeDtypeStruct((NB, NCHUNK, 1, 2 * CH_STATE), F32),
        ],
        scratch_shapes=[pltpu.VMEM((LP, LANES), F32), pltpu.VMEM((LP, 2 * CH_STATE), F32),
                        pltpu.VMEM((LP, LANES), F32)],
        compiler_params=_cparams(("parallel", "parallel")),
        name="ssm_prompt",
    )(h, wb_bf, wc_bf, a_tab, d_tab)


def _ssm_s_kernel(u_ref, sr_ref, si_ref, wb_ref, wc_ref, a_ref, d_ref, z_ref, nr_ref, ni_ref):
    u = u_ref[...]
    bu = jnp.dot(u, wb_ref[0], preferred_element_type=F32, precision=lax.Precision.HIGHEST)
    ar = a_ref[0, 0:1, :]
    ai = a_ref[0, 1:2, :]
    tr, ti = _cmul(ar, ai, sr_ref[...], si_ref[...])
    nr = tr + bu[:, 0:CH_STATE]
    ni = ti + bu[:, CH_STATE:]
    nr_ref[...] = nr
    ni_ref[...] = ni
    s = jnp.concatenate([nr, ni], axis=1).astype(BF16)
    y = jnp.dot(s, wc_ref[0], preferred_element_type=F32) + d_ref[0] * u
    z_ref[...] = jax.nn.gelu(y).astype(BF16)


def _ssm_s_call(h, s0r, s0i, wb_f32, wc_bf, a_tab, d_tab):
    st = pl.BlockSpec((DEC, CH_STATE), lambda k: (0, k))
    return pl.pallas_call(
        _ssm_s_kernel,
        grid=(NCHUNK,),
        in_specs=[
            pl.BlockSpec((DEC, LANES), lambda k: (T_PROMPT // DEC, k)),
            st,
            st,
            pl.BlockSpec((1, LANES, 2 * CH_STATE), lambda k: (k, 0, 0)),
            pl.BlockSpec((1, 2 * CH_STATE, LANES), lambda k: (k, 0, 0)),
            pl.BlockSpec((1, 2, CH_STATE), lambda k: (k, 0, 0)),
            pl.BlockSpec((1, 1, LANES), lambda k: (k, 0, 0)),
        ],
        out_specs=[pl.BlockSpec((DEC, LANES), lambda k: (0, k)), st, st],
        out_shape=[
            jax.ShapeDtypeStruct((DEC, D), BF16),
            jax.ShapeDtypeStruct((DEC, NG * NS), F32),
            jax.ShapeDtypeStruct((DEC, NG * NS), F32),
        ],
        compiler_params=_cparams(("parallel",)),
        name="ssm_sample",
    )(h, s0r, s0i, wb_f32, wc_bf, a_tab, d_tab)


def _glu_ln_kernel(z_ref, zt_ref, w_ref, bg_ref, h_ref, g_ref, b_ref, out_ref, hp_ref):
    z = _prompt_or_tail(z_ref, zt_ref)
    acc = jnp.dot(z, w_ref[...], preferred_element_type=F32) + bg_ref[...]
    m = acc[:, :D] * _sigmoid(acc[:, D:])
    out = _layer_norm(ALPHA * h_ref[...] + m, g_ref[...], b_ref[...])
    out_ref[...] = out
    hp_ref[...] = _pack_pair(out[:, :HALF], out[:, HALF:])


def _glu_ln_call(z_p, z_tail, w_bf, bg, h, g, b):
    vec = pl.BlockSpec((1, D), lambda i: (0, 0))
    return pl.pallas_call(
        _glu_ln_kernel,
        grid=(N_ROW_TILES,),
        in_specs=[
            pl.BlockSpec((TM, D), _PROMPT_TILE),
            pl.BlockSpec((TM, D), lambda i: (0, 0)),
            pl.BlockSpec((D, 2 * D), lambda i: (0, 0), pipeline_mode=pl.Buffered(1)),
            pl.BlockSpec((1, 2 * D), lambda i: (0, 0)),
            pl.BlockSpec((TM, D), lambda i: (i, 0)),
            vec,
            vec,
        ],
        out_specs=[pl.BlockSpec((TM, D), lambda i: (i, 0)), pl.BlockSpec((TM, HALF), lambda i: (i, 0))],
        out_shape=[jax.ShapeDtypeStruct((T_ALL, D), F32), jax.ShapeDtypeStruct((T_ALL, HALF), U32)],
        compiler_params=_cparams(("parallel",)),
        name="glu_ln",
    )(z_p, z_tail, w_bf, bg, h, g, b)


def _block_diag_in(t):
    t4 = t.reshape(NCHUNK, 8, GC, NS)
    eye = jnp.eye(8, dtype=t.dtype)
    return jnp.einsum("kgcn,gh->kgchn", t4, eye).reshape(NCHUNK, LANES, CH_STATE)


def _block_diag_out(t):
    t4 = t.reshape(NCHUNK, 8, GC, NS)
    eye = jnp.eye(8, dtype=t.dtype)
    return jnp.einsum("kgcn,gh->kgnhc", t4, eye).reshape(NCHUNK, CH_STATE, LANES)


def kernel(x_prompt, x_sample, cache_k, cache_v, state_ssm_re, state_ssm_im, meta_tokens, w_qkv, b_qkv, attn_sinks, w_o, b_o, ssm_lam_re, ssm_lam_im, ssm_log_dt, ssm_b_re, ssm_b_im, ssm_c_re, ssm_c_im, ssm_d, w_glu, b_glu, ln_mix_g, ln_mix_b, w_router, router_bias, w_exp_gate, w_exp_up, w_exp_down, w_sh_gate, w_sh_up, w_sh_down, ln_ffn_g, ln_ffn_b):
    moe_w = (w_router, router_bias, w_exp_gate, w_exp_up, w_exp_down, w_sh_gate, w_sh_up, w_sh_down,
             ln_ffn_g, ln_ffn_b)

    front = jnp.concatenate([jnp.zeros((PAD_FRONT, D), F32), meta_tokens], axis=0)
    pieces = []
    for b in range(NB):
        pieces += [front, x_prompt[b]]
    h = jnp.concatenate(pieces + [x_sample.reshape(DEC, D)], axis=0)

    rc, rs1, rs2 = _rope_tables()
    q, k, v = _qkv_call(h, w_qkv[0].astype(BF16), b_qkv[0][None, :], rc, rs1, rs2)
    o_p = _attn_p_call(attn_sinks[0], q, k, v)
    q3 = q[T_PROMPT:].astype(F32).reshape(DEC, NH, HD)
    o_s, ck_new, cv_new = _attn_s_call(
        attn_sinks[0][:, None], q3, k, v,
        cache_k[0].reshape(DEC, BLK, NKV * HD), cache_v[0].reshape(DEC, BLK, NKV * HD))
    o_tail = _tail_tile(o_p, o_s.reshape(DEC, D).astype(BF16))
    h, hp = _oproj_ln_call(o_p, o_tail, w_o[0].astype(BF16), b_o[0][None, :], h,
                           ln_mix_g[0][None, :], ln_mix_b[0][None, :])
    h = _moe_layer(0, False, h, hp, *moe_w)

    kp = k[:T_PROMPT].reshape(NB, LP, NKV, HD)[:, LP - BLK:]
    vp = v[:T_PROMPT].reshape(NB, LP, NKV, HD)[:, LP - BLK:]

    ab_re, ab_im, bb_re, bb_im = _ssm_prep_call(
        ssm_lam_re[0], ssm_lam_im[0], ssm_log_dt[0], ssm_b_re[0], ssm_b_im[0])
    wb = jnp.concatenate([_block_diag_in(bb_re), _block_diag_in(bb_im)], axis=2)
    wc = jnp.concatenate([_block_diag_out(ssm_c_re[0]), -_block_diag_out(ssm_c_im[0])], axis=1)
    wc_bf = wc.astype(BF16)
    a_tab = jnp.stack([ab_re[:, :NS].reshape(NCHUNK, CH_STATE),
                       ab_im[:, :NS].reshape(NCHUNK, CH_STATE)], axis=1)
    d_tab = ssm_d[0].reshape(NCHUNK, 1, LANES)
    z_p, st_p = _ssm_p_call(h, wb.astype(BF16), wc_bf, a_tab, d_tab)
    z_s, sr_new, si_new = _ssm_s_call(
        h, state_ssm_re[0].reshape(DEC, NG * NS), state_ssm_im[0].reshape(DEC, NG * NS),
        wb, wc_bf, a_tab, d_tab)
    h, hp = _glu_ln_call(z_p, _tail_tile(z_p, z_s), w_glu[0].astype(BF16), b_glu[0][None, :], h,
                         ln_mix_g[1][None, :], ln_mix_b[1][None, :])
    y_prompt, y_sample = _moe_layer(1, True, h, hp, *moe_w)
    y_prompt = y_prompt.reshape(NB, SEQ, D)
    y_sample = y_sample.reshape(DEC, 1, D)
    st_p = st_p.reshape(NB, NCHUNK, 2, 8, NS)
    rp = st_p[:, :, 0].reshape(NB, NG, NS)
    ip = st_p[:, :, 1].reshape(NB, NG, NS)
    return (y_prompt, y_sample,
            kp[None], vp[None],
            ck_new.reshape(1, DEC, BLK, NKV, HD), cv_new.reshape(1, DEC, BLK, NKV, HD),
            rp[None], ip[None],
            sr_new.reshape(1, DEC, NG, NS), si_new.reshape(1, DEC, NG, NS))
```

```python
import functools
import math

import jax
import jax.numpy as jnp
import numpy as np
from jax import lax
from jax.experimental import pallas as pl
from jax.experimental.pallas import tpu as pltpu

F32 = jnp.float32
BF16 = jnp.bfloat16
I32 = jnp.int32
U32 = jnp.uint32

D = 2048
HALF = D // 2
NB = 4
N_META = 16
SEQ = 2048
L = N_META + SEQ
BLK = 128
PAD_FRONT = (-L) % BLK
LP = L + PAD_FRONT
NBLK = LP // BLK
T_PROMPT = NB * LP
DEC = 128
T_ALL = T_PROMPT + DEC
PAST_LEN = 8192
HD = 64
NH = 32
NKV = 4
QPK = NH // NKV
QKV = (NH + 2 * NKV) * HD
QK_COLS = (NH + NKV) * HD
ROT = HD // 4
ROT_HALF = ROT // 2
ROPE_THETA = 500000.0
NG = 128
GC = 16
NS = 64
NCHUNK = 16
CH_STATE = 8 * NS
NE = 64
TOPK = 8
NEG = 8
PER_GRP = NE // NEG
TOPG = 4
DE = 512
ROUTED_SCALE = 2.5
DEPTH = 2
ALPHA = (2 * DEPTH) ** 0.25
LN_EPS = 1e-5

V7X_VMEM_BYTES = 64 * 1024 * 1024
VMEM_LIMIT = 56 * 1024 * 1024
LANES = 128

TM = 384
N_ROW_TILES = T_ALL // TM
TAIL_START = (N_ROW_TILES - 1) * TM
TM_E = 256
N_PAIRS = T_ALL * TOPK
NT_E = -(-N_PAIRS // TM_E) + NE
R_ROWS = NT_E * TM_E
TOK_BITS = 14
TOK_MASK = (1 << TOK_BITS) - 1
DUMMY_BASE = TOPK * T_ALL
YK_ROWS = DUMMY_BASE + TM_E
assert T_ALL <= TOK_MASK and T_ALL % TOPK == 0 and T_ALL % TM == 0 and TAIL_START <= T_PROMPT


def _cparams(sem):
    return pltpu.CompilerParams(dimension_semantics=sem, vmem_limit_bytes=VMEM_LIMIT)


def _sigmoid(x):
    return 1.0 / (1.0 + jnp.exp(-x))


def _layer_norm(y, g, b):
    mu = jnp.mean(y, axis=-1, keepdims=True)
    yc = y - mu
    var = jnp.mean(yc * yc, axis=-1, keepdims=True)
    return yc * lax.rsqrt(var + LN_EPS) * g + b


def _pack_pair(lo, hi):
    lo_b = lax.bitcast_convert_type(lo.astype(BF16).astype(F32), U32) >> 16
    hi_b = lax.bitcast_convert_type(hi.astype(BF16).astype(F32), U32) & jnp.uint32(0xFFFF0000)
    return lo_b | hi_b


def _unpack_pair(w):
    lo = lax.bitcast_convert_type(w << 16, F32)
    hi = lax.bitcast_convert_type(w & jnp.uint32(0xFFFF0000), F32)
    return lo, hi


def _qkv_kernel(x_ref, w_ref, b_ref, c_ref, s1_ref, s2_ref, q_ref, k_ref, v_ref):
    xb = x_ref[...].astype(BF16)
    acc = jnp.dot(xb, w_ref[...], preferred_element_type=F32) + b_ref[...]
    c = c_ref[...]
    s1 = s1_ref[...]
    s2 = s2_ref[...]
    for j in range(QK_COLS // LANES):
        blk = acc[:, j * LANES:(j + 1) * LANES]
        r = (blk * c + pltpu.roll(blk, LANES - ROT_HALF, axis=1) * s1
             + pltpu.roll(blk, ROT_HALF, axis=1) * s2)
        if j < D // LANES:
            q_ref[:, j * LANES:(j + 1) * LANES] = (r * (1.0 / math.sqrt(HD))).astype(BF16)
        else:
            jj = j - D // LANES
            k_ref[:, jj * LANES:(jj + 1) * LANES] = r
    v_ref[...] = acc[:, QK_COLS:]


def _qkv_call(x, w_bf, b, rc, rs1, rs2):
    return pl.pallas_call(
        _qkv_kernel,
        grid=(T_ALL // TM,),
        in_specs=[
            pl.BlockSpec((TM, D), lambda i: (i, 0)),
            pl.BlockSpec((D, QKV), lambda i: (0, 0)),
            pl.BlockSpec((1, QKV), lambda i: (0, 0)),
            pl.BlockSpec((TM, LANES), lambda i: (i, 0)),
            pl.BlockSpec((TM, LANES), lambda i: (i, 0)),
            pl.BlockSpec((TM, LANES), lambda i: (i, 0)),
        ],
        out_specs=[
            pl.BlockSpec((TM, D), lambda i: (i, 0)),
            pl.BlockSpec((TM, NKV * HD), lambda i: (i, 0)),
            pl.BlockSpec((TM, NKV * HD), lambda i: (i, 0)),
        ],
        out_shape=[
            jax.ShapeDtypeStruct((T_ALL, D), BF16),
            jax.ShapeDtypeStruct((T_ALL, NKV * HD), F32),
            jax.ShapeDtypeStruct((T_ALL, NKV * HD), F32),
        ],
        compiler_params=_cparams(("parallel",)),
        name="qkv_rope",
    )(x, w_bf, b, rc, rs1, rs2)


def _rope_tables():
    pos_p = jnp.maximum(jnp.arange(LP, dtype=I32) - PAD_FRONT, 0)
    pos = jnp.concatenate([pos_p, jnp.full((8,), PAST_LEN, I32)]).astype(F32)
    inv_freq = ROPE_THETA ** (-jnp.arange(0, ROT, 2, dtype=F32) / ROT)
    ang = pos[:, None] * inv_freq[None, :]
    cos = jnp.cos(ang)
    sin = jnp.sin(ang)
    lane = np.arange(LANES) % HD
    freq = np.arange(ROT_HALF)[:, None]
    first = ((lane[None, :] == freq) & (lane[None, :] < ROT_HALF)).astype(np.float32)
    second = ((lane[None, :] - ROT_HALF == freq) & (lane[None, :] < ROT)).astype(np.float32)
    rest = (lane >= ROT).astype(np.float32)[None, :]
    place = functools.partial(jnp.dot, precision=lax.Precision.HIGHEST)
    c = place(cos, jnp.asarray(first + second)) + jnp.asarray(rest)
    s1 = place(sin, jnp.asarray(-first))
    s2 = place(sin, jnp.asarray(second))

    def all_rows(t):
        return jnp.concatenate([t[:LP]] * NB + [t[LP:]] * (DEC // 8), axis=0)

    return all_rows(c), all_rows(s1), all_rows(s2)


def _attn_p_kernel(sink_ref, q_ref, kp_ref, kc_ref, vp_ref, vc_ref, o_ref):
    j = pl.program_id(0) % NBLK
    keys = jnp.concatenate([kp_ref[...], kc_ref[...]], axis=0).astype(BF16)
    vals = jnp.concatenate([vp_ref[...], vc_ref[...]], axis=0).astype(BF16)
    r = lax.broadcasted_iota(I32, (BLK, 2 * BLK), 0)
    c = lax.broadcasted_iota(I32, (BLK, 2 * BLK), 1)
    dist = BLK + r - c
    kpos = (j - 1) * BLK - PAD_FRONT + c
    mask = (dist >= 0) & (dist <= BLK) & (kpos >= 0)
    for g in range(NKV):
        kg = keys[:, g * HD:(g + 1) * HD]
        vg = vals[:, g * HD:(g + 1) * HD]
        heads = range(g * QPK, (g + 1) * QPK)
        scores = [lax.dot_general(q_ref[:, h * HD:(h + 1) * HD], kg, (((1,), (1,)), ((), ())),
                                  preferred_element_type=F32) for h in heads]
        probs, rdens = [], []
        for h, s in zip(heads, scores):
            s = jnp.where(mask, s, -jnp.inf)
            sk = sink_ref[h]
            m = jnp.maximum(jnp.max(s, axis=1, keepdims=True), sk)
            p = jnp.exp(s - m)
            rdens.append(1.0 / (jnp.sum(p, axis=1, keepdims=True) + jnp.exp(sk - m)))
            probs.append(p.astype(BF16))
        for h, p, rden in zip(heads, probs, rdens):
            oh = jnp.dot(p, vg, preferred_element_type=F32) * rden
            o_ref[:, h * HD:(h + 1) * HD] = oh.astype(BF16)


def _attn_p_call(sinks, q, k, v):
    prev = lambda i: (jnp.where(i % NBLK == 0, i, i - 1), 0)
    cur = lambda i: (i, 0)
    return pl.pallas_call(
        _attn_p_kernel,
        grid=(NB * NBLK,),
        in_specs=[
            pl.BlockSpec(memory_space=pltpu.SMEM),
            pl.BlockSpec((BLK, D), cur),
            pl.BlockSpec((BLK, NKV * HD), prev),
            pl.BlockSpec((BLK, NKV * HD), cur),
            pl.BlockSpec((BLK, NKV * HD), prev),
            pl.BlockSpec((BLK, NKV * HD), cur),
        ],
        out_specs=pl.BlockSpec((BLK, D), cur),
        out_shape=jax.ShapeDtypeStruct((T_PROMPT, D), BF16),
        compiler_params=_cparams(("parallel",)),
        name="attn_prompt",
    )(sinks, q, k, k, v, v)


SEQ_PER_STEP = 16
SEQ_UNROLL = 4


def _attn_s_kernel(sink_ref, q_ref, kn_ref, vn_ref, ck_ref, cv_ref, o_ref, cko_ref, cvo_ref):
    row = lax.broadcasted_iota(I32, (BLK, NKV * HD), 0)
    hrow = lax.broadcasted_iota(I32, (NH, NKV * HD), 0) // QPK
    hlane = lax.broadcasted_iota(I32, (NH, NKV * HD), 1) // HD
    own = hrow == hlane
    sk = sink_ref[...]

    def score_stage(s):
        kn = kn_ref[pl.ds(s, 1), :]
        knr = kn.astype(BF16).astype(F32)
        q = q_ref[s].astype(BF16)
        qe = jnp.where(own, jnp.concatenate([q] * NKV, axis=1), jnp.zeros((), BF16))
        sc = lax.dot_general(qe, ck_ref[s].astype(BF16), (((1,), (1,)), ((), ())),
                             preferred_element_type=F32)
        sn = jnp.sum(qe.astype(F32) * knr, axis=1, keepdims=True)
        return sc, sn

    def softmax_stage(sc, sn):
        m = jnp.maximum(jnp.maximum(jnp.max(sc, axis=1, keepdims=True), sn), sk)
        p = jnp.exp(sc - m)
        pn = jnp.exp(sn - m)
        rden = 1.0 / (jnp.sum(p, axis=1, keepdims=True) + pn + jnp.exp(sk - m))
        return p.astype(BF16), pn.astype(BF16).astype(F32), rden

    def value_stage(s, p, pn, rden):
        vn = vn_ref[pl.ds(s, 1), :]
        vnr = vn.astype(BF16).astype(F32)
        of = jnp.dot(p, cv_ref[s].astype(BF16), preferred_element_type=F32)
        of = jnp.where(own, of + pn * vnr, 0.0)
        og = of[:, 0:HD]
        for g in range(1, NKV):
            og = og + of[:, g * HD:(g + 1) * HD]
        o_ref[s] = og * rden
        cko_ref[s] = jnp.where(row == BLK - 1, kn_ref[pl.ds(s, 1), :], pltpu.roll(ck_ref[s], BLK - 1, axis=0))
        cvo_ref[s] = jnp.where(row == BLK - 1, vn, pltpu.roll(cv_ref[s], BLK - 1, axis=0))

    def body(it, carry):
        seqs = [it * SEQ_UNROLL + u for u in range(SEQ_UNROLL)]
        staged = [score_stage(s) for s in seqs]
        soft = [softmax_stage(*st) for st in staged]
        for s, sm in zip(seqs, soft):
            value_stage(s, *sm)
        return carry

    lax.fori_loop(0, SEQ_PER_STEP // SEQ_UNROLL, body, 0)


def _attn_s_call(sinks_col, q3, k, v, cache_k, cache_v):
    sp = SEQ_PER_STEP
    kv_off = T_PROMPT // sp
    return pl.pallas_call(
        _attn_s_kernel,
        grid=(DEC // sp,),
        in_specs=[
            pl.BlockSpec((NH, 1), lambda i: (0, 0)),
            pl.BlockSpec((sp, NH, HD), lambda i: (i, 0, 0)),
            pl.BlockSpec((sp, NKV * HD), lambda i: (kv_off + i, 0)),
            pl.BlockSpec((sp, NKV * HD), lambda i: (kv_off + i, 0)),
            pl.BlockSpec((sp, BLK, NKV * HD), lambda i: (i, 0, 0)),
            pl.BlockSpec((sp, BLK, NKV * HD), lambda i: (i, 0, 0)),
        ],
        out_specs=[
            pl.BlockSpec((sp, NH, HD), lambda i: (i, 0, 0)),
            pl.BlockSpec((sp, BLK, NKV * HD), lambda i: (i, 0, 0)),
            pl.BlockSpec((sp, BLK, NKV * HD), lambda i: (i, 0, 0)),
        ],
        out_shape=[
            jax.ShapeDtypeStruct((DEC, NH, HD), F32),
            jax.ShapeDtypeStruct((DEC, BLK, NKV * HD), F32),
            jax.ShapeDtypeStruct((DEC, BLK, NKV * HD), F32),
        ],
        compiler_params=_cparams(("parallel",)),
        name="attn_sample",
    )(sinks_col, q3, k, v, cache_k, cache_v)


def _prompt_or_tail(x_ref, tail_ref):
    return jnp.where(pl.program_id(0) == N_ROW_TILES - 1, tail_ref[...], x_ref[...])


def _tail_tile(x_prompt_rows, x_sample_rows):
    return jnp.concatenate([x_prompt_rows[TAIL_START:], x_sample_rows], axis=0)


_PROMPT_TILE = lambda i: (jnp.minimum(i, N_ROW_TILES - 2), 0)


def _oproj_ln_kernel(o_ref, ot_ref, w_ref, bo_ref, h_ref, g_ref, b_ref, out_ref, hp_ref):
    o = _prompt_or_tail(o_ref, ot_ref)
    m = jnp.dot(o, w_ref[...], preferred_element_type=F32) + bo_ref[...]
    out = _layer_norm(ALPHA * h_ref[...] + m, g_ref[...], b_ref[...])
    out_ref[...] = out
    hp_ref[...] = _pack_pair(out[:, :HALF], out[:, HALF:])


def _oproj_ln_call(o_p, o_tail, w_bf, bo, h, g, b):
    vec = pl.BlockSpec((1, D), lambda i: (0, 0))
    return pl.pallas_call(
        _oproj_ln_kernel,
        grid=(N_ROW_TILES,),
        in_specs=[
            pl.BlockSpec((TM, D), _PROMPT_TILE),
            pl.BlockSpec((TM, D), lambda i: (0, 0)),
            pl.BlockSpec((D, D), lambda i: (0, 0)),
            vec,
            pl.BlockSpec((TM, D), lambda i: (i, 0)),
            vec,
            vec,
        ],
        out_specs=[pl.BlockSpec((TM, D), lambda i: (i, 0)), pl.BlockSpec((TM, HALF), lambda i: (i, 0))],
        out_shape=[jax.ShapeDtypeStruct((T_ALL, D), F32), jax.ShapeDtypeStruct((T_ALL, HALF), U32)],
        compiler_params=_cparams(("parallel",)),
        name="oproj_ln",
    )(o_p, o_tail, w_bf, bo, h, g, b)


def _router_kernel(h_ref, wr_ref, rb_ref, eidx_ref, wsel_ref, rank_ref, cnt_ref, carry_ref):
    i = pl.program_id(0)

    @pl.when(i == 0)
    def _():
        carry_ref[...] = jnp.zeros_like(carry_ref)

    h = h_ref[...]
    w = wr_ref[...]
    h_hi = h.astype(BF16)
    h_lo = (h - h_hi.astype(F32)).astype(BF16)
    w_hi = w.astype(BF16)
    w_lo = (w - w_hi.astype(F32)).astype(BF16)
    logits = (jnp.dot(h_hi, w_hi, preferred_element_type=F32)
              + (jnp.dot(h_hi, w_lo, preferred_element_type=F32)
                 + jnp.dot(h_lo, w_hi, preferred_element_type=F32)))
    scores = _sigmoid(logits.T[0:NE, :])
    biased = scores + rb_ref[...]
    ninf = -jnp.inf
    sub = lax.broadcasted_iota(I32, (PER_GRP, TM), 0).astype(F32)
    sc_g = [scores[g * PER_GRP:(g + 1) * PER_GRP, :] for g in range(NEG)]
    b_g = [biased[g * PER_GRP:(g + 1) * PER_GRP, :] for g in range(NEG)]
    e_g = [sub + float(g * PER_GRP) for g in range(NEG)]

    def smax(x):
        return jnp.max(x, axis=0, keepdims=True)

    def smin(x):
        return jnp.min(x, axis=0, keepdims=True)

    gs = []
    for g in range(NEG):
        m1 = smax(b_g[g])
        i1 = smin(jnp.where(b_g[g] == m1, sub, float(PER_GRP)))
        m2 = smax(jnp.where(sub == i1, ninf, b_g[g]))
        gs.append(m1 + m2)

    work = []
    for g in range(NEG):
        beaten = jnp.zeros((1, TM), F32)
        for o in range(NEG):
            if o != g:
                wins = (gs[o] >= gs[g]) if o < g else (gs[o] > gs[g])
                beaten = beaten + jnp.where(wins, 1.0, 0.0)
        work.append(jnp.where(beaten < float(TOPG), b_g[g], ninf))

    idx_rows, w_rows = [], []
    onehot = [jnp.zeros((PER_GRP, TM), F32) for _ in range(NEG)]
    for _ in range(TOPK):
        m = smax(work[0])
        for g in range(1, NEG):
            m = jnp.maximum(m, smax(work[g]))
        ik = smin(jnp.where(work[0] == m, e_g[0], float(NE)))
        for g in range(1, NEG):
            ik = jnp.minimum(ik, smin(jnp.where(work[g] == m, e_g[g], float(NE))))
        wk = jnp.zeros((1, TM), F32)
        for g in range(NEG):
            hit = e_g[g] == ik
            wk = wk + jnp.sum(jnp.where(hit, sc_g[g], 0.0), axis=0, keepdims=True)
            onehot[g] = jnp.where(hit, 1.0, onehot[g])
            work[g] = jnp.where(hit, ninf, work[g])
        idx_rows.append(ik)
        w_rows.append(wk)
    wsum = w_rows[0]
    for wk in w_rows[1:]:
        wsum = wsum + wk

    rr = lax.broadcasted_iota(I32, (TM, TM), 0)
    cc = lax.broadcasted_iota(I32, (TM, TM), 1)
    tri = jnp.where(rr < cc, 1.0, 0.0).astype(BF16)
    oh = jnp.concatenate(onehot, axis=0)
    prefix = jnp.dot(oh.astype(BF16), tri, preferred_element_type=F32) + carry_ref[...]
    carry_ref[...] = carry_ref[...] + jnp.sum(oh, axis=1, keepdims=True)
    cnt_ref[...] = carry_ref[...]

    rank_rows = []
    for k in range(TOPK):
        rk = jnp.zeros((1, TM), F32)
        for g in range(NEG):
            pg = prefix[g * PER_GRP:(g + 1) * PER_GRP, :]
            rk = rk + jnp.sum(jnp.where(e_g[g] == idx_rows[k], pg, 0.0), axis=0, keepdims=True)
        rank_rows.append(rk)
    eidx_ref[...] = jnp.concatenate(idx_rows, axis=0).astype(I32)
    rank_ref[...] = jnp.concatenate(rank_rows, axis=0).astype(I32)
    w_t = jnp.concatenate([wk / wsum * ROUTED_SCALE for wk in w_rows]
                          + [jnp.zeros((LANES - TOPK, TM), F32)], axis=0)
    wsel_ref[...] = w_t.T[:, 0:TOPK]


def _router_call(h, w_router, router_bias):
    tk = pl.BlockSpec((TM, TOPK), lambda i: (i, 0))
    kt = pl.BlockSpec((TOPK, TM), lambda i: (0, i))
    return pl.pallas_call(
        _router_kernel,
        grid=(T_ALL // TM,),
        in_specs=[
            pl.BlockSpec((TM, D), lambda i: (i, 0)),
            pl.BlockSpec((D, LANES), lambda i: (0, 0)),
            pl.BlockSpec((NE, 1), lambda i: (0, 0)),
        ],
        out_specs=[kt, tk, kt, pl.BlockSpec((NE, 1), lambda i: (0, 0))],
        out_shape=[
            jax.ShapeDtypeStruct((TOPK, T_ALL), I32),
            jax.ShapeDtypeStruct((T_ALL, TOPK), F32),
            jax.ShapeDtypeStruct((TOPK, T_ALL), I32),
            jax.ShapeDtypeStruct((NE, 1), F32),
        ],
        scratch_shapes=[pltpu.VMEM((NE, 1), F32)],
        compiler_params=_cparams(("arbitrary",)),
        name="router",
    )(h, jnp.pad(w_router, ((0, 0), (0, LANES - NE))), router_bias[:, None])


def _tiles_of(cnt):
    return (cnt + TM_E - 1) // TM_E


def _plan_kernel(cnt_ref, eidx_ref, rank_ref, pos_ref, te_ref, gi_ref, ne_ref, na_ref, off_s):
    def offsets(e, run):
        off_s[e] = run
        return run + _tiles_of(cnt_ref[e]) * TM_E

    total = lax.fori_loop(0, NE, offsets, 0)
    na = total // TM_E
    na_ref[0] = na

    def idle(t, carry):
        te_ref[t] = NE - 1
        gi_ref[t] = 0
        ne_ref[t] = -1
        return carry

    lax.fori_loop(na, NT_E, idle, 0)

    def forward(e, ordinal):
        t0 = off_s[e] // TM_E
        nt = _tiles_of(cnt_ref[e])

        def mark(t, carry):
            te_ref[t] = e
            gi_ref[t] = ordinal
            return carry

        lax.fori_loop(t0, t0 + nt, mark, 0)
        return ordinal + jnp.where(nt > 0, 1, 0)

    lax.fori_loop(0, NE, forward, 0)

    def backward(i, nxt):
        e = NE - 1 - i
        t0 = off_s[e] // TM_E
        nt = _tiles_of(cnt_ref[e])

        def mark(t, carry):
            ne_ref[t] = nxt
            return carry

        lax.fori_loop(t0, t0 + nt, mark, 0)
        return jnp.where(nt > 0, e, nxt)

    lax.fori_loop(0, NE, backward, -1)

    eidx = eidx_ref[...]
    pos = rank_ref[...]
    for e in range(NE):
        pos = pos + jnp.where(eidx == e, off_s[e], 0)
    pos_ref[...] = pos


def _plan_call(cnt, eidx_t, rank_t):
    smem = pl.BlockSpec(memory_space=pltpu.SMEM)
    vmem = pl.BlockSpec(memory_space=pltpu.VMEM)
    tiles = jax.ShapeDtypeStruct((NT_E,), I32)
    return pl.pallas_call(
        _plan_kernel,
        in_specs=[smem, vmem, vmem],
        out_specs=[vmem, smem, smem, smem, smem],
        out_shape=[jax.ShapeDtypeStruct((TOPK, T_ALL), I32), tiles, tiles, tiles,
                   jax.ShapeDtypeStruct((1,), I32)],
        scratch_shapes=[pltpu.SMEM((NE,), I32)],
        name="moe_plan",
    )(cnt, eidx_t, rank_t)


INFO_UNROLL = 16
WORD_STEP = (1 << TOK_BITS) + 1
assert T_ALL % INFO_UNROLL == 0


def _rowinfo_kernel(pos_ref, pad_ref, info_ref, sem):
    slot = pl.program_id(0)

    @pl.when(slot == 0)
    def _():
        cp = pltpu.make_async_copy(pad_ref, info_ref, sem)
        cp.start()
        cp.wait()

    word0 = (slot * T_ALL) << TOK_BITS

    def body(j, carry):
        base = j * INFO_UNROLL
        wj = word0 + base * WORD_STEP
        for u in range(INFO_UNROLL):
            info_ref[pos_ref[0, 0, base + u]] = wj + u * WORD_STEP
        return carry

    lax.fori_loop(0, T_ALL // INFO_UNROLL, body, 0)


def _rowinfo_call(pos_t):
    rows = np.arange(R_ROWS, dtype=np.int64)
    pads = jnp.asarray(((DUMMY_BASE + (rows & (TM_E - 1))) << TOK_BITS).astype(np.int32))
    return pl.pallas_call(
        _rowinfo_kernel,
        grid=(TOPK,),
        in_specs=[pl.BlockSpec((1, 1, T_ALL), lambda i: (i, 0, 0), memory_space=pltpu.SMEM),
                  pl.BlockSpec(memory_space=pl.ANY)],
        out_specs=pl.BlockSpec(memory_space=pltpu.SMEM),
        out_shape=jax.ShapeDtypeStruct((R_ROWS,), I32),
        scratch_shapes=[pltpu.SemaphoreType.DMA(())],
        compiler_params=pltpu.CompilerParams(dimension_semantics=("arbitrary",)),
        name="moe_rowinfo",
    )(pos_t.reshape(TOPK, 1, T_ALL), pads)


def _expert_kernel(layer, te_ref, gi_ref, ne_ref, na_ref, ginfo_ref, sinfo_ref, hp_ref,
                   wg_hbm, wu_hbm, wd_hbm, yk_ref,
                   xb0, xb1, yb0, yb1, wgf, wuf, wdf, wg_s, wu_s, wd_s, gsem, ssem, wsem, zsem):
    s = pl.program_id(0)
    na = na_ref[0]
    xb = (xb0, xb1)
    yb = (yb0, yb1)

    def weight_copies(e, slot):
        return (pltpu.make_async_copy(wg_hbm.at[layer, e], wgf.at[slot], wsem.at[0]),
                pltpu.make_async_copy(wu_hbm.at[layer, e], wuf.at[slot], wsem.at[1]),
                pltpu.make_async_copy(wd_hbm.at[layer, e], wdf.at[slot], wsem.at[2]))

    def gather_row(par, r, thread=0):
        tok = ginfo_ref[0, 0, r] & TOK_MASK
        pltpu.make_async_copy(hp_ref.at[pl.ds(tok, 1)], xb[par].at[pl.ds(r, 1)], gsem.at[par]).start(
            priority=thread)

    def scatter_row(par, r, thread=0):
        dst = sinfo_ref[0, 0, r] >> TOK_BITS
        pltpu.make_async_copy(yb[par].at[pl.ds(r, 1)], yk_ref.at[pl.ds(dst, 1)], ssem.at[par]).start(
            priority=thread)

    def compute(par, between):
        lo, hi = _unpack_pair(xb[1 - par][...])
        lo = lo.astype(BF16)
        hi = hi.astype(BF16)
        between(0)
        hg = (jnp.dot(lo, wg_s[0:HALF, :], preferred_element_type=F32)
              + jnp.dot(hi, wg_s[HALF:D, :], preferred_element_type=F32))
        between(1)
        hu = (jnp.dot(lo, wu_s[0:HALF, :], preferred_element_type=F32)
              + jnp.dot(hi, wu_s[HALF:D, :], preferred_element_type=F32))
        between(2)
        act = (hg * _sigmoid(hg) * hu).astype(BF16)
        y = jnp.dot(act, wd_s[...], preferred_element_type=F32)
        between(3)
        yb[1 - par][...] = _pack_pair(y[:, :HALF], y[:, HALF:])

    @pl.when(jnp.logical_and(s >= 1, s <= na))
    def _():
        pltpu.make_async_copy(hp_ref.at[pl.ds(0, TM_E)], xb0, gsem.at[(s + 1) % 2]).wait()

    @pl.when(jnp.logical_and(s >= 3, s <= na + 2))
    def _():
        pltpu.make_async_copy(yb0, yk_ref.at[pl.ds(0, TM_E)], ssem.at[(s + 1) % 2]).wait()

    @pl.when(s == 0)
    def _():
        for cp in weight_copies(te_ref[0], 0):
            cp.start(priority=1)
        yb0[...] = jnp.zeros_like(yb0)
        zc = pltpu.make_async_copy(yb0, yk_ref.at[pl.ds(DUMMY_BASE, TM_E)], zsem)
        zc.start()
        zc.wait()

    c = s - 1
    cc = jnp.clip(c, 0, na - 1)
    first = jnp.logical_or(c == 0, te_ref[cc] != te_ref[jnp.maximum(cc - 1, 0)])

    @pl.when(jnp.logical_and(jnp.logical_and(c >= 0, c < na), first))
    def _():
        slot = gi_ref[cc] % 2
        for cp in weight_copies(te_ref[cc], slot):
            cp.wait()
        wg_s[...] = wgf[slot].astype(BF16)
        wu_s[...] = wuf[slot].astype(BF16)
        wd_s[...] = wdf[slot].astype(BF16)
        nxt = ne_ref[cc]

        @pl.when(nxt >= 0)
        def _():
            for cp in weight_copies(nxt, 1 - slot):
                cp.start(priority=1)

    steady = jnp.logical_and(s >= 2, s < na)
    for par in (0, 1):
        mine = (s % 2) == par

        @pl.when(jnp.logical_and(steady, mine))
        def _():
            def between(q):
                for r in range(q * (TM_E // 4), (q + 1) * (TM_E // 4)):
                    gather_row(par, r)
                    scatter_row(par, r, thread=r % 2)

            compute(par, between)

        @pl.when(jnp.logical_and(jnp.logical_not(steady), mine))
        def _():
            @pl.when(s < na)
            def _():
                def g(r, carry):
                    gather_row(par, r)
                    return carry

                lax.fori_loop(0, TM_E, g, 0)

            @pl.when(jnp.logical_and(s >= 1, s <= na))
            def _():
                compute(par, lambda q: None)

            @pl.when(jnp.logical_and(s >= 2, s <= na + 1))
            def _():
                def sc(r, carry):
                    scatter_row(par, r)
                    return carry

                lax.fori_loop(0, TM_E, sc, 0)


def _expert_call(layer, te, gi, ne, nact, rowinfo, hp, w_gate, w_up, w_down):
    def gmap(s, te_r, gi_r, ne_r, na_r):
        return (jnp.minimum(s, na_r[0] - 1), 0, 0)

    def smap(s, te_r, gi_r, ne_r, na_r):
        return (jnp.clip(s - 2, 0, na_r[0] - 1), 0, 0)

    info = rowinfo.reshape(NT_E, 1, TM_E)
    anyspec = pl.BlockSpec(memory_space=pl.ANY)
    return pl.pallas_call(
        functools.partial(_expert_kernel, layer),
        grid_spec=pltpu.PrefetchScalarGridSpec(
            num_scalar_prefetch=4,
            grid=(NT_E + 3,),
            in_specs=[
                pl.BlockSpec((1, 1, TM_E), gmap, memory_space=pltpu.SMEM),
                pl.BlockSpec((1, 1, TM_E), smap, memory_space=pltpu.SMEM),
                anyspec, anyspec, anyspec, anyspec,
            ],
            out_specs=anyspec,
            scratch_shapes=[
                pltpu.VMEM((TM_E, HALF), U32),
                pltpu.VMEM((TM_E, HALF), U32),
                pltpu.VMEM((TM_E, HALF), U32),
                pltpu.VMEM((TM_E, HALF), U32),
                pltpu.VMEM((2, D, DE), F32),
                pltpu.VMEM((2, D, DE), F32),
                pltpu.VMEM((2, DE, D), F32),
                pltpu.VMEM((D, DE), BF16),
                pltpu.VMEM((D, DE), BF16),
                pltpu.VMEM((DE, D), BF16),
                pltpu.SemaphoreType.DMA((2,)),
                pltpu.SemaphoreType.DMA((2,)),
                pltpu.SemaphoreType.DMA((3,)),
                pltpu.SemaphoreType.DMA(()),
            ],
        ),
        out_shape=jax.ShapeDtypeStruct((YK_ROWS, HALF), U32),
        compiler_params=_cparams(("arbitrary",)),
        name="moe_experts",
    )(te, gi, ne, nact, info, info, hp, w_gate, w_up, w_down)


BLK_PER_TILE = TM // BLK


def _final_copies(step, obuf, yp_ref, ys_ref, sem):
    out = []
    slot = step % 2
    for m in range(BLK_PER_TILE):
        g = step * BLK_PER_TILE + m
        b = g // NBLK
        j = g % NBLK
        src = obuf.at[slot, pl.ds(m * BLK, BLK)]
        r0 = pl.multiple_of(jnp.maximum(b * SEQ + (j - 1) * BLK, 0), BLK)
        out.append((jnp.logical_and(g < NB * NBLK, j >= 1),
                    pltpu.make_async_copy(src, yp_ref.at[pl.ds(r0, BLK)], sem.at[slot])))
        out.append((g == NB * NBLK, pltpu.make_async_copy(src, ys_ref, sem.at[slot])))
    return out


def _combine_kernel(final, w_ref, h_ref, *rest):
    yk_refs = rest[:TOPK]
    wsg_ref, wsu_ref, wsd_ref, g_ref, b_ref = rest[TOPK:TOPK + 5]
    outs = rest[TOPK + 5:]
    i = pl.program_id(0)
    if final:
        yp_ref, ys_ref, obuf, osem = outs

        def wait_step(step):
            for cond, cp in _final_copies(step, obuf, yp_ref, ys_ref, osem):
                @pl.when(cond)
                def _():
                    cp.wait()

        @pl.when(i >= 2)
        def _():
            wait_step(i - 2)

    h = h_ref[...]
    hb = h.astype(BF16)
    sg = jnp.dot(hb, wsg_ref[...], preferred_element_type=F32)
    su = jnp.dot(hb, wsu_ref[...], preferred_element_type=F32)
    act = (sg * _sigmoid(sg) * su).astype(BF16)
    y = ALPHA * h + jnp.dot(act, wsd_ref[...], preferred_element_type=F32)
    ylo = y[:, :HALF]
    yhi = y[:, HALF:]
    w = w_ref[...]
    for k in range(TOPK):
        lo, hi = _unpack_pair(yk_refs[k][...])
        wk = w[:, k:k + 1]
        ylo = ylo + wk * lo
        yhi = yhi + wk * hi
    out = _layer_norm(jnp.concatenate([ylo, yhi], axis=1), g_ref[...], b_ref[...])
    if not final:
        outs[0][...] = out
        return
    obuf[i % 2] = out
    for cond, cp in _final_copies(i, obuf, yp_ref, ys_ref, osem):
        @pl.when(cond)
        def _():
            cp.start()

    @pl.when(i == N_ROW_TILES - 1)
    def _():
        wait_step(i - 1)
        wait_step(i)


def _combine_call(final, wsel, h, yk, wsg, wsu, wsd, g, b):
    vec = pl.BlockSpec((1, D), lambda i: (0, 0))
    once = dict(pipeline_mode=pl.Buffered(1))
    yk_specs = [pl.BlockSpec((TM, HALF), lambda i, k=k: (k * N_ROW_TILES + i, 0)) for k in range(TOPK)]
    if final:
        anyspec = pl.BlockSpec(memory_space=pl.ANY)
        out_specs = [anyspec, anyspec]
        out_shape = [jax.ShapeDtypeStruct((NB * SEQ, D), F32), jax.ShapeDtypeStruct((DEC, D), F32)]
        scratch = [pltpu.VMEM((2, TM, D), F32), pltpu.SemaphoreType.DMA((2,))]
    else:
        out_specs = pl.BlockSpec((TM, D), lambda i: (i, 0))
        out_shape = jax.ShapeDtypeStruct((T_ALL, D), F32)
        scratch = []
    return pl.pallas_call(
        functools.partial(_combine_kernel, final),
        grid=(N_ROW_TILES,),
        in_specs=[
            pl.BlockSpec((TM, TOPK), lambda i: (i, 0)),
            pl.BlockSpec((TM, D), lambda i: (i, 0)),
            *yk_specs,
            pl.BlockSpec((D, DE), lambda i: (0, 0), **once),
            pl.BlockSpec((D, DE), lambda i: (0, 0), **once),
            pl.BlockSpec((DE, D), lambda i: (0, 0), **once),
            vec,
            vec,
        ],
        out_specs=out_specs,
        out_shape=out_shape,
        scratch_shapes=scratch,
        compiler_params=_cparams(("arbitrary",)),
        name="moe_combine_final" if final else "moe_combine",
    )(wsel, h, *([yk] * TOPK), wsg, wsu, wsd, g, b)


def _moe_layer(layer, final, h, hp, w_router, router_bias, w_exp_gate, w_exp_up, w_exp_down,
               w_sh_gate, w_sh_up, w_sh_down, ln_g, ln_b):
    eidx_t, wsel, rank_t, counts = _router_call(h, w_router[layer], router_bias[layer])
    pos_t, te, gi, ne, nact = _plan_call(counts[:, 0].astype(I32), eidx_t, rank_t)
    rowinfo = _rowinfo_call(pos_t)
    yk = _expert_call(layer, te, gi, ne, nact, rowinfo, hp, w_exp_gate, w_exp_up, w_exp_down)
    return _combine_call(
        final, wsel, h, yk,
        w_sh_gate[layer].astype(BF16), w_sh_up[layer].astype(BF16), w_sh_down[layer].astype(BF16),
        ln_g[layer][None, :], ln_b[layer][None, :])


def _ssm_prep_kernel(lr_ref, li_ref, ldt_ref, br_ref, bi_ref, abr_ref, abi_ref, bbr_ref, bbi_ref):
    lr = lr_ref[...]
    li = li_ref[...]
    dt = jnp.exp(ldt_ref[...])
    mag = jnp.exp(lr * dt)
    ab_re = mag * jnp.cos(li * dt)
    ab_im = mag * jnp.sin(li * dt)
    den = lr * lr + li * li
    nr = ab_re - 1.0
    ni = ab_im
    cr = (nr * lr + ni * li) / den
    ci = (ni * lr - nr * li) / den
    br = br_ref[...]
    bi = bi_ref[...]
    abr_ref[...] = ab_re
    abi_ref[...] = ab_im
    bbr_ref[...] = cr * br - ci * bi
    bbi_ref[...] = cr * bi + ci * br


def _ssm_prep_call(lam_re, lam_im, log_dt, b_re, b_im):
    wide = (NG, GC * NS)
    lr = jnp.tile(lam_re, (1, GC))
    li = jnp.tile(lam_im, (1, GC))
    ldt = jnp.broadcast_to(log_dt[:, None], wide)
    br = jnp.transpose(b_re, (0, 2, 1)).reshape(wide)
    bi = jnp.transpose(b_im, (0, 2, 1)).reshape(wide)
    sds = jax.ShapeDtypeStruct(wide, F32)
    return pl.pallas_call(
        _ssm_prep_kernel, out_shape=[sds, sds, sds, sds], name="ssm_prep",
        compiler_params=pltpu.CompilerParams(vmem_limit_bytes=VMEM_LIMIT),
    )(lr, li, ldt, br, bi)


def _cmul(ar, ai, xr, xi):
    return ar * xr - ai * xi, ar * xi + ai * xr


NSEG = 8
SEGL = LP // NSEG
STEP_UNROLL = 4
MOVE_UNROLL = 8
assert LP == NSEG * SEGL and SEGL % STEP_UNROLL == 0 and SEGL % MOVE_UNROLL == 0 and PAD_FRONT < SEGL
assert STEP_UNROLL % 2 == 0


def _ssm_p_kernel(u_ref, wb_ref, wc_ref, a_ref, d_ref, z_ref, st_ref, u_scr, s_scr, sb_scr, y_scr):
    def interleave(it, carry):
        for q in range(MOVE_UNROLL):
            t = it * MOVE_UNROLL + q
            u_scr[pl.ds(pl.multiple_of(t * NSEG, NSEG), NSEG), :] = u_ref[pl.ds(t, NSEG, stride=SEGL), :]
        return carry

    lax.fori_loop(0, SEGL // MOVE_UNROLL, interleave, 0)
    row = lax.broadcasted_iota(I32, (LP, LANES), 0)
    is_pad = jnp.logical_and(row % NSEG == 0, row // NSEG < PAD_FRONT)
    u = jnp.where(is_pad, 0.0, u_scr[...])
    s_scr[...] = jnp.dot(u.astype(BF16), wb_ref[0], preferred_element_type=F32)

    ar = a_ref[0, 0:1, :]
    ai = a_ref[0, 1:2, :]

    def group(t):
        r0 = pl.multiple_of(t * NSEG, NSEG)
        return pl.ds(r0, NSEG)

    def local_scan(it, carry):
        sr, si = carry
        for q in range(STEP_UNROLL):
            g = group(it * STEP_UNROLL + q)
            tr, ti = _cmul(ar, ai, sr, si)
            sr = tr + s_scr[g, 0:CH_STATE]
            si = ti + s_scr[g, CH_STATE:]
            s_scr[g, 0:CH_STATE] = sr
            s_scr[g, CH_STATE:] = si
        return sr, si

    zero8 = jnp.zeros((NSEG, CH_STATE), F32)
    er, ei = lax.fori_loop(0, SEGL // STEP_UNROLL, local_scan, (zero8, zero8))

    pr, pi = ar, ai
    acc = None
    bits = SEGL
    while bits:
        if bits & 1:
            acc = (pr, pi) if acc is None else _cmul(pr, pi, *acc)
        bits >>= 1
        if bits:
            pr, pi = _cmul(pr, pi, pr, pi)
    alr, ali = acc
    cr = jnp.zeros((1, CH_STATE), F32)
    ci = jnp.zeros((1, CH_STATE), F32)
    ins_r, ins_i = [], []
    for j in range(NSEG):
        ins_r.append(cr)
        ins_i.append(ci)
        tr, ti = _cmul(alr, ali, cr, ci)
        cr = tr + er[j:j + 1, :]
        ci = ti + ei[j:j + 1, :]
    st_ref[0, 0, :, 0:CH_STATE] = cr
    st_ref[0, 0, :, CH_STATE:] = ci

    def fixup(it, carry):
        dr, di = carry
        for q in range(0, STEP_UNROLL, 2):
            t0 = it * STEP_UNROLL + q
            rows_r, rows_i = [], []
            for t in (t0, t0 + 1):
                g = group(t)
                dr, di = _cmul(ar, ai, dr, di)
                rows_r.append(s_scr[g, 0:CH_STATE] + dr)
                rows_i.append(s_scr[g, CH_STATE:] + di)
            pair = pl.ds(pl.multiple_of(t0 * NSEG, 2 * NSEG), 2 * NSEG)
            sb_scr[pair, 0:CH_STATE] = jnp.concatenate(rows_r, axis=0).astype(BF16)
            sb_scr[pair, CH_STATE:] = jnp.concatenate(rows_i, axis=0).astype(BF16)
        return dr, di

    lax.fori_loop(0, SEGL // STEP_UNROLL, fixup,
                  (jnp.concatenate(ins_r, axis=0), jnp.concatenate(ins_i, axis=0)))

    y_scr[...] = jnp.dot(sb_scr[...], wc_ref[0], preferred_element_type=F32) + d_ref[0] * u

    def deinterleave(it, carry):
        for q in range(MOVE_UNROLL):
            t = it * MOVE_UNROLL + q
            u_scr[pl.ds(t, NSEG, stride=SEGL), :] = y_scr[pl.ds(pl.multiple_of(t * NSEG, NSEG), NSEG), :]
        return carry

    lax.fori_loop(0, SEGL // MOVE_UNROLL, deinterleave, 0)
    z_ref[...] = jax.nn.gelu(u_scr[...]).astype(BF16)


def _ssm_p_call(h, wb_bf, wc_bf, a_tab, d_tab):
    return pl.pallas_call(
        _ssm_p_kernel,
        grid=(NB, NCHUNK),
        in_specs=[
            pl.BlockSpec((LP, LANES), lambda b, k: (b, k)),
            pl.BlockSpec((1, LANES, 2 * CH_STATE), lambda b, k: (k, 0, 0)),
            pl.BlockSpec((1, 2 * CH_STATE, LANES), lambda b, k: (k, 0, 0)),
            pl.BlockSpec((1, 2, CH_STATE), lambda b, k: (k, 0, 0)),
            pl.BlockSpec((1, 1, LANES), lambda b, k: (k, 0, 0)),
        ],
        out_specs=[
            pl.BlockSpec((LP, LANES), lambda b, k: (b, k)),
            pl.BlockSpec((1, 1, 1, 2 * CH_STATE), lambda b, k: (b, k, 0, 0)),
        ],
        out_shape=[
            jax.ShapeDtypeStruct((T_PROMPT, D), BF16),
            jax.ShapeDtypeStruct((NB, NCHUNK, 1, 2 * CH_STATE), F32),
        ],
        scratch_shapes=[pltpu.VMEM((LP, LANES), F32), pltpu.VMEM((LP, 2 * CH_STATE), F32),
                        pltpu.VMEM((LP, 2 * CH_STATE), BF16), pltpu.VMEM((LP, LANES), F32)],
        compiler_params=_cparams(("parallel", "parallel")),
        name="ssm_prompt",
    )(h, wb_bf, wc_bf, a_tab, d_tab)


def _ssm_s_kernel(u_ref, sr_ref, si_ref, wb_ref, wc_ref, a_ref, d_ref, z_ref, nr_ref, ni_ref):
    u = u_ref[...]
    bu = jnp.dot(u, wb_ref[0], preferred_element_type=F32, precision=lax.Precision.HIGHEST)
    ar = a_ref[0, 0:1, :]
    ai = a_ref[0, 1:2, :]
    tr, ti = _cmul(ar, ai, sr_ref[...], si_ref[...])
    nr = tr + bu[:, 0:CH_STATE]
    ni = ti + bu[:, CH_STATE:]
    nr_ref[...] = nr
    ni_ref[...] = ni
    s = jnp.concatenate([nr, ni], axis=1).astype(BF16)
    y = jnp.dot(s, wc_ref[0], preferred_element_type=F32) + d_ref[0] * u
    z_ref[...] = jax.nn.gelu(y).astype(BF16)


def _ssm_s_call(h, s0r, s0i, wb_f32, wc_bf, a_tab, d_tab):
    st = pl.BlockSpec((DEC, CH_STATE), lambda k: (0, k))
    return pl.pallas_call(
        _ssm_s_kernel,
        grid=(NCHUNK,),
        in_specs=[
            pl.BlockSpec((DEC, LANES), lambda k: (T_PROMPT // DEC, k)),
            st,
            st,
            pl.BlockSpec((1, LANES, 2 * CH_STATE), lambda k: (k, 0, 0)),
            pl.BlockSpec((1, 2 * CH_STATE, LANES), lambda k: (k, 0, 0)),
            pl.BlockSpec((1, 2, CH_STATE), lambda k: (k, 0, 0)),
            pl.BlockSpec((1, 1, LANES), lambda k: (k, 0, 0)),
        ],
        out_specs=[pl.BlockSpec((DEC, LANES), lambda k: (0, k)), st, st],
        out_shape=[
            jax.ShapeDtypeStruct((DEC, D), BF16),
            jax.ShapeDtypeStruct((DEC, NG * NS), F32),
            jax.ShapeDtypeStruct((DEC, NG * NS), F32),
        ],
        compiler_params=_cparams(("parallel",)),
        name="ssm_sample",
    )(h, s0r, s0i, wb_f32, wc_bf, a_tab, d_tab)


def _glu_ln_kernel(z_ref, zt_ref, w_ref, bg_ref, h_ref, g_ref, b_ref, out_ref, hp_ref):
    z = _prompt_or_tail(z_ref, zt_ref)
    acc = jnp.dot(z, w_ref[...], preferred_element_type=F32) + bg_ref[...]
    m = acc[:, :D] * _sigmoid(acc[:, D:])
    out = _layer_norm(ALPHA * h_ref[...] + m, g_ref[...], b_ref[...])
    out_ref[...] = out
    hp_ref[...] = _pack_pair(out[:, :HALF], out[:, HALF:])


def _glu_ln_call(z_p, z_tail, w_bf, bg, h, g, b):
    vec = pl.BlockSpec((1, D), lambda i: (0, 0))
    return pl.pallas_call(
        _glu_ln_kernel,
        grid=(N_ROW_TILES,),
        in_specs=[
            pl.BlockSpec((TM, D), _PROMPT_TILE),
            pl.BlockSpec((TM, D), lambda i: (0, 0)),
            pl.BlockSpec((D, 2 * D), lambda i: (0, 0), pipeline_mode=pl.Buffered(1)),
            pl.BlockSpec((1, 2 * D), lambda i: (0, 0)),
            pl.BlockSpec((TM, D), lambda i: (i, 0)),
            vec,
            vec,
        ],
        out_specs=[pl.BlockSpec((TM, D), lambda i: (i, 0)), pl.BlockSpec((TM, HALF), lambda i: (i, 0))],
        out_shape=[jax.ShapeDtypeStruct((T_ALL, D), F32), jax.ShapeDtypeStruct((T_ALL, HALF), U32)],
        compiler_params=_cparams(("parallel",)),
        name="glu_ln",
    )(z_p, z_tail, w_bf, bg, h, g, b)


GROUPS_PER_CHUNK = LANES // GC


def _block_diag_in(t):
    t3 = t.reshape(NCHUNK, LANES, NS)
    same = (np.arange(LANES)[:, None] // GC) == (np.arange(CH_STATE)[None, :] // NS)
    return jnp.tile(t3, (1, 1, GROUPS_PER_CHUNK)) * jnp.asarray(same, t.dtype)


def _block_diag_out(t):
    t3 = jnp.swapaxes(t, 1, 2).reshape(NCHUNK, CH_STATE, GC)
    same = (np.arange(CH_STATE)[:, None] // NS) == (np.arange(LANES)[None, :] // GC)
    return jnp.tile(t3, (1, 1, GROUPS_PER_CHUNK)) * jnp.asarray(same, t.dtype)


def kernel(x_prompt, x_sample, cache_k, cache_v, state_ssm_re, state_ssm_im, meta_tokens, w_qkv, b_qkv, attn_sinks, w_o, b_o, ssm_lam_re, ssm_lam_im, ssm_log_dt, ssm_b_re, ssm_b_im, ssm_c_re, ssm_c_im, ssm_d, w_glu, b_glu, ln_mix_g, ln_mix_b, w_router, router_bias, w_exp_gate, w_exp_up, w_exp_down, w_sh_gate, w_sh_up, w_sh_down, ln_ffn_g, ln_ffn_b):
    moe_w = (w_router, router_bias, w_exp_gate, w_exp_up, w_exp_down, w_sh_gate, w_sh_up, w_sh_down,
             ln_ffn_g, ln_ffn_b)

    front = jnp.concatenate([jnp.zeros((PAD_FRONT, D), F32), meta_tokens], axis=0)
    pieces = []
    for b in range(NB):
        pieces += [front, x_prompt[b]]
    h = jnp.concatenate(pieces + [x_sample.reshape(DEC, D)], axis=0)

    rc, rs1, rs2 = _rope_tables()
    q, k, v = _qkv_call(h, w_qkv[0].astype(BF16), b_qkv[0][None, :], rc, rs1, rs2)
    o_p = _attn_p_call(attn_sinks[0], q, k, v)
    q3 = q[T_PROMPT:].astype(F32).reshape(DEC, NH, HD)
    o_s, ck_new, cv_new = _attn_s_call(
        attn_sinks[0][:, None], q3, k, v,
        cache_k[0].reshape(DEC, BLK, NKV * HD), cache_v[0].reshape(DEC, BLK, NKV * HD))
    o_tail = _tail_tile(o_p, o_s.reshape(DEC, D).astype(BF16))
    h, hp = _oproj_ln_call(o_p, o_tail, w_o[0].astype(BF16), b_o[0][None, :], h,
                           ln_mix_g[0][None, :], ln_mix_b[0][None, :])
    h = _moe_layer(0, False, h, hp, *moe_w)

    def last_window(t):
        rows = [t[(b + 1) * LP - BLK:(b + 1) * LP] for b in range(NB)]
        return jnp.stack(rows).reshape(NB, BLK, NKV, HD)

    kp = last_window(k)
    vp = last_window(v)

    ab_re, ab_im, bb_re, bb_im = _ssm_prep_call(
        ssm_lam_re[0], ssm_lam_im[0], ssm_log_dt[0], ssm_b_re[0], ssm_b_im[0])
    wb = jnp.concatenate([_block_diag_in(bb_re), _block_diag_in(bb_im)], axis=2)
    wc = jnp.concatenate([_block_diag_out(ssm_c_re[0]), -_block_diag_out(ssm_c_im[0])], axis=1)
    wc_bf = wc.astype(BF16)
    a_tab = jnp.stack([ab_re[:, :NS].reshape(NCHUNK, CH_STATE),
                       ab_im[:, :NS].reshape(NCHUNK, CH_STATE)], axis=1)
    d_tab = ssm_d[0].reshape(NCHUNK, 1, LANES)
    z_p, st_p = _ssm_p_call(h, wb.astype(BF16), wc_bf, a_tab, d_tab)
    z_s, sr_new, si_new = _ssm_s_call(
        h, state_ssm_re[0].reshape(DEC, NG * NS), state_ssm_im[0].reshape(DEC, NG * NS),
        wb, wc_bf, a_tab, d_tab)
    h, hp = _glu_ln_call(z_p, _tail_tile(z_p, z_s), w_glu[0].astype(BF16), b_glu[0][None, :], h,
                         ln_mix_g[1][None, :], ln_mix_b[1][None, :])
    y_prompt, y_sample = _moe_layer(1, True, h, hp, *moe_w)
    y_prompt = y_prompt.reshape(NB, SEQ, D)
    y_sample = y_sample.reshape(DEC, 1, D)
    st_p = st_p.reshape(NB, NCHUNK, 2, 8, NS)
    rp = st_p[:, :, 0].reshape(NB, NG, NS)
    ip = st_p[:, :, 1].reshape(NB, NG, NS)
    return (y_prompt, y_sample,
            kp[None], vp[None],
            ck_new.reshape(1, DEC, BLK, NKV, HD), cv_new.reshape(1, DEC, BLK, NKV, HD),
            rp[None], ip[None],
            sr_new.reshape(1, DEC, NG, NS), si_new.reshape(1, DEC, NG, NS))
```

```python
import functools
import math

import jax
import jax.numpy as jnp
import numpy as np
from jax import lax
from jax.experimental import pallas as pl
from jax.experimental.pallas import tpu as pltpu

F32 = jnp.float32
BF16 = jnp.bfloat16
I32 = jnp.int32
U32 = jnp.uint32

D = 2048
HALF = D // 2
NB = 4
N_META = 16
SEQ = 2048
L = N_META + SEQ
BLK = 128
PAD_FRONT = (-L) % BLK
LP = L + PAD_FRONT
NBLK = LP // BLK
T_PROMPT = NB * LP
DEC = 128
T_ALL = T_PROMPT + DEC
PAST_LEN = 8192
HD = 64
NH = 32
NKV = 4
QPK = NH // NKV
QKV = (NH + 2 * NKV) * HD
QK_COLS = (NH + NKV) * HD
ROT = HD // 4
ROT_HALF = ROT // 2
ROPE_THETA = 500000.0
NG = 128
GC = 16
NS = 64
NCHUNK = 16
CH_STATE = 8 * NS
NE = 64
TOPK = 8
NEG = 8
PER_GRP = NE // NEG
TOPG = 4
DE = 512
ROUTED_SCALE = 2.5
DEPTH = 2
ALPHA = (2 * DEPTH) ** 0.25
LN_EPS = 1e-5

V7X_VMEM_BYTES = 64 * 1024 * 1024
VMEM_LIMIT = 56 * 1024 * 1024
LANES = 128

TM = 384
N_ROW_TILES = T_ALL // TM
TAIL_START = (N_ROW_TILES - 1) * TM
TM_E = 256
N_PAIRS = T_ALL * TOPK
NT_E = -(-N_PAIRS // TM_E) + NE
R_ROWS = NT_E * TM_E
TOK_BITS = 14
TOK_MASK = (1 << TOK_BITS) - 1
DUMMY_BASE = TOPK * T_ALL
YK_ROWS = DUMMY_BASE + TM_E
assert T_ALL <= TOK_MASK and T_ALL % TOPK == 0 and T_ALL % TM == 0 and TAIL_START <= T_PROMPT


def _cparams(sem):
    return pltpu.CompilerParams(dimension_semantics=sem, vmem_limit_bytes=VMEM_LIMIT)


def _sigmoid(x):
    return 1.0 / (1.0 + jnp.exp(-x))


def _layer_norm(y, g, b):
    mu = jnp.mean(y, axis=-1, keepdims=True)
    yc = y - mu
    var = jnp.mean(yc * yc, axis=-1, keepdims=True)
    return yc * lax.rsqrt(var + LN_EPS) * g + b


def _pack_pair(lo, hi):
    lo_b = lax.bitcast_convert_type(lo.astype(BF16).astype(F32), U32) >> 16
    hi_b = lax.bitcast_convert_type(hi.astype(BF16).astype(F32), U32) & jnp.uint32(0xFFFF0000)
    return lo_b | hi_b


def _unpack_pair(w):
    lo = lax.bitcast_convert_type(w << 16, F32)
    hi = lax.bitcast_convert_type(w & jnp.uint32(0xFFFF0000), F32)
    return lo, hi


def _qkv_kernel(x_ref, w_ref, b_ref, c_ref, s1_ref, s2_ref, q_ref, k_ref, v_ref):
    xb = x_ref[...].astype(BF16)
    acc = jnp.dot(xb, w_ref[...], preferred_element_type=F32) + b_ref[...]
    c = c_ref[...]
    s1 = s1_ref[...]
    s2 = s2_ref[...]
    for j in range(QK_COLS // LANES):
        blk = acc[:, j * LANES:(j + 1) * LANES]
        r = (blk * c + pltpu.roll(blk, LANES - ROT_HALF, axis=1) * s1
             + pltpu.roll(blk, ROT_HALF, axis=1) * s2)
        if j < D // LANES:
            q_ref[:, j * LANES:(j + 1) * LANES] = (r * (1.0 / math.sqrt(HD))).astype(BF16)
        else:
            jj = j - D // LANES
            k_ref[:, jj * LANES:(jj + 1) * LANES] = r
    v_ref[...] = acc[:, QK_COLS:]


def _qkv_call(x, w_bf, b, rc, rs1, rs2):
    return pl.pallas_call(
        _qkv_kernel,
        grid=(T_ALL // TM,),
        in_specs=[
            pl.BlockSpec((TM, D), lambda i: (i, 0)),
            pl.BlockSpec((D, QKV), lambda i: (0, 0)),
            pl.BlockSpec((1, QKV), lambda i: (0, 0)),
            pl.BlockSpec((TM, LANES), lambda i: (i, 0)),
            pl.BlockSpec((TM, LANES), lambda i: (i, 0)),
            pl.BlockSpec((TM, LANES), lambda i: (i, 0)),
        ],
        out_specs=[
            pl.BlockSpec((TM, D), lambda i: (i, 0)),
            pl.BlockSpec((TM, NKV * HD), lambda i: (i, 0)),
            pl.BlockSpec((TM, NKV * HD), lambda i: (i, 0)),
        ],
        out_shape=[
            jax.ShapeDtypeStruct((T_ALL, D), BF16),
            jax.ShapeDtypeStruct((T_ALL, NKV * HD), F32),
            jax.ShapeDtypeStruct((T_ALL, NKV * HD), F32),
        ],
        compiler_params=_cparams(("parallel",)),
        name="qkv_rope",
    )(x, w_bf, b, rc, rs1, rs2)


def _rope_tables():
    pos_p = jnp.maximum(jnp.arange(LP, dtype=I32) - PAD_FRONT, 0)
    pos = jnp.concatenate([pos_p, jnp.full((8,), PAST_LEN, I32)]).astype(F32)
    inv_freq = ROPE_THETA ** (-jnp.arange(0, ROT, 2, dtype=F32) / ROT)
    ang = pos[:, None] * inv_freq[None, :]
    cos = jnp.cos(ang)
    sin = jnp.sin(ang)
    lane = np.arange(LANES) % HD
    freq = np.arange(ROT_HALF)[:, None]
    first = ((lane[None, :] == freq) & (lane[None, :] < ROT_HALF)).astype(np.float32)
    second = ((lane[None, :] - ROT_HALF == freq) & (lane[None, :] < ROT)).astype(np.float32)
    rest = (lane >= ROT).astype(np.float32)[None, :]
    place = functools.partial(jnp.dot, precision=lax.Precision.HIGHEST)
    c = place(cos, jnp.asarray(first + second)) + jnp.asarray(rest)
    s1 = place(sin, jnp.asarray(-first))
    s2 = place(sin, jnp.asarray(second))

    def all_rows(t):
        return jnp.concatenate([t[:LP]] * NB + [t[LP:]] * (DEC // 8), axis=0)

    return all_rows(c), all_rows(s1), all_rows(s2)


def _attn_p_kernel(sink_ref, q_ref, kp_ref, kc_ref, vp_ref, vc_ref, o_ref):
    j = pl.program_id(0) % NBLK
    keys = jnp.concatenate([kp_ref[...], kc_ref[...]], axis=0).astype(BF16)
    vals = jnp.concatenate([vp_ref[...], vc_ref[...]], axis=0).astype(BF16)
    r = lax.broadcasted_iota(I32, (BLK, 2 * BLK), 0)
    c = lax.broadcasted_iota(I32, (BLK, 2 * BLK), 1)
    dist = BLK + r - c
    kpos = (j - 1) * BLK - PAD_FRONT + c
    mask = (dist >= 0) & (dist <= BLK) & (kpos >= 0)
    for g in range(NKV):
        kg = keys[:, g * HD:(g + 1) * HD]
        vg = vals[:, g * HD:(g + 1) * HD]
        heads = range(g * QPK, (g + 1) * QPK)
        scores = [lax.dot_general(q_ref[:, h * HD:(h + 1) * HD], kg, (((1,), (1,)), ((), ())),
                                  preferred_element_type=F32) for h in heads]
        probs, rdens = [], []
        for h, s in zip(heads, scores):
            s = jnp.where(mask, s, -jnp.inf)
            sk = sink_ref[h]
            m = jnp.maximum(jnp.max(s, axis=1, keepdims=True), sk)
            p = jnp.exp(s - m)
            rdens.append(1.0 / (jnp.sum(p, axis=1, keepdims=True) + jnp.exp(sk - m)))
            probs.append(p.astype(BF16))
        for h, p, rden in zip(heads, probs, rdens):
            oh = jnp.dot(p, vg, preferred_element_type=F32) * rden
            o_ref[:, h * HD:(h + 1) * HD] = oh.astype(BF16)


def _attn_p_call(sinks, q, k, v):
    prev = lambda i: (jnp.where(i % NBLK == 0, i, i - 1), 0)
    cur = lambda i: (i, 0)
    return pl.pallas_call(
        _attn_p_kernel,
        grid=(NB * NBLK,),
        in_specs=[
            pl.BlockSpec(memory_space=pltpu.SMEM),
            pl.BlockSpec((BLK, D), cur),
            pl.BlockSpec((BLK, NKV * HD), prev),
            pl.BlockSpec((BLK, NKV * HD), cur),
            pl.BlockSpec((BLK, NKV * HD), prev),
            pl.BlockSpec((BLK, NKV * HD), cur),
        ],
        out_specs=pl.BlockSpec((BLK, D), cur),
        out_shape=jax.ShapeDtypeStruct((T_PROMPT, D), BF16),
        compiler_params=_cparams(("parallel",)),
        name="attn_prompt",
    )(sinks, q, k, k, v, v)


SEQ_PER_STEP = 16
SEQ_UNROLL = 4


def _attn_s_kernel(sink_ref, q_ref, kn_ref, vn_ref, ck_ref, cv_ref, o_ref, cko_ref, cvo_ref):
    row = lax.broadcasted_iota(I32, (BLK, NKV * HD), 0)
    hrow = lax.broadcasted_iota(I32, (NH, NKV * HD), 0) // QPK
    hlane = lax.broadcasted_iota(I32, (NH, NKV * HD), 1) // HD
    own = hrow == hlane
    sk = sink_ref[...]

    def score_stage(s):
        kn = kn_ref[pl.ds(s, 1), :]
        knr = kn.astype(BF16).astype(F32)
        q = q_ref[s].astype(BF16)
        qe = jnp.where(own, jnp.concatenate([q] * NKV, axis=1), jnp.zeros((), BF16))
        sc = lax.dot_general(qe, ck_ref[s].astype(BF16), (((1,), (1,)), ((), ())),
                             preferred_element_type=F32)
        sn = jnp.sum(qe.astype(F32) * knr, axis=1, keepdims=True)
        return sc, sn

    def softmax_stage(sc, sn):
        m = jnp.maximum(jnp.maximum(jnp.max(sc, axis=1, keepdims=True), sn), sk)
        p = jnp.exp(sc - m)
        pn = jnp.exp(sn - m)
        rden = 1.0 / (jnp.sum(p, axis=1, keepdims=True) + pn + jnp.exp(sk - m))
        return p.astype(BF16), pn.astype(BF16).astype(F32), rden

    def value_stage(s, p, pn, rden):
        vn = vn_ref[pl.ds(s, 1), :]
        vnr = vn.astype(BF16).astype(F32)
        of = jnp.dot(p, cv_ref[s].astype(BF16), preferred_element_type=F32)
        of = jnp.where(own, of + pn * vnr, 0.0)
        og = of[:, 0:HD]
        for g in range(1, NKV):
            og = og + of[:, g * HD:(g + 1) * HD]
        o_ref[s] = og * rden
        cko_ref[s] = jnp.where(row == BLK - 1, kn_ref[pl.ds(s, 1), :], pltpu.roll(ck_ref[s], BLK - 1, axis=0))
        cvo_ref[s] = jnp.where(row == BLK - 1, vn, pltpu.roll(cv_ref[s], BLK - 1, axis=0))

    def body(it, carry):
        seqs = [it * SEQ_UNROLL + u for u in range(SEQ_UNROLL)]
        staged = [score_stage(s) for s in seqs]
        soft = [softmax_stage(*st) for st in staged]
        for s, sm in zip(seqs, soft):
            value_stage(s, *sm)
        return carry

    lax.fori_loop(0, SEQ_PER_STEP // SEQ_UNROLL, body, 0)


def _attn_s_call(sinks_col, q3, k, v, cache_k, cache_v):
    sp = SEQ_PER_STEP
    kv_off = T_PROMPT // sp
    return pl.pallas_call(
        _attn_s_kernel,
        grid=(DEC // sp,),
        in_specs=[
            pl.BlockSpec((NH, 1), lambda i: (0, 0)),
            pl.BlockSpec((sp, NH, HD), lambda i: (i, 0, 0)),
            pl.BlockSpec((sp, NKV * HD), lambda i: (kv_off + i, 0)),
            pl.BlockSpec((sp, NKV * HD), lambda i: (kv_off + i, 0)),
            pl.BlockSpec((sp, BLK, NKV * HD), lambda i: (i, 0, 0)),
            pl.BlockSpec((sp, BLK, NKV * HD), lambda i: (i, 0, 0)),
        ],
        out_specs=[
            pl.BlockSpec((sp, NH, HD), lambda i: (i, 0, 0)),
            pl.BlockSpec((sp, BLK, NKV * HD), lambda i: (i, 0, 0)),
            pl.BlockSpec((sp, BLK, NKV * HD), lambda i: (i, 0, 0)),
        ],
        out_shape=[
            jax.ShapeDtypeStruct((DEC, NH, HD), F32),
            jax.ShapeDtypeStruct((DEC, BLK, NKV * HD), F32),
            jax.ShapeDtypeStruct((DEC, BLK, NKV * HD), F32),
        ],
        compiler_params=_cparams(("parallel",)),
        name="attn_sample",
    )(sinks_col, q3, k, v, cache_k, cache_v)


def _prompt_or_tail(x_ref, tail_ref):
    return jnp.where(pl.program_id(0) == N_ROW_TILES - 1, tail_ref[...], x_ref[...])


def _tail_tile(x_prompt_rows, x_sample_rows):
    return jnp.concatenate([x_prompt_rows[TAIL_START:], x_sample_rows], axis=0)


_PROMPT_TILE = lambda i: (jnp.minimum(i, N_ROW_TILES - 2), 0)


def _oproj_ln_kernel(o_ref, ot_ref, w_ref, bo_ref, h_ref, g_ref, b_ref, out_ref, hp_ref):
    o = _prompt_or_tail(o_ref, ot_ref)
    m = jnp.dot(o, w_ref[...], preferred_element_type=F32) + bo_ref[...]
    out = _layer_norm(ALPHA * h_ref[...] + m, g_ref[...], b_ref[...])
    out_ref[...] = out
    hp_ref[...] = _pack_pair(out[:, :HALF], out[:, HALF:])


def _oproj_ln_call(o_p, o_tail, w_bf, bo, h, g, b):
    vec = pl.BlockSpec((1, D), lambda i: (0, 0))
    return pl.pallas_call(
        _oproj_ln_kernel,
        grid=(N_ROW_TILES,),
        in_specs=[
            pl.BlockSpec((TM, D), _PROMPT_TILE),
            pl.BlockSpec((TM, D), lambda i: (0, 0)),
            pl.BlockSpec((D, D), lambda i: (0, 0)),
            vec,
            pl.BlockSpec((TM, D), lambda i: (i, 0)),
            vec,
            vec,
        ],
        out_specs=[pl.BlockSpec((TM, D), lambda i: (i, 0)), pl.BlockSpec((TM, HALF), lambda i: (i, 0))],
        out_shape=[jax.ShapeDtypeStruct((T_ALL, D), F32), jax.ShapeDtypeStruct((T_ALL, HALF), U32)],
        compiler_params=_cparams(("parallel",)),
        name="oproj_ln",
    )(o_p, o_tail, w_bf, bo, h, g, b)


def _router_kernel(h_ref, wr_ref, rb_ref, eidx_ref, wsel_ref, rank_ref, cnt_ref, carry_ref):
    i = pl.program_id(0)

    @pl.when(i == 0)
    def _():
        carry_ref[...] = jnp.zeros_like(carry_ref)

    h = h_ref[...]
    w = wr_ref[...]
    h_hi = h.astype(BF16)
    h_lo = (h - h_hi.astype(F32)).astype(BF16)
    w_hi = w.astype(BF16)
    w_lo = (w - w_hi.astype(F32)).astype(BF16)
    logits = (jnp.dot(h_hi, w_hi, preferred_element_type=F32)
              + (jnp.dot(h_hi, w_lo, preferred_element_type=F32)
                 + jnp.dot(h_lo, w_hi, preferred_element_type=F32)))
    scores = _sigmoid(logits.T[0:NE, :])
    biased = scores + rb_ref[...]
    ninf = -jnp.inf
    sub = lax.broadcasted_iota(I32, (PER_GRP, TM), 0).astype(F32)
    sc_g = [scores[g * PER_GRP:(g + 1) * PER_GRP, :] for g in range(NEG)]
    b_g = [biased[g * PER_GRP:(g + 1) * PER_GRP, :] for g in range(NEG)]
    e_g = [sub + float(g * PER_GRP) for g in range(NEG)]

    def smax(x):
        return jnp.max(x, axis=0, keepdims=True)

    def smin(x):
        return jnp.min(x, axis=0, keepdims=True)

    gs = []
    for g in range(NEG):
        m1 = smax(b_g[g])
        i1 = smin(jnp.where(b_g[g] == m1, sub, float(PER_GRP)))
        m2 = smax(jnp.where(sub == i1, ninf, b_g[g]))
        gs.append(m1 + m2)

    work = []
    for g in range(NEG):
        beaten = jnp.zeros((1, TM), F32)
        for o in range(NEG):
            if o != g:
                wins = (gs[o] >= gs[g]) if o < g else (gs[o] > gs[g])
                beaten = beaten + jnp.where(wins, 1.0, 0.0)
        work.append(jnp.where(beaten < float(TOPG), b_g[g], ninf))

    idx_rows, w_rows = [], []
    onehot = [jnp.zeros((PER_GRP, TM), F32) for _ in range(NEG)]
    for _ in range(TOPK):
        m = smax(work[0])
        for g in range(1, NEG):
            m = jnp.maximum(m, smax(work[g]))
        ik = smin(jnp.where(work[0] == m, e_g[0], float(NE)))
        for g in range(1, NEG):
            ik = jnp.minimum(ik, smin(jnp.where(work[g] == m, e_g[g], float(NE))))
        wk = jnp.zeros((1, TM), F32)
        for g in range(NEG):
            hit = e_g[g] == ik
            wk = wk + jnp.sum(jnp.where(hit, sc_g[g], 0.0), axis=0, keepdims=True)
            onehot[g] = jnp.where(hit, 1.0, onehot[g])
            work[g] = jnp.where(hit, ninf, work[g])
        idx_rows.append(ik)
        w_rows.append(wk)
    wsum = w_rows[0]
    for wk in w_rows[1:]:
        wsum = wsum + wk

    rr = lax.broadcasted_iota(I32, (TM, TM), 0)
    cc = lax.broadcasted_iota(I32, (TM, TM), 1)
    tri = jnp.where(rr < cc, 1.0, 0.0).astype(BF16)
    oh = jnp.concatenate(onehot, axis=0)
    prefix = jnp.dot(oh.astype(BF16), tri, preferred_element_type=F32) + carry_ref[...]
    carry_ref[...] = carry_ref[...] + jnp.sum(oh, axis=1, keepdims=True)
    cnt_ref[...] = carry_ref[...]

    rank_rows = []
    for k in range(TOPK):
        rk = jnp.zeros((1, TM), F32)
        for g in range(NEG):
            pg = prefix[g * PER_GRP:(g + 1) * PER_GRP, :]
            rk = rk + jnp.sum(jnp.where(e_g[g] == idx_rows[k], pg, 0.0), axis=0, keepdims=True)
        rank_rows.append(rk)
    eidx_ref[...] = jnp.concatenate(idx_rows, axis=0).astype(I32)
    rank_ref[...] = jnp.concatenate(rank_rows, axis=0).astype(I32)
    w_t = jnp.concatenate([wk / wsum * ROUTED_SCALE for wk in w_rows]
                          + [jnp.zeros((LANES - TOPK, TM), F32)], axis=0)
    wsel_ref[...] = w_t.T[:, 0:TOPK]


def _router_call(h, w_router, router_bias):
    tk = pl.BlockSpec((TM, TOPK), lambda i: (i, 0))
    kt = pl.BlockSpec((TOPK, TM), lambda i: (0, i))
    return pl.pallas_call(
        _router_kernel,
        grid=(T_ALL // TM,),
        in_specs=[
            pl.BlockSpec((TM, D), lambda i: (i, 0)),
            pl.BlockSpec((D, LANES), lambda i: (0, 0)),
            pl.BlockSpec((NE, 1), lambda i: (0, 0)),
        ],
        out_specs=[kt, tk, kt, pl.BlockSpec((NE, 1), lambda i: (0, 0))],
        out_shape=[
            jax.ShapeDtypeStruct((TOPK, T_ALL), I32),
            jax.ShapeDtypeStruct((T_ALL, TOPK), F32),
            jax.ShapeDtypeStruct((TOPK, T_ALL), I32),
            jax.ShapeDtypeStruct((NE, 1), F32),
        ],
        scratch_shapes=[pltpu.VMEM((NE, 1), F32)],
        compiler_params=_cparams(("arbitrary",)),
        name="router",
    )(h, jnp.pad(w_router, ((0, 0), (0, LANES - NE))), router_bias[:, None])


def _tiles_of(cnt):
    return (cnt + TM_E - 1) // TM_E


def _plan_kernel(cnt_ref, eidx_ref, rank_ref, pos_ref, te_ref, gi_ref, ne_ref, na_ref, off_s):
    def offsets(e, run):
        off_s[e] = run
        return run + _tiles_of(cnt_ref[e]) * TM_E

    total = lax.fori_loop(0, NE, offsets, 0)
    na = total // TM_E
    na_ref[0] = na

    def idle(t, carry):
        te_ref[t] = NE - 1
        gi_ref[t] = 0
        ne_ref[t] = -1
        return carry

    lax.fori_loop(na, NT_E, idle, 0)

    def forward(e, ordinal):
        t0 = off_s[e] // TM_E
        nt = _tiles_of(cnt_ref[e])

        def mark(t, carry):
            te_ref[t] = e
            gi_ref[t] = ordinal
            return carry

        lax.fori_loop(t0, t0 + nt, mark, 0)
        return ordinal + jnp.where(nt > 0, 1, 0)

    lax.fori_loop(0, NE, forward, 0)

    def backward(i, nxt):
        e = NE - 1 - i
        t0 = off_s[e] // TM_E
        nt = _tiles_of(cnt_ref[e])

        def mark(t, carry):
            ne_ref[t] = nxt
            return carry

        lax.fori_loop(t0, t0 + nt, mark, 0)
        return jnp.where(nt > 0, e, nxt)

    lax.fori_loop(0, NE, backward, -1)

    eidx = eidx_ref[...]
    pos = rank_ref[...]
    for e in range(NE):
        pos = pos + jnp.where(eidx == e, off_s[e], 0)
    pos_ref[...] = pos


def _plan_call(cnt, eidx_t, rank_t):
    smem = pl.BlockSpec(memory_space=pltpu.SMEM)
    vmem = pl.BlockSpec(memory_space=pltpu.VMEM)
    tiles = jax.ShapeDtypeStruct((NT_E,), I32)
    return pl.pallas_call(
        _plan_kernel,
        in_specs=[smem, vmem, vmem],
        out_specs=[vmem, smem, smem, smem, smem, smem],
        out_shape=[jax.ShapeDtypeStruct((TOPK, T_ALL), I32), tiles, tiles, tiles,
                   jax.ShapeDtypeStruct((1,), I32), jax.ShapeDtypeStruct((NE,), I32)],
        name="moe_plan",
    )(cnt, eidx_t, rank_t)


INFO_UNROLL = 16
WORD_STEP = (1 << TOK_BITS) + 1
assert T_ALL % INFO_UNROLL == 0


def _rowinfo_kernel(pos_ref, pad_ref, info_ref, sem):
    slot = pl.program_id(0)

    @pl.when(slot == 0)
    def _():
        cp = pltpu.make_async_copy(pad_ref, info_ref, sem)
        cp.start()
        cp.wait()

    word0 = (slot * T_ALL) << TOK_BITS

    def body(j, carry):
        base = j * INFO_UNROLL
        wj = word0 + base * WORD_STEP
        for u in range(INFO_UNROLL):
            info_ref[pos_ref[0, 0, base + u]] = wj + u * WORD_STEP
        return carry

    lax.fori_loop(0, T_ALL // INFO_UNROLL, body, 0)


def _rowinfo_call(pos_t):
    rows = np.arange(R_ROWS, dtype=np.int64)
    pads = jnp.asarray(((DUMMY_BASE + (rows & (TM_E - 1))) << TOK_BITS).astype(np.int32))
    return pl.pallas_call(
        _rowinfo_kernel,
        grid=(TOPK,),
        in_specs=[pl.BlockSpec((1, 1, T_ALL), lambda i: (i, 0, 0), memory_space=pltpu.SMEM),
                  pl.BlockSpec(memory_space=pl.ANY)],
        out_specs=pl.BlockSpec(memory_space=pltpu.SMEM),
        out_shape=jax.ShapeDtypeStruct((R_ROWS,), I32),
        scratch_shapes=[pltpu.SemaphoreType.DMA(())],
        compiler_params=pltpu.CompilerParams(dimension_semantics=("arbitrary",)),
        name="moe_rowinfo",
    )(pos_t.reshape(TOPK, 1, T_ALL), pads)


def _expert_kernel(layer, te_ref, gi_ref, ne_ref, na_ref, ginfo_ref, sinfo_ref, hp_ref,
                   wg_hbm, wu_hbm, wd_hbm, yk_ref,
                   xb0, xb1, yb0, yb1, wgf, wuf, wdf, wg_s, wu_s, wd_s, gsem, ssem, wsem, zsem):
    s = pl.program_id(0)
    na = na_ref[0]
    xb = (xb0, xb1)
    yb = (yb0, yb1)

    def weight_copies(e, slot):
        return (pltpu.make_async_copy(wg_hbm.at[layer, e], wgf.at[slot], wsem.at[0]),
                pltpu.make_async_copy(wu_hbm.at[layer, e], wuf.at[slot], wsem.at[1]),
                pltpu.make_async_copy(wd_hbm.at[layer, e], wdf.at[slot], wsem.at[2]))

    def gather_row(par, r, thread=0):
        tok = ginfo_ref[0, 0, r] & TOK_MASK
        pltpu.make_async_copy(hp_ref.at[pl.ds(tok, 1)], xb[par].at[pl.ds(r, 1)], gsem.at[par]).start(
            priority=thread)

    def scatter_row(par, r, thread=0):
        dst = sinfo_ref[0, 0, r] >> TOK_BITS
        pltpu.make_async_copy(yb[par].at[pl.ds(r, 1)], yk_ref.at[pl.ds(dst, 1)], ssem.at[par]).start(
            priority=thread)

    def compute(par, between):
        lo, hi = _unpack_pair(xb[1 - par][...])
        lo = lo.astype(BF16)
        hi = hi.astype(BF16)
        between(0)
        hg = (jnp.dot(lo, wg_s[0:HALF, :], preferred_element_type=F32)
              + jnp.dot(hi, wg_s[HALF:D, :], preferred_element_type=F32))
        between(1)
        hu = (jnp.dot(lo, wu_s[0:HALF, :], preferred_element_type=F32)
              + jnp.dot(hi, wu_s[HALF:D, :], preferred_element_type=F32))
        between(2)
        act = (hg * _sigmoid(hg) * hu).astype(BF16)
        y = jnp.dot(act, wd_s[...], preferred_element_type=F32)
        between(3)
        yb[1 - par][...] = _pack_pair(y[:, :HALF], y[:, HALF:])

    @pl.when(jnp.logical_and(s >= 1, s <= na))
    def _():
        pltpu.make_async_copy(hp_ref.at[pl.ds(0, TM_E)], xb0, gsem.at[(s + 1) % 2]).wait()

    @pl.when(jnp.logical_and(s >= 3, s <= na + 2))
    def _():
        pltpu.make_async_copy(yb0, yk_ref.at[pl.ds(0, TM_E)], ssem.at[(s + 1) % 2]).wait()

    @pl.when(s == 0)
    def _():
        for cp in weight_copies(te_ref[0], 0):
            cp.start(priority=1)
        yb0[...] = jnp.zeros_like(yb0)
        zc = pltpu.make_async_copy(yb0, yk_ref.at[pl.ds(DUMMY_BASE, TM_E)], zsem)
        zc.start()
        zc.wait()

    c = s - 1
    cc = jnp.clip(c, 0, na - 1)
    first = jnp.logical_or(c == 0, te_ref[cc] != te_ref[jnp.maximum(cc - 1, 0)])

    @pl.when(jnp.logical_and(jnp.logical_and(c >= 0, c < na), first))
    def _():
        slot = gi_ref[cc] % 2
        for cp in weight_copies(te_ref[cc], slot):
            cp.wait()
        wg_s[...] = wgf[slot].astype(BF16)
        wu_s[...] = wuf[slot].astype(BF16)
        wd_s[...] = wdf[slot].astype(BF16)
        nxt = ne_ref[cc]

        @pl.when(nxt >= 0)
        def _():
            for cp in weight_copies(nxt, 1 - slot):
                cp.start(priority=1)

    steady = jnp.logical_and(s >= 2, s < na)
    for par in (0, 1):
        mine = (s % 2) == par

        @pl.when(jnp.logical_and(steady, mine))
        def _():
            def between(q):
                for r in range(q * (TM_E // 4), (q + 1) * (TM_E // 4)):
                    gather_row(par, r)
                    scatter_row(par, r, thread=r % 2)

            compute(par, between)

        @pl.when(jnp.logical_and(jnp.logical_not(steady), mine))
        def _():
            @pl.when(s < na)
            def _():
                def g(r, carry):
                    gather_row(par, r)
                    return carry

                lax.fori_loop(0, TM_E, g, 0)

            @pl.when(jnp.logical_and(s >= 1, s <= na))
            def _():
                compute(par, lambda q: None)

            @pl.when(jnp.logical_and(s >= 2, s <= na + 1))
            def _():
                def sc(r, carry):
                    scatter_row(par, r)
                    return carry

                lax.fori_loop(0, TM_E, sc, 0)


def _expert_call(layer, te, gi, ne, nact, rowinfo, hp, w_gate, w_up, w_down):
    def gmap(s, te_r, gi_r, ne_r, na_r):
        return (jnp.minimum(s, na_r[0] - 1), 0, 0)

    def smap(s, te_r, gi_r, ne_r, na_r):
        return (jnp.clip(s - 2, 0, na_r[0] - 1), 0, 0)

    info = rowinfo.reshape(NT_E, 1, TM_E)
    anyspec = pl.BlockSpec(memory_space=pl.ANY)
    return pl.pallas_call(
        functools.partial(_expert_kernel, layer),
        grid_spec=pltpu.PrefetchScalarGridSpec(
            num_scalar_prefetch=4,
            grid=(NT_E + 3,),
            in_specs=[
                pl.BlockSpec((1, 1, TM_E), gmap, memory_space=pltpu.SMEM),
                pl.BlockSpec((1, 1, TM_E), smap, memory_space=pltpu.SMEM),
                anyspec, anyspec, anyspec, anyspec,
            ],
            out_specs=anyspec,
            scratch_shapes=[
                pltpu.VMEM((TM_E, HALF), U32),
                pltpu.VMEM((TM_E, HALF), U32),
                pltpu.VMEM((TM_E, HALF), U32),
                pltpu.VMEM((TM_E, HALF), U32),
                pltpu.VMEM((2, D, DE), F32),
                pltpu.VMEM((2, D, DE), F32),
                pltpu.VMEM((2, DE, D), F32),
                pltpu.VMEM((D, DE), BF16),
                pltpu.VMEM((D, DE), BF16),
                pltpu.VMEM((DE, D), BF16),
                pltpu.SemaphoreType.DMA((2,)),
                pltpu.SemaphoreType.DMA((2,)),
                pltpu.SemaphoreType.DMA((3,)),
                pltpu.SemaphoreType.DMA(()),
            ],
        ),
        out_shape=jax.ShapeDtypeStruct((YK_ROWS, HALF), U32),
        compiler_params=_cparams(("arbitrary",)),
        name="moe_experts",
    )(te, gi, ne, nact, info, info, hp, w_gate, w_up, w_down)


BLK_PER_TILE = TM // BLK


def _final_copies(step, obuf, yp_ref, ys_ref, sem):
    out = []
    slot = step % 2
    for m in range(BLK_PER_TILE):
        g = step * BLK_PER_TILE + m
        b = g // NBLK
        j = g % NBLK
        src = obuf.at[slot, pl.ds(m * BLK, BLK)]
        r0 = pl.multiple_of(jnp.maximum(b * SEQ + (j - 1) * BLK, 0), BLK)
        out.append((jnp.logical_and(g < NB * NBLK, j >= 1),
                    pltpu.make_async_copy(src, yp_ref.at[pl.ds(r0, BLK)], sem.at[slot])))
        out.append((g == NB * NBLK, pltpu.make_async_copy(src, ys_ref, sem.at[slot])))
    return out


def _combine_kernel(final, w_ref, h_ref, *rest):
    yk_refs = rest[:TOPK]
    wsg_ref, wsu_ref, wsd_ref, g_ref, b_ref = rest[TOPK:TOPK + 5]
    outs = rest[TOPK + 5:]
    i = pl.program_id(0)
    if final:
        yp_ref, ys_ref, obuf, osem = outs

        def wait_step(step):
            for cond, cp in _final_copies(step, obuf, yp_ref, ys_ref, osem):
                @pl.when(cond)
                def _():
                    cp.wait()

        @pl.when(i >= 2)
        def _():
            wait_step(i - 2)

    h = h_ref[...]
    hb = h.astype(BF16)
    sg = jnp.dot(hb, wsg_ref[...], preferred_element_type=F32)
    su = jnp.dot(hb, wsu_ref[...], preferred_element_type=F32)
    act = (sg * _sigmoid(sg) * su).astype(BF16)
    y = ALPHA * h + jnp.dot(act, wsd_ref[...], preferred_element_type=F32)
    ylo = y[:, :HALF]
    yhi = y[:, HALF:]
    w = w_ref[...]
    for k in range(TOPK):
        lo, hi = _unpack_pair(yk_refs[k][...])
        wk = w[:, k:k + 1]
        ylo = ylo + wk * lo
        yhi = yhi + wk * hi
    out = _layer_norm(jnp.concatenate([ylo, yhi], axis=1), g_ref[...], b_ref[...])
    if not final:
        outs[0][...] = out
        return
    obuf[i % 2] = out
    for cond, cp in _final_copies(i, obuf, yp_ref, ys_ref, osem):
        @pl.when(cond)
        def _():
            cp.start()

    @pl.when(i == N_ROW_TILES - 1)
    def _():
        wait_step(i - 1)
        wait_step(i)


def _combine_call(final, wsel, h, yk, wsg, wsu, wsd, g, b):
    vec = pl.BlockSpec((1, D), lambda i: (0, 0))
    once = dict(pipeline_mode=pl.Buffered(1))
    yk_specs = [pl.BlockSpec((TM, HALF), lambda i, k=k: (k * N_ROW_TILES + i, 0)) for k in range(TOPK)]
    if final:
        anyspec = pl.BlockSpec(memory_space=pl.ANY)
        out_specs = [anyspec, anyspec]
        out_shape = [jax.ShapeDtypeStruct((NB * SEQ, D), F32), jax.ShapeDtypeStruct((DEC, D), F32)]
        scratch = [pltpu.VMEM((2, TM, D), F32), pltpu.SemaphoreType.DMA((2,))]
    else:
        out_specs = pl.BlockSpec((TM, D), lambda i: (i, 0))
        out_shape = jax.ShapeDtypeStruct((T_ALL, D), F32)
        scratch = []
    return pl.pallas_call(
        functools.partial(_combine_kernel, final),
        grid=(N_ROW_TILES,),
        in_specs=[
            pl.BlockSpec((TM, TOPK), lambda i: (i, 0)),
            pl.BlockSpec((TM, D), lambda i: (i, 0)),
            *yk_specs,
            pl.BlockSpec((D, DE), lambda i: (0, 0), **once),
            pl.BlockSpec((D, DE), lambda i: (0, 0), **once),
            pl.BlockSpec((DE, D), lambda i: (0, 0), **once),
            vec,
            vec,
        ],
        out_specs=out_specs,
        out_shape=out_shape,
        scratch_shapes=scratch,
        compiler_params=_cparams(("arbitrary",)),
        name="moe_combine_final" if final else "moe_combine",
    )(wsel, h, *([yk] * TOPK), wsg, wsu, wsd, g, b)


def _dispatch_kernel(cnt_ref, off_ref, na_ref, pos_ref, hp_ref, xs_ref, zero_ref, sem, zsem):
    i = pl.program_id(0)

    def issue(t, carry):
        for k in range(TOPK):
            p = pos_ref[0, 0, k * TM + t]
            pltpu.make_async_copy(hp_ref.at[pl.ds(t, 1)], xs_ref.at[pl.ds(p, 1)], sem).start(
                priority=k % 2)
        return carry

    lax.fori_loop(0, TM, issue, 0)

    @pl.when(i == 0)
    def _():
        zero_ref[...] = jnp.zeros_like(zero_ref)

        def row_copy(r):
            return pltpu.make_async_copy(zero_ref.at[pl.ds(0, 1)], xs_ref.at[pl.ds(r, 1)], zsem)

        def group_copy(q):
            r8 = pl.multiple_of(q * 8, 8)
            return pltpu.make_async_copy(zero_ref.at[pl.ds(0, 8)], xs_ref.at[pl.ds(r8, 8)], zsem)

        def tile_copy(t):
            r0 = pl.multiple_of(t * TM_E, TM_E)
            return pltpu.make_async_copy(zero_ref, xs_ref.at[pl.ds(r0, TM_E)], zsem)

        def run(copy, lo, hi):
            def start(x, carry):
                copy(x).start()
                return carry

            def wait(x, carry):
                copy(x).wait()
                return carry

            lax.fori_loop(lo, hi, start, 0)
            lax.fori_loop(lo, hi, wait, 0)

        def per_expert(e, carry):
            lo = off_ref[e] + cnt_ref[e]
            hi = off_ref[e] + _tiles_of(cnt_ref[e]) * TM_E
            mid = jnp.minimum(((lo + 7) // 8) * 8, hi)
            run(row_copy, lo, mid)
            run(group_copy, mid // 8, hi // 8)
            return carry

        lax.fori_loop(0, NE, per_expert, 0)
        run(tile_copy, na_ref[0], NT_E)

    for _ in range(TOPK):
        pltpu.make_async_copy(hp_ref, xs_ref.at[pl.ds(0, TM)], sem).wait()


def _dispatch_call(cnt, off, nact, pos_tiles, hp):
    return pl.pallas_call(
        _dispatch_kernel,
        grid_spec=pltpu.PrefetchScalarGridSpec(
            num_scalar_prefetch=3,
            grid=(N_ROW_TILES,),
            in_specs=[
                pl.BlockSpec((1, 1, TM * TOPK), lambda i, c, o, n: (i, 0, 0), memory_space=pltpu.SMEM),
                pl.BlockSpec((TM, HALF), lambda i, c, o, n: (i, 0)),
            ],
            out_specs=pl.BlockSpec(memory_space=pl.ANY),
            scratch_shapes=[
                pltpu.VMEM((TM_E, HALF), U32),
                pltpu.SemaphoreType.DMA(()),
                pltpu.SemaphoreType.DMA(()),
            ],
        ),
        out_shape=jax.ShapeDtypeStruct((R_ROWS, HALF), U32),
        compiler_params=_cparams(("arbitrary",)),
        name="moe_dispatch",
    )(cnt, off, nact, pos_tiles, hp)


def _experts_kernel(layer, te_ref, gi_ref, ne_ref, na_ref, xs_ref, wg_hbm, wu_hbm, wd_hbm, ys_ref,
                    wgf, wuf, wdf, wg_s, wu_s, wd_s, wsem):
    i = pl.program_id(0)
    na = na_ref[0]

    def weight_copies(e, slot):
        return (pltpu.make_async_copy(wg_hbm.at[layer, e], wgf.at[slot], wsem.at[0]),
                pltpu.make_async_copy(wu_hbm.at[layer, e], wuf.at[slot], wsem.at[1]),
                pltpu.make_async_copy(wd_hbm.at[layer, e], wdf.at[slot], wsem.at[2]))

    @pl.when(i == 0)
    def _():
        for cp in weight_copies(te_ref[0], 0):
            cp.start()

    ic = jnp.minimum(i, na - 1)
    first = jnp.logical_or(i == 0, te_ref[ic] != te_ref[jnp.maximum(ic - 1, 0)])

    @pl.when(jnp.logical_and(i < na, first))
    def _():
        slot = gi_ref[ic] % 2
        for cp in weight_copies(te_ref[ic], slot):
            cp.wait()
        wg_s[...] = wgf[slot].astype(BF16)
        wu_s[...] = wuf[slot].astype(BF16)
        wd_s[...] = wdf[slot].astype(BF16)
        nxt = ne_ref[ic]

        @pl.when(nxt >= 0)
        def _():
            for cp in weight_copies(nxt, 1 - slot):
                cp.start()

    @pl.when(i < na)
    def _():
        lo, hi = _unpack_pair(xs_ref[...])
        lo = lo.astype(BF16)
        hi = hi.astype(BF16)
        hg = (jnp.dot(lo, wg_s[0:HALF, :], preferred_element_type=F32)
              + jnp.dot(hi, wg_s[HALF:D, :], preferred_element_type=F32))
        hu = (jnp.dot(lo, wu_s[0:HALF, :], preferred_element_type=F32)
              + jnp.dot(hi, wu_s[HALF:D, :], preferred_element_type=F32))
        act = (hg * _sigmoid(hg) * hu).astype(BF16)
        y = jnp.dot(act, wd_s[...], preferred_element_type=F32)
        ys_ref[...] = _pack_pair(y[:, :HALF], y[:, HALF:])

    @pl.when(i >= na)
    def _():
        ys_ref[...] = jnp.zeros_like(ys_ref)


def _experts_call(layer, te, gi, ne, nact, xs, w_gate, w_up, w_down):
    anyspec = pl.BlockSpec(memory_space=pl.ANY)
    return pl.pallas_call(
        functools.partial(_experts_kernel, layer),
        grid_spec=pltpu.PrefetchScalarGridSpec(
            num_scalar_prefetch=4,
            grid=(NT_E,),
            in_specs=[
                pl.BlockSpec((TM_E, HALF), lambda i, te_r, gi_r, ne_r, na_r: (jnp.minimum(i, na_r[0] - 1), 0)),
                anyspec, anyspec, anyspec,
            ],
            out_specs=pl.BlockSpec((TM_E, HALF), lambda i, te_r, gi_r, ne_r, na_r: (i, 0)),
            scratch_shapes=[
                pltpu.VMEM((2, D, DE), F32),
                pltpu.VMEM((2, D, DE), F32),
                pltpu.VMEM((2, DE, D), F32),
                pltpu.VMEM((D, DE), BF16),
                pltpu.VMEM((D, DE), BF16),
                pltpu.VMEM((DE, D), BF16),
                pltpu.SemaphoreType.DMA((3,)),
            ],
        ),
        out_shape=jax.ShapeDtypeStruct((R_ROWS, HALF), U32),
        compiler_params=_cparams(("arbitrary",)),
        name="moe_experts",
    )(te, gi, ne, nact, xs, w_gate, w_up, w_down)


def _gather_kernel(final, pos_ref, posn_ref, w_ref, h_ref, ys_ref, wsg_ref, wsu_ref, wsd_ref,
                   g_ref, b_ref, *rest):
    i = pl.program_id(0)
    if final:
        yp_ref, ys_out_ref, gbuf, gsem, obuf, osem = rest

        def wait_step(step):
            for cond, cp in _final_copies(step, obuf, yp_ref, ys_out_ref, osem):
                @pl.when(cond)
                def _():
                    cp.wait()

        @pl.when(i >= 2)
        def _():
            wait_step(i - 2)
    else:
        out_ref, gbuf, gsem = rest

    def issue(idx_ref, slot):
        def body(t, carry):
            for k in range(TOPK):
                p = idx_ref[0, 0, k * TM + t]
                pltpu.make_async_copy(ys_ref.at[pl.ds(p, 1)], gbuf.at[slot, k, pl.ds(t, 1)],
                                      gsem.at[slot]).start(priority=k % 2)
            return carry

        lax.fori_loop(0, TM, body, 0)

    @pl.when(i == 0)
    def _():
        issue(pos_ref, 0)

    @pl.when(i + 1 < N_ROW_TILES)
    def _():
        issue(posn_ref, (i + 1) % 2)

    h = h_ref[...]
    hb = h.astype(BF16)
    sg = jnp.dot(hb, wsg_ref[...], preferred_element_type=F32)
    su = jnp.dot(hb, wsu_ref[...], preferred_element_type=F32)
    act = (sg * _sigmoid(sg) * su).astype(BF16)
    y = ALPHA * h + jnp.dot(act, wsd_ref[...], preferred_element_type=F32)
    ylo = y[:, :HALF]
    yhi = y[:, HALF:]

    slot = i % 2
    for k in range(TOPK):
        pltpu.make_async_copy(ys_ref.at[pl.ds(0, TM)], gbuf.at[slot, k], gsem.at[slot]).wait()
    w = w_ref[...]
    for k in range(TOPK):
        lo, hi = _unpack_pair(gbuf[slot, k])
        wk = w[:, k:k + 1]
        ylo = ylo + wk * lo
        yhi = yhi + wk * hi
    out = _layer_norm(jnp.concatenate([ylo, yhi], axis=1), g_ref[...], b_ref[...])
    if not final:
        out_ref[...] = out
        return
    obuf[i % 2] = out
    for cond, cp in _final_copies(i, obuf, yp_ref, ys_out_ref, osem):
        @pl.when(cond)
        def _():
            cp.start()

    @pl.when(i == N_ROW_TILES - 1)
    def _():
        wait_step(i - 1)
        wait_step(i)


def _gather_call(final, pos_tiles, wsel, h, ys, wsg, wsu, wsd, g, b):
    vec = pl.BlockSpec((1, D), lambda i: (0, 0))
    once = dict(pipeline_mode=pl.Buffered(1))
    anyspec = pl.BlockSpec(memory_space=pl.ANY)
    scratch = [pltpu.VMEM((2, TOPK, TM, HALF), U32), pltpu.SemaphoreType.DMA((2,))]
    if final:
        out_specs = [anyspec, anyspec]
        out_shape = [jax.ShapeDtypeStruct((NB * SEQ, D), F32), jax.ShapeDtypeStruct((DEC, D), F32)]
        scratch += [pltpu.VMEM((2, TM, D), F32), pltpu.SemaphoreType.DMA((2,))]
    else:
        out_specs = pl.BlockSpec((TM, D), lambda i: (i, 0))
        out_shape = jax.ShapeDtypeStruct((T_ALL, D), F32)
    smem_tile = lambda f: pl.BlockSpec((1, 1, TM * TOPK), f, memory_space=pltpu.SMEM)
    return pl.pallas_call(
        functools.partial(_gather_kernel, final),
        grid=(N_ROW_TILES,),
        in_specs=[
            smem_tile(lambda i: (i, 0, 0)),
            smem_tile(lambda i: (jnp.minimum(i + 1, N_ROW_TILES - 1), 0, 0)),
            pl.BlockSpec((TM, TOPK), lambda i: (i, 0)),
            pl.BlockSpec((TM, D), lambda i: (i, 0)),
            anyspec,
            pl.BlockSpec((D, DE), lambda i: (0, 0), **once),
            pl.BlockSpec((D, DE), lambda i: (0, 0), **once),
            pl.BlockSpec((DE, D), lambda i: (0, 0), **once),
            vec,
            vec,
        ],
        out_specs=out_specs,
        out_shape=out_shape,
        scratch_shapes=scratch,
        compiler_params=_cparams(("arbitrary",)),
        name="moe_combine_final" if final else "moe_combine",
    )(pos_tiles, pos_tiles, wsel, h, ys, wsg, wsu, wsd, g, b)


def _moe_layer(layer, final, h, hp, w_router, router_bias, w_exp_gate, w_exp_up, w_exp_down,
               w_sh_gate, w_sh_up, w_sh_down, ln_g, ln_b):
    eidx_t, wsel, rank_t, counts = _router_call(h, w_router[layer], router_bias[layer])
    cnt = counts[:, 0].astype(I32)
    pos_t, te, gi, ne, nact, off = _plan_call(cnt, eidx_t, rank_t)
    pos_tiles = pos_t.reshape(TOPK, N_ROW_TILES, TM).transpose(1, 0, 2).reshape(N_ROW_TILES, 1, TOPK * TM)
    xs = _dispatch_call(cnt, off, nact, pos_tiles, hp)
    ys = _experts_call(layer, te, gi, ne, nact, xs, w_exp_gate, w_exp_up, w_exp_down)
    return _gather_call(
        final, pos_tiles, wsel, h, ys,
        w_sh_gate[layer].astype(BF16), w_sh_up[layer].astype(BF16), w_sh_down[layer].astype(BF16),
        ln_g[layer][None, :], ln_b[layer][None, :])


def _ssm_prep_kernel(lr_ref, li_ref, ldt_ref, br_ref, bi_ref, abr_ref, abi_ref, bbr_ref, bbi_ref):
    lr = lr_ref[...]
    li = li_ref[...]
    dt = jnp.exp(ldt_ref[...])
    mag = jnp.exp(lr * dt)
    ab_re = mag * jnp.cos(li * dt)
    ab_im = mag * jnp.sin(li * dt)
    den = lr * lr + li * li
    nr = ab_re - 1.0
    ni = ab_im
    cr = (nr * lr + ni * li) / den
    ci = (ni * lr - nr * li) / den
    br = br_ref[...]
    bi = bi_ref[...]
    abr_ref[...] = ab_re
    abi_ref[...] = ab_im
    bbr_ref[...] = cr * br - ci * bi
    bbi_ref[...] = cr * bi + ci * br


def _ssm_prep_call(lam_re, lam_im, log_dt, b_re, b_im):
    wide = (NG, GC * NS)
    lr = jnp.tile(lam_re, (1, GC))
    li = jnp.tile(lam_im, (1, GC))
    ldt = jnp.broadcast_to(log_dt[:, None], wide)
    br = jnp.transpose(b_re, (0, 2, 1)).reshape(wide)
    bi = jnp.transpose(b_im, (0, 2, 1)).reshape(wide)
    sds = jax.ShapeDtypeStruct(wide, F32)
    return pl.pallas_call(
        _ssm_prep_kernel, out_shape=[sds, sds, sds, sds], name="ssm_prep",
        compiler_params=pltpu.CompilerParams(vmem_limit_bytes=VMEM_LIMIT),
    )(lr, li, ldt, br, bi)


def _cmul(ar, ai, xr, xi):
    return ar * xr - ai * xi, ar * xi + ai * xr


NSEG = 8
SEGL = LP // NSEG
STEP_UNROLL = 4
MOVE_UNROLL = 8
assert LP == NSEG * SEGL and SEGL % STEP_UNROLL == 0 and SEGL % MOVE_UNROLL == 0 and PAD_FRONT < SEGL
assert STEP_UNROLL % 2 == 0


def _ssm_p_kernel(u_ref, wb_ref, wc_ref, a_ref, d_ref, z_ref, st_ref, u_scr, s_scr, sb_scr, y_scr):
    def interleave(it, carry):
        for q in range(MOVE_UNROLL):
            t = it * MOVE_UNROLL + q
            u_scr[pl.ds(pl.multiple_of(t * NSEG, NSEG), NSEG), :] = u_ref[pl.ds(t, NSEG, stride=SEGL), :]
        return carry

    lax.fori_loop(0, SEGL // MOVE_UNROLL, interleave, 0)
    row = lax.broadcasted_iota(I32, (LP, LANES), 0)
    is_pad = jnp.logical_and(row % NSEG == 0, row // NSEG < PAD_FRONT)
    u = jnp.where(is_pad, 0.0, u_scr[...])
    s_scr[...] = jnp.dot(u.astype(BF16), wb_ref[0], preferred_element_type=F32)

    ar = a_ref[0, 0:1, :]
    ai = a_ref[0, 1:2, :]

    def group(t):
        r0 = pl.multiple_of(t * NSEG, NSEG)
        return pl.ds(r0, NSEG)

    def local_scan(it, carry):
        sr, si = carry
        for q in range(STEP_UNROLL):
            g = group(it * STEP_UNROLL + q)
            tr, ti = _cmul(ar, ai, sr, si)
            sr = tr + s_scr[g, 0:CH_STATE]
            si = ti + s_scr[g, CH_STATE:]
            s_scr[g, 0:CH_STATE] = sr
            s_scr[g, CH_STATE:] = si
        return sr, si

    zero8 = jnp.zeros((NSEG, CH_STATE), F32)
    er, ei = lax.fori_loop(0, SEGL // STEP_UNROLL, local_scan, (zero8, zero8))

    pr, pi = ar, ai
    acc = None
    bits = SEGL
    while bits:
        if bits & 1:
            acc = (pr, pi) if acc is None else _cmul(pr, pi, *acc)
        bits >>= 1
        if bits:
            pr, pi = _cmul(pr, pi, pr, pi)
    alr, ali = acc
    cr = jnp.zeros((1, CH_STATE), F32)
    ci = jnp.zeros((1, CH_STATE), F32)
    ins_r, ins_i = [], []
    for j in range(NSEG):
        ins_r.append(cr)
        ins_i.append(ci)
        tr, ti = _cmul(alr, ali, cr, ci)
        cr = tr + er[j:j + 1, :]
        ci = ti + ei[j:j + 1, :]
    st_ref[0, 0, :, 0:CH_STATE] = cr
    st_ref[0, 0, :, CH_STATE:] = ci

    def fixup(it, carry):
        dr, di = carry
        for q in range(0, STEP_UNROLL, 2):
            t0 = it * STEP_UNROLL + q
            rows_r, rows_i = [], []
            for t in (t0, t0 + 1):
                g = group(t)
                dr, di = _cmul(ar, ai, dr, di)
                rows_r.append(s_scr[g, 0:CH_STATE] + dr)
                rows_i.append(s_scr[g, CH_STATE:] + di)
            pair = pl.ds(pl.multiple_of(t0 * NSEG, 2 * NSEG), 2 * NSEG)
            sb_scr[pair, 0:CH_STATE] = jnp.concatenate(rows_r, axis=0).astype(BF16)
            sb_scr[pair, CH_STATE:] = jnp.concatenate(rows_i, axis=0).astype(BF16)
        return dr, di

    lax.fori_loop(0, SEGL // STEP_UNROLL, fixup,
                  (jnp.concatenate(ins_r, axis=0), jnp.concatenate(ins_i, axis=0)))

    y_scr[...] = jnp.dot(sb_scr[...], wc_ref[0], preferred_element_type=F32) + d_ref[0] * u

    def deinterleave(it, carry):
        for q in range(MOVE_UNROLL):
            t = it * MOVE_UNROLL + q
            u_scr[pl.ds(t, NSEG, stride=SEGL), :] = y_scr[pl.ds(pl.multiple_of(t * NSEG, NSEG), NSEG), :]
        return carry

    lax.fori_loop(0, SEGL // MOVE_UNROLL, deinterleave, 0)
    z_ref[...] = jax.nn.gelu(u_scr[...]).astype(BF16)


def _ssm_p_call(h, wb_bf, wc_bf, a_tab, d_tab):
    return pl.pallas_call(
        _ssm_p_kernel,
        grid=(NB, NCHUNK),
        in_specs=[
            pl.BlockSpec((LP, LANES), lambda b, k: (b, k)),
            pl.BlockSpec((1, LANES, 2 * CH_STATE), lambda b, k: (k, 0, 0)),
            pl.BlockSpec((1, 2 * CH_STATE, LANES), lambda b, k: (k, 0, 0)),
            pl.BlockSpec((1, 2, CH_STATE), lambda b, k: (k, 0, 0)),
            pl.BlockSpec((1, 1, LANES), lambda b, k: (k, 0, 0)),
        ],
        out_specs=[
            pl.BlockSpec((LP, LANES), lambda b, k: (b, k)),
            pl.BlockSpec((1, 1, 1, 2 * CH_STATE), lambda b, k: (b, k, 0, 0)),
        ],
        out_shape=[
            jax.ShapeDtypeStruct((T_PROMPT, D), BF16),
            jax.ShapeDtypeStruct((NB, NCHUNK, 1, 2 * CH_STATE), F32),
        ],
        scratch_shapes=[pltpu.VMEM((LP, LANES), F32), pltpu.VMEM((LP, 2 * CH_STATE), F32),
                        pltpu.VMEM((LP, 2 * CH_STATE), BF16), pltpu.VMEM((LP, LANES), F32)],
        compiler_params=_cparams(("parallel", "parallel")),
        name="ssm_prompt",
    )(h, wb_bf, wc_bf, a_tab, d_tab)


def _ssm_s_kernel(u_ref, sr_ref, si_ref, wb_ref, wc_ref, a_ref, d_ref, z_ref, nr_ref, ni_ref):
    u = u_ref[...]
    bu = jnp.dot(u, wb_ref[0], preferred_element_type=F32, precision=lax.Precision.HIGHEST)
    ar = a_ref[0, 0:1, :]
    ai = a_ref[0, 1:2, :]
    tr, ti = _cmul(ar, ai, sr_ref[...], si_ref[...])
    nr = tr + bu[:, 0:CH_STATE]
    ni = ti + bu[:, CH_STATE:]
    nr_ref[...] = nr
    ni_ref[...] = ni
    s = jnp.concatenate([nr, ni], axis=1).astype(BF16)
    y = jnp.dot(s, wc_ref[0], preferred_element_type=F32) + d_ref[0] * u
    z_ref[...] = jax.nn.gelu(y).astype(BF16)


def _ssm_s_call(h, s0r, s0i, wb_f32, wc_bf, a_tab, d_tab):
    st = pl.BlockSpec((DEC, CH_STATE), lambda k: (0, k))
    return pl.pallas_call(
        _ssm_s_kernel,
        grid=(NCHUNK,),
        in_specs=[
            pl.BlockSpec((DEC, LANES), lambda k: (T_PROMPT // DEC, k)),
            st,
            st,
            pl.BlockSpec((1, LANES, 2 * CH_STATE), lambda k: (k, 0, 0)),
            pl.BlockSpec((1, 2 * CH_STATE, LANES), lambda k: (k, 0, 0)),
            pl.BlockSpec((1, 2, CH_STATE), lambda k: (k, 0, 0)),
            pl.BlockSpec((1, 1, LANES), lambda k: (k, 0, 0)),
        ],
        out_specs=[pl.BlockSpec((DEC, LANES), lambda k: (0, k)), st, st],
        out_shape=[
            jax.ShapeDtypeStruct((DEC, D), BF16),
            jax.ShapeDtypeStruct((DEC, NG * NS), F32),
            jax.ShapeDtypeStruct((DEC, NG * NS), F32),
        ],
        compiler_params=_cparams(("parallel",)),
        name="ssm_sample",
    )(h, s0r, s0i, wb_f32, wc_bf, a_tab, d_tab)


def _glu_ln_kernel(z_ref, zt_ref, w_ref, bg_ref, h_ref, g_ref, b_ref, out_ref, hp_ref):
    z = _prompt_or_tail(z_ref, zt_ref)
    acc = jnp.dot(z, w_ref[...], preferred_element_type=F32) + bg_ref[...]
    m = acc[:, :D] * _sigmoid(acc[:, D:])
    out = _layer_norm(ALPHA * h_ref[...] + m, g_ref[...], b_ref[...])
    out_ref[...] = out
    hp_ref[...] = _pack_pair(out[:, :HALF], out[:, HALF:])


def _glu_ln_call(z_p, z_tail, w_bf, bg, h, g, b):
    vec = pl.BlockSpec((1, D), lambda i: (0, 0))
    return pl.pallas_call(
        _glu_ln_kernel,
        grid=(N_ROW_TILES,),
        in_specs=[
            pl.BlockSpec((TM, D), _PROMPT_TILE),
            pl.BlockSpec((TM, D), lambda i: (0, 0)),
            pl.BlockSpec((D, 2 * D), lambda i: (0, 0), pipeline_mode=pl.Buffered(1)),
            pl.BlockSpec((1, 2 * D), lambda i: (0, 0)),
            pl.BlockSpec((TM, D), lambda i: (i, 0)),
            vec,
            vec,
        ],
        out_specs=[pl.BlockSpec((TM, D), lambda i: (i, 0)), pl.BlockSpec((TM, HALF), lambda i: (i, 0))],
        out_shape=[jax.ShapeDtypeStruct((T_ALL, D), F32), jax.ShapeDtypeStruct((T_ALL, HALF), U32)],
        compiler_params=_cparams(("parallel",)),
        name="glu_ln",
    )(z_p, z_tail, w_bf, bg, h, g, b)


GROUPS_PER_CHUNK = LANES // GC


def _block_diag_in(t):
    t3 = t.reshape(NCHUNK, LANES, NS)
    same = (np.arange(LANES)[:, None] // GC) == (np.arange(CH_STATE)[None, :] // NS)
    return jnp.tile(t3, (1, 1, GROUPS_PER_CHUNK)) * jnp.asarray(same, t.dtype)


def _block_diag_out(t):
    t3 = jnp.swapaxes(t, 1, 2).reshape(NCHUNK, CH_STATE, GC)
    same = (np.arange(CH_STATE)[:, None] // NS) == (np.arange(LANES)[None, :] // GC)
    return jnp.tile(t3, (1, 1, GROUPS_PER_CHUNK)) * jnp.asarray(same, t.dtype)


def kernel(x_prompt, x_sample, cache_k, cache_v, state_ssm_re, state_ssm_im, meta_tokens, w_qkv, b_qkv, attn_sinks, w_o, b_o, ssm_lam_re, ssm_lam_im, ssm_log_dt, ssm_b_re, ssm_b_im, ssm_c_re, ssm_c_im, ssm_d, w_glu, b_glu, ln_mix_g, ln_mix_b, w_router, router_bias, w_exp_gate, w_exp_up, w_exp_down, w_sh_gate, w_sh_up, w_sh_down, ln_ffn_g, ln_ffn_b):
    moe_w = (w_router, router_bias, w_exp_gate, w_exp_up, w_exp_down, w_sh_gate, w_sh_up, w_sh_down,
             ln_ffn_g, ln_ffn_b)

    front = jnp.concatenate([jnp.zeros((PAD_FRONT, D), F32), meta_tokens], axis=0)
    pieces = []
    for b in range(NB):
        pieces += [front, x_prompt[b]]
    h = jnp.concatenate(pieces + [x_sample.reshape(DEC, D)], axis=0)

    rc, rs1, rs2 = _rope_tables()
    q, k, v = _qkv_call(h, w_qkv[0].astype(BF16), b_qkv[0][None, :], rc, rs1, rs2)
    o_p = _attn_p_call(attn_sinks[0], q, k, v)
    q3 = q[T_PROMPT:].astype(F32).reshape(DEC, NH, HD)
    o_s, ck_new, cv_new = _attn_s_call(
        attn_sinks[0][:, None], q3, k, v,
        cache_k[0].reshape(DEC, BLK, NKV * HD), cache_v[0].reshape(DEC, BLK, NKV * HD))
    o_tail = _tail_tile(o_p, o_s.reshape(DEC, D).astype(BF16))
    h, hp = _oproj_ln_call(o_p, o_tail, w_o[0].astype(BF16), b_o[0][None, :], h,
                           ln_mix_g[0][None, :], ln_mix_b[0][None, :])
    h = _moe_layer(0, False, h, hp, *moe_w)

    def last_window(t):
        rows = [t[(b + 1) * LP - BLK:(b + 1) * LP] for b in range(NB)]
        return jnp.stack(rows).reshape(NB, BLK, NKV, HD)

    kp = last_window(k)
    vp = last_window(v)

    ab_re, ab_im, bb_re, bb_im = _ssm_prep_call(
        ssm_lam_re[0], ssm_lam_im[0], ssm_log_dt[0], ssm_b_re[0], ssm_b_im[0])
    wb = jnp.concatenate([_block_diag_in(bb_re), _block_diag_in(bb_im)], axis=2)
    wc = jnp.concatenate([_block_diag_out(ssm_c_re[0]), -_block_diag_out(ssm_c_im[0])], axis=1)
    wc_bf = wc.astype(BF16)
    a_tab = jnp.stack([ab_re[:, :NS].reshape(NCHUNK, CH_STATE),
                       ab_im[:, :NS].reshape(NCHUNK, CH_STATE)], axis=1)
    d_tab = ssm_d[0].reshape(NCHUNK, 1, LANES)
    z_p, st_p = _ssm_p_call(h, wb.astype(BF16), wc_bf, a_tab, d_tab)
    z_s, sr_new, si_new = _ssm_s_call(
        h, state_ssm_re[0].reshape(DEC, NG * NS), state_ssm_im[0].reshape(DEC, NG * NS),
        wb, wc_bf, a_tab, d_tab)
    h, hp = _glu_ln_call(z_p, _tail_tile(z_p, z_s), w_glu[0].astype(BF16), b_glu[0][None, :], h,
                         ln_mix_g[1][None, :], ln_mix_b[1][None, :])
    y_prompt, y_sample = _moe_layer(1, True, h, hp, *moe_w)
    y_prompt = y_prompt.reshape(NB, SEQ, D)
    y_sample = y_sample.reshape(DEC, 1, D)
    st_p = st_p.reshape(NB, NCHUNK, 2, 8, NS)
    rp = st_p[:, :, 0].reshape(NB, NG, NS)
    ip = st_p[:, :, 1].reshape(NB, NG, NS)
    return (y_prompt, y_sample,
            kp[None], vp[None],
            ck_new.reshape(1, DEC, BLK, NKV, HD), cv_new.reshape(1, DEC, BLK, NKV, HD),
            rp[None], ip[None],
            sr_new.reshape(1, DEC, NG, NS), si_new.reshape(1, DEC, NG, NS))
```

```python
import functools
import math

import jax
import jax.numpy as jnp
import numpy as np
from jax import lax
from jax.experimental import pallas as pl
from jax.experimental.pallas import tpu as pltpu

F32 = jnp.float32
BF16 = jnp.bfloat16
I32 = jnp.int32
U32 = jnp.uint32

D = 2048
HALF = D // 2
NB = 4
N_META = 16
SEQ = 2048
L = N_META + SEQ
BLK = 128
PAD_FRONT = (-L) % BLK
LP = L + PAD_FRONT
NBLK = LP // BLK
T_PROMPT = NB * LP
DEC = 128
T_ALL = T_PROMPT + DEC
PAST_LEN = 8192
HD = 64
NH = 32
NKV = 4
QPK = NH // NKV
QKV = (NH + 2 * NKV) * HD
QK_COLS = (NH + NKV) * HD
ROT = HD // 4
ROT_HALF = ROT // 2
ROPE_THETA = 500000.0
NG = 128
GC = 16
NS = 64
NCHUNK = 16
CH_STATE = 8 * NS
NE = 64
TOPK = 8
NEG = 8
PER_GRP = NE // NEG
TOPG = 4
DE = 512
ROUTED_SCALE = 2.5
DEPTH = 2
ALPHA = (2 * DEPTH) ** 0.25
LN_EPS = 1e-5

V7X_VMEM_BYTES = 64 * 1024 * 1024
VMEM_LIMIT = 56 * 1024 * 1024
LANES = 128

TM = 384
N_ROW_TILES = T_ALL // TM
TAIL_START = (N_ROW_TILES - 1) * TM
TM_E = 256
N_PAIRS = T_ALL * TOPK
NT_E = -(-N_PAIRS // TM_E) + NE
R_ROWS = NT_E * TM_E
assert T_ALL % TM == 0 and TAIL_START <= T_PROMPT


def _cparams(sem):
    return pltpu.CompilerParams(dimension_semantics=sem, vmem_limit_bytes=VMEM_LIMIT)


def _sigmoid(x):
    return 1.0 / (1.0 + jnp.exp(-x))


def _layer_norm(y, g, b):
    mu = jnp.mean(y, axis=-1, keepdims=True)
    yc = y - mu
    var = jnp.mean(yc * yc, axis=-1, keepdims=True)
    return yc * lax.rsqrt(var + LN_EPS) * g + b


def _pack_pair(lo, hi):
    lo_b = lax.bitcast_convert_type(lo.astype(BF16).astype(F32), U32) >> 16
    hi_b = lax.bitcast_convert_type(hi.astype(BF16).astype(F32), U32) & jnp.uint32(0xFFFF0000)
    return lo_b | hi_b


def _unpack_pair(w):
    lo = lax.bitcast_convert_type(w << 16, F32)
    hi = lax.bitcast_convert_type(w & jnp.uint32(0xFFFF0000), F32)
    return lo, hi


def _qkv_kernel(x_ref, w_ref, b_ref, c_ref, s1_ref, s2_ref, q_ref, k_ref, v_ref):
    xb = x_ref[...].astype(BF16)
    acc = jnp.dot(xb, w_ref[...], preferred_element_type=F32) + b_ref[...]
    c = c_ref[...]
    s1 = s1_ref[...]
    s2 = s2_ref[...]
    for j in range(QK_COLS // LANES):
        blk = acc[:, j * LANES:(j + 1) * LANES]
        r = (blk * c + pltpu.roll(blk, LANES - ROT_HALF, axis=1) * s1
             + pltpu.roll(blk, ROT_HALF, axis=1) * s2)
        if j < D // LANES:
            q_ref[:, j * LANES:(j + 1) * LANES] = (r * (1.0 / math.sqrt(HD))).astype(BF16)
        else:
            jj = j - D // LANES
            k_ref[:, jj * LANES:(jj + 1) * LANES] = r
    v_ref[...] = acc[:, QK_COLS:]


def _qkv_call(x, w_bf, b, rc, rs1, rs2):
    return pl.pallas_call(
        _qkv_kernel,
        grid=(T_ALL // TM,),
        in_specs=[
            pl.BlockSpec((TM, D), lambda i: (i, 0)),
            pl.BlockSpec((D, QKV), lambda i: (0, 0)),
            pl.BlockSpec((1, QKV), lambda i: (0, 0)),
            pl.BlockSpec((TM, LANES), lambda i: (i, 0)),
            pl.BlockSpec((TM, LANES), lambda i: (i, 0)),
            pl.BlockSpec((TM, LANES), lambda i: (i, 0)),
        ],
        out_specs=[
            pl.BlockSpec((TM, D), lambda i: (i, 0)),
            pl.BlockSpec((TM, NKV * HD), lambda i: (i, 0)),
            pl.BlockSpec((TM, NKV * HD), lambda i: (i, 0)),
        ],
        out_shape=[
            jax.ShapeDtypeStruct((T_ALL, D), BF16),
            jax.ShapeDtypeStruct((T_ALL, NKV * HD), F32),
            jax.ShapeDtypeStruct((T_ALL, NKV * HD), F32),
        ],
        compiler_params=_cparams(("parallel",)),
        name="qkv_rope",
    )(x, w_bf, b, rc, rs1, rs2)


def _rope_tables():
    pos_p = jnp.maximum(jnp.arange(LP, dtype=I32) - PAD_FRONT, 0)
    pos = jnp.concatenate([pos_p, jnp.full((8,), PAST_LEN, I32)]).astype(F32)
    inv_freq = ROPE_THETA ** (-jnp.arange(0, ROT, 2, dtype=F32) / ROT)
    ang = pos[:, None] * inv_freq[None, :]
    cos = jnp.cos(ang)
    sin = jnp.sin(ang)
    lane = np.arange(LANES) % HD
    freq = np.arange(ROT_HALF)[:, None]
    first = ((lane[None, :] == freq) & (lane[None, :] < ROT_HALF)).astype(np.float32)
    second = ((lane[None, :] - ROT_HALF == freq) & (lane[None, :] < ROT)).astype(np.float32)
    rest = (lane >= ROT).astype(np.float32)[None, :]
    place = functools.partial(jnp.dot, precision=lax.Precision.HIGHEST)
    c = place(cos, jnp.asarray(first + second)) + jnp.asarray(rest)
    s1 = place(sin, jnp.asarray(-first))
    s2 = place(sin, jnp.asarray(second))

    def all_rows(t):
        return jnp.concatenate([t[:LP]] * NB + [t[LP:]] * (DEC // 8), axis=0)

    return all_rows(c), all_rows(s1), all_rows(s2)


def _attn_p_kernel(sink_ref, q_ref, kp_ref, kc_ref, vp_ref, vc_ref, o_ref):
    j = pl.program_id(0) % NBLK
    keys = jnp.concatenate([kp_ref[...], kc_ref[...]], axis=0).astype(BF16)
    vals = jnp.concatenate([vp_ref[...], vc_ref[...]], axis=0).astype(BF16)
    r = lax.broadcasted_iota(I32, (BLK, 2 * BLK), 0)
    c = lax.broadcasted_iota(I32, (BLK, 2 * BLK), 1)
    dist = BLK + r - c
    kpos = (j - 1) * BLK - PAD_FRONT + c
    mask = (dist >= 0) & (dist <= BLK) & (kpos >= 0)
    for g in range(NKV):
        kg = keys[:, g * HD:(g + 1) * HD]
        vg = vals[:, g * HD:(g + 1) * HD]
        heads = range(g * QPK, (g + 1) * QPK)
        scores = [lax.dot_general(q_ref[:, h * HD:(h + 1) * HD], kg, (((1,), (1,)), ((), ())),
                                  preferred_element_type=F32) for h in heads]
        probs, rdens = [], []
        for h, s in zip(heads, scores):
            s = jnp.where(mask, s, -jnp.inf)
            sk = sink_ref[h]
            m = jnp.maximum(jnp.max(s, axis=1, keepdims=True), sk)
            p = jnp.exp(s - m)
            rdens.append(1.0 / (jnp.sum(p, axis=1, keepdims=True) + jnp.exp(sk - m)))
            probs.append(p.astype(BF16))
        for h, p, rden in zip(heads, probs, rdens):
            oh = jnp.dot(p, vg, preferred_element_type=F32) * rden
            o_ref[:, h * HD:(h + 1) * HD] = oh.astype(BF16)


def _attn_p_call(sinks, q, k, v):
    prev = lambda i: (jnp.where(i % NBLK == 0, i, i - 1), 0)
    cur = lambda i: (i, 0)
    return pl.pallas_call(
        _attn_p_kernel,
        grid=(NB * NBLK,),
        in_specs=[
            pl.BlockSpec(memory_space=pltpu.SMEM),
            pl.BlockSpec((BLK, D), cur),
            pl.BlockSpec((BLK, NKV * HD), prev),
            pl.BlockSpec((BLK, NKV * HD), cur),
            pl.BlockSpec((BLK, NKV * HD), prev),
            pl.BlockSpec((BLK, NKV * HD), cur),
        ],
        out_specs=pl.BlockSpec((BLK, D), cur),
        out_shape=jax.ShapeDtypeStruct((T_PROMPT, D), BF16),
        compiler_params=_cparams(("parallel",)),
        name="attn_prompt",
    )(sinks, q, k, k, v, v)


SEQ_PER_STEP = 16
SEQ_UNROLL = 4


def _attn_s_kernel(sink_ref, q_ref, kn_ref, vn_ref, ck_ref, cv_ref, o_ref, cko_ref, cvo_ref):
    row = lax.broadcasted_iota(I32, (BLK, NKV * HD), 0)
    hrow = lax.broadcasted_iota(I32, (NH, NKV * HD), 0) // QPK
    hlane = lax.broadcasted_iota(I32, (NH, NKV * HD), 1) // HD
    own = hrow == hlane
    sk = sink_ref[...]

    def score_stage(s):
        kn = kn_ref[pl.ds(s, 1), :]
        knr = kn.astype(BF16).astype(F32)
        q = q_ref[s].astype(BF16)
        qe = jnp.where(own, jnp.concatenate([q] * NKV, axis=1), jnp.zeros((), BF16))
        sc = lax.dot_general(qe, ck_ref[s].astype(BF16), (((1,), (1,)), ((), ())),
                             preferred_element_type=F32)
        sn = jnp.sum(qe.astype(F32) * knr, axis=1, keepdims=True)
        return sc, sn

    def softmax_stage(sc, sn):
        m = jnp.maximum(jnp.maximum(jnp.max(sc, axis=1, keepdims=True), sn), sk)
        p = jnp.exp(sc - m)
        pn = jnp.exp(sn - m)
        rden = 1.0 / (jnp.sum(p, axis=1, keepdims=True) + pn + jnp.exp(sk - m))
        return p.astype(BF16), pn.astype(BF16).astype(F32), rden

    def value_stage(s, p, pn, rden):
        vn = vn_ref[pl.ds(s, 1), :]
        vnr = vn.astype(BF16).astype(F32)
        of = jnp.dot(p, cv_ref[s].astype(BF16), preferred_element_type=F32)
        of = jnp.where(own, of + pn * vnr, 0.0)
        og = of[:, 0:HD]
        for g in range(1, NKV):
            og = og + of[:, g * HD:(g + 1) * HD]
        o_ref[s] = og * rden
        cko_ref[s] = jnp.where(row == BLK - 1, kn_ref[pl.ds(s, 1), :], pltpu.roll(ck_ref[s], BLK - 1, axis=0))
        cvo_ref[s] = jnp.where(row == BLK - 1, vn, pltpu.roll(cv_ref[s], BLK - 1, axis=0))

    def body(it, carry):
        seqs = [it * SEQ_UNROLL + u for u in range(SEQ_UNROLL)]
        staged = [score_stage(s) for s in seqs]
        soft = [softmax_stage(*st) for st in staged]
        for s, sm in zip(seqs, soft):
            value_stage(s, *sm)
        return carry

    lax.fori_loop(0, SEQ_PER_STEP // SEQ_UNROLL, body, 0)


def _attn_s_call(sinks_col, q3, k, v, cache_k, cache_v):
    sp = SEQ_PER_STEP
    kv_off = T_PROMPT // sp
    return pl.pallas_call(
        _attn_s_kernel,
        grid=(DEC // sp,),
        in_specs=[
            pl.BlockSpec((NH, 1), lambda i: (0, 0)),
            pl.BlockSpec((sp, NH, HD), lambda i: (i, 0, 0)),
            pl.BlockSpec((sp, NKV * HD), lambda i: (kv_off + i, 0)),
            pl.BlockSpec((sp, NKV * HD), lambda i: (kv_off + i, 0)),
            pl.BlockSpec((sp, BLK, NKV * HD), lambda i: (i, 0, 0)),
            pl.BlockSpec((sp, BLK, NKV * HD), lambda i: (i, 0, 0)),
        ],
        out_specs=[
            pl.BlockSpec((sp, NH, HD), lambda i: (i, 0, 0)),
            pl.BlockSpec((sp, BLK, NKV * HD), lambda i: (i, 0, 0)),
            pl.BlockSpec((sp, BLK, NKV * HD), lambda i: (i, 0, 0)),
        ],
        out_shape=[
            jax.ShapeDtypeStruct((DEC, NH, HD), F32),
            jax.ShapeDtypeStruct((DEC, BLK, NKV * HD), F32),
            jax.ShapeDtypeStruct((DEC, BLK, NKV * HD), F32),
        ],
        compiler_params=_cparams(("parallel",)),
        name="attn_sample",
    )(sinks_col, q3, k, v, cache_k, cache_v)


def _prompt_or_tail(x_ref, tail_ref):
    return jnp.where(pl.program_id(0) == N_ROW_TILES - 1, tail_ref[...], x_ref[...])


def _tail_tile(x_prompt_rows, x_sample_rows):
    return jnp.concatenate([x_prompt_rows[TAIL_START:], x_sample_rows], axis=0)


_PROMPT_TILE = lambda i: (jnp.minimum(i, N_ROW_TILES - 2), 0)


def _oproj_ln_kernel(o_ref, ot_ref, w_ref, bo_ref, h_ref, g_ref, b_ref, out_ref, hp_ref):
    o = _prompt_or_tail(o_ref, ot_ref)
    m = jnp.dot(o, w_ref[...], preferred_element_type=F32) + bo_ref[...]
    out = _layer_norm(ALPHA * h_ref[...] + m, g_ref[...], b_ref[...])
    out_ref[...] = out
    hp_ref[...] = _pack_pair(out[:, :HALF], out[:, HALF:])


def _oproj_ln_call(o_p, o_tail, w_bf, bo, h, g, b):
    vec = pl.BlockSpec((1, D), lambda i: (0, 0))
    return pl.pallas_call(
        _oproj_ln_kernel,
        grid=(N_ROW_TILES,),
        in_specs=[
            pl.BlockSpec((TM, D), _PROMPT_TILE),
            pl.BlockSpec((TM, D), lambda i: (0, 0)),
            pl.BlockSpec((D, D), lambda i: (0, 0)),
            vec,
            pl.BlockSpec((TM, D), lambda i: (i, 0)),
            vec,
            vec,
        ],
        out_specs=[pl.BlockSpec((TM, D), lambda i: (i, 0)), pl.BlockSpec((TM, HALF), lambda i: (i, 0))],
        out_shape=[jax.ShapeDtypeStruct((T_ALL, D), F32), jax.ShapeDtypeStruct((T_ALL, HALF), U32)],
        compiler_params=_cparams(("parallel",)),
        name="oproj_ln",
    )(o_p, o_tail, w_bf, bo, h, g, b)


def _router_kernel(h_ref, wr_ref, rb_ref, eidx_ref, wsel_ref, rank_ref, cnt_ref, carry_ref):
    i = pl.program_id(0)

    @pl.when(i == 0)
    def _():
        carry_ref[...] = jnp.zeros_like(carry_ref)

    h = h_ref[...]
    w = wr_ref[...]
    h_hi = h.astype(BF16)
    h_lo = (h - h_hi.astype(F32)).astype(BF16)
    w_hi = w.astype(BF16)
    w_lo = (w - w_hi.astype(F32)).astype(BF16)
    logits = (jnp.dot(h_hi, w_hi, preferred_element_type=F32)
              + (jnp.dot(h_hi, w_lo, preferred_element_type=F32)
                 + jnp.dot(h_lo, w_hi, preferred_element_type=F32)))
    scores = _sigmoid(logits.T[0:NE, :])
    biased = scores + rb_ref[...]
    ninf = -jnp.inf
    sub = lax.broadcasted_iota(I32, (PER_GRP, TM), 0).astype(F32)
    sc_g = [scores[g * PER_GRP:(g + 1) * PER_GRP, :] for g in range(NEG)]
    b_g = [biased[g * PER_GRP:(g + 1) * PER_GRP, :] for g in range(NEG)]
    e_g = [sub + float(g * PER_GRP) for g in range(NEG)]

    def smax(x):
        return jnp.max(x, axis=0, keepdims=True)

    def smin(x):
        return jnp.min(x, axis=0, keepdims=True)

    gs = []
    for g in range(NEG):
        m1 = smax(b_g[g])
        i1 = smin(jnp.where(b_g[g] == m1, sub, float(PER_GRP)))
        m2 = smax(jnp.where(sub == i1, ninf, b_g[g]))
        gs.append(m1 + m2)

    work = []
    for g in range(NEG):
        beaten = jnp.zeros((1, TM), F32)
        for o in range(NEG):
            if o != g:
                wins = (gs[o] >= gs[g]) if o < g else (gs[o] > gs[g])
                beaten = beaten + jnp.where(wins, 1.0, 0.0)
        work.append(jnp.where(beaten < float(TOPG), b_g[g], ninf))

    idx_rows, w_rows = [], []
    onehot = [jnp.zeros((PER_GRP, TM), F32) for _ in range(NEG)]
    for _ in range(TOPK):
        m = smax(work[0])
        for g in range(1, NEG):
            m = jnp.maximum(m, smax(work[g]))
        ik = smin(jnp.where(work[0] == m, e_g[0], float(NE)))
        for g in range(1, NEG):
            ik = jnp.minimum(ik, smin(jnp.where(work[g] == m, e_g[g], float(NE))))
        wk = jnp.zeros((1, TM), F32)
        for g in range(NEG):
            hit = e_g[g] == ik
            wk = wk + jnp.sum(jnp.where(hit, sc_g[g], 0.0), axis=0, keepdims=True)
            onehot[g] = jnp.where(hit, 1.0, onehot[g])
            work[g] = jnp.where(hit, ninf, work[g])
        idx_rows.append(ik)
        w_rows.append(wk)
    wsum = w_rows[0]
    for wk in w_rows[1:]:
        wsum = wsum + wk

    rr = lax.broadcasted_iota(I32, (TM, TM), 0)
    cc = lax.broadcasted_iota(I32, (TM, TM), 1)
    tri = jnp.where(rr < cc, 1.0, 0.0).astype(BF16)
    oh = jnp.concatenate(onehot, axis=0)
    prefix = jnp.dot(oh.astype(BF16), tri, preferred_element_type=F32) + carry_ref[...]
    carry_ref[...] = carry_ref[...] + jnp.sum(oh, axis=1, keepdims=True)
    cnt_ref[...] = carry_ref[...]

    rank_rows = []
    for k in range(TOPK):
        rk = jnp.zeros((1, TM), F32)
        for g in range(NEG):
            pg = prefix[g * PER_GRP:(g + 1) * PER_GRP, :]
            rk = rk + jnp.sum(jnp.where(e_g[g] == idx_rows[k], pg, 0.0), axis=0, keepdims=True)
        rank_rows.append(rk)
    eidx_ref[...] = jnp.concatenate(idx_rows, axis=0).astype(I32)
    rank_ref[...] = jnp.concatenate(rank_rows, axis=0).astype(I32)
    w_t = jnp.concatenate([wk / wsum * ROUTED_SCALE for wk in w_rows]
                          + [jnp.zeros((LANES - TOPK, TM), F32)], axis=0)
    wsel_ref[...] = w_t.T[:, 0:TOPK]


def _router_call(h, w_router, router_bias):
    tk = pl.BlockSpec((TM, TOPK), lambda i: (i, 0))
    kt = pl.BlockSpec((TOPK, TM), lambda i: (0, i))
    return pl.pallas_call(
        _router_kernel,
        grid=(T_ALL // TM,),
        in_specs=[
            pl.BlockSpec((TM, D), lambda i: (i, 0)),
            pl.BlockSpec((D, LANES), lambda i: (0, 0)),
            pl.BlockSpec((NE, 1), lambda i: (0, 0)),
        ],
        out_specs=[kt, tk, kt, pl.BlockSpec((NE, 1), lambda i: (0, 0))],
        out_shape=[
            jax.ShapeDtypeStruct((TOPK, T_ALL), I32),
            jax.ShapeDtypeStruct((T_ALL, TOPK), F32),
            jax.ShapeDtypeStruct((TOPK, T_ALL), I32),
            jax.ShapeDtypeStruct((NE, 1), F32),
        ],
        scratch_shapes=[pltpu.VMEM((NE, 1), F32)],
        compiler_params=_cparams(("arbitrary",)),
        name="router",
    )(h, jnp.pad(w_router, ((0, 0), (0, LANES - NE))), router_bias[:, None])


def _tiles_of(cnt):
    return (cnt + TM_E - 1) // TM_E


def _plan_kernel(cnt_ref, eidx_ref, rank_ref, pos_ref, te_ref, gi_ref, ne_ref, na_ref, off_s):
    def offsets(e, run):
        off_s[e] = run
        return run + _tiles_of(cnt_ref[e]) * TM_E

    total = lax.fori_loop(0, NE, offsets, 0)
    na = total // TM_E
    na_ref[0] = na

    def idle(t, carry):
        te_ref[t] = NE - 1
        gi_ref[t] = 0
        ne_ref[t] = -1
        return carry

    lax.fori_loop(na, NT_E, idle, 0)

    def forward(e, ordinal):
        t0 = off_s[e] // TM_E
        nt = _tiles_of(cnt_ref[e])

        def mark(t, carry):
            te_ref[t] = e
            gi_ref[t] = ordinal
            return carry

        lax.fori_loop(t0, t0 + nt, mark, 0)
        return ordinal + jnp.where(nt > 0, 1, 0)

    lax.fori_loop(0, NE, forward, 0)

    def backward(i, nxt):
        e = NE - 1 - i
        t0 = off_s[e] // TM_E
        nt = _tiles_of(cnt_ref[e])

        def mark(t, carry):
            ne_ref[t] = nxt
            return carry

        lax.fori_loop(t0, t0 + nt, mark, 0)
        return jnp.where(nt > 0, e, nxt)

    lax.fori_loop(0, NE, backward, -1)

    eidx = eidx_ref[...]
    pos = rank_ref[...]
    for e in range(NE):
        pos = pos + jnp.where(eidx == e, off_s[e], 0)
    pos_ref[...] = pos


def _plan_call(cnt, eidx_t, rank_t):
    smem = pl.BlockSpec(memory_space=pltpu.SMEM)
    vmem = pl.BlockSpec(memory_space=pltpu.VMEM)
    tiles = jax.ShapeDtypeStruct((NT_E,), I32)
    return pl.pallas_call(
        _plan_kernel,
        in_specs=[smem, vmem, vmem],
        out_specs=[vmem, smem, smem, smem, smem, smem],
        out_shape=[jax.ShapeDtypeStruct((TOPK, T_ALL), I32), tiles, tiles, tiles,
                   jax.ShapeDtypeStruct((1,), I32), jax.ShapeDtypeStruct((NE,), I32)],
        name="moe_plan",
    )(cnt, eidx_t, rank_t)


BLK_PER_TILE = TM // BLK


def _final_copies(step, obuf, yp_ref, ys_ref, sem):
    out = []
    slot = step % 2
    for m in range(BLK_PER_TILE):
        g = step * BLK_PER_TILE + m
        b = g // NBLK
        j = g % NBLK
        src = obuf.at[slot, pl.ds(m * BLK, BLK)]
        r0 = pl.multiple_of(jnp.maximum(b * SEQ + (j - 1) * BLK, 0), BLK)
        out.append((jnp.logical_and(g < NB * NBLK, j >= 1),
                    pltpu.make_async_copy(src, yp_ref.at[pl.ds(r0, BLK)], sem.at[slot])))
        out.append((g == NB * NBLK, pltpu.make_async_copy(src, ys_ref, sem.at[slot])))
    return out


def _dispatch_kernel(cnt_ref, off_ref, na_ref, pos_ref, hp_ref, xs_ref, zero_ref, sem, zsem):
    i = pl.program_id(0)

    def issue(t, carry):
        for k in range(TOPK):
            p = pos_ref[0, 0, k * TM + t]
            pltpu.make_async_copy(hp_ref.at[pl.ds(t, 1)], xs_ref.at[pl.ds(p, 1)], sem).start(
                priority=k % 2)
        return carry

    lax.fori_loop(0, TM, issue, 0)

    @pl.when(i == 0)
    def _():
        zero_ref[...] = jnp.zeros_like(zero_ref)

        def row_copy(r):
            return pltpu.make_async_copy(zero_ref.at[pl.ds(0, 1)], xs_ref.at[pl.ds(r, 1)], zsem)

        def group_copy(q):
            r8 = pl.multiple_of(q * 8, 8)
            return pltpu.make_async_copy(zero_ref.at[pl.ds(0, 8)], xs_ref.at[pl.ds(r8, 8)], zsem)

        def tile_copy(t):
            r0 = pl.multiple_of(t * TM_E, TM_E)
            return pltpu.make_async_copy(zero_ref, xs_ref.at[pl.ds(r0, TM_E)], zsem)

        def run(copy, lo, hi):
            def start(x, carry):
                copy(x).start()
                return carry

            def wait(x, carry):
                copy(x).wait()
                return carry

            lax.fori_loop(lo, hi, start, 0)
            lax.fori_loop(lo, hi, wait, 0)

        def per_expert(e, carry):
            lo = off_ref[e] + cnt_ref[e]
            hi = off_ref[e] + _tiles_of(cnt_ref[e]) * TM_E
            mid = jnp.minimum(((lo + 7) // 8) * 8, hi)
            run(row_copy, lo, mid)
            run(group_copy, mid // 8, hi // 8)
            return carry

        lax.fori_loop(0, NE, per_expert, 0)
        run(tile_copy, na_ref[0], NT_E)

    for _ in range(TOPK):
        pltpu.make_async_copy(hp_ref, xs_ref.at[pl.ds(0, TM)], sem).wait()


def _dispatch_call(cnt, off, nact, pos_tiles, hp):
    return pl.pallas_call(
        _dispatch_kernel,
        grid_spec=pltpu.PrefetchScalarGridSpec(
            num_scalar_prefetch=3,
            grid=(N_ROW_TILES,),
            in_specs=[
                pl.BlockSpec((1, 1, TM * TOPK), lambda i, c, o, n: (i, 0, 0), memory_space=pltpu.SMEM),
                pl.BlockSpec((TM, HALF), lambda i, c, o, n: (i, 0)),
            ],
            out_specs=pl.BlockSpec(memory_space=pl.ANY),
            scratch_shapes=[
                pltpu.VMEM((TM_E, HALF), U32),
                pltpu.SemaphoreType.DMA(()),
                pltpu.SemaphoreType.DMA(()),
            ],
        ),
        out_shape=jax.ShapeDtypeStruct((R_ROWS, HALF), U32),
        compiler_params=_cparams(("arbitrary",)),
        name="moe_dispatch",
    )(cnt, off, nact, pos_tiles, hp)


def _experts_kernel(layer, te_ref, gi_ref, ne_ref, na_ref, xs_ref, wg_hbm, wu_hbm, wd_hbm, ys_ref,
                    wgf, wuf, wdf, wg_s, wu_s, wd_s, wsem):
    i = pl.program_id(0)
    na = na_ref[0]

    def weight_copies(e, slot):
        return (pltpu.make_async_copy(wg_hbm.at[layer, e], wgf.at[slot], wsem.at[0]),
                pltpu.make_async_copy(wu_hbm.at[layer, e], wuf.at[slot], wsem.at[1]),
                pltpu.make_async_copy(wd_hbm.at[layer, e], wdf.at[slot], wsem.at[2]))

    @pl.when(i == 0)
    def _():
        for cp in weight_copies(te_ref[0], 0):
            cp.start()

    ic = jnp.minimum(i, na - 1)
    first = jnp.logical_or(i == 0, te_ref[ic] != te_ref[jnp.maximum(ic - 1, 0)])

    @pl.when(jnp.logical_and(i < na, first))
    def _():
        slot = gi_ref[ic] % 2
        for cp in weight_copies(te_ref[ic], slot):
            cp.wait()
        wg_s[...] = wgf[slot].astype(BF16)
        wu_s[...] = wuf[slot].astype(BF16)
        wd_s[...] = wdf[slot].astype(BF16)
        nxt = ne_ref[ic]

        @pl.when(nxt >= 0)
        def _():
            for cp in weight_copies(nxt, 1 - slot):
                cp.start()

    @pl.when(i < na)
    def _():
        lo, hi = _unpack_pair(xs_ref[...])
        lo = lo.astype(BF16)
        hi = hi.astype(BF16)
        hg = (jnp.dot(lo, wg_s[0:HALF, :], preferred_element_type=F32)
              + jnp.dot(hi, wg_s[HALF:D, :], preferred_element_type=F32))
        hu = (jnp.dot(lo, wu_s[0:HALF, :], preferred_element_type=F32)
              + jnp.dot(hi, wu_s[HALF:D, :], preferred_element_type=F32))
        act = (hg * _sigmoid(hg) * hu).astype(BF16)
        y = jnp.dot(act, wd_s[...], preferred_element_type=F32)
        ys_ref[...] = _pack_pair(y[:, :HALF], y[:, HALF:])

    @pl.when(i >= na)
    def _():
        ys_ref[...] = jnp.zeros_like(ys_ref)


def _experts_call(layer, te, gi, ne, nact, xs, w_gate, w_up, w_down):
    anyspec = pl.BlockSpec(memory_space=pl.ANY)
    return pl.pallas_call(
        functools.partial(_experts_kernel, layer),
        grid_spec=pltpu.PrefetchScalarGridSpec(
            num_scalar_prefetch=4,
            grid=(NT_E,),
            in_specs=[
                pl.BlockSpec((TM_E, HALF), lambda i, te_r, gi_r, ne_r, na_r: (jnp.minimum(i, na_r[0] - 1), 0)),
                anyspec, anyspec, anyspec,
            ],
            out_specs=pl.BlockSpec((TM_E, HALF), lambda i, te_r, gi_r, ne_r, na_r: (i, 0)),
            scratch_shapes=[
                pltpu.VMEM((2, D, DE), F32),
                pltpu.VMEM((2, D, DE), F32),
                pltpu.VMEM((2, DE, D), F32),
                pltpu.VMEM((D, DE), BF16),
                pltpu.VMEM((D, DE), BF16),
                pltpu.VMEM((DE, D), BF16),
                pltpu.SemaphoreType.DMA((3,)),
            ],
        ),
        out_shape=jax.ShapeDtypeStruct((R_ROWS, HALF), U32),
        compiler_params=_cparams(("arbitrary",)),
        name="moe_experts",
    )(te, gi, ne, nact, xs, w_gate, w_up, w_down)


def _gather_kernel(final, pos_ref, posn_ref, w_ref, h_ref, ys_ref, wsg_ref, wsu_ref, wsd_ref,
                   g_ref, b_ref, *rest):
    i = pl.program_id(0)
    if final:
        yp_ref, ys_out_ref, gbuf, gsem, obuf, osem = rest

        def wait_step(step):
            for cond, cp in _final_copies(step, obuf, yp_ref, ys_out_ref, osem):
                @pl.when(cond)
                def _():
                    cp.wait()

        @pl.when(i >= 2)
        def _():
            wait_step(i - 2)
    else:
        out_ref, gbuf, gsem = rest

    def issue(idx_ref, slot):
        def body(t, carry):
            for k in range(TOPK):
                p = idx_ref[0, 0, k * TM + t]
                pltpu.make_async_copy(ys_ref.at[pl.ds(p, 1)], gbuf.at[slot, k, pl.ds(t, 1)],
                                      gsem.at[slot]).start(priority=k % 2)
            return carry

        lax.fori_loop(0, TM, body, 0)

    @pl.when(i == 0)
    def _():
        issue(pos_ref, 0)

    @pl.when(i + 1 < N_ROW_TILES)
    def _():
        issue(posn_ref, (i + 1) % 2)

    h = h_ref[...]
    hb = h.astype(BF16)
    sg = jnp.dot(hb, wsg_ref[...], preferred_element_type=F32)
    su = jnp.dot(hb, wsu_ref[...], preferred_element_type=F32)
    act = (sg * _sigmoid(sg) * su).astype(BF16)
    y = ALPHA * h + jnp.dot(act, wsd_ref[...], preferred_element_type=F32)
    ylo = y[:, :HALF]
    yhi = y[:, HALF:]

    slot = i % 2
    for k in range(TOPK):
        pltpu.make_async_copy(ys_ref.at[pl.ds(0, TM)], gbuf.at[slot, k], gsem.at[slot]).wait()
    w = w_ref[...]
    for k in range(TOPK):
        lo, hi = _unpack_pair(gbuf[slot, k])
        wk = w[:, k:k + 1]
        ylo = ylo + wk * lo
        yhi = yhi + wk * hi
    out = _layer_norm(jnp.concatenate([ylo, yhi], axis=1), g_ref[...], b_ref[...])
    if not final:
        out_ref[...] = out
        return
    obuf[i % 2] = out
    for cond, cp in _final_copies(i, obuf, yp_ref, ys_out_ref, osem):
        @pl.when(cond)
        def _():
            cp.start()

    @pl.when(i == N_ROW_TILES - 1)
    def _():
        wait_step(i - 1)
        wait_step(i)


def _gather_call(final, pos_tiles, wsel, h, ys, wsg, wsu, wsd, g, b):
    vec = pl.BlockSpec((1, D), lambda i: (0, 0))
    once = dict(pipeline_mode=pl.Buffered(1))
    anyspec = pl.BlockSpec(memory_space=pl.ANY)
    scratch = [pltpu.VMEM((2, TOPK, TM, HALF), U32), pltpu.SemaphoreType.DMA((2,))]
    if final:
        out_specs = [anyspec, anyspec]
        out_shape = [jax.ShapeDtypeStruct((NB * SEQ, D), F32), jax.ShapeDtypeStruct((DEC, D), F32)]
        scratch += [pltpu.VMEM((2, TM, D), F32), pltpu.SemaphoreType.DMA((2,))]
    else:
        out_specs = pl.BlockSpec((TM, D), lambda i: (i, 0))
        out_shape = jax.ShapeDtypeStruct((T_ALL, D), F32)
    smem_tile = lambda f: pl.BlockSpec((1, 1, TM * TOPK), f, memory_space=pltpu.SMEM)
    return pl.pallas_call(
        functools.partial(_gather_kernel, final),
        grid=(N_ROW_TILES,),
        in_specs=[
            smem_tile(lambda i: (i, 0, 0)),
            smem_tile(lambda i: (jnp.minimum(i + 1, N_ROW_TILES - 1), 0, 0)),
            pl.BlockSpec((TM, TOPK), lambda i: (i, 0)),
            pl.BlockSpec((TM, D), lambda i: (i, 0)),
            anyspec,
            pl.BlockSpec((D, DE), lambda i: (0, 0), **once),
            pl.BlockSpec((D, DE), lambda i: (0, 0), **once),
            pl.BlockSpec((DE, D), lambda i: (0, 0), **once),
            vec,
            vec,
        ],
        out_specs=out_specs,
        out_shape=out_shape,
        scratch_shapes=scratch,
        compiler_params=_cparams(("arbitrary",)),
        name="moe_combine_final" if final else "moe_combine",
    )(pos_tiles, pos_tiles, wsel, h, ys, wsg, wsu, wsd, g, b)


def _moe_layer(layer, final, h, hp, w_router, router_bias, w_exp_gate, w_exp_up, w_exp_down,
               w_sh_gate, w_sh_up, w_sh_down, ln_g, ln_b):
    eidx_t, wsel, rank_t, counts = _router_call(h, w_router[layer], router_bias[layer])
    cnt = counts[:, 0].astype(I32)
    pos_t, te, gi, ne, nact, off = _plan_call(cnt, eidx_t, rank_t)
    pos_tiles = pos_t.reshape(TOPK, N_ROW_TILES, TM).transpose(1, 0, 2).reshape(N_ROW_TILES, 1, TOPK * TM)
    xs = _dispatch_call(cnt, off, nact, pos_tiles, hp)
    ys = _experts_call(layer, te, gi, ne, nact, xs, w_exp_gate, w_exp_up, w_exp_down)
    return _gather_call(
        final, pos_tiles, wsel, h, ys,
        w_sh_gate[layer].astype(BF16), w_sh_up[layer].astype(BF16), w_sh_down[layer].astype(BF16),
        ln_g[layer][None, :], ln_b[layer][None, :])


def _ssm_prep_kernel(lr_ref, li_ref, ldt_ref, br_ref, bi_ref, abr_ref, abi_ref, bbr_ref, bbi_ref):
    lr = lr_ref[...]
    li = li_ref[...]
    dt = jnp.exp(ldt_ref[...])
    mag = jnp.exp(lr * dt)
    ab_re = mag * jnp.cos(li * dt)
    ab_im = mag * jnp.sin(li * dt)
    den = lr * lr + li * li
    nr = ab_re - 1.0
    ni = ab_im
    cr = (nr * lr + ni * li) / den
    ci = (ni * lr - nr * li) / den
    br = br_ref[...]
    bi = bi_ref[...]
    abr_ref[...] = ab_re
    abi_ref[...] = ab_im
    bbr_ref[...] = cr * br - ci * bi
    bbi_ref[...] = cr * bi + ci * br


def _ssm_prep_call(lam_re, lam_im, log_dt, b_re, b_im):
    wide = (NG, GC * NS)
    lr = jnp.tile(lam_re, (1, GC))
    li = jnp.tile(lam_im, (1, GC))
    ldt = jnp.broadcast_to(log_dt[:, None], wide)
    br = jnp.transpose(b_re, (0, 2, 1)).reshape(wide)
    bi = jnp.transpose(b_im, (0, 2, 1)).reshape(wide)
    sds = jax.ShapeDtypeStruct(wide, F32)
    return pl.pallas_call(
        _ssm_prep_kernel, out_shape=[sds, sds, sds, sds], name="ssm_prep",
        compiler_params=pltpu.CompilerParams(vmem_limit_bytes=VMEM_LIMIT),
    )(lr, li, ldt, br, bi)


def _cmul(ar, ai, xr, xi):
    return ar * xr - ai * xi, ar * xi + ai * xr


NSEG = 8
SEGL = LP // NSEG
STEP_UNROLL = 4
MOVE_UNROLL = 8
assert LP == NSEG * SEGL and SEGL % STEP_UNROLL == 0 and SEGL % MOVE_UNROLL == 0 and PAD_FRONT < SEGL
assert STEP_UNROLL % 2 == 0


def _ssm_p_kernel(u_ref, wb_ref, wc_ref, a_ref, d_ref, z_ref, st_ref, u_scr, s_scr, sb_scr, y_scr):
    def interleave(it, carry):
        for q in range(MOVE_UNROLL):
            t = it * MOVE_UNROLL + q
            u_scr[pl.ds(pl.multiple_of(t * NSEG, NSEG), NSEG), :] = u_ref[pl.ds(t, NSEG, stride=SEGL), :]
        return carry

    lax.fori_loop(0, SEGL // MOVE_UNROLL, interleave, 0)
    row = lax.broadcasted_iota(I32, (LP, LANES), 0)
    is_pad = jnp.logical_and(row % NSEG == 0, row // NSEG < PAD_FRONT)
    u = jnp.where(is_pad, 0.0, u_scr[...])
    s_scr[...] = jnp.dot(u.astype(BF16), wb_ref[0], preferred_element_type=F32)

    ar = a_ref[0, 0:1, :]
    ai = a_ref[0, 1:2, :]

    def group(t):
        r0 = pl.multiple_of(t * NSEG, NSEG)
        return pl.ds(r0, NSEG)

    def local_scan(it, carry):
        sr, si = carry
        for q in range(STEP_UNROLL):
            g = group(it * STEP_UNROLL + q)
            tr, ti = _cmul(ar, ai, sr, si)
            sr = tr + s_scr[g, 0:CH_STATE]
            si = ti + s_scr[g, CH_STATE:]
            s_scr[g, 0:CH_STATE] = sr
            s_scr[g, CH_STATE:] = si
        return sr, si

    zero8 = jnp.zeros((NSEG, CH_STATE), F32)
    er, ei = lax.fori_loop(0, SEGL // STEP_UNROLL, local_scan, (zero8, zero8))

    pr, pi = ar, ai
    acc = None
    bits = SEGL
    while bits:
        if bits & 1:
            acc = (pr, pi) if acc is None else _cmul(pr, pi, *acc)
        bits >>= 1
        if bits:
            pr, pi = _cmul(pr, pi, pr, pi)
    alr, ali = acc
    cr = jnp.zeros((1, CH_STATE), F32)
    ci = jnp.zeros((1, CH_STATE), F32)
    ins_r, ins_i = [], []
    for j in range(NSEG):
        ins_r.append(cr)
        ins_i.append(ci)
        tr, ti = _cmul(alr, ali, cr, ci)
        cr = tr + er[j:j + 1, :]
        ci = ti + ei[j:j + 1, :]
    st_ref[0, 0, :, 0:CH_STATE] = cr
    st_ref[0, 0, :, CH_STATE:] = ci

    def fixup(it, carry):
        dr, di = carry
        for q in range(0, STEP_UNROLL, 2):
            t0 = it * STEP_UNROLL + q
            rows_r, rows_i = [], []
            for t in (t0, t0 + 1):
                g = group(t)
                dr, di = _cmul(ar, ai, dr, di)
                rows_r.append(s_scr[g, 0:CH_STATE] + dr)
                rows_i.append(s_scr[g, CH_STATE:] + di)
            pair = pl.ds(pl.multiple_of(t0 * NSEG, 2 * NSEG), 2 * NSEG)
            sb_scr[pair, 0:CH_STATE] = jnp.concatenate(rows_r, axis=0).astype(BF16)
            sb_scr[pair, CH_STATE:] = jnp.concatenate(rows_i, axis=0).astype(BF16)
        return dr, di

    lax.fori_loop(0, SEGL // STEP_UNROLL, fixup,
                  (jnp.concatenate(ins_r, axis=0), jnp.concatenate(ins_i, axis=0)))

    y_scr[...] = jnp.dot(sb_scr[...], wc_ref[0], preferred_element_type=F32) + d_ref[0] * u

    def deinterleave(it, carry):
        for q in range(MOVE_UNROLL):
            t = it * MOVE_UNROLL + q
            u_scr[pl.ds(t, NSEG, stride=SEGL), :] = y_scr[pl.ds(pl.multiple_of(t * NSEG, NSEG), NSEG), :]
        return carry

    lax.fori_loop(0, SEGL // MOVE_UNROLL, deinterleave, 0)
    z_ref[...] = jax.nn.gelu(u_scr[...]).astype(BF16)


def _ssm_p_call(h, wb_bf, wc_bf, a_tab, d_tab):
    return pl.pallas_call(
        _ssm_p_kernel,
        grid=(NB, NCHUNK),
        in_specs=[
            pl.BlockSpec((LP, LANES), lambda b, k: (b, k)),
            pl.BlockSpec((1, LANES, 2 * CH_STATE), lambda b, k: (k, 0, 0)),
            pl.BlockSpec((1, 2 * CH_STATE, LANES), lambda b, k: (k, 0, 0)),
            pl.BlockSpec((1, 2, CH_STATE), lambda b, k: (k, 0, 0)),
            pl.BlockSpec((1, 1, LANES), lambda b, k: (k, 0, 0)),
        ],
        out_specs=[
            pl.BlockSpec((LP, LANES), lambda b, k: (b, k)),
            pl.BlockSpec((1, 1, 1, 2 * CH_STATE), lambda b, k: (b, k, 0, 0)),
        ],
        out_shape=[
            jax.ShapeDtypeStruct((T_PROMPT, D), BF16),
            jax.ShapeDtypeStruct((NB, NCHUNK, 1, 2 * CH_STATE), F32),
        ],
        scratch_shapes=[pltpu.VMEM((LP, LANES), F32), pltpu.VMEM((LP, 2 * CH_STATE), F32),
                        pltpu.VMEM((LP, 2 * CH_STATE), BF16), pltpu.VMEM((LP, LANES), F32)],
        compiler_params=_cparams(("parallel", "parallel")),
        name="ssm_prompt",
    )(h, wb_bf, wc_bf, a_tab, d_tab)


def _ssm_s_kernel(u_ref, sr_ref, si_ref, wb_ref, wc_ref, a_ref, d_ref, z_ref, nr_ref, ni_ref):
    u = u_ref[...]
    bu = jnp.dot(u, wb_ref[0], preferred_element_type=F32, precision=lax.Precision.HIGHEST)
    ar = a_ref[0, 0:1, :]
    ai = a_ref[0, 1:2, :]
    tr, ti = _cmul(ar, ai, sr_ref[...], si_ref[...])
    nr = tr + bu[:, 0:CH_STATE]
    ni = ti + bu[:, CH_STATE:]
    nr_ref[...] = nr
    ni_ref[...] = ni
    s = jnp.concatenate([nr, ni], axis=1).astype(BF16)
    y = jnp.dot(s, wc_ref[0], preferred_element_type=F32) + d_ref[0] * u
    z_ref[...] = jax.nn.gelu(y).astype(BF16)


def _ssm_s_call(h, s0r, s0i, wb_f32, wc_bf, a_tab, d_tab):
    st = pl.BlockSpec((DEC, CH_STATE), lambda k: (0, k))
    return pl.pallas_call(
        _ssm_s_kernel,
        grid=(NCHUNK,),
        in_specs=[
            pl.BlockSpec((DEC, LANES), lambda k: (T_PROMPT // DEC, k)),
            st,
            st,
            pl.BlockSpec((1, LANES, 2 * CH_STATE), lambda k: (k, 0, 0)),
            pl.BlockSpec((1, 2 * CH_STATE, LANES), lambda k: (k, 0, 0)),
            pl.BlockSpec((1, 2, CH_STATE), lambda k: (k, 0, 0)),
            pl.BlockSpec((1, 1, LANES), lambda k: (k, 0, 0)),
        ],
        out_specs=[pl.BlockSpec((DEC, LANES), lambda k: (0, k)), st, st],
        out_shape=[
            jax.ShapeDtypeStruct((DEC, D), BF16),
            jax.ShapeDtypeStruct((DEC, NG * NS), F32),
            jax.ShapeDtypeStruct((DEC, NG * NS), F32),
        ],
        compiler_params=_cparams(("parallel",)),
        name="ssm_sample",
    )(h, s0r, s0i, wb_f32, wc_bf, a_tab, d_tab)


def _glu_ln_kernel(z_ref, zt_ref, w_ref, bg_ref, h_ref, g_ref, b_ref, out_ref, hp_ref):
    z = _prompt_or_tail(z_ref, zt_ref)
    acc = jnp.dot(z, w_ref[...], preferred_element_type=F32) + bg_ref[...]
    m = acc[:, :D] * _sigmoid(acc[:, D:])
    out = _layer_norm(ALPHA * h_ref[...] + m, g_ref[...], b_ref[...])
    out_ref[...] = out
    hp_ref[...] = _pack_pair(out[:, :HALF], out[:, HALF:])


def _glu_ln_call(z_p, z_tail, w_bf, bg, h, g, b):
    vec = pl.BlockSpec((1, D), lambda i: (0, 0))
    return pl.pallas_call(
        _glu_ln_kernel,
        grid=(N_ROW_TILES,),
        in_specs=[
            pl.BlockSpec((TM, D), _PROMPT_TILE),
            pl.BlockSpec((TM, D), lambda i: (0, 0)),
            pl.BlockSpec((D, 2 * D), lambda i: (0, 0), pipeline_mode=pl.Buffered(1)),
            pl.BlockSpec((1, 2 * D), lambda i: (0, 0)),
            pl.BlockSpec((TM, D), lambda i: (i, 0)),
            vec,
            vec,
        ],
        out_specs=[pl.BlockSpec((TM, D), lambda i: (i, 0)), pl.BlockSpec((TM, HALF), lambda i: (i, 0))],
        out_shape=[jax.ShapeDtypeStruct((T_ALL, D), F32), jax.ShapeDtypeStruct((T_ALL, HALF), U32)],
        compiler_params=_cparams(("parallel",)),
        name="glu_ln",
    )(z_p, z_tail, w_bf, bg, h, g, b)


GROUPS_PER_CHUNK = LANES // GC


def _block_diag_in(t):
    t3 = t.reshape(NCHUNK, LANES, NS)
    same = (np.arange(LANES)[:, None] // GC) == (np.arange(CH_STATE)[None, :] // NS)
    return jnp.tile(t3, (1, 1, GROUPS_PER_CHUNK)) * jnp.asarray(same, t.dtype)


def _block_diag_out(t):
    t3 = jnp.swapaxes(t, 1, 2).reshape(NCHUNK, CH_STATE, GC)
    same = (np.arange(CH_STATE)[:, None] // NS) == (np.arange(LANES)[None, :] // GC)
    return jnp.tile(t3, (1, 1, GROUPS_PER_CHUNK)) * jnp.asarray(same, t.dtype)


def kernel(x_prompt, x_sample, cache_k, cache_v, state_ssm_re, state_ssm_im, meta_tokens, w_qkv, b_qkv, attn_sinks, w_o, b_o, ssm_lam_re, ssm_lam_im, ssm_log_dt, ssm_b_re, ssm_b_im, ssm_c_re, ssm_c_im, ssm_d, w_glu, b_glu, ln_mix_g, ln_mix_b, w_router, router_bias, w_exp_gate, w_exp_up, w_exp_down, w_sh_gate, w_sh_up, w_sh_down, ln_ffn_g, ln_ffn_b):
    moe_w = (w_router, router_bias, w_exp_gate, w_exp_up, w_exp_down, w_sh_gate, w_sh_up, w_sh_down,
             ln_ffn_g, ln_ffn_b)

    front = jnp.concatenate([jnp.zeros((PAD_FRONT, D), F32), meta_tokens], axis=0)
    pieces = []
    for b in range(NB):
        pieces += [front, x_prompt[b]]
    h = jnp.concatenate(pieces + [x_sample.reshape(DEC, D)], axis=0)

    rc, rs1, rs2 = _rope_tables()
    q, k, v = _qkv_call(h, w_qkv[0].astype(BF16), b_qkv[0][None, :], rc, rs1, rs2)
    o_p = _attn_p_call(attn_sinks[0], q, k, v)
    q3 = q[T_PROMPT:].astype(F32).reshape(DEC, NH, HD)
    o_s, ck_new, cv_new = _attn_s_call(
        attn_sinks[0][:, None], q3, k, v,
        cache_k[0].reshape(DEC, BLK, NKV * HD), cache_v[0].reshape(DEC, BLK, NKV * HD))
    o_tail = _tail_tile(o_p, o_s.reshape(DEC, D).astype(BF16))
    h, hp = _oproj_ln_call(o_p, o_tail, w_o[0].astype(BF16), b_o[0][None, :], h,
                           ln_mix_g[0][None, :], ln_mix_b[0][None, :])
    h = _moe_layer(0, False, h, hp, *moe_w)

    def last_window(t):
        rows = [t[(b + 1) * LP - BLK:(b + 1) * LP] for b in range(NB)]
        return jnp.stack(rows).reshape(NB, BLK, NKV, HD)

    kp = last_window(k)
    vp = last_window(v)

    ab_re, ab_im, bb_re, bb_im = _ssm_prep_call(
        ssm_lam_re[0], ssm_lam_im[0], ssm_log_dt[0], ssm_b_re[0], ssm_b_im[0])
    wb = jnp.concatenate([_block_diag_in(bb_re), _block_diag_in(bb_im)], axis=2)
    wc = jnp.concatenate([_block_diag_out(ssm_c_re[0]), -_block_diag_out(ssm_c_im[0])], axis=1)
    wc_bf = wc.astype(BF16)
    a_tab = jnp.stack([ab_re[:, :NS].reshape(NCHUNK, CH_STATE),
                       ab_im[:, :NS].reshape(NCHUNK, CH_STATE)], axis=1)
    d_tab = ssm_d[0].reshape(NCHUNK, 1, LANES)
    z_p, st_p = _ssm_p_call(h, wb.astype(BF16), wc_bf, a_tab, d_tab)
    z_s, sr_new, si_new = _ssm_s_call(
        h, state_ssm_re[0].reshape(DEC, NG * NS), state_ssm_im[0].reshape(DEC, NG * NS),
        wb, wc_bf, a_tab, d_tab)
    h, hp = _glu_ln_call(z_p, _tail_tile(z_p, z_s), w_glu[0].astype(BF16), b_glu[0][None, :], h,
                         ln_mix_g[1][None, :], ln_mix_b[1][None, :])
    y_prompt, y_sample = _moe_layer(1, True, h, hp, *moe_w)
    y_prompt = y_prompt.reshape(NB, SEQ, D)
    y_sample = y_sample.reshape(DEC, 1, D)
    st_p = st_p.reshape(NB, NCHUNK, 2, 8, NS)
    rp = st_p[:, :, 0].reshape(NB, NG, NS)
    ip = st_p[:, :, 1].reshape(NB, NG, NS)
    return (y_prompt, y_sample,
            kp[None], vp[None],
            ck_new.reshape(1, DEC, BLK, NKV, HD), cv_new.reshape(1, DEC, BLK, NKV, HD),
            rp[None], ip[None],
            sr_new.reshape(1, DEC, NG, NS), si_new.reshape(1, DEC, NG, NS))
```

```python
import functools
import math

import jax
import jax.numpy as jnp
import numpy as np
from jax import lax
from jax.experimental import pallas as pl
from jax.experimental.pallas import tpu as pltpu

F32 = jnp.float32
BF16 = jnp.bfloat16
I32 = jnp.int32
U32 = jnp.uint32

D = 2048
HALF = D // 2
NB = 4
N_META = 16
SEQ = 2048
L = N_META + SEQ
BLK = 128
PAD_FRONT = (-L) % BLK
LP = L + PAD_FRONT
NBLK = LP // BLK
T_PROMPT = NB * LP
DEC = 128
T_ALL = T_PROMPT + DEC
PAST_LEN = 8192
HD = 64
NH = 32
NKV = 4
QPK = NH // NKV
QKV = (NH + 2 * NKV) * HD
QK_COLS = (NH + NKV) * HD
ROT = HD // 4
ROT_HALF = ROT // 2
ROPE_THETA = 500000.0
NG = 128
GC = 16
NS = 64
NCHUNK = 16
CH_STATE = 8 * NS
NE = 64
TOPK = 8
NEG = 8
PER_GRP = NE // NEG
TOPG = 4
DE = 512
ROUTED_SCALE = 2.5
DEPTH = 2
ALPHA = (2 * DEPTH) ** 0.25
LN_EPS = 1e-5

V7X_VMEM_BYTES = 64 * 1024 * 1024
VMEM_LIMIT = 56 * 1024 * 1024
LANES = 128

TM = 384
N_ROW_TILES = T_ALL // TM
TAIL_START = (N_ROW_TILES - 1) * TM
TM_E = 256
N_PAIRS = T_ALL * TOPK
NT_E = -(-N_PAIRS // TM_E) + NE
R_ROWS = NT_E * TM_E
assert T_ALL % TM == 0 and TAIL_START <= T_PROMPT


def _cparams(sem):
    return pltpu.CompilerParams(dimension_semantics=sem, vmem_limit_bytes=VMEM_LIMIT)


def _sigmoid(x):
    return 1.0 / (1.0 + jnp.exp(-x))


def _layer_norm(y, g, b):
    mu = jnp.mean(y, axis=-1, keepdims=True)
    yc = y - mu
    var = jnp.mean(yc * yc, axis=-1, keepdims=True)
    return yc * lax.rsqrt(var + LN_EPS) * g + b


def _pack_pair(lo, hi):
    lo_b = lax.bitcast_convert_type(lo.astype(BF16).astype(F32), U32) >> 16
    hi_b = lax.bitcast_convert_type(hi.astype(BF16).astype(F32), U32) & jnp.uint32(0xFFFF0000)
    return lo_b | hi_b


def _unpack_pair(w):
    lo = lax.bitcast_convert_type(w << 16, F32)
    hi = lax.bitcast_convert_type(w & jnp.uint32(0xFFFF0000), F32)
    return lo, hi


ROW_TILE = HALF // LANES
assert ROW_TILE == 8


def _store_rows_as_tiles(ref, rows):
    n = rows.shape[0]
    for j in range(ROW_TILE):
        ref[pl.ds(j, n, stride=ROW_TILE), :] = rows[:, j * LANES:(j + 1) * LANES]


def _load_tiles_as_rows(ref, n):
    return jnp.concatenate([ref[pl.ds(j, n, stride=ROW_TILE), :] for j in range(ROW_TILE)], axis=1)


def _qkv_kernel(x_ref, w_ref, b_ref, c_ref, s1_ref, s2_ref, q_ref, k_ref, v_ref):
    xb = x_ref[...].astype(BF16)
    acc = jnp.dot(xb, w_ref[...], preferred_element_type=F32) + b_ref[...]
    c = c_ref[...]
    s1 = s1_ref[...]
    s2 = s2_ref[...]
    for j in range(QK_COLS // LANES):
        blk = acc[:, j * LANES:(j + 1) * LANES]
        r = (blk * c + pltpu.roll(blk, LANES - ROT_HALF, axis=1) * s1
             + pltpu.roll(blk, ROT_HALF, axis=1) * s2)
        if j < D // LANES:
            q_ref[:, j * LANES:(j + 1) * LANES] = (r * (1.0 / math.sqrt(HD))).astype(BF16)
        else:
            jj = j - D // LANES
            k_ref[:, jj * LANES:(jj + 1) * LANES] = r
    v_ref[...] = acc[:, QK_COLS:]


def _qkv_call(x, w_bf, b, rc, rs1, rs2):
    return pl.pallas_call(
        _qkv_kernel,
        grid=(T_ALL // TM,),
        in_specs=[
            pl.BlockSpec((TM, D), lambda i: (i, 0)),
            pl.BlockSpec((D, QKV), lambda i: (0, 0)),
            pl.BlockSpec((1, QKV), lambda i: (0, 0)),
            pl.BlockSpec((TM, LANES), lambda i: (i, 0)),
            pl.BlockSpec((TM, LANES), lambda i: (i, 0)),
            pl.BlockSpec((TM, LANES), lambda i: (i, 0)),
        ],
        out_specs=[
            pl.BlockSpec((TM, D), lambda i: (i, 0)),
            pl.BlockSpec((TM, NKV * HD), lambda i: (i, 0)),
            pl.BlockSpec((TM, NKV * HD), lambda i: (i, 0)),
        ],
        out_shape=[
            jax.ShapeDtypeStruct((T_ALL, D), BF16),
            jax.ShapeDtypeStruct((T_ALL, NKV * HD), F32),
            jax.ShapeDtypeStruct((T_ALL, NKV * HD), F32),
        ],
        compiler_params=_cparams(("parallel",)),
        name="qkv_rope",
    )(x, w_bf, b, rc, rs1, rs2)


def _rope_tables():
    pos_p = jnp.maximum(jnp.arange(LP, dtype=I32) - PAD_FRONT, 0)
    pos = jnp.concatenate([pos_p, jnp.full((8,), PAST_LEN, I32)]).astype(F32)
    inv_freq = ROPE_THETA ** (-jnp.arange(0, ROT, 2, dtype=F32) / ROT)
    ang = pos[:, None] * inv_freq[None, :]
    cos = jnp.cos(ang)
    sin = jnp.sin(ang)
    lane = np.arange(LANES) % HD
    freq = np.arange(ROT_HALF)[:, None]
    first = ((lane[None, :] == freq) & (lane[None, :] < ROT_HALF)).astype(np.float32)
    second = ((lane[None, :] - ROT_HALF == freq) & (lane[None, :] < ROT)).astype(np.float32)
    rest = (lane >= ROT).astype(np.float32)[None, :]
    place = functools.partial(jnp.dot, precision=lax.Precision.HIGHEST)
    c = place(cos, jnp.asarray(first + second)) + jnp.asarray(rest)
    s1 = place(sin, jnp.asarray(-first))
    s2 = place(sin, jnp.asarray(second))

    def all_rows(t):
        return jnp.concatenate([t[:LP]] * NB + [t[LP:]] * (DEC // 8), axis=0)

    return all_rows(c), all_rows(s1), all_rows(s2)


def _attn_p_kernel(sink_ref, q_ref, kp_ref, kc_ref, vp_ref, vc_ref, o_ref):
    j = pl.program_id(0) % NBLK
    keys = jnp.concatenate([kp_ref[...], kc_ref[...]], axis=0).astype(BF16)
    vals = jnp.concatenate([vp_ref[...], vc_ref[...]], axis=0).astype(BF16)
    r = lax.broadcasted_iota(I32, (BLK, 2 * BLK), 0)
    c = lax.broadcasted_iota(I32, (BLK, 2 * BLK), 1)
    dist = BLK + r - c
    kpos = (j - 1) * BLK - PAD_FRONT + c
    mask = (dist >= 0) & (dist <= BLK) & (kpos >= 0)
    for g in range(NKV):
        kg = keys[:, g * HD:(g + 1) * HD]
        vg = vals[:, g * HD:(g + 1) * HD]
        heads = range(g * QPK, (g + 1) * QPK)
        scores = [lax.dot_general(q_ref[:, h * HD:(h + 1) * HD], kg, (((1,), (1,)), ((), ())),
                                  preferred_element_type=F32) for h in heads]
        probs, rdens = [], []
        for h, s in zip(heads, scores):
            s = jnp.where(mask, s, -jnp.inf)
            sk = sink_ref[h]
            m = jnp.maximum(jnp.max(s, axis=1, keepdims=True), sk)
            p = jnp.exp(s - m)
            rdens.append(1.0 / (jnp.sum(p, axis=1, keepdims=True) + jnp.exp(sk - m)))
            probs.append(p.astype(BF16))
        for h, p, rden in zip(heads, probs, rdens):
            oh = jnp.dot(p, vg, preferred_element_type=F32) * rden
            o_ref[:, h * HD:(h + 1) * HD] = oh.astype(BF16)


def _attn_p_call(sinks, q, k, v):
    prev = lambda i: (jnp.where(i % NBLK == 0, i, i - 1), 0)
    cur = lambda i: (i, 0)
    return pl.pallas_call(
        _attn_p_kernel,
        grid=(NB * NBLK,),
        in_specs=[
            pl.BlockSpec(memory_space=pltpu.SMEM),
            pl.BlockSpec((BLK, D), cur),
            pl.BlockSpec((BLK, NKV * HD), prev),
            pl.BlockSpec((BLK, NKV * HD), cur),
            pl.BlockSpec((BLK, NKV * HD), prev),
            pl.BlockSpec((BLK, NKV * HD), cur),
        ],
        out_specs=pl.BlockSpec((BLK, D), cur),
        out_shape=jax.ShapeDtypeStruct((T_PROMPT, D), BF16),
        compiler_params=_cparams(("parallel",)),
        name="attn_prompt",
    )(sinks, q, k, k, v, v)


SEQ_PER_STEP = 16
SEQ_UNROLL = 4


def _attn_s_kernel(sink_ref, q_ref, kn_ref, vn_ref, ck_ref, cv_ref, o_ref, cko_ref, cvo_ref):
    row = lax.broadcasted_iota(I32, (BLK, NKV * HD), 0)
    hrow = lax.broadcasted_iota(I32, (NH, NKV * HD), 0) // QPK
    hlane = lax.broadcasted_iota(I32, (NH, NKV * HD), 1) // HD
    own = hrow == hlane
    sk = sink_ref[...]

    def score_stage(s):
        kn = kn_ref[pl.ds(s, 1), :]
        knr = kn.astype(BF16).astype(F32)
        q = q_ref[s].astype(BF16)
        qe = jnp.where(own, jnp.concatenate([q] * NKV, axis=1), jnp.zeros((), BF16))
        sc = lax.dot_general(qe, ck_ref[s].astype(BF16), (((1,), (1,)), ((), ())),
                             preferred_element_type=F32)
        sn = jnp.sum(qe.astype(F32) * knr, axis=1, keepdims=True)
        return sc, sn

    def softmax_stage(sc, sn):
        m = jnp.maximum(jnp.maximum(jnp.max(sc, axis=1, keepdims=True), sn), sk)
        p = jnp.exp(sc - m)
        pn = jnp.exp(sn - m)
        rden = 1.0 / (jnp.sum(p, axis=1, keepdims=True) + pn + jnp.exp(sk - m))
        return p.astype(BF16), pn.astype(BF16).astype(F32), rden

    def value_stage(s, p, pn, rden):
        vn = vn_ref[pl.ds(s, 1), :]
        vnr = vn.astype(BF16).astype(F32)
        of = jnp.dot(p, cv_ref[s].astype(BF16), preferred_element_type=F32)
        of = jnp.where(own, of + pn * vnr, 0.0)
        og = of[:, 0:HD]
        for g in range(1, NKV):
            og = og + of[:, g * HD:(g + 1) * HD]
        o_ref[s] = og * rden
        cko_ref[s] = jnp.where(row == BLK - 1, kn_ref[pl.ds(s, 1), :], pltpu.roll(ck_ref[s], BLK - 1, axis=0))
        cvo_ref[s] = jnp.where(row == BLK - 1, vn, pltpu.roll(cv_ref[s], BLK - 1, axis=0))

    def body(it, carry):
        seqs = [it * SEQ_UNROLL + u for u in range(SEQ_UNROLL)]
        staged = [score_stage(s) for s in seqs]
        soft = [softmax_stage(*st) for st in staged]
        for s, sm in zip(seqs, soft):
            value_stage(s, *sm)
        return carry

    lax.fori_loop(0, SEQ_PER_STEP // SEQ_UNROLL, body, 0)


def _attn_s_call(sinks_col, q3, k, v, cache_k, cache_v):
    sp = SEQ_PER_STEP
    kv_off = T_PROMPT // sp
    return pl.pallas_call(
        _attn_s_kernel,
        grid=(DEC // sp,),
        in_specs=[
            pl.BlockSpec((NH, 1), lambda i: (0, 0)),
            pl.BlockSpec((sp, NH, HD), lambda i: (i, 0, 0)),
            pl.BlockSpec((sp, NKV * HD), lambda i: (kv_off + i, 0)),
            pl.BlockSpec((sp, NKV * HD), lambda i: (kv_off + i, 0)),
            pl.BlockSpec((sp, BLK, NKV * HD), lambda i: (i, 0, 0)),
            pl.BlockSpec((sp, BLK, NKV * HD), lambda i: (i, 0, 0)),
        ],
        out_specs=[
            pl.BlockSpec((sp, NH, HD), lambda i: (i, 0, 0)),
            pl.BlockSpec((sp, BLK, NKV * HD), lambda i: (i, 0, 0)),
            pl.BlockSpec((sp, BLK, NKV * HD), lambda i: (i, 0, 0)),
        ],
        out_shape=[
            jax.ShapeDtypeStruct((DEC, NH, HD), F32),
            jax.ShapeDtypeStruct((DEC, BLK, NKV * HD), F32),
            jax.ShapeDtypeStruct((DEC, BLK, NKV * HD), F32),
        ],
        compiler_params=_cparams(("parallel",)),
        name="attn_sample",
    )(sinks_col, q3, k, v, cache_k, cache_v)


def _prompt_or_tail(x_ref, tail_ref):
    return jnp.where(pl.program_id(0) == N_ROW_TILES - 1, tail_ref[...], x_ref[...])


def _tail_tile(x_prompt_rows, x_sample_rows):
    return jnp.concatenate([x_prompt_rows[TAIL_START:], x_sample_rows], axis=0)


_PROMPT_TILE = lambda i: (jnp.minimum(i, N_ROW_TILES - 2), 0)


def _oproj_ln_kernel(o_ref, ot_ref, w_ref, bo_ref, h_ref, g_ref, b_ref, out_ref, hp_ref):
    o = _prompt_or_tail(o_ref, ot_ref)
    m = jnp.dot(o, w_ref[...], preferred_element_type=F32) + bo_ref[...]
    out = _layer_norm(ALPHA * h_ref[...] + m, g_ref[...], b_ref[...])
    out_ref[...] = out
    _store_rows_as_tiles(hp_ref, _pack_pair(out[:, :HALF], out[:, HALF:]))


def _oproj_ln_call(o_p, o_tail, w_bf, bo, h, g, b):
    vec = pl.BlockSpec((1, D), lambda i: (0, 0))
    return pl.pallas_call(
        _oproj_ln_kernel,
        grid=(N_ROW_TILES,),
        in_specs=[
            pl.BlockSpec((TM, D), _PROMPT_TILE),
            pl.BlockSpec((TM, D), lambda i: (0, 0)),
            pl.BlockSpec((D, D), lambda i: (0, 0)),
            vec,
            pl.BlockSpec((TM, D), lambda i: (i, 0)),
            vec,
            vec,
        ],
        out_specs=[pl.BlockSpec((TM, D), lambda i: (i, 0)),
                   pl.BlockSpec((TM * ROW_TILE, LANES), lambda i: (i, 0))],
        out_shape=[jax.ShapeDtypeStruct((T_ALL, D), F32),
                   jax.ShapeDtypeStruct((T_ALL * ROW_TILE, LANES), U32)],
        compiler_params=_cparams(("parallel",)),
        name="oproj_ln",
    )(o_p, o_tail, w_bf, bo, h, g, b)


def _router_kernel(h_ref, wr_ref, rb_ref, eidx_ref, wsel_ref, rank_ref, cnt_ref, carry_ref):
    i = pl.program_id(0)

    @pl.when(i == 0)
    def _():
        carry_ref[...] = jnp.zeros_like(carry_ref)

    h = h_ref[...]
    w = wr_ref[...]
    h_hi = h.astype(BF16)
    h_lo = (h - h_hi.astype(F32)).astype(BF16)
    w_hi = w.astype(BF16)
    w_lo = (w - w_hi.astype(F32)).astype(BF16)
    logits = (jnp.dot(h_hi, w_hi, preferred_element_type=F32)
              + (jnp.dot(h_hi, w_lo, preferred_element_type=F32)
                 + jnp.dot(h_lo, w_hi, preferred_element_type=F32)))
    scores = _sigmoid(logits.T[0:NE, :])
    biased = scores + rb_ref[...]
    ninf = -jnp.inf
    sub = lax.broadcasted_iota(I32, (PER_GRP, TM), 0).astype(F32)
    sc_g = [scores[g * PER_GRP:(g + 1) * PER_GRP, :] for g in range(NEG)]
    b_g = [biased[g * PER_GRP:(g + 1) * PER_GRP, :] for g in range(NEG)]
    e_g = [sub + float(g * PER_GRP) for g in range(NEG)]

    def smax(x):
        return jnp.max(x, axis=0, keepdims=True)

    def smin(x):
        return jnp.min(x, axis=0, keepdims=True)

    gs = []
    for g in range(NEG):
        m1 = smax(b_g[g])
        i1 = smin(jnp.where(b_g[g] == m1, sub, float(PER_GRP)))
        m2 = smax(jnp.where(sub == i1, ninf, b_g[g]))
        gs.append(m1 + m2)

    work = []
    for g in range(NEG):
        beaten = jnp.zeros((1, TM), F32)
        for o in range(NEG):
            if o != g:
                wins = (gs[o] >= gs[g]) if o < g else (gs[o] > gs[g])
                beaten = beaten + jnp.where(wins, 1.0, 0.0)
        work.append(jnp.where(beaten < float(TOPG), b_g[g], ninf))

    idx_rows, w_rows = [], []
    onehot = [jnp.zeros((PER_GRP, TM), F32) for _ in range(NEG)]
    for _ in range(TOPK):
        m = smax(work[0])
        for g in range(1, NEG):
            m = jnp.maximum(m, smax(work[g]))
        ik = smin(jnp.where(work[0] == m, e_g[0], float(NE)))
        for g in range(1, NEG):
            ik = jnp.minimum(ik, smin(jnp.where(work[g] == m, e_g[g], float(NE))))
        wk = jnp.zeros((1, TM), F32)
        for g in range(NEG):
            hit = e_g[g] == ik
            wk = wk + jnp.sum(jnp.where(hit, sc_g[g], 0.0), axis=0, keepdims=True)
            onehot[g] = jnp.where(hit, 1.0, onehot[g])
            work[g] = jnp.where(hit, ninf, work[g])
        idx_rows.append(ik)
        w_rows.append(wk)
    wsum = w_rows[0]
    for wk in w_rows[1:]:
        wsum = wsum + wk

    rr = lax.broadcasted_iota(I32, (TM, TM), 0)
    cc = lax.broadcasted_iota(I32, (TM, TM), 1)
    tri = jnp.where(rr < cc, 1.0, 0.0).astype(BF16)
    oh = jnp.concatenate(onehot, axis=0)
    prefix = jnp.dot(oh.astype(BF16), tri, preferred_element_type=F32) + carry_ref[...]
    carry_ref[...] = carry_ref[...] + jnp.sum(oh, axis=1, keepdims=True)
    cnt_ref[...] = carry_ref[...]

    rank_rows = []
    for k in range(TOPK):
        rk = jnp.zeros((1, TM), F32)
        for g in range(NEG):
            pg = prefix[g * PER_GRP:(g + 1) * PER_GRP, :]
            rk = rk + jnp.sum(jnp.where(e_g[g] == idx_rows[k], pg, 0.0), axis=0, keepdims=True)
        rank_rows.append(rk)
    eidx_ref[...] = jnp.concatenate(idx_rows, axis=0).astype(I32)
    rank_ref[...] = jnp.concatenate(rank_rows, axis=0).astype(I32)
    w_t = jnp.concatenate([wk / wsum * ROUTED_SCALE for wk in w_rows]
                          + [jnp.zeros((LANES - TOPK, TM), F32)], axis=0)
    wsel_ref[...] = w_t.T[:, 0:TOPK]


def _router_call(h, w_router, router_bias):
    tk = pl.BlockSpec((TM, TOPK), lambda i: (i, 0))
    kt = pl.BlockSpec((TOPK, TM), lambda i: (0, i))
    return pl.pallas_call(
        _router_kernel,
        grid=(T_ALL // TM,),
        in_specs=[
            pl.BlockSpec((TM, D), lambda i: (i, 0)),
            pl.BlockSpec((D, LANES), lambda i: (0, 0)),
            pl.BlockSpec((NE, 1), lambda i: (0, 0)),
        ],
        out_specs=[kt, tk, kt, pl.BlockSpec((NE, 1), lambda i: (0, 0))],
        out_shape=[
            jax.ShapeDtypeStruct((TOPK, T_ALL), I32),
            jax.ShapeDtypeStruct((T_ALL, TOPK), F32),
            jax.ShapeDtypeStruct((TOPK, T_ALL), I32),
            jax.ShapeDtypeStruct((NE, 1), F32),
        ],
        scratch_shapes=[pltpu.VMEM((NE, 1), F32)],
        compiler_params=_cparams(("arbitrary",)),
        name="router",
    )(h, jnp.pad(w_router, ((0, 0), (0, LANES - NE))), router_bias[:, None])


def _tiles_of(cnt):
    return (cnt + TM_E - 1) // TM_E


def _plan_kernel(cnt_ref, eidx_ref, rank_ref, pos_ref, te_ref, gi_ref, ne_ref, na_ref, off_s):
    def offsets(e, run):
        off_s[e] = run
        return run + _tiles_of(cnt_ref[e]) * TM_E

    total = lax.fori_loop(0, NE, offsets, 0)
    na = total // TM_E
    na_ref[0] = na

    def idle(t, carry):
        te_ref[t] = NE - 1
        gi_ref[t] = 0
        ne_ref[t] = -1
        return carry

    lax.fori_loop(na, NT_E, idle, 0)

    def forward(e, ordinal):
        t0 = off_s[e] // TM_E
        nt = _tiles_of(cnt_ref[e])

        def mark(t, carry):
            te_ref[t] = e
            gi_ref[t] = ordinal
            return carry

        lax.fori_loop(t0, t0 + nt, mark, 0)
        return ordinal + jnp.where(nt > 0, 1, 0)

    lax.fori_loop(0, NE, forward, 0)

    def backward(i, nxt):
        e = NE - 1 - i
        t0 = off_s[e] // TM_E
        nt = _tiles_of(cnt_ref[e])

        def mark(t, carry):
            ne_ref[t] = nxt
            return carry

        lax.fori_loop(t0, t0 + nt, mark, 0)
        return jnp.where(nt > 0, e, nxt)

    lax.fori_loop(0, NE, backward, -1)

    eidx = eidx_ref[...]
    pos = rank_ref[...]
    for e in range(NE):
        pos = pos + jnp.where(eidx == e, off_s[e], 0)
    pos_ref[...] = pos


def _plan_call(cnt, eidx_t, rank_t):
    smem = pl.BlockSpec(memory_space=pltpu.SMEM)
    vmem = pl.BlockSpec(memory_space=pltpu.VMEM)
    tiles = jax.ShapeDtypeStruct((NT_E,), I32)
    return pl.pallas_call(
        _plan_kernel,
        in_specs=[smem, vmem, vmem],
        out_specs=[vmem, smem, smem, smem, smem, smem],
        out_shape=[jax.ShapeDtypeStruct((TOPK, T_ALL), I32), tiles, tiles, tiles,
                   jax.ShapeDtypeStruct((1,), I32), jax.ShapeDtypeStruct((NE,), I32)],
        name="moe_plan",
    )(cnt, eidx_t, rank_t)


BLK_PER_TILE = TM // BLK


def _final_copies(step, obuf, yp_ref, ys_ref, sem):
    out = []
    slot = step % 2
    for m in range(BLK_PER_TILE):
        g = step * BLK_PER_TILE + m
        b = g // NBLK
        j = g % NBLK
        src = obuf.at[slot, pl.ds(m * BLK, BLK)]
        r0 = pl.multiple_of(jnp.maximum(b * SEQ + (j - 1) * BLK, 0), BLK)
        out.append((jnp.logical_and(g < NB * NBLK, j >= 1),
                    pltpu.make_async_copy(src, yp_ref.at[pl.ds(r0, BLK)], sem.at[slot])))
        out.append((g == NB * NBLK, pltpu.make_async_copy(src, ys_ref, sem.at[slot])))
    return out


def _dispatch_kernel(cnt_ref, off_ref, na_ref, pos_ref, hp_ref, xs_ref, zero_ref, sem, zsem):
    i = pl.program_id(0)

    def issue(t, carry):
        src = hp_ref.at[pl.ds(pl.multiple_of(t * ROW_TILE, ROW_TILE), ROW_TILE)]
        for k in range(TOPK):
            p = pl.multiple_of(pos_ref[0, 0, k * TM + t], ROW_TILE)
            pltpu.make_async_copy(src, xs_ref.at[pl.ds(p, ROW_TILE)], sem).start(priority=k % 2)
        return carry

    lax.fori_loop(0, TM, issue, 0)

    @pl.when(i == 0)
    def _():
        zero_ref[...] = jnp.zeros_like(zero_ref)

        def rows_copy(first_row, n_rows):
            r0 = pl.multiple_of(first_row * ROW_TILE, ROW_TILE)
            return pltpu.make_async_copy(zero_ref.at[pl.ds(0, n_rows * ROW_TILE)],
                                         xs_ref.at[pl.ds(r0, n_rows * ROW_TILE)], zsem)

        def row_copy(r):
            return rows_copy(r, 1)

        def group_copy(q):
            return rows_copy(q * 8, 8)

        def tile_copy(t):
            return rows_copy(t * TM_E, TM_E)

        def run(copy, lo, hi):
            def start(x, carry):
                copy(x).start()
                return carry

            def wait(x, carry):
                copy(x).wait()
                return carry

            lax.fori_loop(lo, hi, start, 0)
            lax.fori_loop(lo, hi, wait, 0)

        def per_expert(e, carry):
            lo = off_ref[e] + cnt_ref[e]
            hi = off_ref[e] + _tiles_of(cnt_ref[e]) * TM_E
            mid = jnp.minimum(((lo + 7) // 8) * 8, hi)
            run(row_copy, lo, mid)
            run(group_copy, mid // 8, hi // 8)
            return carry

        lax.fori_loop(0, NE, per_expert, 0)
        run(tile_copy, na_ref[0], NT_E)

    for _ in range(TOPK):
        pltpu.make_async_copy(hp_ref, xs_ref.at[pl.ds(0, TM * ROW_TILE)], sem).wait()


def _dispatch_call(cnt, off, nact, pos_tiles, hp):
    return pl.pallas_call(
        _dispatch_kernel,
        grid_spec=pltpu.PrefetchScalarGridSpec(
            num_scalar_prefetch=3,
            grid=(N_ROW_TILES,),
            in_specs=[
                pl.BlockSpec((1, 1, TM * TOPK), lambda i, c, o, n: (i, 0, 0), memory_space=pltpu.SMEM),
                pl.BlockSpec((TM * ROW_TILE, LANES), lambda i, c, o, n: (i, 0)),
            ],
            out_specs=pl.BlockSpec(memory_space=pl.ANY),
            scratch_shapes=[
                pltpu.VMEM((TM_E * ROW_TILE, LANES), U32),
                pltpu.SemaphoreType.DMA(()),
                pltpu.SemaphoreType.DMA(()),
            ],
        ),
        out_shape=jax.ShapeDtypeStruct((R_ROWS * ROW_TILE, LANES), U32),
        compiler_params=_cparams(("arbitrary",)),
        name="moe_dispatch",
    )(cnt, off, nact, pos_tiles, hp)


def _experts_kernel(layer, te_ref, gi_ref, ne_ref, na_ref, xs_ref, wg_hbm, wu_hbm, wd_hbm, ys_ref,
                    wgf, wuf, wdf, wg_s, wu_s, wd_s, wsem):
    i = pl.program_id(0)
    na = na_ref[0]

    def weight_copies(e, slot):
        return (pltpu.make_async_copy(wg_hbm.at[layer, e], wgf.at[slot], wsem.at[0]),
                pltpu.make_async_copy(wu_hbm.at[layer, e], wuf.at[slot], wsem.at[1]),
                pltpu.make_async_copy(wd_hbm.at[layer, e], wdf.at[slot], wsem.at[2]))

    @pl.when(i == 0)
    def _():
        for cp in weight_copies(te_ref[0], 0):
            cp.start()

    ic = jnp.minimum(i, na - 1)
    first = jnp.logical_or(i == 0, te_ref[ic] != te_ref[jnp.maximum(ic - 1, 0)])

    @pl.when(jnp.logical_and(i < na, first))
    def _():
        slot = gi_ref[ic] % 2
        for cp in weight_copies(te_ref[ic], slot):
            cp.wait()
        wg_s[...] = wgf[slot].astype(BF16)
        wu_s[...] = wuf[slot].astype(BF16)
        wd_s[...] = wdf[slot].astype(BF16)
        nxt = ne_ref[ic]

        @pl.when(nxt >= 0)
        def _():
            for cp in weight_copies(nxt, 1 - slot):
                cp.start()

    @pl.when(i < na)
    def _():
        lo, hi = _unpack_pair(_load_tiles_as_rows(xs_ref, TM_E))
        lo = lo.astype(BF16)
        hi = hi.astype(BF16)
        hg = (jnp.dot(lo, wg_s[0:HALF, :], preferred_element_type=F32)
              + jnp.dot(hi, wg_s[HALF:D, :], preferred_element_type=F32))
        hu = (jnp.dot(lo, wu_s[0:HALF, :], preferred_element_type=F32)
              + jnp.dot(hi, wu_s[HALF:D, :], preferred_element_type=F32))
        act = (hg * _sigmoid(hg) * hu).astype(BF16)
        y = jnp.dot(act, wd_s[...], preferred_element_type=F32)
        _store_rows_as_tiles(ys_ref, _pack_pair(y[:, :HALF], y[:, HALF:]))

    @pl.when(i >= na)
    def _():
        ys_ref[...] = jnp.zeros_like(ys_ref)


def _experts_call(layer, te, gi, ne, nact, xs, w_gate, w_up, w_down):
    anyspec = pl.BlockSpec(memory_space=pl.ANY)
    return pl.pallas_call(
        functools.partial(_experts_kernel, layer),
        grid_spec=pltpu.PrefetchScalarGridSpec(
            num_scalar_prefetch=4,
            grid=(NT_E,),
            in_specs=[
                pl.BlockSpec((TM_E * ROW_TILE, LANES),
                             lambda i, te_r, gi_r, ne_r, na_r: (jnp.minimum(i, na_r[0] - 1), 0)),
                anyspec, anyspec, anyspec,
            ],
            out_specs=pl.BlockSpec((TM_E * ROW_TILE, LANES), lambda i, te_r, gi_r, ne_r, na_r: (i, 0)),
            scratch_shapes=[
                pltpu.VMEM((2, D, DE), F32),
                pltpu.VMEM((2, D, DE), F32),
                pltpu.VMEM((2, DE, D), F32),
                pltpu.VMEM((D, DE), BF16),
                pltpu.VMEM((D, DE), BF16),
                pltpu.VMEM((DE, D), BF16),
                pltpu.SemaphoreType.DMA((3,)),
            ],
        ),
        out_shape=jax.ShapeDtypeStruct((R_ROWS * ROW_TILE, LANES), U32),
        compiler_params=_cparams(("arbitrary",)),
        name="moe_experts",
    )(te, gi, ne, nact, xs, w_gate, w_up, w_down)


def _gather_kernel(final, pos_ref, posn_ref, w_ref, h_ref, ys_ref, wsg_ref, wsu_ref, wsd_ref,
                   g_ref, b_ref, *rest):
    i = pl.program_id(0)
    if final:
        yp_ref, ys_out_ref, gbuf, gsem, obuf, osem = rest

        def wait_step(step):
            for cond, cp in _final_copies(step, obuf, yp_ref, ys_out_ref, osem):
                @pl.when(cond)
                def _():
                    cp.wait()

        @pl.when(i >= 2)
        def _():
            wait_step(i - 2)
    else:
        out_ref, gbuf, gsem = rest

    def issue(idx_ref, slot):
        def body(t, carry):
            r0 = pl.multiple_of(t * ROW_TILE, ROW_TILE)
            for k in range(TOPK):
                p = pl.multiple_of(idx_ref[0, 0, k * TM + t], ROW_TILE)
                pltpu.make_async_copy(ys_ref.at[pl.ds(p, ROW_TILE)], gbuf.at[slot, k, pl.ds(r0, ROW_TILE)],
                                      gsem.at[slot]).start(priority=k % 2)
            return carry

        lax.fori_loop(0, TM, body, 0)

    @pl.when(i == 0)
    def _():
        issue(pos_ref, 0)

    @pl.when(i + 1 < N_ROW_TILES)
    def _():
        issue(posn_ref, (i + 1) % 2)

    h = h_ref[...]
    hb = h.astype(BF16)
    sg = jnp.dot(hb, wsg_ref[...], preferred_element_type=F32)
    su = jnp.dot(hb, wsu_ref[...], preferred_element_type=F32)
    act = (sg * _sigmoid(sg) * su).astype(BF16)
    y = ALPHA * h + jnp.dot(act, wsd_ref[...], preferred_element_type=F32)
    ylo = y[:, :HALF]
    yhi = y[:, HALF:]

    slot = i % 2
    for k in range(TOPK):
        pltpu.make_async_copy(ys_ref.at[pl.ds(0, TM * ROW_TILE)], gbuf.at[slot, k], gsem.at[slot]).wait()
    w = w_ref[...]
    for k in range(TOPK):
        lo, hi = _unpack_pair(_load_tiles_as_rows(gbuf.at[slot, k], TM))
        wk = w[:, k:k + 1]
        ylo = ylo + wk * lo
        yhi = yhi + wk * hi
    out = _layer_norm(jnp.concatenate([ylo, yhi], axis=1), g_ref[...], b_ref[...])
    if not final:
        out_ref[...] = out
        return
    obuf[i % 2] = out
    for cond, cp in _final_copies(i, obuf, yp_ref, ys_out_ref, osem):
        @pl.when(cond)
        def _():
            cp.start()

    @pl.when(i == N_ROW_TILES - 1)
    def _():
        wait_step(i - 1)
        wait_step(i)


def _gather_call(final, pos_tiles, wsel, h, ys, wsg, wsu, wsd, g, b):
    vec = pl.BlockSpec((1, D), lambda i: (0, 0))
    once = dict(pipeline_mode=pl.Buffered(1))
    anyspec = pl.BlockSpec(memory_space=pl.ANY)
    scratch = [pltpu.VMEM((2, TOPK, TM * ROW_TILE, LANES), U32), pltpu.SemaphoreType.DMA((2,))]
    if final:
        out_specs = [anyspec, anyspec]
        out_shape = [jax.ShapeDtypeStruct((NB * SEQ, D), F32), jax.ShapeDtypeStruct((DEC, D), F32)]
        scratch += [pltpu.VMEM((2, TM, D), F32), pltpu.SemaphoreType.DMA((2,))]
    else:
        out_specs = pl.BlockSpec((TM, D), lambda i: (i, 0))
        out_shape = jax.ShapeDtypeStruct((T_ALL, D), F32)
    smem_tile = lambda f: pl.BlockSpec((1, 1, TM * TOPK), f, memory_space=pltpu.SMEM)
    return pl.pallas_call(
        functools.partial(_gather_kernel, final),
        grid=(N_ROW_TILES,),
        in_specs=[
            smem_tile(lambda i: (i, 0, 0)),
            smem_tile(lambda i: (jnp.minimum(i + 1, N_ROW_TILES - 1), 0, 0)),
            pl.BlockSpec((TM, TOPK), lambda i: (i, 0)),
            pl.BlockSpec((TM, D), lambda i: (i, 0)),
            anyspec,
            pl.BlockSpec((D, DE), lambda i: (0, 0), **once),
            pl.BlockSpec((D, DE), lambda i: (0, 0), **once),
            pl.BlockSpec((DE, D), lambda i: (0, 0), **once),
            vec,
            vec,
        ],
        out_specs=out_specs,
        out_shape=out_shape,
        scratch_shapes=scratch,
        compiler_params=_cparams(("arbitrary",)),
        name="moe_combine_final" if final else "moe_combine",
    )(pos_tiles, pos_tiles, wsel, h, ys, wsg, wsu, wsd, g, b)


def _moe_layer(layer, final, h, hp, w_router, router_bias, w_exp_gate, w_exp_up, w_exp_down,
               w_sh_gate, w_sh_up, w_sh_down, ln_g, ln_b):
    eidx_t, wsel, rank_t, counts = _router_call(h, w_router[layer], router_bias[layer])
    cnt = counts[:, 0].astype(I32)
    pos_t, te, gi, ne, nact, off = _plan_call(cnt, eidx_t, rank_t)
    pos_tiles = (pos_t * ROW_TILE).reshape(TOPK, N_ROW_TILES, TM).transpose(1, 0, 2).reshape(
        N_ROW_TILES, 1, TOPK * TM)
    xs = _dispatch_call(cnt, off, nact, pos_tiles, hp)
    ys = _experts_call(layer, te, gi, ne, nact, xs, w_exp_gate, w_exp_up, w_exp_down)
    return _gather_call(
        final, pos_tiles, wsel, h, ys,
        w_sh_gate[layer].astype(BF16), w_sh_up[layer].astype(BF16), w_sh_down[layer].astype(BF16),
        ln_g[layer][None, :], ln_b[layer][None, :])


def _ssm_prep_kernel(lr_ref, li_ref, ldt_ref, br_ref, bi_ref, abr_ref, abi_ref, bbr_ref, bbi_ref):
    lr = lr_ref[...]
    li = li_ref[...]
    dt = jnp.exp(ldt_ref[...])
    mag = jnp.exp(lr * dt)
    ab_re = mag * jnp.cos(li * dt)
    ab_im = mag * jnp.sin(li * dt)
    den = lr * lr + li * li
    nr = ab_re - 1.0
    ni = ab_im
    cr = (nr * lr + ni * li) / den
    ci = (ni * lr - nr * li) / den
    br = br_ref[...]
    bi = bi_ref[...]
    abr_ref[...] = ab_re
    abi_ref[...] = ab_im
    bbr_ref[...] = cr * br - ci * bi
    bbi_ref[...] = cr * bi + ci * br


def _ssm_prep_call(lam_re, lam_im, log_dt, b_re, b_im):
    wide = (NG, GC * NS)
    lr = jnp.tile(lam_re, (1, GC))
    li = jnp.tile(lam_im, (1, GC))
    ldt = jnp.broadcast_to(log_dt[:, None], wide)
    br = jnp.transpose(b_re, (0, 2, 1)).reshape(wide)
    bi = jnp.transpose(b_im, (0, 2, 1)).reshape(wide)
    sds = jax.ShapeDtypeStruct(wide, F32)
    return pl.pallas_call(
        _ssm_prep_kernel, out_shape=[sds, sds, sds, sds], name="ssm_prep",
        compiler_params=pltpu.CompilerParams(vmem_limit_bytes=VMEM_LIMIT),
    )(lr, li, ldt, br, bi)


def _cmul(ar, ai, xr, xi):
    return ar * xr - ai * xi, ar * xi + ai * xr


NSEG = 8
SEGL = LP // NSEG
STEP_UNROLL = 4
MOVE_UNROLL = 8
assert LP == NSEG * SEGL and SEGL % STEP_UNROLL == 0 and SEGL % MOVE_UNROLL == 0 and PAD_FRONT < SEGL
assert STEP_UNROLL % 2 == 0


def _ssm_p_kernel(u_ref, wb_ref, wc_ref, a_ref, d_ref, z_ref, st_ref, u_scr, s_scr, sb_scr, y_scr):
    def interleave(it, carry):
        for q in range(MOVE_UNROLL):
            t = it * MOVE_UNROLL + q
            u_scr[pl.ds(pl.multiple_of(t * NSEG, NSEG), NSEG), :] = u_ref[pl.ds(t, NSEG, stride=SEGL), :]
        return carry

    lax.fori_loop(0, SEGL // MOVE_UNROLL, interleave, 0)
    row = lax.broadcasted_iota(I32, (LP, LANES), 0)
    is_pad = jnp.logical_and(row % NSEG == 0, row // NSEG < PAD_FRONT)
    u = jnp.where(is_pad, 0.0, u_scr[...])
    s_scr[...] = jnp.dot(u.astype(BF16), wb_ref[0], preferred_element_type=F32)

    ar = a_ref[0, 0:1, :]
    ai = a_ref[0, 1:2, :]

    def group(t):
        r0 = pl.multiple_of(t * NSEG, NSEG)
        return pl.ds(r0, NSEG)

    def local_scan(it, carry):
        sr, si = carry
        for q in range(STEP_UNROLL):
            g = group(it * STEP_UNROLL + q)
            tr, ti = _cmul(ar, ai, sr, si)
            sr = tr + s_scr[g, 0:CH_STATE]
            si = ti + s_scr[g, CH_STATE:]
            s_scr[g, 0:CH_STATE] = sr
            s_scr[g, CH_STATE:] = si
        return sr, si

    zero8 = jnp.zeros((NSEG, CH_STATE), F32)
    er, ei = lax.fori_loop(0, SEGL // STEP_UNROLL, local_scan, (zero8, zero8))

    pr, pi = ar, ai
    acc = None
    bits = SEGL
    while bits:
        if bits & 1:
            acc = (pr, pi) if acc is None else _cmul(pr, pi, *acc)
        bits >>= 1
        if bits:
            pr, pi = _cmul(pr, pi, pr, pi)
    alr, ali = acc
    cr = jnp.zeros((1, CH_STATE), F32)
    ci = jnp.zeros((1, CH_STATE), F32)
    ins_r, ins_i = [], []
    for j in range(NSEG):
        ins_r.append(cr)
        ins_i.append(ci)
        tr, ti = _cmul(alr, ali, cr, ci)
        cr = tr + er[j:j + 1, :]
        ci = ti + ei[j:j + 1, :]
    st_ref[0, 0, :, 0:CH_STATE] = cr
    st_ref[0, 0, :, CH_STATE:] = ci

    def fixup(it, carry):
        dr, di = carry
        for q in range(0, STEP_UNROLL, 2):
            t0 = it * STEP_UNROLL + q
            rows_r, rows_i = [], []
            for t in (t0, t0 + 1):
                g = group(t)
                dr, di = _cmul(ar, ai, dr, di)
                rows_r.append(s_scr[g, 0:CH_STATE] + dr)
                rows_i.append(s_scr[g, CH_STATE:] + di)
            pair = pl.ds(pl.multiple_of(t0 * NSEG, 2 * NSEG), 2 * NSEG)
            sb_scr[pair, 0:CH_STATE] = jnp.concatenate(rows_r, axis=0).astype(BF16)
            sb_scr[pair, CH_STATE:] = jnp.concatenate(rows_i, axis=0).astype(BF16)
        return dr, di

    lax.fori_loop(0, SEGL // STEP_UNROLL, fixup,
                  (jnp.concatenate(ins_r, axis=0), jnp.concatenate(ins_i, axis=0)))

    y_scr[...] = jnp.dot(sb_scr[...], wc_ref[0], preferred_element_type=F32) + d_ref[0] * u

    def deinterleave(it, carry):
        for q in range(MOVE_UNROLL):
            t = it * MOVE_UNROLL + q
            u_scr[pl.ds(t, NSEG, stride=SEGL), :] = y_scr[pl.ds(pl.multiple_of(t * NSEG, NSEG), NSEG), :]
        return carry

    lax.fori_loop(0, SEGL // MOVE_UNROLL, deinterleave, 0)
    z_ref[...] = jax.nn.gelu(u_scr[...]).astype(BF16)


def _ssm_p_call(h, wb_bf, wc_bf, a_tab, d_tab):
    return pl.pallas_call(
        _ssm_p_kernel,
        grid=(NB, NCHUNK),
        in_specs=[
            pl.BlockSpec((LP, LANES), lambda b, k: (b, k)),
            pl.BlockSpec((1, LANES, 2 * CH_STATE), lambda b, k: (k, 0, 0)),
            pl.BlockSpec((1, 2 * CH_STATE, LANES), lambda b, k: (k, 0, 0)),
            pl.BlockSpec((1, 2, CH_STATE), lambda b, k: (k, 0, 0)),
            pl.BlockSpec((1, 1, LANES), lambda b, k: (k, 0, 0)),
        ],
        out_specs=[
            pl.BlockSpec((LP, LANES), lambda b, k: (b, k)),
            pl.BlockSpec((1, 1, 1, 2 * CH_STATE), lambda b, k: (b, k, 0, 0)),
        ],
        out_shape=[
            jax.ShapeDtypeStruct((T_PROMPT, D), BF16),
            jax.ShapeDtypeStruct((NB, NCHUNK, 1, 2 * CH_STATE), F32),
        ],
        scratch_shapes=[pltpu.VMEM((LP, LANES), F32), pltpu.VMEM((LP, 2 * CH_STATE), F32),
                        pltpu.VMEM((LP, 2 * CH_STATE), BF16), pltpu.VMEM((LP, LANES), F32)],
        compiler_params=_cparams(("parallel", "parallel")),
        name="ssm_prompt",
    )(h, wb_bf, wc_bf, a_tab, d_tab)


def _ssm_s_kernel(u_ref, sr_ref, si_ref, wb_ref, wc_ref, a_ref, d_ref, z_ref, nr_ref, ni_ref):
    u = u_ref[...]
    bu = jnp.dot(u, wb_ref[0], preferred_element_type=F32, precision=lax.Precision.HIGHEST)
    ar = a_ref[0, 0:1, :]
    ai = a_ref[0, 1:2, :]
    tr, ti = _cmul(ar, ai, sr_ref[...], si_ref[...])
    nr = tr + bu[:, 0:CH_STATE]
    ni = ti + bu[:, CH_STATE:]
    nr_ref[...] = nr
    ni_ref[...] = ni
    s = jnp.concatenate([nr, ni], axis=1).astype(BF16)
    y = jnp.dot(s, wc_ref[0], preferred_element_type=F32) + d_ref[0] * u
    z_ref[...] = jax.nn.gelu(y).astype(BF16)


def _ssm_s_call(h, s0r, s0i, wb_f32, wc_bf, a_tab, d_tab):
    st = pl.BlockSpec((DEC, CH_STATE), lambda k: (0, k))
    return pl.pallas_call(
        _ssm_s_kernel,
        grid=(NCHUNK,),
        in_specs=[
            pl.BlockSpec((DEC, LANES), lambda k: (T_PROMPT // DEC, k)),
            st,
            st,
            pl.BlockSpec((1, LANES, 2 * CH_STATE), lambda k: (k, 0, 0)),
            pl.BlockSpec((1, 2 * CH_STATE, LANES), lambda k: (k, 0, 0)),
            pl.BlockSpec((1, 2, CH_STATE), lambda k: (k, 0, 0)),
            pl.BlockSpec((1, 1, LANES), lambda k: (k, 0, 0)),
        ],
        out_specs=[pl.BlockSpec((DEC, LANES), lambda k: (0, k)), st, st],
        out_shape=[
            jax.ShapeDtypeStruct((DEC, D), BF16),
            jax.ShapeDtypeStruct((DEC, NG * NS), F32),
            jax.ShapeDtypeStruct((DEC, NG * NS), F32),
        ],
        compiler_params=_cparams(("parallel",)),
        name="ssm_sample",
    )(h, s0r, s0i, wb_f32, wc_bf, a_tab, d_tab)


def _glu_ln_kernel(z_ref, zt_ref, w_ref, bg_ref, h_ref, g_ref, b_ref, out_ref, hp_ref):
    z = _prompt_or_tail(z_ref, zt_ref)
    acc = jnp.dot(z, w_ref[...], preferred_element_type=F32) + bg_ref[...]
    m = acc[:, :D] * _sigmoid(acc[:, D:])
    out = _layer_norm(ALPHA * h_ref[...] + m, g_ref[...], b_ref[...])
    out_ref[...] = out
    _store_rows_as_tiles(hp_ref, _pack_pair(out[:, :HALF], out[:, HALF:]))


def _glu_ln_call(z_p, z_tail, w_bf, bg, h, g, b):
    vec = pl.BlockSpec((1, D), lambda i: (0, 0))
    return pl.pallas_call(
        _glu_ln_kernel,
        grid=(N_ROW_TILES,),
        in_specs=[
            pl.BlockSpec((TM, D), _PROMPT_TILE),
            pl.BlockSpec((TM, D), lambda i: (0, 0)),
            pl.BlockSpec((D, 2 * D), lambda i: (0, 0), pipeline_mode=pl.Buffered(1)),
            pl.BlockSpec((1, 2 * D), lambda i: (0, 0)),
            pl.BlockSpec((TM, D), lambda i: (i, 0)),
            vec,
            vec,
        ],
        out_specs=[pl.BlockSpec((TM, D), lambda i: (i, 0)),
                   pl.BlockSpec((TM * ROW_TILE, LANES), lambda i: (i, 0))],
        out_shape=[jax.ShapeDtypeStruct((T_ALL, D), F32),
                   jax.ShapeDtypeStruct((T_ALL * ROW_TILE, LANES), U32)],
        compiler_params=_cparams(("parallel",)),
        name="glu_ln",
    )(z_p, z_tail, w_bf, bg, h, g, b)


GROUPS_PER_CHUNK = LANES // GC


def _block_diag_in(t):
    t3 = t.reshape(NCHUNK, LANES, NS)
    same = (np.arange(LANES)[:, None] // GC) == (np.arange(CH_STATE)[None, :] // NS)
    return jnp.tile(t3, (1, 1, GROUPS_PER_CHUNK)) * jnp.asarray(same, t.dtype)


def _block_diag_out(t):
    t3 = jnp.swapaxes(t, 1, 2).reshape(NCHUNK, CH_STATE, GC)
    same = (np.arange(CH_STATE)[:, None] // NS) == (np.arange(LANES)[None, :] // GC)
    return jnp.tile(t3, (1, 1, GROUPS_PER_CHUNK)) * jnp.asarray(same, t.dtype)


def kernel(x_prompt, x_sample, cache_k, cache_v, state_ssm_re, state_ssm_im, meta_tokens, w_qkv, b_qkv, attn_sinks, w_o, b_o, ssm_lam_re, ssm_lam_im, ssm_log_dt, ssm_b_re, ssm_b_im, ssm_c_re, ssm_c_im, ssm_d, w_glu, b_glu, ln_mix_g, ln_mix_b, w_router, router_bias, w_exp_gate, w_exp_up, w_exp_down, w_sh_gate, w_sh_up, w_sh_down, ln_ffn_g, ln_ffn_b):
    moe_w = (w_router, router_bias, w_exp_gate, w_exp_up, w_exp_down, w_sh_gate, w_sh_up, w_sh_down,
             ln_ffn_g, ln_ffn_b)

    front = jnp.concatenate([jnp.zeros((PAD_FRONT, D), F32), meta_tokens], axis=0)
    pieces = []
    for b in range(NB):
        pieces += [front, x_prompt[b]]
    h = jnp.concatenate(pieces + [x_sample.reshape(DEC, D)], axis=0)

    rc, rs1, rs2 = _rope_tables()
    q, k, v = _qkv_call(h, w_qkv[0].astype(BF16), b_qkv[0][None, :], rc, rs1, rs2)
    o_p = _attn_p_call(attn_sinks[0], q, k, v)
    q3 = q[T_PROMPT:].astype(F32).reshape(DEC, NH, HD)
    o_s, ck_new, cv_new = _attn_s_call(
        attn_sinks[0][:, None], q3, k, v,
        cache_k[0].reshape(DEC, BLK, NKV * HD), cache_v[0].reshape(DEC, BLK, NKV * HD))
    o_tail = _tail_tile(o_p, o_s.reshape(DEC, D).astype(BF16))
    h, hp = _oproj_ln_call(o_p, o_tail, w_o[0].astype(BF16), b_o[0][None, :], h,
                           ln_mix_g[0][None, :], ln_mix_b[0][None, :])
    h = _moe_layer(0, False, h, hp, *moe_w)

    def last_window(t):
        rows = [t[(b + 1) * LP - BLK:(b + 1) * LP] for b in range(NB)]
        return jnp.stack(rows).reshape(NB, BLK, NKV, HD)

    kp = last_window(k)
    vp = last_window(v)

    ab_re, ab_im, bb_re, bb_im = _ssm_prep_call(
        ssm_lam_re[0], ssm_lam_im[0], ssm_log_dt[0], ssm_b_re[0], ssm_b_im[0])
    wb = jnp.concatenate([_block_diag_in(bb_re), _block_diag_in(bb_im)], axis=2)
    wc = jnp.concatenate([_block_diag_out(ssm_c_re[0]), -_block_diag_out(ssm_c_im[0])], axis=1)
    wc_bf = wc.astype(BF16)
    a_tab = jnp.stack([ab_re[:, :NS].reshape(NCHUNK, CH_STATE),
                       ab_im[:, :NS].reshape(NCHUNK, CH_STATE)], axis=1)
    d_tab = ssm_d[0].reshape(NCHUNK, 1, LANES)
    z_p, st_p = _ssm_p_call(h, wb.astype(BF16), wc_bf, a_tab, d_tab)
    z_s, sr_new, si_new = _ssm_s_call(
        h, state_ssm_re[0].reshape(DEC, NG * NS), state_ssm_im[0].reshape(DEC, NG * NS),
        wb, wc_bf, a_tab, d_tab)
    h, hp = _glu_ln_call(z_p, _tail_tile(z_p, z_s), w_glu[0].astype(BF16), b_glu[0][None, :], h,
                         ln_mix_g[1][None, :], ln_mix_b[1][None, :])
    y_prompt, y_sample = _moe_layer(1, True, h, hp, *moe_w)
    y_prompt = y_prompt.reshape(NB, SEQ, D)
    y_sample = y_sample.reshape(DEC, 1, D)
    st_p = st_p.reshape(NB, NCHUNK, 2, 8, NS)
    rp = st_p[:, :, 0].reshape(NB, NG, NS)
    ip = st_p[:, :, 1].reshape(NB, NG, NS)
    return (y_prompt, y_sample,
            kp[None], vp[None],
            ck_new.reshape(1, DEC, BLK, NKV, HD), cv_new.reshape(1, DEC, BLK, NKV, HD),
            rp[None], ip[None],
            sr_new.reshape(1, DEC, NG, NS), si_new.reshape(1, DEC, NG, NS))
```

```python
import functools
import math

import jax
import jax.numpy as jnp
import numpy as np
from jax import lax
from jax.experimental import pallas as pl
from jax.experimental.pallas import tpu as pltpu

F32 = jnp.float32
BF16 = jnp.bfloat16
I32 = jnp.int32
U32 = jnp.uint32

D = 2048
HALF = D // 2
NB = 4
N_META = 16
SEQ = 2048
L = N_META + SEQ
BLK = 128
PAD_FRONT = (-L) % BLK
LP = L + PAD_FRONT
NBLK = LP // BLK
T_PROMPT = NB * LP
DEC = 128
T_ALL = T_PROMPT + DEC
PAST_LEN = 8192
HD = 64
NH = 32
NKV = 4
QPK = NH // NKV
QKV = (NH + 2 * NKV) * HD
QK_COLS = (NH + NKV) * HD
ROT = HD // 4
ROT_HALF = ROT // 2
ROPE_THETA = 500000.0
NG = 128
GC = 16
NS = 64
NCHUNK = 16
CH_STATE = 8 * NS
NE = 64
TOPK = 8
NEG = 8
PER_GRP = NE // NEG
TOPG = 4
DE = 512
ROUTED_SCALE = 2.5
DEPTH = 2
ALPHA = (2 * DEPTH) ** 0.25
LN_EPS = 1e-5

V7X_VMEM_BYTES = 64 * 1024 * 1024
VMEM_LIMIT = 56 * 1024 * 1024
LANES = 128

TM = 384
N_ROW_TILES = T_ALL // TM
TAIL_START = (N_ROW_TILES - 1) * TM
TM_E = 256
N_PAIRS = T_ALL * TOPK
NT_E = -(-N_PAIRS // TM_E) + NE
R_ROWS = NT_E * TM_E
assert T_ALL % TM == 0 and TAIL_START <= T_PROMPT


def _cparams(sem):
    return pltpu.CompilerParams(dimension_semantics=sem, vmem_limit_bytes=VMEM_LIMIT)


def _sigmoid(x):
    return 1.0 / (1.0 + jnp.exp(-x))


def _layer_norm(y, g, b):
    mu = jnp.mean(y, axis=-1, keepdims=True)
    yc = y - mu
    var = jnp.mean(yc * yc, axis=-1, keepdims=True)
    return yc * lax.rsqrt(var + LN_EPS) * g + b


def _pack_pair(lo, hi):
    lo_b = lax.bitcast_convert_type(lo.astype(BF16).astype(F32), U32) >> 16
    hi_b = lax.bitcast_convert_type(hi.astype(BF16).astype(F32), U32) & jnp.uint32(0xFFFF0000)
    return lo_b | hi_b


def _unpack_pair(w):
    lo = lax.bitcast_convert_type(w << 16, F32)
    hi = lax.bitcast_convert_type(w & jnp.uint32(0xFFFF0000), F32)
    return lo, hi


ROW_TILE = HALF // LANES
assert ROW_TILE == 8


def _store_rows_as_tiles(ref, rows):
    n = rows.shape[0]
    for j in range(ROW_TILE):
        ref[pl.ds(j, n, stride=ROW_TILE), :] = rows[:, j * LANES:(j + 1) * LANES]


def _load_tiles_as_rows(ref, n):
    return jnp.concatenate([ref[pl.ds(j, n, stride=ROW_TILE), :] for j in range(ROW_TILE)], axis=1)


def _qkv_kernel(x_ref, w_ref, b_ref, c_ref, s1_ref, s2_ref, q_ref, k_ref, v_ref):
    xb = x_ref[...].astype(BF16)
    acc = jnp.dot(xb, w_ref[...], preferred_element_type=F32) + b_ref[...]
    c = c_ref[...]
    s1 = s1_ref[...]
    s2 = s2_ref[...]
    for j in range(QK_COLS // LANES):
        blk = acc[:, j * LANES:(j + 1) * LANES]
        r = (blk * c + pltpu.roll(blk, LANES - ROT_HALF, axis=1) * s1
             + pltpu.roll(blk, ROT_HALF, axis=1) * s2)
        if j < D // LANES:
            q_ref[:, j * LANES:(j + 1) * LANES] = (r * (1.0 / math.sqrt(HD))).astype(BF16)
        else:
            jj = j - D // LANES
            k_ref[:, jj * LANES:(jj + 1) * LANES] = r
    v_ref[...] = acc[:, QK_COLS:]


def _qkv_call(x, w_bf, b, rc, rs1, rs2):
    return pl.pallas_call(
        _qkv_kernel,
        grid=(T_ALL // TM,),
        in_specs=[
            pl.BlockSpec((TM, D), lambda i: (i, 0)),
            pl.BlockSpec((D, QKV), lambda i: (0, 0)),
            pl.BlockSpec((1, QKV), lambda i: (0, 0)),
            pl.BlockSpec((TM, LANES), lambda i: (i, 0)),
            pl.BlockSpec((TM, LANES), lambda i: (i, 0)),
            pl.BlockSpec((TM, LANES), lambda i: (i, 0)),
        ],
        out_specs=[
            pl.BlockSpec((TM, D), lambda i: (i, 0)),
            pl.BlockSpec((TM, NKV * HD), lambda i: (i, 0)),
            pl.BlockSpec((TM, NKV * HD), lambda i: (i, 0)),
        ],
        out_shape=[
            jax.ShapeDtypeStruct((T_ALL, D), BF16),
            jax.ShapeDtypeStruct((T_ALL, NKV * HD), F32),
            jax.ShapeDtypeStruct((T_ALL, NKV * HD), F32),
        ],
        compiler_params=_cparams(("parallel",)),
        name="qkv_rope",
    )(x, w_bf, b, rc, rs1, rs2)


def _rope_tables():
    pos_p = jnp.maximum(jnp.arange(LP, dtype=I32) - PAD_FRONT, 0)
    pos = jnp.concatenate([pos_p, jnp.full((8,), PAST_LEN, I32)]).astype(F32)
    inv_freq = ROPE_THETA ** (-jnp.arange(0, ROT, 2, dtype=F32) / ROT)
    ang = pos[:, None] * inv_freq[None, :]
    cos = jnp.cos(ang)
    sin = jnp.sin(ang)
    lane = np.arange(LANES) % HD
    freq = np.arange(ROT_HALF)[:, None]
    first = ((lane[None, :] == freq) & (lane[None, :] < ROT_HALF)).astype(np.float32)
    second = ((lane[None, :] - ROT_HALF == freq) & (lane[None, :] < ROT)).astype(np.float32)
    rest = (lane >= ROT).astype(np.float32)[None, :]
    place = functools.partial(jnp.dot, precision=lax.Precision.HIGHEST)
    c = place(cos, jnp.asarray(first + second)) + jnp.asarray(rest)
    s1 = place(sin, jnp.asarray(-first))
    s2 = place(sin, jnp.asarray(second))

    def all_rows(t):
        return jnp.concatenate([t[:LP]] * NB + [t[LP:]] * (DEC // 8), axis=0)

    return all_rows(c), all_rows(s1), all_rows(s2)


def _attn_p_kernel(sink_ref, q_ref, kp_ref, kc_ref, vp_ref, vc_ref, o_ref):
    j = pl.program_id(0) % NBLK
    keys = jnp.concatenate([kp_ref[...], kc_ref[...]], axis=0).astype(BF16)
    vals = jnp.concatenate([vp_ref[...], vc_ref[...]], axis=0).astype(BF16)
    r = lax.broadcasted_iota(I32, (BLK, 2 * BLK), 0)
    c = lax.broadcasted_iota(I32, (BLK, 2 * BLK), 1)
    dist = BLK + r - c
    kpos = (j - 1) * BLK - PAD_FRONT + c
    mask = (dist >= 0) & (dist <= BLK) & (kpos >= 0)
    for g in range(NKV):
        kg = keys[:, g * HD:(g + 1) * HD]
        vg = vals[:, g * HD:(g + 1) * HD]
        heads = range(g * QPK, (g + 1) * QPK)
        scores = [lax.dot_general(q_ref[:, h * HD:(h + 1) * HD], kg, (((1,), (1,)), ((), ())),
                                  preferred_element_type=F32) for h in heads]
        probs, rdens = [], []
        for h, s in zip(heads, scores):
            s = jnp.where(mask, s, -jnp.inf)
            sk = sink_ref[h]
            m = jnp.maximum(jnp.max(s, axis=1, keepdims=True), sk)
            p = jnp.exp(s - m)
            rdens.append(1.0 / (jnp.sum(p, axis=1, keepdims=True) + jnp.exp(sk - m)))
            probs.append(p.astype(BF16))
        for h, p, rden in zip(heads, probs, rdens):
            oh = jnp.dot(p, vg, preferred_element_type=F32) * rden
            o_ref[:, h * HD:(h + 1) * HD] = oh.astype(BF16)


def _attn_p_call(sinks, q, k, v):
    prev = lambda i: (jnp.where(i % NBLK == 0, i, i - 1), 0)
    cur = lambda i: (i, 0)
    return pl.pallas_call(
        _attn_p_kernel,
        grid=(NB * NBLK,),
        in_specs=[
            pl.BlockSpec(memory_space=pltpu.SMEM),
            pl.BlockSpec((BLK, D), cur),
            pl.BlockSpec((BLK, NKV * HD), prev),
            pl.BlockSpec((BLK, NKV * HD), cur),
            pl.BlockSpec((BLK, NKV * HD), prev),
            pl.BlockSpec((BLK, NKV * HD), cur),
        ],
        out_specs=pl.BlockSpec((BLK, D), cur),
        out_shape=jax.ShapeDtypeStruct((T_PROMPT, D), BF16),
        compiler_params=_cparams(("parallel",)),
        name="attn_prompt",
    )(sinks, q, k, k, v, v)


SEQ_PER_STEP = 16
SEQ_UNROLL = 4


def _attn_s_kernel(sink_ref, q_ref, kn_ref, vn_ref, ck_ref, cv_ref, o_ref, cko_ref, cvo_ref):
    row = lax.broadcasted_iota(I32, (BLK, NKV * HD), 0)
    hrow = lax.broadcasted_iota(I32, (NH, NKV * HD), 0) // QPK
    hlane = lax.broadcasted_iota(I32, (NH, NKV * HD), 1) // HD
    own = hrow == hlane
    sk = sink_ref[...]

    def score_stage(s):
        kn = kn_ref[pl.ds(s, 1), :]
        knr = kn.astype(BF16).astype(F32)
        q = q_ref[s].astype(BF16)
        qe = jnp.where(own, jnp.concatenate([q] * NKV, axis=1), jnp.zeros((), BF16))
        sc = lax.dot_general(qe, ck_ref[s].astype(BF16), (((1,), (1,)), ((), ())),
                             preferred_element_type=F32)
        sn = jnp.sum(qe.astype(F32) * knr, axis=1, keepdims=True)
        return sc, sn

    def softmax_stage(sc, sn):
        m = jnp.maximum(jnp.maximum(jnp.max(sc, axis=1, keepdims=True), sn), sk)
        p = jnp.exp(sc - m)
        pn = jnp.exp(sn - m)
        rden = 1.0 / (jnp.sum(p, axis=1, keepdims=True) + pn + jnp.exp(sk - m))
        return p.astype(BF16), pn.astype(BF16).astype(F32), rden

    def value_stage(s, p, pn, rden):
        vn = vn_ref[pl.ds(s, 1), :]
        vnr = vn.astype(BF16).astype(F32)
        of = jnp.dot(p, cv_ref[s].astype(BF16), preferred_element_type=F32)
        of = jnp.where(own, of + pn * vnr, 0.0)
        og = of[:, 0:HD]
        for g in range(1, NKV):
            og = og + of[:, g * HD:(g + 1) * HD]
        o_ref[s] = og * rden
        cko_ref[s] = jnp.where(row == BLK - 1, kn_ref[pl.ds(s, 1), :], pltpu.roll(ck_ref[s], BLK - 1, axis=0))
        cvo_ref[s] = jnp.where(row == BLK - 1, vn, pltpu.roll(cv_ref[s], BLK - 1, axis=0))

    def body(it, carry):
        seqs = [it * SEQ_UNROLL + u for u in range(SEQ_UNROLL)]
        staged = [score_stage(s) for s in seqs]
        soft = [softmax_stage(*st) for st in staged]
        for s, sm in zip(seqs, soft):
            value_stage(s, *sm)
        return carry

    lax.fori_loop(0, SEQ_PER_STEP // SEQ_UNROLL, body, 0)


def _attn_s_call(sinks_col, q3, k, v, cache_k, cache_v):
    sp = SEQ_PER_STEP
    kv_off = T_PROMPT // sp
    return pl.pallas_call(
        _attn_s_kernel,
        grid=(DEC // sp,),
        in_specs=[
            pl.BlockSpec((NH, 1), lambda i: (0, 0)),
            pl.BlockSpec((sp, NH, HD), lambda i: (i, 0, 0)),
            pl.BlockSpec((sp, NKV * HD), lambda i: (kv_off + i, 0)),
            pl.BlockSpec((sp, NKV * HD), lambda i: (kv_off + i, 0)),
            pl.BlockSpec((sp, BLK, NKV * HD), lambda i: (i, 0, 0)),
            pl.BlockSpec((sp, BLK, NKV * HD), lambda i: (i, 0, 0)),
        ],
        out_specs=[
            pl.BlockSpec((sp, NH, HD), lambda i: (i, 0, 0)),
            pl.BlockSpec((sp, BLK, NKV * HD), lambda i: (i, 0, 0)),
            pl.BlockSpec((sp, BLK, NKV * HD), lambda i: (i, 0, 0)),
        ],
        out_shape=[
            jax.ShapeDtypeStruct((DEC, NH, HD), F32),
            jax.ShapeDtypeStruct((DEC, BLK, NKV * HD), F32),
            jax.ShapeDtypeStruct((DEC, BLK, NKV * HD), F32),
        ],
        compiler_params=_cparams(("parallel",)),
        name="attn_sample",
    )(sinks_col, q3, k, v, cache_k, cache_v)


def _prompt_or_tail(x_ref, tail_ref):
    return jnp.where(pl.program_id(0) == N_ROW_TILES - 1, tail_ref[...], x_ref[...])


def _tail_tile(x_prompt_rows, x_sample_rows):
    return jnp.concatenate([x_prompt_rows[TAIL_START:], x_sample_rows], axis=0)


_PROMPT_TILE = lambda i: (jnp.minimum(i, N_ROW_TILES - 2), 0)


def _oproj_ln_kernel(o_ref, ot_ref, w_ref, bo_ref, h_ref, g_ref, b_ref, out_ref, hp_ref):
    o = _prompt_or_tail(o_ref, ot_ref)
    m = jnp.dot(o, w_ref[...], preferred_element_type=F32) + bo_ref[...]
    out = _layer_norm(ALPHA * h_ref[...] + m, g_ref[...], b_ref[...])
    out_ref[...] = out
    _store_rows_as_tiles(hp_ref, _pack_pair(out[:, :HALF], out[:, HALF:]))


def _oproj_ln_call(o_p, o_tail, w_bf, bo, h, g, b):
    vec = pl.BlockSpec((1, D), lambda i: (0, 0))
    return pl.pallas_call(
        _oproj_ln_kernel,
        grid=(N_ROW_TILES,),
        in_specs=[
            pl.BlockSpec((TM, D), _PROMPT_TILE),
            pl.BlockSpec((TM, D), lambda i: (0, 0)),
            pl.BlockSpec((D, D), lambda i: (0, 0)),
            vec,
            pl.BlockSpec((TM, D), lambda i: (i, 0)),
            vec,
            vec,
        ],
        out_specs=[pl.BlockSpec((TM, D), lambda i: (i, 0)),
                   pl.BlockSpec((TM * ROW_TILE, LANES), lambda i: (i, 0))],
        out_shape=[jax.ShapeDtypeStruct((T_ALL, D), F32),
                   jax.ShapeDtypeStruct((T_ALL * ROW_TILE, LANES), U32)],
        compiler_params=_cparams(("parallel",)),
        name="oproj_ln",
    )(o_p, o_tail, w_bf, bo, h, g, b)


def _router_kernel(h_ref, wr_ref, rb_ref, eidx_ref, wsel_ref, rank_ref, cnt_ref, carry_ref):
    i = pl.program_id(0)

    @pl.when(i == 0)
    def _():
        carry_ref[...] = jnp.zeros_like(carry_ref)

    h = h_ref[...]
    w = wr_ref[...]
    h_hi = h.astype(BF16)
    h_lo = (h - h_hi.astype(F32)).astype(BF16)
    w_hi = w.astype(BF16)
    w_lo = (w - w_hi.astype(F32)).astype(BF16)
    logits = (jnp.dot(h_hi, w_hi, preferred_element_type=F32)
              + (jnp.dot(h_hi, w_lo, preferred_element_type=F32)
                 + jnp.dot(h_lo, w_hi, preferred_element_type=F32)))
    scores = _sigmoid(logits.T[0:NE, :])
    biased = scores + rb_ref[...]
    ninf = -jnp.inf
    sub = lax.broadcasted_iota(I32, (PER_GRP, TM), 0).astype(F32)
    sc_g = [scores[g * PER_GRP:(g + 1) * PER_GRP, :] for g in range(NEG)]
    b_g = [biased[g * PER_GRP:(g + 1) * PER_GRP, :] for g in range(NEG)]
    e_g = [sub + float(g * PER_GRP) for g in range(NEG)]

    def smax(x):
        return jnp.max(x, axis=0, keepdims=True)

    def smin(x):
        return jnp.min(x, axis=0, keepdims=True)

    gs = []
    for g in range(NEG):
        m1 = smax(b_g[g])
        i1 = smin(jnp.where(b_g[g] == m1, sub, float(PER_GRP)))
        m2 = smax(jnp.where(sub == i1, ninf, b_g[g]))
        gs.append(m1 + m2)

    work = []
    for g in range(NEG):
        beaten = jnp.zeros((1, TM), F32)
        for o in range(NEG):
            if o != g:
                wins = (gs[o] >= gs[g]) if o < g else (gs[o] > gs[g])
                beaten = beaten + jnp.where(wins, 1.0, 0.0)
        work.append(jnp.where(beaten < float(TOPG), b_g[g], ninf))

    idx_rows, w_rows = [], []
    onehot = [jnp.zeros((PER_GRP, TM), F32) for _ in range(NEG)]
    for _ in range(TOPK):
        m = smax(work[0])
        for g in range(1, NEG):
            m = jnp.maximum(m, smax(work[g]))
        ik = smin(jnp.where(work[0] == m, e_g[0], float(NE)))
        for g in range(1, NEG):
            ik = jnp.minimum(ik, smin(jnp.where(work[g] == m, e_g[g], float(NE))))
        wk = jnp.zeros((1, TM), F32)
        for g in range(NEG):
            hit = e_g[g] == ik
            wk = wk + jnp.sum(jnp.where(hit, sc_g[g], 0.0), axis=0, keepdims=True)
            onehot[g] = jnp.where(hit, 1.0, onehot[g])
            work[g] = jnp.where(hit, ninf, work[g])
        idx_rows.append(ik)
        w_rows.append(wk)
    wsum = w_rows[0]
    for wk in w_rows[1:]:
        wsum = wsum + wk

    rr = lax.broadcasted_iota(I32, (TM, TM), 0)
    cc = lax.broadcasted_iota(I32, (TM, TM), 1)
    tri = jnp.where(rr < cc, 1.0, 0.0).astype(BF16)
    oh = jnp.concatenate(onehot, axis=0)
    prefix = jnp.dot(oh.astype(BF16), tri, preferred_element_type=F32) + carry_ref[...]
    carry_ref[...] = carry_ref[...] + jnp.sum(oh, axis=1, keepdims=True)
    cnt_ref[...] = carry_ref[...]

    rank_rows = []
    for k in range(TOPK):
        rk = jnp.zeros((1, TM), F32)
        for g in range(NEG):
            pg = prefix[g * PER_GRP:(g + 1) * PER_GRP, :]
            rk = rk + jnp.sum(jnp.where(e_g[g] == idx_rows[k], pg, 0.0), axis=0, keepdims=True)
        rank_rows.append(rk)
    eidx_ref[...] = jnp.concatenate(idx_rows, axis=0).astype(I32)
    rank_ref[...] = jnp.concatenate(rank_rows, axis=0).astype(I32)
    w_t = jnp.concatenate([wk / wsum * ROUTED_SCALE for wk in w_rows]
                          + [jnp.zeros((LANES - TOPK, TM), F32)], axis=0)
    wsel_ref[...] = w_t.T[:, 0:TOPK]


def _router_call(h, w_router, router_bias):
    tk = pl.BlockSpec((TM, TOPK), lambda i: (i, 0))
    kt = pl.BlockSpec((TOPK, TM), lambda i: (0, i))
    return pl.pallas_call(
        _router_kernel,
        grid=(T_ALL // TM,),
        in_specs=[
            pl.BlockSpec((TM, D), lambda i: (i, 0)),
            pl.BlockSpec((D, LANES), lambda i: (0, 0)),
            pl.BlockSpec((NE, 1), lambda i: (0, 0)),
        ],
        out_specs=[kt, tk, kt, pl.BlockSpec((NE, 1), lambda i: (0, 0))],
        out_shape=[
            jax.ShapeDtypeStruct((TOPK, T_ALL), I32),
            jax.ShapeDtypeStruct((T_ALL, TOPK), F32),
            jax.ShapeDtypeStruct((TOPK, T_ALL), I32),
            jax.ShapeDtypeStruct((NE, 1), F32),
        ],
        scratch_shapes=[pltpu.VMEM((NE, 1), F32)],
        compiler_params=_cparams(("arbitrary",)),
        name="router",
    )(h, jnp.pad(w_router, ((0, 0), (0, LANES - NE))), router_bias[:, None])


def _tiles_of(cnt):
    return (cnt + TM_E - 1) // TM_E


def _plan_kernel(cnt_ref, eidx_ref, rank_ref, pos_ref, te_ref, gi_ref, ne_ref, na_ref, off_s):
    def offsets(e, run):
        off_s[e] = run
        return run + _tiles_of(cnt_ref[e]) * TM_E

    total = lax.fori_loop(0, NE, offsets, 0)
    na = total // TM_E
    na_ref[0] = na

    def idle(t, carry):
        te_ref[t] = NE - 1
        gi_ref[t] = 0
        ne_ref[t] = -1
        return carry

    lax.fori_loop(na, NT_E, idle, 0)

    def forward(e, ordinal):
        t0 = off_s[e] // TM_E
        nt = _tiles_of(cnt_ref[e])

        def mark(t, carry):
            te_ref[t] = e
            gi_ref[t] = ordinal
            return carry

        lax.fori_loop(t0, t0 + nt, mark, 0)
        return ordinal + jnp.where(nt > 0, 1, 0)

    lax.fori_loop(0, NE, forward, 0)

    def backward(i, nxt):
        e = NE - 1 - i
        t0 = off_s[e] // TM_E
        nt = _tiles_of(cnt_ref[e])

        def mark(t, carry):
            ne_ref[t] = nxt
            return carry

        lax.fori_loop(t0, t0 + nt, mark, 0)
        return jnp.where(nt > 0, e, nxt)

    lax.fori_loop(0, NE, backward, -1)

    eidx = eidx_ref[...]
    pos = rank_ref[...]
    for e in range(NE):
        pos = pos + jnp.where(eidx == e, off_s[e], 0)
    pos_ref[...] = pos


def _plan_call(cnt, eidx_t, rank_t):
    smem = pl.BlockSpec(memory_space=pltpu.SMEM)
    vmem = pl.BlockSpec(memory_space=pltpu.VMEM)
    tiles = jax.ShapeDtypeStruct((NT_E,), I32)
    return pl.pallas_call(
        _plan_kernel,
        in_specs=[smem, vmem, vmem],
        out_specs=[vmem, smem, smem, smem, smem, smem],
        out_shape=[jax.ShapeDtypeStruct((TOPK, T_ALL), I32), tiles, tiles, tiles,
                   jax.ShapeDtypeStruct((1,), I32), jax.ShapeDtypeStruct((NE,), I32)],
        name="moe_plan",
    )(cnt, eidx_t, rank_t)


BLK_PER_TILE = TM // BLK


def _final_copies(step, obuf, yp_ref, ys_ref, sem):
    out = []
    slot = step % 2
    for m in range(BLK_PER_TILE):
        g = step * BLK_PER_TILE + m
        b = g // NBLK
        j = g % NBLK
        src = obuf.at[slot, pl.ds(m * BLK, BLK)]
        r0 = pl.multiple_of(jnp.maximum(b * SEQ + (j - 1) * BLK, 0), BLK)
        out.append((jnp.logical_and(g < NB * NBLK, j >= 1),
                    pltpu.make_async_copy(src, yp_ref.at[pl.ds(r0, BLK)], sem.at[slot])))
        out.append((g == NB * NBLK, pltpu.make_async_copy(src, ys_ref, sem.at[slot])))
    return out


def _dispatch_kernel(cnt_ref, off_ref, na_ref, pos_ref, hp_ref, xs_ref, zero_ref, sem, zsem):
    i = pl.program_id(0)

    def issue(t, carry):
        src = hp_ref.at[pl.ds(pl.multiple_of(t * ROW_TILE, ROW_TILE), ROW_TILE)]
        for k in range(TOPK):
            p = pl.multiple_of(pos_ref[0, 0, k * TM + t], ROW_TILE)
            pltpu.make_async_copy(src, xs_ref.at[pl.ds(p, ROW_TILE)], sem).start(priority=k % 2)
        return carry

    lax.fori_loop(0, TM, issue, 0)

    @pl.when(i == 0)
    def _():
        zero_ref[...] = jnp.zeros_like(zero_ref)

        def rows_copy(first_row, n_rows):
            r0 = pl.multiple_of(first_row * ROW_TILE, ROW_TILE)
            return pltpu.make_async_copy(zero_ref.at[pl.ds(0, n_rows * ROW_TILE)],
                                         xs_ref.at[pl.ds(r0, n_rows * ROW_TILE)], zsem)

        def row_copy(r):
            return rows_copy(r, 1)

        def group_copy(q):
            return rows_copy(q * 8, 8)

        def tile_copy(t):
            return rows_copy(t * TM_E, TM_E)

        def run(copy, lo, hi):
            def start(x, carry):
                copy(x).start()
                return carry

            def wait(x, carry):
                copy(x).wait()
                return carry

            lax.fori_loop(lo, hi, start, 0)
            lax.fori_loop(lo, hi, wait, 0)

        def per_expert(e, carry):
            lo = off_ref[e] + cnt_ref[e]
            hi = off_ref[e] + _tiles_of(cnt_ref[e]) * TM_E
            mid = jnp.minimum(((lo + 7) // 8) * 8, hi)
            run(row_copy, lo, mid)
            run(group_copy, mid // 8, hi // 8)
            return carry

        lax.fori_loop(0, NE, per_expert, 0)
        run(tile_copy, na_ref[0], NT_E)

    for _ in range(TOPK):
        pltpu.make_async_copy(hp_ref, xs_ref.at[pl.ds(0, TM * ROW_TILE)], sem).wait()


def _dispatch_call(cnt, off, nact, pos_tiles, hp):
    return pl.pallas_call(
        _dispatch_kernel,
        grid_spec=pltpu.PrefetchScalarGridSpec(
            num_scalar_prefetch=3,
            grid=(N_ROW_TILES,),
            in_specs=[
                pl.BlockSpec((1, 1, TM * TOPK), lambda i, c, o, n: (i, 0, 0), memory_space=pltpu.SMEM),
                pl.BlockSpec((TM * ROW_TILE, LANES), lambda i, c, o, n: (i, 0)),
            ],
            out_specs=pl.BlockSpec(memory_space=pl.ANY),
            scratch_shapes=[
                pltpu.VMEM((TM_E * ROW_TILE, LANES), U32),
                pltpu.SemaphoreType.DMA(()),
                pltpu.SemaphoreType.DMA(()),
            ],
        ),
        out_shape=jax.ShapeDtypeStruct((R_ROWS * ROW_TILE, LANES), U32),
        compiler_params=_cparams(("arbitrary",)),
        name="moe_dispatch",
    )(cnt, off, nact, pos_tiles, hp)


def _experts_kernel(layer, te_ref, gi_ref, ne_ref, na_ref, xs_ref, wg_hbm, wu_hbm, wd_hbm, ys_ref,
                    wgf, wuf, wdf, wg_s, wu_s, wd_s, wsem):
    i = pl.program_id(0)
    na = na_ref[0]

    def weight_copies(e, slot):
        return (pltpu.make_async_copy(wg_hbm.at[layer, e], wgf.at[slot], wsem.at[0]),
                pltpu.make_async_copy(wu_hbm.at[layer, e], wuf.at[slot], wsem.at[1]),
                pltpu.make_async_copy(wd_hbm.at[layer, e], wdf.at[slot], wsem.at[2]))

    @pl.when(i == 0)
    def _():
        for cp in weight_copies(te_ref[0], 0):
            cp.start()

    ic = jnp.minimum(i, na - 1)
    first = jnp.logical_or(i == 0, te_ref[ic] != te_ref[jnp.maximum(ic - 1, 0)])

    @pl.when(jnp.logical_and(i < na, first))
    def _():
        slot = gi_ref[ic] % 2
        for cp in weight_copies(te_ref[ic], slot):
            cp.wait()
        for static_slot in (0, 1):
            @pl.when(slot == static_slot)
            def _():
                wg_s[...] = wgf[static_slot].astype(BF16)
                wu_s[...] = wuf[static_slot].astype(BF16)
                wd_s[...] = wdf[static_slot].astype(BF16)
        nxt = ne_ref[ic]

        @pl.when(nxt >= 0)
        def _():
            for cp in weight_copies(nxt, 1 - slot):
                cp.start()

    @pl.when(i < na)
    def _():
        lo, hi = _unpack_pair(_load_tiles_as_rows(xs_ref, TM_E))
        lo = lo.astype(BF16)
        hi = hi.astype(BF16)
        hg = (jnp.dot(lo, wg_s[0:HALF, :], preferred_element_type=F32)
              + jnp.dot(hi, wg_s[HALF:D, :], preferred_element_type=F32))
        hu = (jnp.dot(lo, wu_s[0:HALF, :], preferred_element_type=F32)
              + jnp.dot(hi, wu_s[HALF:D, :], preferred_element_type=F32))
        act = (hg * _sigmoid(hg) * hu).astype(BF16)
        y = jnp.dot(act, wd_s[...], preferred_element_type=F32)
        _store_rows_as_tiles(ys_ref, _pack_pair(y[:, :HALF], y[:, HALF:]))

    @pl.when(i >= na)
    def _():
        ys_ref[...] = jnp.zeros_like(ys_ref)


def _experts_call(layer, te, gi, ne, nact, xs, w_gate, w_up, w_down):
    anyspec = pl.BlockSpec(memory_space=pl.ANY)
    return pl.pallas_call(
        functools.partial(_experts_kernel, layer),
        grid_spec=pltpu.PrefetchScalarGridSpec(
            num_scalar_prefetch=4,
            grid=(NT_E,),
            in_specs=[
                pl.BlockSpec((TM_E * ROW_TILE, LANES),
                             lambda i, te_r, gi_r, ne_r, na_r: (jnp.minimum(i, na_r[0] - 1), 0)),
                anyspec, anyspec, anyspec,
            ],
            out_specs=pl.BlockSpec((TM_E * ROW_TILE, LANES), lambda i, te_r, gi_r, ne_r, na_r: (i, 0)),
            scratch_shapes=[
                pltpu.VMEM((2, D, DE), F32),
                pltpu.VMEM((2, D, DE), F32),
                pltpu.VMEM((2, DE, D), F32),
                pltpu.VMEM((D, DE), BF16),
                pltpu.VMEM((D, DE), BF16),
                pltpu.VMEM((DE, D), BF16),
                pltpu.SemaphoreType.DMA((3,)),
            ],
        ),
        out_shape=jax.ShapeDtypeStruct((R_ROWS * ROW_TILE, LANES), U32),
        compiler_params=_cparams(("arbitrary",)),
        name="moe_experts",
    )(te, gi, ne, nact, xs, w_gate, w_up, w_down)


def _gather_kernel(final, pos_ref, posn_ref, w_ref, h_ref, ys_ref, wsg_ref, wsu_ref, wsd_ref,
                   g_ref, b_ref, *rest):
    i = pl.program_id(0)
    if final:
        yp_ref, ys_out_ref, gbuf, gsem, obuf, osem = rest

        def wait_step(step):
            for cond, cp in _final_copies(step, obuf, yp_ref, ys_out_ref, osem):
                @pl.when(cond)
                def _():
                    cp.wait()

        @pl.when(i >= 2)
        def _():
            wait_step(i - 2)
    else:
        out_ref, gbuf, gsem = rest

    def issue(idx_ref, slot):
        def body(t, carry):
            r0 = pl.multiple_of(t * ROW_TILE, ROW_TILE)
            for k in range(TOPK):
                p = pl.multiple_of(idx_ref[0, 0, k * TM + t], ROW_TILE)
                pltpu.make_async_copy(ys_ref.at[pl.ds(p, ROW_TILE)], gbuf.at[slot, k, pl.ds(r0, ROW_TILE)],
                                      gsem.at[slot]).start(priority=k % 2)
            return carry

        lax.fori_loop(0, TM, body, 0)

    @pl.when(i == 0)
    def _():
        issue(pos_ref, 0)

    @pl.when(i + 1 < N_ROW_TILES)
    def _():
        issue(posn_ref, (i + 1) % 2)

    h = h_ref[...]
    hb = h.astype(BF16)
    sg = jnp.dot(hb, wsg_ref[...], preferred_element_type=F32)
    su = jnp.dot(hb, wsu_ref[...], preferred_element_type=F32)
    act = (sg * _sigmoid(sg) * su).astype(BF16)
    y = ALPHA * h + jnp.dot(act, wsd_ref[...], preferred_element_type=F32)
    ylo = y[:, :HALF]
    yhi = y[:, HALF:]

    slot = i % 2
    for k in range(TOPK):
        pltpu.make_async_copy(ys_ref.at[pl.ds(0, TM * ROW_TILE)], gbuf.at[slot, k], gsem.at[slot]).wait()
    w = w_ref[...]
    for k in range(TOPK):
        lo, hi = _unpack_pair(_load_tiles_as_rows(gbuf.at[slot, k], TM))
        wk = w[:, k:k + 1]
        ylo = ylo + wk * lo
        yhi = yhi + wk * hi
    out = _layer_norm(jnp.concatenate([ylo, yhi], axis=1), g_ref[...], b_ref[...])
    if not final:
        out_ref[...] = out
        return
    obuf[i % 2] = out
    for cond, cp in _final_copies(i, obuf, yp_ref, ys_out_ref, osem):
        @pl.when(cond)
        def _():
            cp.start()

    @pl.when(i == N_ROW_TILES - 1)
    def _():
        wait_step(i - 1)
        wait_step(i)


def _gather_call(final, pos_tiles, wsel, h, ys, wsg, wsu, wsd, g, b):
    vec = pl.BlockSpec((1, D), lambda i: (0, 0))
    once = dict(pipeline_mode=pl.Buffered(1))
    anyspec = pl.BlockSpec(memory_space=pl.ANY)
    scratch = [pltpu.VMEM((2, TOPK, TM * ROW_TILE, LANES), U32), pltpu.SemaphoreType.DMA((2,))]
    if final:
        out_specs = [anyspec, anyspec]
        out_shape = [jax.ShapeDtypeStruct((NB * SEQ, D), F32), jax.ShapeDtypeStruct((DEC, D), F32)]
        scratch += [pltpu.VMEM((2, TM, D), F32), pltpu.SemaphoreType.DMA((2,))]
    else:
        out_specs = pl.BlockSpec((TM, D), lambda i: (i, 0))
        out_shape = jax.ShapeDtypeStruct((T_ALL, D), F32)
    smem_tile = lambda f: pl.BlockSpec((1, 1, TM * TOPK), f, memory_space=pltpu.SMEM)
    return pl.pallas_call(
        functools.partial(_gather_kernel, final),
        grid=(N_ROW_TILES,),
        in_specs=[
            smem_tile(lambda i: (i, 0, 0)),
            smem_tile(lambda i: (jnp.minimum(i + 1, N_ROW_TILES - 1), 0, 0)),
            pl.BlockSpec((TM, TOPK), lambda i: (i, 0)),
            pl.BlockSpec((TM, D), lambda i: (i, 0)),
            anyspec,
            pl.BlockSpec((D, DE), lambda i: (0, 0), **once),
            pl.BlockSpec((D, DE), lambda i: (0, 0), **once),
            pl.BlockSpec((DE, D), lambda i: (0, 0), **once),
            vec,
            vec,
        ],
        out_specs=out_specs,
        out_shape=out_shape,
        scratch_shapes=scratch,
        compiler_params=_cparams(("arbitrary",)),
        name="moe_combine_final" if final else "moe_combine",
    )(pos_tiles, pos_tiles, wsel, h, ys, wsg, wsu, wsd, g, b)


def _moe_layer(layer, final, h, hp, w_router, router_bias, w_exp_gate, w_exp_up, w_exp_down,
               w_sh_gate, w_sh_up, w_sh_down, ln_g, ln_b):
    eidx_t, wsel, rank_t, counts = _router_call(h, w_router[layer], router_bias[layer])
    cnt = counts[:, 0].astype(I32)
    pos_t, te, gi, ne, nact, off = _plan_call(cnt, eidx_t, rank_t)
    pos_tiles = (pos_t * ROW_TILE).reshape(TOPK, N_ROW_TILES, TM).transpose(1, 0, 2).reshape(
        N_ROW_TILES, 1, TOPK * TM)
    xs = _dispatch_call(cnt, off, nact, pos_tiles, hp)
    ys = _experts_call(layer, te, gi, ne, nact, xs, w_exp_gate, w_exp_up, w_exp_down)
    return _gather_call(
        final, pos_tiles, wsel, h, ys,
        w_sh_gate[layer].astype(BF16), w_sh_up[layer].astype(BF16), w_sh_down[layer].astype(BF16),
        ln_g[layer][None, :], ln_b[layer][None, :])


def _ssm_prep_kernel(lr_ref, li_ref, ldt_ref, br_ref, bi_ref, abr_ref, abi_ref, bbr_ref, bbi_ref):
    lr = lr_ref[...]
    li = li_ref[...]
    dt = jnp.exp(ldt_ref[...])
    mag = jnp.exp(lr * dt)
    ab_re = mag * jnp.cos(li * dt)
    ab_im = mag * jnp.sin(li * dt)
    den = lr * lr + li * li
    nr = ab_re - 1.0
    ni = ab_im
    cr = (nr * lr + ni * li) / den
    ci = (ni * lr - nr * li) / den
    br = br_ref[...]
    bi = bi_ref[...]
    abr_ref[...] = ab_re
    abi_ref[...] = ab_im
    bbr_ref[...] = cr * br - ci * bi
    bbi_ref[...] = cr * bi + ci * br


def _ssm_prep_call(lam_re, lam_im, log_dt, b_re, b_im):
    wide = (NG, GC * NS)
    lr = jnp.tile(lam_re, (1, GC))
    li = jnp.tile(lam_im, (1, GC))
    ldt = jnp.broadcast_to(log_dt[:, None], wide)
    br = jnp.transpose(b_re, (0, 2, 1)).reshape(wide)
    bi = jnp.transpose(b_im, (0, 2, 1)).reshape(wide)
    sds = jax.ShapeDtypeStruct(wide, F32)
    return pl.pallas_call(
        _ssm_prep_kernel, out_shape=[sds, sds, sds, sds], name="ssm_prep",
        compiler_params=pltpu.CompilerParams(vmem_limit_bytes=VMEM_LIMIT),
    )(lr, li, ldt, br, bi)


def _cmul(ar, ai, xr, xi):
    return ar * xr - ai * xi, ar * xi + ai * xr


NSEG = 8
SEGL = LP // NSEG
STEP_UNROLL = 4
MOVE_UNROLL = 8
assert LP == NSEG * SEGL and SEGL % STEP_UNROLL == 0 and SEGL % MOVE_UNROLL == 0 and PAD_FRONT < SEGL
assert STEP_UNROLL % 2 == 0


def _ssm_p_kernel(u_ref, wb_ref, wc_ref, a_ref, d_ref, z_ref, st_ref, u_scr, s_scr, sb_scr, y_scr):
    def interleave(it, carry):
        for q in range(MOVE_UNROLL):
            t = it * MOVE_UNROLL + q
            u_scr[pl.ds(pl.multiple_of(t * NSEG, NSEG), NSEG), :] = u_ref[pl.ds(t, NSEG, stride=SEGL), :]
        return carry

    lax.fori_loop(0, SEGL // MOVE_UNROLL, interleave, 0)
    row = lax.broadcasted_iota(I32, (LP, LANES), 0)
    is_pad = jnp.logical_and(row % NSEG == 0, row // NSEG < PAD_FRONT)
    u = jnp.where(is_pad, 0.0, u_scr[...])
    s_scr[...] = jnp.dot(u.astype(BF16), wb_ref[0], preferred_element_type=F32)

    ar = a_ref[0, 0:1, :]
    ai = a_ref[0, 1:2, :]

    def group(t):
        r0 = pl.multiple_of(t * NSEG, NSEG)
        return pl.ds(r0, NSEG)

    def local_scan(it, carry):
        sr, si = carry
        for q in range(STEP_UNROLL):
            g = group(it * STEP_UNROLL + q)
            tr, ti = _cmul(ar, ai, sr, si)
            sr = tr + s_scr[g, 0:CH_STATE]
            si = ti + s_scr[g, CH_STATE:]
            s_scr[g, 0:CH_STATE] = sr
            s_scr[g, CH_STATE:] = si
        return sr, si

    zero8 = jnp.zeros((NSEG, CH_STATE), F32)
    er, ei = lax.fori_loop(0, SEGL // STEP_UNROLL, local_scan, (zero8, zero8))

    pr, pi = ar, ai
    acc = None
    bits = SEGL
    while bits:
        if bits & 1:
            acc = (pr, pi) if acc is None else _cmul(pr, pi, *acc)
        bits >>= 1
        if bits:
            pr, pi = _cmul(pr, pi, pr, pi)
    alr, ali = acc
    cr = jnp.zeros((1, CH_STATE), F32)
    ci = jnp.zeros((1, CH_STATE), F32)
    ins_r, ins_i = [], []
    for j in range(NSEG):
        ins_r.append(cr)
        ins_i.append(ci)
        tr, ti = _cmul(alr, ali, cr, ci)
        cr = tr + er[j:j + 1, :]
        ci = ti + ei[j:j + 1, :]
    st_ref[0, 0, :, 0:CH_STATE] = cr
    st_ref[0, 0, :, CH_STATE:] = ci

    def fixup(it, carry):
        dr, di = carry
        for q in range(0, STEP_UNROLL, 2):
            t0 = it * STEP_UNROLL + q
            rows_r, rows_i = [], []
            for t in (t0, t0 + 1):
                g = group(t)
                dr, di = _cmul(ar, ai, dr, di)
                rows_r.append(s_scr[g, 0:CH_STATE] + dr)
                rows_i.append(s_scr[g, CH_STATE:] + di)
            pair = pl.ds(pl.multiple_of(t0 * NSEG, 2 * NSEG), 2 * NSEG)
            sb_scr[pair, 0:CH_STATE] = jnp.concatenate(rows_r, axis=0).astype(BF16)
            sb_scr[pair, CH_STATE:] = jnp.concatenate(rows_i, axis=0).astype(BF16)
        return dr, di

    lax.fori_loop(0, SEGL // STEP_UNROLL, fixup,
                  (jnp.concatenate(ins_r, axis=0), jnp.concatenate(ins_i, axis=0)))

    y_scr[...] = jnp.dot(sb_scr[...], wc_ref[0], preferred_element_type=F32) + d_ref[0] * u

    def deinterleave(it, carry):
        for q in range(MOVE_UNROLL):
            t = it * MOVE_UNROLL + q
            u_scr[pl.ds(t, NSEG, stride=SEGL), :] = y_scr[pl.ds(pl.multiple_of(t * NSEG, NSEG), NSEG), :]
        return carry

    lax.fori_loop(0, SEGL // MOVE_UNROLL, deinterleave, 0)
    z_ref[...] = jax.nn.gelu(u_scr[...]).astype(BF16)


def _ssm_p_call(h, wb_bf, wc_bf, a_tab, d_tab):
    return pl.pallas_call(
        _ssm_p_kernel,
        grid=(NB, NCHUNK),
        in_specs=[
            pl.BlockSpec((LP, LANES), lambda b, k: (b, k)),
            pl.BlockSpec((1, LANES, 2 * CH_STATE), lambda b, k: (k, 0, 0)),
            pl.BlockSpec((1, 2 * CH_STATE, LANES), lambda b, k: (k, 0, 0)),
            pl.BlockSpec((1, 2, CH_STATE), lambda b, k: (k, 0, 0)),
            pl.BlockSpec((1, 1, LANES), lambda b, k: (k, 0, 0)),
        ],
        out_specs=[
            pl.BlockSpec((LP, LANES), lambda b, k: (b, k)),
            pl.BlockSpec((1, 1, 1, 2 * CH_STATE), lambda b, k: (b, k, 0, 0)),
        ],
        out_shape=[
            jax.ShapeDtypeStruct((T_PROMPT, D), BF16),
            jax.ShapeDtypeStruct((NB, NCHUNK, 1, 2 * CH_STATE), F32),
        ],
        scratch_shapes=[pltpu.VMEM((LP, LANES), F32), pltpu.VMEM((LP, 2 * CH_STATE), F32),
                        pltpu.VMEM((LP, 2 * CH_STATE), BF16), pltpu.VMEM((LP, LANES), F32)],
        compiler_params=_cparams(("parallel", "parallel")),
        name="ssm_prompt",
    )(h, wb_bf, wc_bf, a_tab, d_tab)


def _ssm_s_kernel(u_ref, sr_ref, si_ref, wb_ref, wc_ref, a_ref, d_ref, z_ref, nr_ref, ni_ref):
    u = u_ref[...]
    bu = jnp.dot(u, wb_ref[0], preferred_element_type=F32, precision=lax.Precision.HIGHEST)
    ar = a_ref[0, 0:1, :]
    ai = a_ref[0, 1:2, :]
    tr, ti = _cmul(ar, ai, sr_ref[...], si_ref[...])
    nr = tr + bu[:, 0:CH_STATE]
    ni = ti + bu[:, CH_STATE:]
    nr_ref[...] = nr
    ni_ref[...] = ni
    s = jnp.concatenate([nr, ni], axis=1).astype(BF16)
    y = jnp.dot(s, wc_ref[0], preferred_element_type=F32) + d_ref[0] * u
    z_ref[...] = jax.nn.gelu(y).astype(BF16)


def _ssm_s_call(h, s0r, s0i, wb_f32, wc_bf, a_tab, d_tab):
    st = pl.BlockSpec((DEC, CH_STATE), lambda k: (0, k))
    return pl.pallas_call(
        _ssm_s_kernel,
        grid=(NCHUNK,),
        in_specs=[
            pl.BlockSpec((DEC, LANES), lambda k: (T_PROMPT // DEC, k)),
            st,
            st,
            pl.BlockSpec((1, LANES, 2 * CH_STATE), lambda k: (k, 0, 0)),
            pl.BlockSpec((1, 2 * CH_STATE, LANES), lambda k: (k, 0, 0)),
            pl.BlockSpec((1, 2, CH_STATE), lambda k: (k, 0, 0)),
            pl.BlockSpec((1, 1, LANES), lambda k: (k, 0, 0)),
        ],
        out_specs=[pl.BlockSpec((DEC, LANES), lambda k: (0, k)), st, st],
        out_shape=[
            jax.ShapeDtypeStruct((DEC, D), BF16),
            jax.ShapeDtypeStruct((DEC, NG * NS), F32),
            jax.ShapeDtypeStruct((DEC, NG * NS), F32),
        ],
        compiler_params=_cparams(("parallel",)),
        name="ssm_sample",
    )(h, s0r, s0i, wb_f32, wc_bf, a_tab, d_tab)


def _glu_ln_kernel(z_ref, zt_ref, w_ref, bg_ref, h_ref, g_ref, b_ref, out_ref, hp_ref):
    z = _prompt_or_tail(z_ref, zt_ref)
    acc = jnp.dot(z, w_ref[...], preferred_element_type=F32) + bg_ref[...]
    m = acc[:, :D] * _sigmoid(acc[:, D:])
    out = _layer_norm(ALPHA * h_ref[...] + m, g_ref[...], b_ref[...])
    out_ref[...] = out
    _store_rows_as_tiles(hp_ref, _pack_pair(out[:, :HALF], out[:, HALF:]))


def _glu_ln_call(z_p, z_tail, w_bf, bg, h, g, b):
    vec = pl.BlockSpec((1, D), lambda i: (0, 0))
    return pl.pallas_call(
        _glu_ln_kernel,
        grid=(N_ROW_TILES,),
        in_specs=[
            pl.BlockSpec((TM, D), _PROMPT_TILE),
            pl.BlockSpec((TM, D), lambda i: (0, 0)),
            pl.BlockSpec((D, 2 * D), lambda i: (0, 0), pipeline_mode=pl.Buffered(1)),
            pl.BlockSpec((1, 2 * D), lambda i: (0, 0)),
            pl.BlockSpec((TM, D), lambda i: (i, 0)),
            vec,
            vec,
        ],
        out_specs=[pl.BlockSpec((TM, D), lambda i: (i, 0)),
                   pl.BlockSpec((TM * ROW_TILE, LANES), lambda i: (i, 0))],
        out_shape=[jax.ShapeDtypeStruct((T_ALL, D), F32),
                   jax.ShapeDtypeStruct((T_ALL * ROW_TILE, LANES), U32)],
        compiler_params=_cparams(("parallel",)),
        name="glu_ln",
    )(z_p, z_tail, w_bf, bg, h, g, b)


GROUPS_PER_CHUNK = LANES // GC


def _block_diag_in(t):
    t3 = t.reshape(NCHUNK, LANES, NS)
    same = (np.arange(LANES)[:, None] // GC) == (np.arange(CH_STATE)[None, :] // NS)
    return jnp.tile(t3, (1, 1, GROUPS_PER_CHUNK)) * jnp.asarray(same, t.dtype)


def _block_diag_out(t):
    t3 = jnp.swapaxes(t, 1, 2).reshape(NCHUNK, CH_STATE, GC)
    same = (np.arange(CH_STATE)[:, None] // NS) == (np.arange(LANES)[None, :] // GC)
    return jnp.tile(t3, (1, 1, GROUPS_PER_CHUNK)) * jnp.asarray(same, t.dtype)


def kernel(x_prompt, x_sample, cache_k, cache_v, state_ssm_re, state_ssm_im, meta_tokens, w_qkv, b_qkv, attn_sinks, w_o, b_o, ssm_lam_re, ssm_lam_im, ssm_log_dt, ssm_b_re, ssm_b_im, ssm_c_re, ssm_c_im, ssm_d, w_glu, b_glu, ln_mix_g, ln_mix_b, w_router, router_bias, w_exp_gate, w_exp_up, w_exp_down, w_sh_gate, w_sh_up, w_sh_down, ln_ffn_g, ln_ffn_b):
    moe_w = (w_router, router_bias, w_exp_gate, w_exp_up, w_exp_down, w_sh_gate, w_sh_up, w_sh_down,
             ln_ffn_g, ln_ffn_b)

    front = jnp.concatenate([jnp.zeros((PAD_FRONT, D), F32), meta_tokens], axis=0)
    pieces = []
    for b in range(NB):
        pieces += [front, x_prompt[b]]
    h = jnp.concatenate(pieces + [x_sample.reshape(DEC, D)], axis=0)

    rc, rs1, rs2 = _rope_tables()
    q, k, v = _qkv_call(h, w_qkv[0].astype(BF16), b_qkv[0][None, :], rc, rs1, rs2)
    o_p = _attn_p_call(attn_sinks[0], q, k, v)
    q3 = q[T_PROMPT:].astype(F32).reshape(DEC, NH, HD)
    o_s, ck_new, cv_new = _attn_s_call(
        attn_sinks[0][:, None], q3, k, v,
        cache_k[0].reshape(DEC, BLK, NKV * HD), cache_v[0].reshape(DEC, BLK, NKV * HD))
    o_tail = _tail_tile(o_p, o_s.reshape(DEC, D).astype(BF16))
    h, hp = _oproj_ln_call(o_p, o_tail, w_o[0].astype(BF16), b_o[0][None, :], h,
                           ln_mix_g[0][None, :], ln_mix_b[0][None, :])
    h = _moe_layer(0, False, h, hp, *moe_w)

    def last_window(t):
        rows = [t[(b + 1) * LP - BLK:(b + 1) * LP] for b in range(NB)]
        return jnp.stack(rows).reshape(NB, BLK, NKV, HD)

    kp = last_window(k)
    vp = last_window(v)

    ab_re, ab_im, bb_re, bb_im = _ssm_prep_call(
        ssm_lam_re[0], ssm_lam_im[0], ssm_log_dt[0], ssm_b_re[0], ssm_b_im[0])
    wb = jnp.concatenate([_block_diag_in(bb_re), _block_diag_in(bb_im)], axis=2)
    wc = jnp.concatenate([_block_diag_out(ssm_c_re[0]), -_block_diag_out(ssm_c_im[0])], axis=1)
    wc_bf = wc.astype(BF16)
    a_tab = jnp.stack([ab_re[:, :NS].reshape(NCHUNK, CH_STATE),
                       ab_im[:, :NS].reshape(NCHUNK, CH_STATE)], axis=1)
    d_tab = ssm_d[0].reshape(NCHUNK, 1, LANES)
    z_p, st_p = _ssm_p_call(h, wb.astype(BF16), wc_bf, a_tab, d_tab)
    z_s, sr_new, si_new = _ssm_s_call(
        h, state_ssm_re[0].reshape(DEC, NG * NS), state_ssm_im[0].reshape(DEC, NG * NS),
        wb, wc_bf, a_tab, d_tab)
    h, hp = _glu_ln_call(z_p, _tail_tile(z_p, z_s), w_glu[0].astype(BF16), b_glu[0][None, :], h,
                         ln_mix_g[1][None, :], ln_mix_b[1][None, :])
    y_prompt, y_sample = _moe_layer(1, True, h, hp, *moe_w)
    y_prompt = y_prompt.reshape(NB, SEQ, D)
    y_sample = y_sample.reshape(DEC, 1, D)
    st_p = st_p.reshape(NB, NCHUNK, 2, 8, NS)
    rp = st_p[:, :, 0].reshape(NB, NG, NS)
    ip = st_p[:, :, 1].reshape(NB, NG, NS)
    return (y_prompt, y_sample,
            kp[None], vp[None],
            ck_new.reshape(1, DEC, BLK, NKV, HD), cv_new.reshape(1, DEC, BLK, NKV, HD),
            rp[None], ip[None],
            sr_new.reshape(1, DEC, NG, NS), si_new.reshape(1, DEC, NG, NS))
```

```python
import functools
import math

import jax
import jax.numpy as jnp
import numpy as np
from jax import lax
from jax.experimental import pallas as pl
from jax.experimental.pallas import tpu as pltpu

F32 = jnp.float32
BF16 = jnp.bfloat16
I32 = jnp.int32
U32 = jnp.uint32

D = 2048
HALF = D // 2
NB = 4
N_META = 16
SEQ = 2048
L = N_META + SEQ
BLK = 128
PAD_FRONT = (-L) % BLK
LP = L + PAD_FRONT
NBLK = LP // BLK
T_PROMPT = NB * LP
DEC = 128
T_ALL = T_PROMPT + DEC
PAST_LEN = 8192
HD = 64
NH = 32
NKV = 4
QPK = NH // NKV
QKV = (NH + 2 * NKV) * HD
QK_COLS = (NH + NKV) * HD
ROT = HD // 4
ROT_HALF = ROT // 2
ROPE_THETA = 500000.0
NG = 128
GC = 16
NS = 64
NCHUNK = 16
CH_STATE = 8 * NS
NE = 64
TOPK = 8
NEG = 8
PER_GRP = NE // NEG
TOPG = 4
DE = 512
ROUTED_SCALE = 2.5
DEPTH = 2
ALPHA = (2 * DEPTH) ** 0.25
LN_EPS = 1e-5

V7X_VMEM_BYTES = 64 * 1024 * 1024
VMEM_LIMIT = 56 * 1024 * 1024
LANES = 128

TM = 384
N_ROW_TILES = T_ALL // TM
TAIL_START = (N_ROW_TILES - 1) * TM
TM_E = 256
N_PAIRS = T_ALL * TOPK
NT_E = -(-N_PAIRS // TM_E) + NE
R_ROWS = NT_E * TM_E
assert T_ALL % TM == 0 and TAIL_START <= T_PROMPT


def _cparams(sem):
    return pltpu.CompilerParams(dimension_semantics=sem, vmem_limit_bytes=VMEM_LIMIT)


def _sigmoid(x):
    return 1.0 / (1.0 + jnp.exp(-x))


def _layer_norm(y, g, b):
    mu = jnp.mean(y, axis=-1, keepdims=True)
    yc = y - mu
    var = jnp.mean(yc * yc, axis=-1, keepdims=True)
    return yc * lax.rsqrt(var + LN_EPS) * g + b


def _pack_pair(lo, hi):
    lo_b = lax.bitcast_convert_type(lo.astype(BF16).astype(F32), U32) >> 16
    hi_b = lax.bitcast_convert_type(hi.astype(BF16).astype(F32), U32) & jnp.uint32(0xFFFF0000)
    return lo_b | hi_b


def _unpack_pair(w):
    lo = lax.bitcast_convert_type(w << 16, F32)
    hi = lax.bitcast_convert_type(w & jnp.uint32(0xFFFF0000), F32)
    return lo, hi


ROW_TILE = HALF // LANES
assert ROW_TILE == 8


def _store_rows_as_tiles(ref, rows):
    n = rows.shape[0]
    for j in range(ROW_TILE):
        ref[pl.ds(j, n, stride=ROW_TILE), :] = rows[:, j * LANES:(j + 1) * LANES]


def _load_tiles_as_rows(ref, n):
    return jnp.concatenate([ref[pl.ds(j, n, stride=ROW_TILE), :] for j in range(ROW_TILE)], axis=1)


def _qkv_kernel(x_ref, w_ref, b_ref, c_ref, s1_ref, s2_ref, q_ref, k_ref, v_ref):
    xb = x_ref[...].astype(BF16)
    acc = jnp.dot(xb, w_ref[...], preferred_element_type=F32) + b_ref[...]
    c = c_ref[...]
    s1 = s1_ref[...]
    s2 = s2_ref[...]
    for j in range(QK_COLS // LANES):
        blk = acc[:, j * LANES:(j + 1) * LANES]
        r = (blk * c + pltpu.roll(blk, LANES - ROT_HALF, axis=1) * s1
             + pltpu.roll(blk, ROT_HALF, axis=1) * s2)
        if j < D // LANES:
            q_ref[:, j * LANES:(j + 1) * LANES] = (r * (1.0 / math.sqrt(HD))).astype(BF16)
        else:
            jj = j - D // LANES
            k_ref[:, jj * LANES:(jj + 1) * LANES] = r
    v_ref[...] = acc[:, QK_COLS:]


def _qkv_call(x, w_bf, b, rc, rs1, rs2):
    return pl.pallas_call(
        _qkv_kernel,
        grid=(T_ALL // TM,),
        in_specs=[
            pl.BlockSpec((TM, D), lambda i: (i, 0)),
            pl.BlockSpec((D, QKV), lambda i: (0, 0)),
            pl.BlockSpec((1, QKV), lambda i: (0, 0)),
            pl.BlockSpec((TM, LANES), lambda i: (i, 0)),
            pl.BlockSpec((TM, LANES), lambda i: (i, 0)),
            pl.BlockSpec((TM, LANES), lambda i: (i, 0)),
        ],
        out_specs=[
            pl.BlockSpec((TM, D), lambda i: (i, 0)),
            pl.BlockSpec((TM, NKV * HD), lambda i: (i, 0)),
            pl.BlockSpec((TM, NKV * HD), lambda i: (i, 0)),
        ],
        out_shape=[
            jax.ShapeDtypeStruct((T_ALL, D), BF16),
            jax.ShapeDtypeStruct((T_ALL, NKV * HD), F32),
            jax.ShapeDtypeStruct((T_ALL, NKV * HD), F32),
        ],
        compiler_params=_cparams(("parallel",)),
        name="qkv_rope",
    )(x, w_bf, b, rc, rs1, rs2)


def _rope_tables():
    pos_p = jnp.maximum(jnp.arange(LP, dtype=I32) - PAD_FRONT, 0)
    pos = jnp.concatenate([pos_p, jnp.full((8,), PAST_LEN, I32)]).astype(F32)
    inv_freq = ROPE_THETA ** (-jnp.arange(0, ROT, 2, dtype=F32) / ROT)
    ang = pos[:, None] * inv_freq[None, :]
    cos = jnp.cos(ang)
    sin = jnp.sin(ang)
    lane = np.arange(LANES) % HD
    freq = np.arange(ROT_HALF)[:, None]
    first = ((lane[None, :] == freq) & (lane[None, :] < ROT_HALF)).astype(np.float32)
    second = ((lane[None, :] - ROT_HALF == freq) & (lane[None, :] < ROT)).astype(np.float32)
    rest = (lane >= ROT).astype(np.float32)[None, :]
    place = functools.partial(jnp.dot, precision=lax.Precision.HIGHEST)
    c = place(cos, jnp.asarray(first + second)) + jnp.asarray(rest)
    s1 = place(sin, jnp.asarray(-first))
    s2 = place(sin, jnp.asarray(second))

    def all_rows(t):
        return jnp.concatenate([t[:LP]] * NB + [t[LP:]] * (DEC // 8), axis=0)

    return all_rows(c), all_rows(s1), all_rows(s2)


def _attn_p_kernel(sink_ref, q_ref, kp_ref, kc_ref, vp_ref, vc_ref, o_ref):
    j = pl.program_id(0) % NBLK
    keys = jnp.concatenate([kp_ref[...], kc_ref[...]], axis=0).astype(BF16)
    vals = jnp.concatenate([vp_ref[...], vc_ref[...]], axis=0).astype(BF16)
    r = lax.broadcasted_iota(I32, (BLK, 2 * BLK), 0)
    c = lax.broadcasted_iota(I32, (BLK, 2 * BLK), 1)
    dist = BLK + r - c
    kpos = (j - 1) * BLK - PAD_FRONT + c
    mask = (dist >= 0) & (dist <= BLK) & (kpos >= 0)
    for g in range(NKV):
        kg = keys[:, g * HD:(g + 1) * HD]
        vg = vals[:, g * HD:(g + 1) * HD]
        heads = range(g * QPK, (g + 1) * QPK)
        scores = [lax.dot_general(q_ref[:, h * HD:(h + 1) * HD], kg, (((1,), (1,)), ((), ())),
                                  preferred_element_type=F32) for h in heads]
        probs, rdens = [], []
        for h, s in zip(heads, scores):
            s = jnp.where(mask, s, -jnp.inf)
            sk = sink_ref[h]
            m = jnp.maximum(jnp.max(s, axis=1, keepdims=True), sk)
            p = jnp.exp(s - m)
            rdens.append(1.0 / (jnp.sum(p, axis=1, keepdims=True) + jnp.exp(sk - m)))
            probs.append(p.astype(BF16))
        for h, p, rden in zip(heads, probs, rdens):
            oh = jnp.dot(p, vg, preferred_element_type=F32) * rden
            o_ref[:, h * HD:(h + 1) * HD] = oh.astype(BF16)


def _attn_p_call(sinks, q, k, v):
    prev = lambda i: (jnp.where(i % NBLK == 0, i, i - 1), 0)
    cur = lambda i: (i, 0)
    return pl.pallas_call(
        _attn_p_kernel,
        grid=(NB * NBLK,),
        in_specs=[
            pl.BlockSpec(memory_space=pltpu.SMEM),
            pl.BlockSpec((BLK, D), cur),
            pl.BlockSpec((BLK, NKV * HD), prev),
            pl.BlockSpec((BLK, NKV * HD), cur),
            pl.BlockSpec((BLK, NKV * HD), prev),
            pl.BlockSpec((BLK, NKV * HD), cur),
        ],
        out_specs=pl.BlockSpec((BLK, D), cur),
        out_shape=jax.ShapeDtypeStruct((T_PROMPT, D), BF16),
        compiler_params=_cparams(("parallel",)),
        name="attn_prompt",
    )(sinks, q, k, k, v, v)


SEQ_PER_STEP = 16
SEQ_UNROLL = 4


def _attn_s_kernel(sink_ref, q_ref, kn_ref, vn_ref, ck_ref, cv_ref, o_ref, cko_ref, cvo_ref):
    row = lax.broadcasted_iota(I32, (BLK, NKV * HD), 0)
    hrow = lax.broadcasted_iota(I32, (NH, NKV * HD), 0) // QPK
    hlane = lax.broadcasted_iota(I32, (NH, NKV * HD), 1) // HD
    own = hrow == hlane
    sk = sink_ref[...]

    def score_stage(s):
        kn = kn_ref[pl.ds(s, 1), :]
        knr = kn.astype(BF16).astype(F32)
        q = q_ref[s].astype(BF16)
        qe = jnp.where(own, jnp.concatenate([q] * NKV, axis=1), jnp.zeros((), BF16))
        sc = lax.dot_general(qe, ck_ref[s].astype(BF16), (((1,), (1,)), ((), ())),
                             preferred_element_type=F32)
        sn = jnp.sum(qe.astype(F32) * knr, axis=1, keepdims=True)
        return sc, sn

    def softmax_stage(sc, sn):
        m = jnp.maximum(jnp.maximum(jnp.max(sc, axis=1, keepdims=True), sn), sk)
        p = jnp.exp(sc - m)
        pn = jnp.exp(sn - m)
        rden = 1.0 / (jnp.sum(p, axis=1, keepdims=True) + pn + jnp.exp(sk - m))
        return p.astype(BF16), pn.astype(BF16).astype(F32), rden

    def value_stage(s, p, pn, rden):
        vn = vn_ref[pl.ds(s, 1), :]
        vnr = vn.astype(BF16).astype(F32)
        of = jnp.dot(p, cv_ref[s].astype(BF16), preferred_element_type=F32)
        of = jnp.where(own, of + pn * vnr, 0.0)
        og = of[:, 0:HD]
        for g in range(1, NKV):
            og = og + of[:, g * HD:(g + 1) * HD]
        o_ref[s] = og * rden
        cko_ref[s] = jnp.where(row == BLK - 1, kn_ref[pl.ds(s, 1), :], pltpu.roll(ck_ref[s], BLK - 1, axis=0))
        cvo_ref[s] = jnp.where(row == BLK - 1, vn, pltpu.roll(cv_ref[s], BLK - 1, axis=0))

    def body(it, carry):
        seqs = [it * SEQ_UNROLL + u for u in range(SEQ_UNROLL)]
        staged = [score_stage(s) for s in seqs]
        soft = [softmax_stage(*st) for st in staged]
        for s, sm in zip(seqs, soft):
            value_stage(s, *sm)
        return carry

    lax.fori_loop(0, SEQ_PER_STEP // SEQ_UNROLL, body, 0)


def _attn_s_call(sinks_col, q3, k, v, cache_k, cache_v):
    sp = SEQ_PER_STEP
    kv_off = T_PROMPT // sp
    return pl.pallas_call(
        _attn_s_kernel,
        grid=(DEC // sp,),
        in_specs=[
            pl.BlockSpec((NH, 1), lambda i: (0, 0)),
            pl.BlockSpec((sp, NH, HD), lambda i: (i, 0, 0)),
            pl.BlockSpec((sp, NKV * HD), lambda i: (kv_off + i, 0)),
            pl.BlockSpec((sp, NKV * HD), lambda i: (kv_off + i, 0)),
            pl.BlockSpec((sp, BLK, NKV * HD), lambda i: (i, 0, 0)),
            pl.BlockSpec((sp, BLK, NKV * HD), lambda i: (i, 0, 0)),
        ],
        out_specs=[
            pl.BlockSpec((sp, NH, HD), lambda i: (i, 0, 0)),
            pl.BlockSpec((sp, BLK, NKV * HD), lambda i: (i, 0, 0)),
            pl.BlockSpec((sp, BLK, NKV * HD), lambda i: (i, 0, 0)),
        ],
        out_shape=[
            jax.ShapeDtypeStruct((DEC, NH, HD), F32),
            jax.ShapeDtypeStruct((DEC, BLK, NKV * HD), F32),
            jax.ShapeDtypeStruct((DEC, BLK, NKV * HD), F32),
        ],
        compiler_params=_cparams(("parallel",)),
        name="attn_sample",
    )(sinks_col, q3, k, v, cache_k, cache_v)


def _prompt_or_tail(x_ref, tail_ref):
    return jnp.where(pl.program_id(0) == N_ROW_TILES - 1, tail_ref[...], x_ref[...])


def _tail_tile(x_prompt_rows, x_sample_rows):
    return jnp.concatenate([x_prompt_rows[TAIL_START:], x_sample_rows], axis=0)


_PROMPT_TILE = lambda i: (jnp.minimum(i, N_ROW_TILES - 2), 0)


def _oproj_ln_kernel(o_ref, ot_ref, w_ref, bo_ref, h_ref, g_ref, b_ref, out_ref, hp_ref):
    o = _prompt_or_tail(o_ref, ot_ref)
    m = jnp.dot(o, w_ref[...], preferred_element_type=F32) + bo_ref[...]
    out = _layer_norm(ALPHA * h_ref[...] + m, g_ref[...], b_ref[...])
    out_ref[...] = out
    _store_rows_as_tiles(hp_ref, _pack_pair(out[:, :HALF], out[:, HALF:]))


def _oproj_ln_call(o_p, o_tail, w_bf, bo, h, g, b):
    vec = pl.BlockSpec((1, D), lambda i: (0, 0))
    return pl.pallas_call(
        _oproj_ln_kernel,
        grid=(N_ROW_TILES,),
        in_specs=[
            pl.BlockSpec((TM, D), _PROMPT_TILE),
            pl.BlockSpec((TM, D), lambda i: (0, 0)),
            pl.BlockSpec((D, D), lambda i: (0, 0)),
            vec,
            pl.BlockSpec((TM, D), lambda i: (i, 0)),
            vec,
            vec,
        ],
        out_specs=[pl.BlockSpec((TM, D), lambda i: (i, 0)),
                   pl.BlockSpec((TM * ROW_TILE, LANES), lambda i: (i, 0))],
        out_shape=[jax.ShapeDtypeStruct((T_ALL, D), F32),
                   jax.ShapeDtypeStruct((T_ALL * ROW_TILE, LANES), U32)],
        compiler_params=_cparams(("parallel",)),
        name="oproj_ln",
    )(o_p, o_tail, w_bf, bo, h, g, b)


def _router_kernel(h_ref, wr_ref, rb_ref, eidx_ref, wsel_ref, rank_ref, cnt_ref, carry_ref):
    i = pl.program_id(0)

    @pl.when(i == 0)
    def _():
        carry_ref[...] = jnp.zeros_like(carry_ref)

    h = h_ref[...]
    w = wr_ref[...]
    h_hi = h.astype(BF16)
    h_lo = (h - h_hi.astype(F32)).astype(BF16)
    w_hi = w.astype(BF16)
    w_lo = (w - w_hi.astype(F32)).astype(BF16)
    logits = (jnp.dot(h_hi, w_hi, preferred_element_type=F32)
              + (jnp.dot(h_hi, w_lo, preferred_element_type=F32)
                 + jnp.dot(h_lo, w_hi, preferred_element_type=F32)))
    scores = _sigmoid(logits.T[0:NE, :])
    biased = scores + rb_ref[...]
    ninf = -jnp.inf
    sub = lax.broadcasted_iota(I32, (PER_GRP, TM), 0).astype(F32)
    sc_g = [scores[g * PER_GRP:(g + 1) * PER_GRP, :] for g in range(NEG)]
    b_g = [biased[g * PER_GRP:(g + 1) * PER_GRP, :] for g in range(NEG)]
    e_g = [sub + float(g * PER_GRP) for g in range(NEG)]

    def smax(x):
        return jnp.max(x, axis=0, keepdims=True)

    def smin(x):
        return jnp.min(x, axis=0, keepdims=True)

    gs = []
    for g in range(NEG):
        m1 = smax(b_g[g])
        i1 = smin(jnp.where(b_g[g] == m1, sub, float(PER_GRP)))
        m2 = smax(jnp.where(sub == i1, ninf, b_g[g]))
        gs.append(m1 + m2)

    work = []
    for g in range(NEG):
        beaten = jnp.zeros((1, TM), F32)
        for o in range(NEG):
            if o != g:
                wins = (gs[o] >= gs[g]) if o < g else (gs[o] > gs[g])
                beaten = beaten + jnp.where(wins, 1.0, 0.0)
        work.append(jnp.where(beaten < float(TOPG), b_g[g], ninf))

    idx_rows, w_rows = [], []
    onehot = [jnp.zeros((PER_GRP, TM), F32) for _ in range(NEG)]
    for _ in range(TOPK):
        m = smax(work[0])
        for g in range(1, NEG):
            m = jnp.maximum(m, smax(work[g]))
        ik = smin(jnp.where(work[0] == m, e_g[0], float(NE)))
        for g in range(1, NEG):
            ik = jnp.minimum(ik, smin(jnp.where(work[g] == m, e_g[g], float(NE))))
        wk = jnp.zeros((1, TM), F32)
        for g in range(NEG):
            hit = e_g[g] == ik
            wk = wk + jnp.sum(jnp.where(hit, sc_g[g], 0.0), axis=0, keepdims=True)
            onehot[g] = jnp.where(hit, 1.0, onehot[g])
            work[g] = jnp.where(hit, ninf, work[g])
        idx_rows.append(ik)
        w_rows.append(wk)
    wsum = w_rows[0]
    for wk in w_rows[1:]:
        wsum = wsum + wk

    rr = lax.broadcasted_iota(I32, (TM, TM), 0)
    cc = lax.broadcasted_iota(I32, (TM, TM), 1)
    tri = jnp.where(rr < cc, 1.0, 0.0).astype(BF16)
    oh = jnp.concatenate(onehot, axis=0)
    prefix = jnp.dot(oh.astype(BF16), tri, preferred_element_type=F32) + carry_ref[...]
    carry_ref[...] = carry_ref[...] + jnp.sum(oh, axis=1, keepdims=True)
    cnt_ref[...] = carry_ref[...]

    rank_rows = []
    for k in range(TOPK):
        rk = jnp.zeros((1, TM), F32)
        for g in range(NEG):
            pg = prefix[g * PER_GRP:(g + 1) * PER_GRP, :]
            rk = rk + jnp.sum(jnp.where(e_g[g] == idx_rows[k], pg, 0.0), axis=0, keepdims=True)
        rank_rows.append(rk)
    eidx_ref[...] = jnp.concatenate(idx_rows, axis=0).astype(I32)
    rank_ref[...] = jnp.concatenate(rank_rows, axis=0).astype(I32)
    w_t = jnp.concatenate([wk / wsum * ROUTED_SCALE for wk in w_rows]
                          + [jnp.zeros((LANES - TOPK, TM), F32)], axis=0)
    wsel_ref[...] = w_t.T[:, 0:TOPK]


def _router_call(h, w_router, router_bias):
    tk = pl.BlockSpec((TM, TOPK), lambda i: (i, 0))
    kt = pl.BlockSpec((TOPK, TM), lambda i: (0, i))
    return pl.pallas_call(
        _router_kernel,
        grid=(T_ALL // TM,),
        in_specs=[
            pl.BlockSpec((TM, D), lambda i: (i, 0)),
            pl.BlockSpec((D, LANES), lambda i: (0, 0)),
            pl.BlockSpec((NE, 1), lambda i: (0, 0)),
        ],
        out_specs=[kt, tk, kt, pl.BlockSpec((NE, 1), lambda i: (0, 0))],
        out_shape=[
            jax.ShapeDtypeStruct((TOPK, T_ALL), I32),
            jax.ShapeDtypeStruct((T_ALL, TOPK), F32),
            jax.ShapeDtypeStruct((TOPK, T_ALL), I32),
            jax.ShapeDtypeStruct((NE, 1), F32),
        ],
        scratch_shapes=[pltpu.VMEM((NE, 1), F32)],
        compiler_params=_cparams(("arbitrary",)),
        name="router",
    )(h, jnp.pad(w_router, ((0, 0), (0, LANES - NE))), router_bias[:, None])


NEXT_SHIFT = 8
NEXT_MASK = (1 << NEXT_SHIFT) - 1
assert NE < NEXT_MASK


def _tiles_of(cnt):
    return (cnt + TM_E - 1) // TM_E


def _plan_kernel(cnt_ref, eidx_ref, rank_ref, pos_ref, te_ref, gi_ref, ne_ref, na_ref, off_s):
    def offsets(e, run):
        off_s[e] = run
        return run + _tiles_of(cnt_ref[e]) * TM_E

    total = lax.fori_loop(0, NE, offsets, 0)
    na = total // TM_E
    na_ref[0] = na

    def idle(t, carry):
        te_ref[t] = NE - 1
        gi_ref[t] = 0
        ne_ref[t] = 0
        return carry

    lax.fori_loop(na, NT_E, idle, 0)

    def forward(e, ordinal):
        t0 = off_s[e] // TM_E
        nt = _tiles_of(cnt_ref[e])

        def mark(t, carry):
            te_ref[t] = e
            gi_ref[t] = ordinal
            return carry

        lax.fori_loop(t0, t0 + nt, mark, 0)
        return ordinal + jnp.where(nt > 0, 1, 0)

    lax.fori_loop(0, NE, forward, 0)

    def backward(i, carry):
        nxt, after = carry
        e = NE - 1 - i
        t0 = off_s[e] // TM_E
        nt = _tiles_of(cnt_ref[e])

        def mark(t, c):
            ne_ref[t] = (nxt + 1) | ((after + 1) << NEXT_SHIFT)
            return c

        lax.fori_loop(t0, t0 + nt, mark, 0)
        return jnp.where(nt > 0, e, nxt), jnp.where(nt > 0, nxt, after)

    lax.fori_loop(0, NE, backward, (-1, -1))

    eidx = eidx_ref[...]
    pos = rank_ref[...]
    for e in range(NE):
        pos = pos + jnp.where(eidx == e, off_s[e], 0)
    pos_ref[...] = pos


def _plan_call(cnt, eidx_t, rank_t):
    smem = pl.BlockSpec(memory_space=pltpu.SMEM)
    vmem = pl.BlockSpec(memory_space=pltpu.VMEM)
    tiles = jax.ShapeDtypeStruct((NT_E,), I32)
    return pl.pallas_call(
        _plan_kernel,
        in_specs=[smem, vmem, vmem],
        out_specs=[vmem, smem, smem, smem, smem, smem],
        out_shape=[jax.ShapeDtypeStruct((TOPK, T_ALL), I32), tiles, tiles, tiles,
                   jax.ShapeDtypeStruct((1,), I32), jax.ShapeDtypeStruct((NE,), I32)],
        name="moe_plan",
    )(cnt, eidx_t, rank_t)


BLK_PER_TILE = TM // BLK


def _final_copies(step, obuf, yp_ref, ys_ref, sem):
    out = []
    slot = step % 2
    for m in range(BLK_PER_TILE):
        g = step * BLK_PER_TILE + m
        b = g // NBLK
        j = g % NBLK
        src = obuf.at[slot, pl.ds(m * BLK, BLK)]
        r0 = pl.multiple_of(jnp.maximum(b * SEQ + (j - 1) * BLK, 0), BLK)
        out.append((jnp.logical_and(g < NB * NBLK, j >= 1),
                    pltpu.make_async_copy(src, yp_ref.at[pl.ds(r0, BLK)], sem.at[slot])))
        out.append((g == NB * NBLK, pltpu.make_async_copy(src, ys_ref, sem.at[slot])))
    return out


def _dispatch_kernel(cnt_ref, off_ref, na_ref, pos_ref, hp_ref, xs_ref, zero_ref, sem, zsem):
    i = pl.program_id(0)

    def issue(t, carry):
        src = hp_ref.at[pl.ds(pl.multiple_of(t * ROW_TILE, ROW_TILE), ROW_TILE)]
        for k in range(TOPK):
            p = pl.multiple_of(pos_ref[0, 0, k * TM + t], ROW_TILE)
            pltpu.make_async_copy(src, xs_ref.at[pl.ds(p, ROW_TILE)], sem).start(priority=k % 2)
        return carry

    lax.fori_loop(0, TM, issue, 0)

    @pl.when(i == 0)
    def _():
        zero_ref[...] = jnp.zeros_like(zero_ref)

        def rows_copy(first_row, n_rows):
            r0 = pl.multiple_of(first_row * ROW_TILE, ROW_TILE)
            return pltpu.make_async_copy(zero_ref.at[pl.ds(0, n_rows * ROW_TILE)],
                                         xs_ref.at[pl.ds(r0, n_rows * ROW_TILE)], zsem)

        def row_copy(r):
            return rows_copy(r, 1)

        def group_copy(q):
            return rows_copy(q * 8, 8)

        def tile_copy(t):
            return rows_copy(t * TM_E, TM_E)

        def run(copy, lo, hi):
            def start(x, carry):
                copy(x).start()
                return carry

            def wait(x, carry):
                copy(x).wait()
                return carry

            lax.fori_loop(lo, hi, start, 0)
            lax.fori_loop(lo, hi, wait, 0)

        def per_expert(e, carry):
            lo = off_ref[e] + cnt_ref[e]
            hi = off_ref[e] + _tiles_of(cnt_ref[e]) * TM_E
            mid = jnp.minimum(((lo + 7) // 8) * 8, hi)
            run(row_copy, lo, mid)
            run(group_copy, mid // 8, hi // 8)
            return carry

        lax.fori_loop(0, NE, per_expert, 0)
        run(tile_copy, na_ref[0], NT_E)

    for _ in range(TOPK):
        pltpu.make_async_copy(hp_ref, xs_ref.at[pl.ds(0, TM * ROW_TILE)], sem).wait()


def _dispatch_call(cnt, off, nact, pos_tiles, hp):
    return pl.pallas_call(
        _dispatch_kernel,
        grid_spec=pltpu.PrefetchScalarGridSpec(
            num_scalar_prefetch=3,
            grid=(N_ROW_TILES,),
            in_specs=[
                pl.BlockSpec((1, 1, TM * TOPK), lambda i, c, o, n: (i, 0, 0), memory_space=pltpu.SMEM),
                pl.BlockSpec((TM * ROW_TILE, LANES), lambda i, c, o, n: (i, 0)),
            ],
            out_specs=pl.BlockSpec(memory_space=pl.ANY),
            scratch_shapes=[
                pltpu.VMEM((TM_E * ROW_TILE, LANES), U32),
                pltpu.SemaphoreType.DMA(()),
                pltpu.SemaphoreType.DMA(()),
            ],
        ),
        out_shape=jax.ShapeDtypeStruct((R_ROWS * ROW_TILE, LANES), U32),
        compiler_params=_cparams(("arbitrary",)),
        name="moe_dispatch",
    )(cnt, off, nact, pos_tiles, hp)


def _experts_kernel(layer, te_ref, gi_ref, ne_ref, na_ref, xs_ref, wg_hbm, wu_hbm, wd_hbm, ys_ref,
                    wgf, wuf, wdf, wg_s, wu_s, wd_s, wsem):
    i = pl.program_id(0)
    na = na_ref[0]

    def weight_copies(e, slot):
        return (pltpu.make_async_copy(wg_hbm.at[layer, e], wgf.at[slot], wsem.at[slot, 0]),
                pltpu.make_async_copy(wu_hbm.at[layer, e], wuf.at[slot], wsem.at[slot, 1]),
                pltpu.make_async_copy(wd_hbm.at[layer, e], wdf.at[slot], wsem.at[slot, 2]))

    @pl.when(i == 0)
    def _():
        for cp in weight_copies(te_ref[0], 0):
            cp.start(priority=1)
        second = (ne_ref[0] & NEXT_MASK) - 1

        @pl.when(second >= 0)
        def _():
            for cp in weight_copies(second, 1):
                cp.start(priority=1)

    ic = jnp.minimum(i, na - 1)
    first = jnp.logical_or(i == 0, te_ref[ic] != te_ref[jnp.maximum(ic - 1, 0)])

    @pl.when(jnp.logical_and(i < na, first))
    def _():
        slot = gi_ref[ic] % 2
        for cp in weight_copies(te_ref[ic], slot):
            cp.wait()
        wg_s[...] = wgf[slot].astype(BF16)
        wu_s[...] = wuf[slot].astype(BF16)
        wd_s[...] = wdf[slot].astype(BF16)
        after = (ne_ref[ic] >> NEXT_SHIFT) - 1

        @pl.when(after >= 0)
        def _():
            for cp in weight_copies(after, slot):
                cp.start(priority=1)

    @pl.when(i < na)
    def _():
        lo, hi = _unpack_pair(_load_tiles_as_rows(xs_ref, TM_E))
        lo = lo.astype(BF16)
        hi = hi.astype(BF16)
        hg = (jnp.dot(lo, wg_s[0:HALF, :], preferred_element_type=F32)
              + jnp.dot(hi, wg_s[HALF:D, :], preferred_element_type=F32))
        hu = (jnp.dot(lo, wu_s[0:HALF, :], preferred_element_type=F32)
              + jnp.dot(hi, wu_s[HALF:D, :], preferred_element_type=F32))
        act = (hg * _sigmoid(hg) * hu).astype(BF16)
        y = jnp.dot(act, wd_s[...], preferred_element_type=F32)
        _store_rows_as_tiles(ys_ref, _pack_pair(y[:, :HALF], y[:, HALF:]))

    @pl.when(i >= na)
    def _():
        ys_ref[...] = jnp.zeros_like(ys_ref)


def _experts_call(layer, te, gi, ne, nact, xs, w_gate, w_up, w_down):
    anyspec = pl.BlockSpec(memory_space=pl.ANY)
    return pl.pallas_call(
        functools.partial(_experts_kernel, layer),
        grid_spec=pltpu.PrefetchScalarGridSpec(
            num_scalar_prefetch=4,
            grid=(NT_E,),
            in_specs=[
                pl.BlockSpec((TM_E * ROW_TILE, LANES),
                             lambda i, te_r, gi_r, ne_r, na_r: (jnp.minimum(i, na_r[0] - 1), 0)),
                anyspec, anyspec, anyspec,
            ],
            out_specs=pl.BlockSpec((TM_E * ROW_TILE, LANES), lambda i, te_r, gi_r, ne_r, na_r: (i, 0)),
            scratch_shapes=[
                pltpu.VMEM((2, D, DE), F32),
                pltpu.VMEM((2, D, DE), F32),
                pltpu.VMEM((2, DE, D), F32),
                pltpu.VMEM((D, DE), BF16),
                pltpu.VMEM((D, DE), BF16),
                pltpu.VMEM((DE, D), BF16),
                pltpu.SemaphoreType.DMA((2, 3)),
            ],
        ),
        out_shape=jax.ShapeDtypeStruct((R_ROWS * ROW_TILE, LANES), U32),
        compiler_params=_cparams(("arbitrary",)),
        name="moe_experts",
    )(te, gi, ne, nact, xs, w_gate, w_up, w_down)


def _gather_kernel(final, pos_ref, posn_ref, w_ref, h_ref, ys_ref, wsg_ref, wsu_ref, wsd_ref,
                   g_ref, b_ref, *rest):
    i = pl.program_id(0)
    if final:
        yp_ref, ys_out_ref, gbuf, gsem, obuf, osem = rest

        def wait_step(step):
            for cond, cp in _final_copies(step, obuf, yp_ref, ys_out_ref, osem):
                @pl.when(cond)
                def _():
                    cp.wait()

        @pl.when(i >= 2)
        def _():
            wait_step(i - 2)
    else:
        out_ref, gbuf, gsem = rest

    def issue(idx_ref, slot):
        def body(t, carry):
            r0 = pl.multiple_of(t * ROW_TILE, ROW_TILE)
            for k in range(TOPK):
                p = pl.multiple_of(idx_ref[0, 0, k * TM + t], ROW_TILE)
                pltpu.make_async_copy(ys_ref.at[pl.ds(p, ROW_TILE)], gbuf.at[slot, k, pl.ds(r0, ROW_TILE)],
                                      gsem.at[slot]).start(priority=k % 2)
            return carry

        lax.fori_loop(0, TM, body, 0)

    @pl.when(i == 0)
    def _():
        issue(pos_ref, 0)

    @pl.when(i + 1 < N_ROW_TILES)
    def _():
        issue(posn_ref, (i + 1) % 2)

    h = h_ref[...]
    hb = h.astype(BF16)
    sg = jnp.dot(hb, wsg_ref[...], preferred_element_type=F32)
    su = jnp.dot(hb, wsu_ref[...], preferred_element_type=F32)
    act = (sg * _sigmoid(sg) * su).astype(BF16)
    y = ALPHA * h + jnp.dot(act, wsd_ref[...], preferred_element_type=F32)
    ylo = y[:, :HALF]
    yhi = y[:, HALF:]

    slot = i % 2
    for k in range(TOPK):
        pltpu.make_async_copy(ys_ref.at[pl.ds(0, TM * ROW_TILE)], gbuf.at[slot, k], gsem.at[slot]).wait()
    w = w_ref[...]
    for k in range(TOPK):
        lo, hi = _unpack_pair(_load_tiles_as_rows(gbuf.at[slot, k], TM))
        wk = w[:, k:k + 1]
        ylo = ylo + wk * lo
        yhi = yhi + wk * hi
    out = _layer_norm(jnp.concatenate([ylo, yhi], axis=1), g_ref[...], b_ref[...])
    if not final:
        out_ref[...] = out
        return
    obuf[i % 2] = out
    for cond, cp in _final_copies(i, obuf, yp_ref, ys_out_ref, osem):
        @pl.when(cond)
        def _():
            cp.start()

    @pl.when(i == N_ROW_TILES - 1)
    def _():
        wait_step(i - 1)
        wait_step(i)


def _gather_call(final, pos_tiles, wsel, h, ys, wsg, wsu, wsd, g, b):
    vec = pl.BlockSpec((1, D), lambda i: (0, 0))
    once = dict(pipeline_mode=pl.Buffered(1))
    anyspec = pl.BlockSpec(memory_space=pl.ANY)
    scratch = [pltpu.VMEM((2, TOPK, TM * ROW_TILE, LANES), U32), pltpu.SemaphoreType.DMA((2,))]
    if final:
        out_specs = [anyspec, anyspec]
        out_shape = [jax.ShapeDtypeStruct((NB * SEQ, D), F32), jax.ShapeDtypeStruct((DEC, D), F32)]
        scratch += [pltpu.VMEM((2, TM, D), F32), pltpu.SemaphoreType.DMA((2,))]
    else:
        out_specs = pl.BlockSpec((TM, D), lambda i: (i, 0))
        out_shape = jax.ShapeDtypeStruct((T_ALL, D), F32)
    smem_tile = lambda f: pl.BlockSpec((1, 1, TM * TOPK), f, memory_space=pltpu.SMEM)
    return pl.pallas_call(
        functools.partial(_gather_kernel, final),
        grid=(N_ROW_TILES,),
        in_specs=[
            smem_tile(lambda i: (i, 0, 0)),
            smem_tile(lambda i: (jnp.minimum(i + 1, N_ROW_TILES - 1), 0, 0)),
            pl.BlockSpec((TM, TOPK), lambda i: (i, 0)),
            pl.BlockSpec((TM, D), lambda i: (i, 0)),
            anyspec,
            pl.BlockSpec((D, DE), lambda i: (0, 0), **once),
            pl.BlockSpec((D, DE), lambda i: (0, 0), **once),
            pl.BlockSpec((DE, D), lambda i: (0, 0), **once),
            vec,
            vec,
        ],
        out_specs=out_specs,
        out_shape=out_shape,
        scratch_shapes=scratch,
        compiler_params=_cparams(("arbitrary",)),
        name="moe_combine_final" if final else "moe_combine",
    )(pos_tiles, pos_tiles, wsel, h, ys, wsg, wsu, wsd, g, b)


def _moe_layer(layer, final, h, hp, w_router, router_bias, w_exp_gate, w_exp_up, w_exp_down,
               w_sh_gate, w_sh_up, w_sh_down, ln_g, ln_b):
    eidx_t, wsel, rank_t, counts = _router_call(h, w_router[layer], router_bias[layer])
    cnt = counts[:, 0].astype(I32)
    pos_t, te, gi, ne, nact, off = _plan_call(cnt, eidx_t, rank_t)
    pos_tiles = (pos_t * ROW_TILE).reshape(TOPK, N_ROW_TILES, TM).transpose(1, 0, 2).reshape(
        N_ROW_TILES, 1, TOPK * TM)
    xs = _dispatch_call(cnt, off, nact, pos_tiles, hp)
    ys = _experts_call(layer, te, gi, ne, nact, xs, w_exp_gate, w_exp_up, w_exp_down)
    return _gather_call(
        final, pos_tiles, wsel, h, ys,
        w_sh_gate[layer].astype(BF16), w_sh_up[layer].astype(BF16), w_sh_down[layer].astype(BF16),
        ln_g[layer][None, :], ln_b[layer][None, :])


def _ssm_prep_kernel(lr_ref, li_ref, ldt_ref, br_ref, bi_ref, abr_ref, abi_ref, bbr_ref, bbi_ref):
    lr = lr_ref[...]
    li = li_ref[...]
    dt = jnp.exp(ldt_ref[...])
    mag = jnp.exp(lr * dt)
    ab_re = mag * jnp.cos(li * dt)
    ab_im = mag * jnp.sin(li * dt)
    den = lr * lr + li * li
    nr = ab_re - 1.0
    ni = ab_im
    cr = (nr * lr + ni * li) / den
    ci = (ni * lr - nr * li) / den
    br = br_ref[...]
    bi = bi_ref[...]
    abr_ref[...] = ab_re
    abi_ref[...] = ab_im
    bbr_ref[...] = cr * br - ci * bi
    bbi_ref[...] = cr * bi + ci * br


def _ssm_prep_call(lam_re, lam_im, log_dt, b_re, b_im):
    wide = (NG, GC * NS)
    lr = jnp.tile(lam_re, (1, GC))
    li = jnp.tile(lam_im, (1, GC))
    ldt = jnp.broadcast_to(log_dt[:, None], wide)
    br = jnp.transpose(b_re, (0, 2, 1)).reshape(wide)
    bi = jnp.transpose(b_im, (0, 2, 1)).reshape(wide)
    sds = jax.ShapeDtypeStruct(wide, F32)
    return pl.pallas_call(
        _ssm_prep_kernel, out_shape=[sds, sds, sds, sds], name="ssm_prep",
        compiler_params=pltpu.CompilerParams(vmem_limit_bytes=VMEM_LIMIT),
    )(lr, li, ldt, br, bi)


def _cmul(ar, ai, xr, xi):
    return ar * xr - ai * xi, ar * xi + ai * xr


NSEG = 8
SEGL = LP // NSEG
STEP_UNROLL = 4
MOVE_UNROLL = 8
assert LP == NSEG * SEGL and SEGL % STEP_UNROLL == 0 and SEGL % MOVE_UNROLL == 0 and PAD_FRONT < SEGL
assert STEP_UNROLL % 2 == 0


def _ssm_p_kernel(u_ref, wb_ref, wc_ref, a_ref, d_ref, z_ref, st_ref, u_scr, s_scr, sb_scr, y_scr):
    def interleave(it, carry):
        for q in range(MOVE_UNROLL):
            t = it * MOVE_UNROLL + q
            u_scr[pl.ds(pl.multiple_of(t * NSEG, NSEG), NSEG), :] = u_ref[pl.ds(t, NSEG, stride=SEGL), :]
        return carry

    lax.fori_loop(0, SEGL // MOVE_UNROLL, interleave, 0)
    row = lax.broadcasted_iota(I32, (LP, LANES), 0)
    is_pad = jnp.logical_and(row % NSEG == 0, row // NSEG < PAD_FRONT)
    u = jnp.where(is_pad, 0.0, u_scr[...])
    s_scr[...] = jnp.dot(u.astype(BF16), wb_ref[0], preferred_element_type=F32)

    ar = a_ref[0, 0:1, :]
    ai = a_ref[0, 1:2, :]

    def group(t):
        r0 = pl.multiple_of(t * NSEG, NSEG)
        return pl.ds(r0, NSEG)

    def local_scan(it, carry):
        sr, si = carry
        for q in range(STEP_UNROLL):
            g = group(it * STEP_UNROLL + q)
            tr, ti = _cmul(ar, ai, sr, si)
            sr = tr + s_scr[g, 0:CH_STATE]
            si = ti + s_scr[g, CH_STATE:]
            s_scr[g, 0:CH_STATE] = sr
            s_scr[g, CH_STATE:] = si
        return sr, si

    zero8 = jnp.zeros((NSEG, CH_STATE), F32)
    er, ei = lax.fori_loop(0, SEGL // STEP_UNROLL, local_scan, (zero8, zero8))

    pr, pi = ar, ai
    acc = None
    bits = SEGL
    while bits:
        if bits & 1:
            acc = (pr, pi) if acc is None else _cmul(pr, pi, *acc)
        bits >>= 1
        if bits:
            pr, pi = _cmul(pr, pi, pr, pi)
    alr, ali = acc
    cr = jnp.zeros((1, CH_STATE), F32)
    ci = jnp.zeros((1, CH_STATE), F32)
    ins_r, ins_i = [], []
    for j in range(NSEG):
        ins_r.append(cr)
        ins_i.append(ci)
        tr, ti = _cmul(alr, ali, cr, ci)
        cr = tr + er[j:j + 1, :]
        ci = ti + ei[j:j + 1, :]
    st_ref[0, 0, :, 0:CH_STATE] = cr
    st_ref[0, 0, :, CH_STATE:] = ci

    def fixup(it, carry):
        dr, di = carry
        for q in range(0, STEP_UNROLL, 2):
            t0 = it * STEP_UNROLL + q
            rows_r, rows_i = [], []
            for t in (t0, t0 + 1):
                g = group(t)
                dr, di = _cmul(ar, ai, dr, di)
                rows_r.append(s_scr[g, 0:CH_STATE] + dr)
                rows_i.append(s_scr[g, CH_STATE:] + di)
            pair = pl.ds(pl.multiple_of(t0 * NSEG, 2 * NSEG), 2 * NSEG)
            sb_scr[pair, 0:CH_STATE] = jnp.concatenate(rows_r, axis=0).astype(BF16)
            sb_scr[pair, CH_STATE:] = jnp.concatenate(rows_i, axis=0).astype(BF16)
        return dr, di

    lax.fori_loop(0, SEGL // STEP_UNROLL, fixup,
                  (jnp.concatenate(ins_r, axis=0), jnp.concatenate(ins_i, axis=0)))

    y_scr[...] = jnp.dot(sb_scr[...], wc_ref[0], preferred_element_type=F32) + d_ref[0] * u

    def deinterleave(it, carry):
        for q in range(MOVE_UNROLL):
            t = it * MOVE_UNROLL + q
            u_scr[pl.ds(t, NSEG, stride=SEGL), :] = y_scr[pl.ds(pl.multiple_of(t * NSEG, NSEG), NSEG), :]
        return carry

    lax.fori_loop(0, SEGL // MOVE_UNROLL, deinterleave, 0)
    z_ref[...] = jax.nn.gelu(u_scr[...]).astype(BF16)


def _ssm_p_call(h, wb_bf, wc_bf, a_tab, d_tab):
    return pl.pallas_call(
        _ssm_p_kernel,
        grid=(NB, NCHUNK),
        in_specs=[
            pl.BlockSpec((LP, LANES), lambda b, k: (b, k)),
            pl.BlockSpec((1, LANES, 2 * CH_STATE), lambda b, k: (k, 0, 0)),
            pl.BlockSpec((1, 2 * CH_STATE, LANES), lambda b, k: (k, 0, 0)),
            pl.BlockSpec((1, 2, CH_STATE), lambda b, k: (k, 0, 0)),
            pl.BlockSpec((1, 1, LANES), lambda b, k: (k, 0, 0)),
        ],
        out_specs=[
            pl.BlockSpec((LP, LANES), lambda b, k: (b, k)),
            pl.BlockSpec((1, 1, 1, 2 * CH_STATE), lambda b, k: (b, k, 0, 0)),
        ],
        out_shape=[
            jax.ShapeDtypeStruct((T_PROMPT, D), BF16),
            jax.ShapeDtypeStruct((NB, NCHUNK, 1, 2 * CH_STATE), F32),
        ],
        scratch_shapes=[pltpu.VMEM((LP, LANES), F32), pltpu.VMEM((LP, 2 * CH_STATE), F32),
                        pltpu.VMEM((LP, 2 * CH_STATE), BF16), pltpu.VMEM((LP, LANES), F32)],
        compiler_params=_cparams(("parallel", "parallel")),
        name="ssm_prompt",
    )(h, wb_bf, wc_bf, a_tab, d_tab)


def _ssm_s_kernel(u_ref, sr_ref, si_ref, wb_ref, wc_ref, a_ref, d_ref, z_ref, nr_ref, ni_ref):
    u = u_ref[...]
    bu = jnp.dot(u, wb_ref[0], preferred_element_type=F32, precision=lax.Precision.HIGHEST)
    ar = a_ref[0, 0:1, :]
    ai = a_ref[0, 1:2, :]
    tr, ti = _cmul(ar, ai, sr_ref[...], si_ref[...])
    nr = tr + bu[:, 0:CH_STATE]
    ni = ti + bu[:, CH_STATE:]
    nr_ref[...] = nr
    ni_ref[...] = ni
    s = jnp.concatenate([nr, ni], axis=1).astype(BF16)
    y = jnp.dot(s, wc_ref[0], preferred_element_type=F32) + d_ref[0] * u
    z_ref[...] = jax.nn.gelu(y).astype(BF16)


def _ssm_s_call(h, s0r, s0i, wb_f32, wc_bf, a_tab, d_tab):
    st = pl.BlockSpec((DEC, CH_STATE), lambda k: (0, k))
    return pl.pallas_call(
        _ssm_s_kernel,
        grid=(NCHUNK,),
        in_specs=[
            pl.BlockSpec((DEC, LANES), lambda k: (T_PROMPT // DEC, k)),
            st,
            st,
            pl.BlockSpec((1, LANES, 2 * CH_STATE), lambda k: (k, 0, 0)),
            pl.BlockSpec((1, 2 * CH_STATE, LANES), lambda k: (k, 0, 0)),
            pl.BlockSpec((1, 2, CH_STATE), lambda k: (k, 0, 0)),
            pl.BlockSpec((1, 1, LANES), lambda k: (k, 0, 0)),
        ],
        out_specs=[pl.BlockSpec((DEC, LANES), lambda k: (0, k)), st, st],
        out_shape=[
            jax.ShapeDtypeStruct((DEC, D), BF16),
            jax.ShapeDtypeStruct((DEC, NG * NS), F32),
            jax.ShapeDtypeStruct((DEC, NG * NS), F32),
        ],
        compiler_params=_cparams(("parallel",)),
        name="ssm_sample",
    )(h, s0r, s0i, wb_f32, wc_bf, a_tab, d_tab)


def _glu_ln_kernel(z_ref, zt_ref, w_ref, bg_ref, h_ref, g_ref, b_ref, out_ref, hp_ref):
    z = _prompt_or_tail(z_ref, zt_ref)
    acc = jnp.dot(z, w_ref[...], preferred_element_type=F32) + bg_ref[...]
    m = acc[:, :D] * _sigmoid(acc[:, D:])
    out = _layer_norm(ALPHA * h_ref[...] + m, g_ref[...], b_ref[...])
    out_ref[...] = out
    _store_rows_as_tiles(hp_ref, _pack_pair(out[:, :HALF], out[:, HALF:]))


def _glu_ln_call(z_p, z_tail, w_bf, bg, h, g, b):
    vec = pl.BlockSpec((1, D), lambda i: (0, 0))
    return pl.pallas_call(
        _glu_ln_kernel,
        grid=(N_ROW_TILES,),
        in_specs=[
            pl.BlockSpec((TM, D), _PROMPT_TILE),
            pl.BlockSpec((TM, D), lambda i: (0, 0)),
            pl.BlockSpec((D, 2 * D), lambda i: (0, 0), pipeline_mode=pl.Buffered(1)),
            pl.BlockSpec((1, 2 * D), lambda i: (0, 0)),
            pl.BlockSpec((TM, D), lambda i: (i, 0)),
            vec,
            vec,
        ],
        out_specs=[pl.BlockSpec((TM, D), lambda i: (i, 0)),
                   pl.BlockSpec((TM * ROW_TILE, LANES), lambda i: (i, 0))],
        out_shape=[jax.ShapeDtypeStruct((T_ALL, D), F32),
                   jax.ShapeDtypeStruct((T_ALL * ROW_TILE, LANES), U32)],
        compiler_params=_cparams(("parallel",)),
        name="glu_ln",
    )(z_p, z_tail, w_bf, bg, h, g, b)


GROUPS_PER_CHUNK = LANES // GC


def _block_diag_in(t):
    t3 = t.reshape(NCHUNK, LANES, NS)
    same = (np.arange(LANES)[:, None] // GC) == (np.arange(CH_STATE)[None, :] // NS)
    return jnp.tile(t3, (1, 1, GROUPS_PER_CHUNK)) * jnp.asarray(same, t.dtype)


def _block_diag_out(t):
    t3 = jnp.swapaxes(t, 1, 2).reshape(NCHUNK, CH_STATE, GC)
    same = (np.arange(CH_STATE)[:, None] // NS) == (np.arange(LANES)[None, :] // GC)
    return jnp.tile(t3, (1, 1, GROUPS_PER_CHUNK)) * jnp.asarray(same, t.dtype)


def kernel(x_prompt, x_sample, cache_k, cache_v, state_ssm_re, state_ssm_im, meta_tokens, w_qkv, b_qkv, attn_sinks, w_o, b_o, ssm_lam_re, ssm_lam_im, ssm_log_dt, ssm_b_re, ssm_b_im, ssm_c_re, ssm_c_im, ssm_d, w_glu, b_glu, ln_mix_g, ln_mix_b, w_router, router_bias, w_exp_gate, w_exp_up, w_exp_down, w_sh_gate, w_sh_up, w_sh_down, ln_ffn_g, ln_ffn_b):
    moe_w = (w_router, router_bias, w_exp_gate, w_exp_up, w_exp_down, w_sh_gate, w_sh_up, w_sh_down,
             ln_ffn_g, ln_ffn_b)

    front = jnp.concatenate([jnp.zeros((PAD_FRONT, D), F32), meta_tokens], axis=0)
    pieces = []
    for b in range(NB):
        pieces += [front, x_prompt[b]]
    h = jnp.concatenate(pieces + [x_sample.reshape(DEC, D)], axis=0)

    rc, rs1, rs2 = _rope_tables()
    q, k, v = _qkv_call(h, w_qkv[0].astype(BF16), b_qkv[0][None, :], rc, rs1, rs2)
    o_p = _attn_p_call(attn_sinks[0], q, k, v)
    q3 = q[T_PROMPT:].astype(F32).reshape(DEC, NH, HD)
    o_s, ck_new, cv_new = _attn_s_call(
        attn_sinks[0][:, None], q3, k, v,
        cache_k[0].reshape(DEC, BLK, NKV * HD), cache_v[0].reshape(DEC, BLK, NKV * HD))
    o_tail = _tail_tile(o_p, o_s.reshape(DEC, D).astype(BF16))
    h, hp = _oproj_ln_call(o_p, o_tail, w_o[0].astype(BF16), b_o[0][None, :], h,
                           ln_mix_g[0][None, :], ln_mix_b[0][None, :])
    h = _moe_layer(0, False, h, hp, *moe_w)

    def last_window(t):
        rows = [t[(b + 1) * LP - BLK:(b + 1) * LP] for b in range(NB)]
        return jnp.stack(rows).reshape(NB, BLK, NKV, HD)

    kp = last_window(k)
    vp = last_window(v)

    ab_re, ab_im, bb_re, bb_im = _ssm_prep_call(
        ssm_lam_re[0], ssm_lam_im[0], ssm_log_dt[0], ssm_b_re[0], ssm_b_im[0])
    wb = jnp.concatenate([_block_diag_in(bb_re), _block_diag_in(bb_im)], axis=2)
    wc = jnp.concatenate([_block_diag_out(ssm_c_re[0]), -_block_diag_out(ssm_c_im[0])], axis=1)
    wc_bf = wc.astype(BF16)
    a_tab = jnp.stack([ab_re[:, :NS].reshape(NCHUNK, CH_STATE),
                       ab_im[:, :NS].reshape(NCHUNK, CH_STATE)], axis=1)
    d_tab = ssm_d[0].reshape(NCHUNK, 1, LANES)
    z_p, st_p = _ssm_p_call(h, wb.astype(BF16), wc_bf, a_tab, d_tab)
    z_s, sr_new, si_new = _ssm_s_call(
        h, state_ssm_re[0].reshape(DEC, NG * NS), state_ssm_im[0].reshape(DEC, NG * NS),
        wb, wc_bf, a_tab, d_tab)
    h, hp = _glu_ln_call(z_p, _tail_tile(z_p, z_s), w_glu[0].astype(BF16), b_glu[0][None, :], h,
                         ln_mix_g[1][None, :], ln_mix_b[1][None, :])
    y_prompt, y_sample = _moe_layer(1, True, h, hp, *moe_w)
    y_prompt = y_prompt.reshape(NB, SEQ, D)
    y_sample = y_sample.reshape(DEC, 1, D)
    st_p = st_p.reshape(NB, NCHUNK, 2, 8, NS)
    rp = st_p[:, :, 0].reshape(NB, NG, NS)
    ip = st_p[:, :, 1].reshape(NB, NG, NS)
    return (y_prompt, y_sample,
            kp[None], vp[None],
            ck_new.reshape(1, DEC, BLK, NKV, HD), cv_new.reshape(1, DEC, BLK, NKV, HD),
            rp[None], ip[None],
            sr_new.reshape(1, DEC, NG, NS), si_new.reshape(1, DEC, NG, NS))
```

```python
import functools
import math

import jax
import jax.numpy as jnp
import numpy as np
from jax import lax
from jax.experimental import pallas as pl
from jax.experimental.pallas import tpu as pltpu

F32 = jnp.float32
BF16 = jnp.bfloat16
I32 = jnp.int32
U32 = jnp.uint32

D = 2048
HALF = D // 2
NB = 4
N_META = 16
SEQ = 2048
L = N_META + SEQ
BLK = 128
PAD_FRONT = (-L) % BLK
LP = L + PAD_FRONT
NBLK = LP // BLK
T_PROMPT = NB * LP
DEC = 128
T_ALL = T_PROMPT + DEC
PAST_LEN = 8192
HD = 64
NH = 32
NKV = 4
QPK = NH // NKV
QKV = (NH + 2 * NKV) * HD
QK_COLS = (NH + NKV) * HD
ROT = HD // 4
ROT_HALF = ROT // 2
ROPE_THETA = 500000.0
NG = 128
GC = 16
NS = 64
NCHUNK = 16
CH_STATE = 8 * NS
NE = 64
TOPK = 8
NEG = 8
PER_GRP = NE // NEG
TOPG = 4
DE = 512
ROUTED_SCALE = 2.5
DEPTH = 2
ALPHA = (2 * DEPTH) ** 0.25
LN_EPS = 1e-5

V7X_VMEM_BYTES = 64 * 1024 * 1024
VMEM_LIMIT = 56 * 1024 * 1024
LANES = 128

TM = 384
N_ROW_TILES = T_ALL // TM
TAIL_START = (N_ROW_TILES - 1) * TM
TM_E = 256
N_PAIRS = T_ALL * TOPK
NT_E = -(-N_PAIRS // TM_E) + NE
R_ROWS = NT_E * TM_E
assert T_ALL % TM == 0 and TAIL_START <= T_PROMPT


def _cparams(sem):
    return pltpu.CompilerParams(dimension_semantics=sem, vmem_limit_bytes=VMEM_LIMIT)


def _sigmoid(x):
    return 1.0 / (1.0 + jnp.exp(-x))


def _layer_norm(y, g, b):
    mu = jnp.mean(y, axis=-1, keepdims=True)
    yc = y - mu
    var = jnp.mean(yc * yc, axis=-1, keepdims=True)
    return yc * lax.rsqrt(var + LN_EPS) * g + b


def _pack_pair(lo, hi):
    lo_b = lax.bitcast_convert_type(lo.astype(BF16).astype(F32), U32) >> 16
    hi_b = lax.bitcast_convert_type(hi.astype(BF16).astype(F32), U32) & jnp.uint32(0xFFFF0000)
    return lo_b | hi_b


def _unpack_pair(w):
    lo = lax.bitcast_convert_type(w << 16, F32)
    hi = lax.bitcast_convert_type(w & jnp.uint32(0xFFFF0000), F32)
    return lo, hi


ROW_TILE = HALF // LANES
assert ROW_TILE == 8


def _store_rows_as_tiles(ref, rows):
    n = rows.shape[0]
    for j in range(ROW_TILE):
        ref[pl.ds(j, n, stride=ROW_TILE), :] = rows[:, j * LANES:(j + 1) * LANES]


def _load_tiles_as_rows(ref, n):
    return jnp.concatenate([ref[pl.ds(j, n, stride=ROW_TILE), :] for j in range(ROW_TILE)], axis=1)


def _qkv_kernel(x_ref, w_ref, b_ref, c_ref, s1_ref, s2_ref, q_ref, k_ref, v_ref):
    xb = x_ref[...].astype(BF16)
    acc = jnp.dot(xb, w_ref[...], preferred_element_type=F32) + b_ref[...]
    c = c_ref[...]
    s1 = s1_ref[...]
    s2 = s2_ref[...]
    for j in range(QK_COLS // LANES):
        blk = acc[:, j * LANES:(j + 1) * LANES]
        r = (blk * c + pltpu.roll(blk, LANES - ROT_HALF, axis=1) * s1
             + pltpu.roll(blk, ROT_HALF, axis=1) * s2)
        if j < D // LANES:
            q_ref[:, j * LANES:(j + 1) * LANES] = (r * (1.0 / math.sqrt(HD))).astype(BF16)
        else:
            jj = j - D // LANES
            k_ref[:, jj * LANES:(jj + 1) * LANES] = r
    v_ref[...] = acc[:, QK_COLS:]


def _qkv_call(x, w_bf, b, rc, rs1, rs2):
    return pl.pallas_call(
        _qkv_kernel,
        grid=(T_ALL // TM,),
        in_specs=[
            pl.BlockSpec((TM, D), lambda i: (i, 0)),
            pl.BlockSpec((D, QKV), lambda i: (0, 0)),
            pl.BlockSpec((1, QKV), lambda i: (0, 0)),
            pl.BlockSpec((TM, LANES), lambda i: (i, 0)),
            pl.BlockSpec((TM, LANES), lambda i: (i, 0)),
            pl.BlockSpec((TM, LANES), lambda i: (i, 0)),
        ],
        out_specs=[
            pl.BlockSpec((TM, D), lambda i: (i, 0)),
            pl.BlockSpec((TM, NKV * HD), lambda i: (i, 0)),
            pl.BlockSpec((TM, NKV * HD), lambda i: (i, 0)),
        ],
        out_shape=[
            jax.ShapeDtypeStruct((T_ALL, D), BF16),
            jax.ShapeDtypeStruct((T_ALL, NKV * HD), F32),
            jax.ShapeDtypeStruct((T_ALL, NKV * HD), F32),
        ],
        compiler_params=_cparams(("parallel",)),
        name="qkv_rope",
    )(x, w_bf, b, rc, rs1, rs2)


def _rope_tables():
    pos_p = jnp.maximum(jnp.arange(LP, dtype=I32) - PAD_FRONT, 0)
    pos = jnp.concatenate([pos_p, jnp.full((8,), PAST_LEN, I32)]).astype(F32)
    inv_freq = ROPE_THETA ** (-jnp.arange(0, ROT, 2, dtype=F32) / ROT)
    ang = pos[:, None] * inv_freq[None, :]
    cos = jnp.cos(ang)
    sin = jnp.sin(ang)
    lane = np.arange(LANES) % HD
    freq = np.arange(ROT_HALF)[:, None]
    first = ((lane[None, :] == freq) & (lane[None, :] < ROT_HALF)).astype(np.float32)
    second = ((lane[None, :] - ROT_HALF == freq) & (lane[None, :] < ROT)).astype(np.float32)
    rest = (lane >= ROT).astype(np.float32)[None, :]
    place = functools.partial(jnp.dot, precision=lax.Precision.HIGHEST)
    c = place(cos, jnp.asarray(first + second)) + jnp.asarray(rest)
    s1 = place(sin, jnp.asarray(-first))
    s2 = place(sin, jnp.asarray(second))

    def all_rows(t):
        return jnp.concatenate([t[:LP]] * NB + [t[LP:]] * (DEC // 8), axis=0)

    return all_rows(c), all_rows(s1), all_rows(s2)


def _attn_p_kernel(sink_ref, q_ref, kp_ref, kc_ref, vp_ref, vc_ref, o_ref):
    j = pl.program_id(0) % NBLK
    keys = jnp.concatenate([kp_ref[...], kc_ref[...]], axis=0).astype(BF16)
    vals = jnp.concatenate([vp_ref[...], vc_ref[...]], axis=0).astype(BF16)
    r = lax.broadcasted_iota(I32, (BLK, 2 * BLK), 0)
    c = lax.broadcasted_iota(I32, (BLK, 2 * BLK), 1)
    dist = BLK + r - c
    kpos = (j - 1) * BLK - PAD_FRONT + c
    mask = (dist >= 0) & (dist <= BLK) & (kpos >= 0)
    for g in range(NKV):
        kg = keys[:, g * HD:(g + 1) * HD]
        vg = vals[:, g * HD:(g + 1) * HD]
        heads = range(g * QPK, (g + 1) * QPK)
        scores = [lax.dot_general(q_ref[:, h * HD:(h + 1) * HD], kg, (((1,), (1,)), ((), ())),
                                  preferred_element_type=F32) for h in heads]
        probs, rdens = [], []
        for h, s in zip(heads, scores):
            s = jnp.where(mask, s, -jnp.inf)
            sk = sink_ref[h]
            m = jnp.maximum(jnp.max(s, axis=1, keepdims=True), sk)
            p = jnp.exp(s - m)
            rdens.append(1.0 / (jnp.sum(p, axis=1, keepdims=True) + jnp.exp(sk - m)))
            probs.append(p.astype(BF16))
        for h, p, rden in zip(heads, probs, rdens):
            oh = jnp.dot(p, vg, preferred_element_type=F32) * rden
            o_ref[:, h * HD:(h + 1) * HD] = oh.astype(BF16)


def _attn_p_call(sinks, q, k, v):
    prev = lambda i: (jnp.where(i % NBLK == 0, i, i - 1), 0)
    cur = lambda i: (i, 0)
    return pl.pallas_call(
        _attn_p_kernel,
        grid=(NB * NBLK,),
        in_specs=[
            pl.BlockSpec(memory_space=pltpu.SMEM),
            pl.BlockSpec((BLK, D), cur),
            pl.BlockSpec((BLK, NKV * HD), prev),
            pl.BlockSpec((BLK, NKV * HD), cur),
            pl.BlockSpec((BLK, NKV * HD), prev),
            pl.BlockSpec((BLK, NKV * HD), cur),
        ],
        out_specs=pl.BlockSpec((BLK, D), cur),
        out_shape=jax.ShapeDtypeStruct((T_PROMPT, D), BF16),
        compiler_params=_cparams(("parallel",)),
        name="attn_prompt",
    )(sinks, q, k, k, v, v)


SEQ_PER_STEP = 16
SEQ_UNROLL = 4


def _attn_s_kernel(sink_ref, q_ref, kn_ref, vn_ref, ck_ref, cv_ref, o_ref, cko_ref, cvo_ref):
    row = lax.broadcasted_iota(I32, (BLK, NKV * HD), 0)
    hrow = lax.broadcasted_iota(I32, (NH, NKV * HD), 0) // QPK
    hlane = lax.broadcasted_iota(I32, (NH, NKV * HD), 1) // HD
    own = hrow == hlane
    sk = sink_ref[...]

    def score_stage(s):
        kn = kn_ref[pl.ds(s, 1), :]
        knr = kn.astype(BF16).astype(F32)
        q = q_ref[s].astype(BF16)
        qe = jnp.where(own, jnp.concatenate([q] * NKV, axis=1), jnp.zeros((), BF16))
        sc = lax.dot_general(qe, ck_ref[s].astype(BF16), (((1,), (1,)), ((), ())),
                             preferred_element_type=F32)
        sn = jnp.sum(qe.astype(F32) * knr, axis=1, keepdims=True)
        return sc, sn

    def softmax_stage(sc, sn):
        m = jnp.maximum(jnp.maximum(jnp.max(sc, axis=1, keepdims=True), sn), sk)
        p = jnp.exp(sc - m)
        pn = jnp.exp(sn - m)
        rden = 1.0 / (jnp.sum(p, axis=1, keepdims=True) + pn + jnp.exp(sk - m))
        return p.astype(BF16), pn.astype(BF16).astype(F32), rden

    def value_stage(s, p, pn, rden):
        vn = vn_ref[pl.ds(s, 1), :]
        vnr = vn.astype(BF16).astype(F32)
        of = jnp.dot(p, cv_ref[s].astype(BF16), preferred_element_type=F32)
        of = jnp.where(own, of + pn * vnr, 0.0)
        og = of[:, 0:HD]
        for g in range(1, NKV):
            og = og + of[:, g * HD:(g + 1) * HD]
        o_ref[s] = og * rden
        cko_ref[s] = jnp.where(row == BLK - 1, kn_ref[pl.ds(s, 1), :], pltpu.roll(ck_ref[s], BLK - 1, axis=0))
        cvo_ref[s] = jnp.where(row == BLK - 1, vn, pltpu.roll(cv_ref[s], BLK - 1, axis=0))

    def body(it, carry):
        seqs = [it * SEQ_UNROLL + u for u in range(SEQ_UNROLL)]
        staged = [score_stage(s) for s in seqs]
        soft = [softmax_stage(*st) for st in staged]
        for s, sm in zip(seqs, soft):
            value_stage(s, *sm)
        return carry

    lax.fori_loop(0, SEQ_PER_STEP // SEQ_UNROLL, body, 0)


def _attn_s_call(sinks_col, q3, k, v, cache_k, cache_v):
    sp = SEQ_PER_STEP
    kv_off = T_PROMPT // sp
    return pl.pallas_call(
        _attn_s_kernel,
        grid=(DEC // sp,),
        in_specs=[
            pl.BlockSpec((NH, 1), lambda i: (0, 0)),
            pl.BlockSpec((sp, NH, HD), lambda i: (i, 0, 0)),
            pl.BlockSpec((sp, NKV * HD), lambda i: (kv_off + i, 0)),
            pl.BlockSpec((sp, NKV * HD), lambda i: (kv_off + i, 0)),
            pl.BlockSpec((sp, BLK, NKV * HD), lambda i: (i, 0, 0)),
            pl.BlockSpec((sp, BLK, NKV * HD), lambda i: (i, 0, 0)),
        ],
        out_specs=[
            pl.BlockSpec((sp, NH, HD), lambda i: (i, 0, 0)),
            pl.BlockSpec((sp, BLK, NKV * HD), lambda i: (i, 0, 0)),
            pl.BlockSpec((sp, BLK, NKV * HD), lambda i: (i, 0, 0)),
        ],
        out_shape=[
            jax.ShapeDtypeStruct((DEC, NH, HD), F32),
            jax.ShapeDtypeStruct((DEC, BLK, NKV * HD), F32),
            jax.ShapeDtypeStruct((DEC, BLK, NKV * HD), F32),
        ],
        compiler_params=_cparams(("parallel",)),
        name="attn_sample",
    )(sinks_col, q3, k, v, cache_k, cache_v)


def _prompt_or_tail(x_ref, tail_ref):
    return jnp.where(pl.program_id(0) == N_ROW_TILES - 1, tail_ref[...], x_ref[...])


def _tail_tile(x_prompt_rows, x_sample_rows):
    return jnp.concatenate([x_prompt_rows[TAIL_START:], x_sample_rows], axis=0)


_PROMPT_TILE = lambda i: (jnp.minimum(i, N_ROW_TILES - 2), 0)


def _oproj_ln_kernel(o_ref, ot_ref, w_ref, bo_ref, h_ref, g_ref, b_ref, out_ref, hp_ref):
    o = _prompt_or_tail(o_ref, ot_ref)
    m = jnp.dot(o, w_ref[...], preferred_element_type=F32) + bo_ref[...]
    out = _layer_norm(ALPHA * h_ref[...] + m, g_ref[...], b_ref[...])
    out_ref[...] = out
    _store_rows_as_tiles(hp_ref, _pack_pair(out[:, :HALF], out[:, HALF:]))


def _oproj_ln_call(o_p, o_tail, w_bf, bo, h, g, b):
    vec = pl.BlockSpec((1, D), lambda i: (0, 0))
    return pl.pallas_call(
        _oproj_ln_kernel,
        grid=(N_ROW_TILES,),
        in_specs=[
            pl.BlockSpec((TM, D), _PROMPT_TILE),
            pl.BlockSpec((TM, D), lambda i: (0, 0)),
            pl.BlockSpec((D, D), lambda i: (0, 0)),
            vec,
            pl.BlockSpec((TM, D), lambda i: (i, 0)),
            vec,
            vec,
        ],
        out_specs=[pl.BlockSpec((TM, D), lambda i: (i, 0)),
                   pl.BlockSpec((TM * ROW_TILE, LANES), lambda i: (i, 0))],
        out_shape=[jax.ShapeDtypeStruct((T_ALL, D), F32),
                   jax.ShapeDtypeStruct((T_ALL * ROW_TILE, LANES), U32)],
        compiler_params=_cparams(("parallel",)),
        name="oproj_ln",
    )(o_p, o_tail, w_bf, bo, h, g, b)


def _router_kernel(h_ref, wr_ref, rb_ref, eidx_ref, wsel_ref, rank_ref, cnt_ref, carry_ref):
    i = pl.program_id(0)

    @pl.when(i == 0)
    def _():
        carry_ref[...] = jnp.zeros_like(carry_ref)

    h = h_ref[...]
    w = wr_ref[...]
    h_hi = h.astype(BF16)
    h_lo = (h - h_hi.astype(F32)).astype(BF16)
    w_hi = w.astype(BF16)
    w_lo = (w - w_hi.astype(F32)).astype(BF16)
    logits = (jnp.dot(h_hi, w_hi, preferred_element_type=F32)
              + (jnp.dot(h_hi, w_lo, preferred_element_type=F32)
                 + jnp.dot(h_lo, w_hi, preferred_element_type=F32)))
    scores = _sigmoid(logits.T[0:NE, :])
    biased = scores + rb_ref[...]
    ninf = -jnp.inf
    sub = lax.broadcasted_iota(I32, (PER_GRP, TM), 0).astype(F32)
    sc_g = [scores[g * PER_GRP:(g + 1) * PER_GRP, :] for g in range(NEG)]
    b_g = [biased[g * PER_GRP:(g + 1) * PER_GRP, :] for g in range(NEG)]
    e_g = [sub + float(g * PER_GRP) for g in range(NEG)]

    def smax(x):
        return jnp.max(x, axis=0, keepdims=True)

    def smin(x):
        return jnp.min(x, axis=0, keepdims=True)

    gs = []
    for g in range(NEG):
        m1 = smax(b_g[g])
        i1 = smin(jnp.where(b_g[g] == m1, sub, float(PER_GRP)))
        m2 = smax(jnp.where(sub == i1, ninf, b_g[g]))
        gs.append(m1 + m2)

    work = []
    for g in range(NEG):
        beaten = jnp.zeros((1, TM), F32)
        for o in range(NEG):
            if o != g:
                wins = (gs[o] >= gs[g]) if o < g else (gs[o] > gs[g])
                beaten = beaten + jnp.where(wins, 1.0, 0.0)
        work.append(jnp.where(beaten < float(TOPG), b_g[g], ninf))

    idx_rows, w_rows = [], []
    onehot = [jnp.zeros((PER_GRP, TM), F32) for _ in range(NEG)]
    for _ in range(TOPK):
        m = smax(work[0])
        for g in range(1, NEG):
            m = jnp.maximum(m, smax(work[g]))
        ik = smin(jnp.where(work[0] == m, e_g[0], float(NE)))
        for g in range(1, NEG):
            ik = jnp.minimum(ik, smin(jnp.where(work[g] == m, e_g[g], float(NE))))
        wk = jnp.zeros((1, TM), F32)
        for g in range(NEG):
            hit = e_g[g] == ik
            wk = wk + jnp.sum(jnp.where(hit, sc_g[g], 0.0), axis=0, keepdims=True)
            onehot[g] = jnp.where(hit, 1.0, onehot[g])
            work[g] = jnp.where(hit, ninf, work[g])
        idx_rows.append(ik)
        w_rows.append(wk)
    wsum = w_rows[0]
    for wk in w_rows[1:]:
        wsum = wsum + wk

    rr = lax.broadcasted_iota(I32, (TM, TM), 0)
    cc = lax.broadcasted_iota(I32, (TM, TM), 1)
    tri = jnp.where(rr < cc, 1.0, 0.0).astype(BF16)
    oh = jnp.concatenate(onehot, axis=0)
    prefix = jnp.dot(oh.astype(BF16), tri, preferred_element_type=F32) + carry_ref[...]
    carry_ref[...] = carry_ref[...] + jnp.sum(oh, axis=1, keepdims=True)
    cnt_ref[...] = carry_ref[...]

    rank_rows = []
    for k in range(TOPK):
        rk = jnp.zeros((1, TM), F32)
        for g in range(NEG):
            pg = prefix[g * PER_GRP:(g + 1) * PER_GRP, :]
            rk = rk + jnp.sum(jnp.where(e_g[g] == idx_rows[k], pg, 0.0), axis=0, keepdims=True)
        rank_rows.append(rk)
    eidx_ref[...] = jnp.concatenate(idx_rows, axis=0).astype(I32)
    rank_ref[...] = jnp.concatenate(rank_rows, axis=0).astype(I32)
    w_t = jnp.concatenate([wk / wsum * ROUTED_SCALE for wk in w_rows]
                          + [jnp.zeros((LANES - TOPK, TM), F32)], axis=0)
    wsel_ref[...] = w_t.T[:, 0:TOPK]


def _router_call(h, w_router, router_bias):
    tk = pl.BlockSpec((TM, TOPK), lambda i: (i, 0))
    kt = pl.BlockSpec((TOPK, TM), lambda i: (0, i))
    return pl.pallas_call(
        _router_kernel,
        grid=(T_ALL // TM,),
        in_specs=[
            pl.BlockSpec((TM, D), lambda i: (i, 0)),
            pl.BlockSpec((D, LANES), lambda i: (0, 0)),
            pl.BlockSpec((NE, 1), lambda i: (0, 0)),
        ],
        out_specs=[kt, tk, kt, pl.BlockSpec((NE, 1), lambda i: (0, 0))],
        out_shape=[
            jax.ShapeDtypeStruct((TOPK, T_ALL), I32),
            jax.ShapeDtypeStruct((T_ALL, TOPK), F32),
            jax.ShapeDtypeStruct((TOPK, T_ALL), I32),
            jax.ShapeDtypeStruct((NE, 1), F32),
        ],
        scratch_shapes=[pltpu.VMEM((NE, 1), F32)],
        compiler_params=_cparams(("arbitrary",)),
        name="router",
    )(h, jnp.pad(w_router, ((0, 0), (0, LANES - NE))), router_bias[:, None])


def _tiles_of(cnt):
    return (cnt + TM_E - 1) // TM_E


def _plan_kernel(cnt_ref, eidx_ref, rank_ref, pos_ref, te_ref, gi_ref, ne_ref, na_ref, off_s):
    def offsets(e, run):
        off_s[e] = run
        return run + _tiles_of(cnt_ref[e]) * TM_E

    total = lax.fori_loop(0, NE, offsets, 0)
    na = total // TM_E
    na_ref[0] = na

    def idle(t, carry):
        te_ref[t] = NE - 1
        gi_ref[t] = 0
        ne_ref[t] = -1
        return carry

    lax.fori_loop(na, NT_E, idle, 0)

    def forward(e, ordinal):
        t0 = off_s[e] // TM_E
        nt = _tiles_of(cnt_ref[e])

        def mark(t, carry):
            te_ref[t] = e
            gi_ref[t] = ordinal
            return carry

        lax.fori_loop(t0, t0 + nt, mark, 0)
        return ordinal + jnp.where(nt > 0, 1, 0)

    lax.fori_loop(0, NE, forward, 0)

    def backward(i, nxt):
        e = NE - 1 - i
        t0 = off_s[e] // TM_E
        nt = _tiles_of(cnt_ref[e])

        def mark(t, carry):
            ne_ref[t] = nxt
            return carry

        lax.fori_loop(t0, t0 + nt, mark, 0)
        return jnp.where(nt > 0, e, nxt)

    lax.fori_loop(0, NE, backward, -1)

    eidx = eidx_ref[...]
    pos = rank_ref[...]
    for e in range(NE):
        pos = pos + jnp.where(eidx == e, off_s[e], 0)
    pos_ref[...] = pos


def _plan_call(cnt, eidx_t, rank_t):
    smem = pl.BlockSpec(memory_space=pltpu.SMEM)
    vmem = pl.BlockSpec(memory_space=pltpu.VMEM)
    tiles = jax.ShapeDtypeStruct((NT_E,), I32)
    return pl.pallas_call(
        _plan_kernel,
        in_specs=[smem, vmem, vmem],
        out_specs=[vmem, smem, smem, smem, smem, smem],
        out_shape=[jax.ShapeDtypeStruct((TOPK, T_ALL), I32), tiles, tiles, tiles,
                   jax.ShapeDtypeStruct((1,), I32), jax.ShapeDtypeStruct((NE,), I32)],
        name="moe_plan",
    )(cnt, eidx_t, rank_t)


BLK_PER_TILE = TM // BLK


def _final_copies(step, obuf, yp_ref, ys_ref, sem):
    out = []
    slot = step % 2
    for m in range(BLK_PER_TILE):
        g = step * BLK_PER_TILE + m
        b = g // NBLK
        j = g % NBLK
        src = obuf.at[slot, pl.ds(m * BLK, BLK)]
        r0 = pl.multiple_of(jnp.maximum(b * SEQ + (j - 1) * BLK, 0), BLK)
        out.append((jnp.logical_and(g < NB * NBLK, j >= 1),
                    pltpu.make_async_copy(src, yp_ref.at[pl.ds(r0, BLK)], sem.at[slot])))
        out.append((g == NB * NBLK, pltpu.make_async_copy(src, ys_ref, sem.at[slot])))
    return out


def _dispatch_kernel(cnt_ref, off_ref, na_ref, pos_ref, hp_ref, xs_ref, zero_ref, sem, zsem):
    i = pl.program_id(0)

    def issue(t, carry):
        src = hp_ref.at[pl.ds(pl.multiple_of(t * ROW_TILE, ROW_TILE), ROW_TILE)]
        for k in range(TOPK):
            p = pl.multiple_of(pos_ref[0, 0, k * TM + t], ROW_TILE)
            pltpu.make_async_copy(src, xs_ref.at[pl.ds(p, ROW_TILE)], sem).start(priority=k % 2)
        return carry

    lax.fori_loop(0, TM, issue, 0)

    @pl.when(i == 0)
    def _():
        zero_ref[...] = jnp.zeros_like(zero_ref)

        def rows_copy(first_row, n_rows):
            r0 = pl.multiple_of(first_row * ROW_TILE, ROW_TILE)
            return pltpu.make_async_copy(zero_ref.at[pl.ds(0, n_rows * ROW_TILE)],
                                         xs_ref.at[pl.ds(r0, n_rows * ROW_TILE)], zsem)

        def row_copy(r):
            return rows_copy(r, 1)

        def group_copy(q):
            return rows_copy(q * 8, 8)

        def tile_copy(t):
            return rows_copy(t * TM_E, TM_E)

        def run(copy, lo, hi):
            def start(x, carry):
                copy(x).start()
                return carry

            def wait(x, carry):
                copy(x).wait()
                return carry

            lax.fori_loop(lo, hi, start, 0)
            lax.fori_loop(lo, hi, wait, 0)

        def per_expert(e, carry):
            lo = off_ref[e] + cnt_ref[e]
            hi = off_ref[e] + _tiles_of(cnt_ref[e]) * TM_E
            mid = jnp.minimum(((lo + 7) // 8) * 8, hi)
            run(row_copy, lo, mid)
            run(group_copy, mid // 8, hi // 8)
            return carry

        lax.fori_loop(0, NE, per_expert, 0)
        run(tile_copy, na_ref[0], NT_E)

    for _ in range(TOPK):
        pltpu.make_async_copy(hp_ref, xs_ref.at[pl.ds(0, TM * ROW_TILE)], sem).wait()


def _dispatch_call(cnt, off, nact, pos_tiles, hp):
    return pl.pallas_call(
        _dispatch_kernel,
        grid_spec=pltpu.PrefetchScalarGridSpec(
            num_scalar_prefetch=3,
            grid=(N_ROW_TILES,),
            in_specs=[
                pl.BlockSpec((1, 1, TM * TOPK), lambda i, c, o, n: (i, 0, 0), memory_space=pltpu.SMEM),
                pl.BlockSpec((TM * ROW_TILE, LANES), lambda i, c, o, n: (i, 0)),
            ],
            out_specs=pl.BlockSpec(memory_space=pl.ANY),
            scratch_shapes=[
                pltpu.VMEM((TM_E * ROW_TILE, LANES), U32),
                pltpu.SemaphoreType.DMA(()),
                pltpu.SemaphoreType.DMA(()),
            ],
        ),
        out_shape=jax.ShapeDtypeStruct((R_ROWS * ROW_TILE, LANES), U32),
        compiler_params=_cparams(("arbitrary",)),
        name="moe_dispatch",
    )(cnt, off, nact, pos_tiles, hp)


def _experts_kernel(layer, te_ref, gi_ref, ne_ref, na_ref, xs_ref, wg_hbm, wu_hbm, wd_hbm, ys_ref,
                    wgf, wuf, wdf, wg_s, wu_s, wd_s, wsem):
    i = pl.program_id(0)
    na = na_ref[0]

    def weight_copies(e, slot):
        return (pltpu.make_async_copy(wg_hbm.at[layer, e], wgf.at[slot], wsem.at[slot, 0]),
                pltpu.make_async_copy(wu_hbm.at[layer, e], wuf.at[slot], wsem.at[slot, 1]),
                pltpu.make_async_copy(wd_hbm.at[layer, e], wdf.at[slot], wsem.at[slot, 2]))

    @pl.when(i == 0)
    def _():
        for cp in weight_copies(te_ref[0], 0):
            cp.start(priority=1)

    ic = jnp.minimum(i, na - 1)
    first = jnp.logical_or(i == 0, te_ref[ic] != te_ref[jnp.maximum(ic - 1, 0)])

    @pl.when(jnp.logical_and(i < na, first))
    def _():
        slot = gi_ref[ic] % 2
        nxt = ne_ref[ic]

        @pl.when(nxt >= 0)
        def _():
            for cp in weight_copies(nxt, 1 - slot):
                cp.start(priority=1)

        for cp in weight_copies(te_ref[ic], slot):
            cp.wait()
        wg_s[...] = wgf[slot].astype(BF16)
        wu_s[...] = wuf[slot].astype(BF16)
        wd_s[...] = wdf[slot].astype(BF16)

    @pl.when(i < na)
    def _():
        lo, hi = _unpack_pair(_load_tiles_as_rows(xs_ref, TM_E))
        lo = lo.astype(BF16)
        hi = hi.astype(BF16)
        hg = (jnp.dot(lo, wg_s[0:HALF, :], preferred_element_type=F32)
              + jnp.dot(hi, wg_s[HALF:D, :], preferred_element_type=F32))
        hu = (jnp.dot(lo, wu_s[0:HALF, :], preferred_element_type=F32)
              + jnp.dot(hi, wu_s[HALF:D, :], preferred_element_type=F32))
        act = (hg * _sigmoid(hg) * hu).astype(BF16)
        y = jnp.dot(act, wd_s[...], preferred_element_type=F32)
        _store_rows_as_tiles(ys_ref, _pack_pair(y[:, :HALF], y[:, HALF:]))

    @pl.when(i >= na)
    def _():
        ys_ref[...] = jnp.zeros_like(ys_ref)


def _experts_call(layer, te, gi, ne, nact, xs, w_gate, w_up, w_down):
    anyspec = pl.BlockSpec(memory_space=pl.ANY)
    return pl.pallas_call(
        functools.partial(_experts_kernel, layer),
        grid_spec=pltpu.PrefetchScalarGridSpec(
            num_scalar_prefetch=4,
            grid=(NT_E,),
            in_specs=[
                pl.BlockSpec((TM_E * ROW_TILE, LANES),
                             lambda i, te_r, gi_r, ne_r, na_r: (jnp.minimum(i, na_r[0] - 1), 0)),
                anyspec, anyspec, anyspec,
            ],
            out_specs=pl.BlockSpec((TM_E * ROW_TILE, LANES), lambda i, te_r, gi_r, ne_r, na_r: (i, 0)),
            scratch_shapes=[
                pltpu.VMEM((2, D, DE), F32),
                pltpu.VMEM((2, D, DE), F32),
                pltpu.VMEM((2, DE, D), F32),
                pltpu.VMEM((D, DE), BF16),
                pltpu.VMEM((D, DE), BF16),
                pltpu.VMEM((DE, D), BF16),
                pltpu.SemaphoreType.DMA((2, 3)),
            ],
        ),
        out_shape=jax.ShapeDtypeStruct((R_ROWS * ROW_TILE, LANES), U32),
        compiler_params=_cparams(("arbitrary",)),
        name="moe_experts",
    )(te, gi, ne, nact, xs, w_gate, w_up, w_down)


def _gather_kernel(final, pos_ref, posn_ref, w_ref, h_ref, ys_ref, wsg_ref, wsu_ref, wsd_ref,
                   g_ref, b_ref, *rest):
    i = pl.program_id(0)
    if final:
        yp_ref, ys_out_ref, gbuf, gsem, obuf, osem = rest

        def wait_step(step):
            for cond, cp in _final_copies(step, obuf, yp_ref, ys_out_ref, osem):
                @pl.when(cond)
                def _():
                    cp.wait()

        @pl.when(i >= 2)
        def _():
            wait_step(i - 2)
    else:
        out_ref, gbuf, gsem = rest

    def issue(idx_ref, slot):
        def body(t, carry):
            r0 = pl.multiple_of(t * ROW_TILE, ROW_TILE)
            for k in range(TOPK):
                p = pl.multiple_of(idx_ref[0, 0, k * TM + t], ROW_TILE)
                pltpu.make_async_copy(ys_ref.at[pl.ds(p, ROW_TILE)], gbuf.at[slot, k, pl.ds(r0, ROW_TILE)],
                                      gsem.at[slot]).start(priority=k % 2)
            return carry

        lax.fori_loop(0, TM, body, 0)

    @pl.when(i == 0)
    def _():
        issue(pos_ref, 0)

    @pl.when(i + 1 < N_ROW_TILES)
    def _():
        issue(posn_ref, (i + 1) % 2)

    h = h_ref[...]
    hb = h.astype(BF16)
    sg = jnp.dot(hb, wsg_ref[...], preferred_element_type=F32)
    su = jnp.dot(hb, wsu_ref[...], preferred_element_type=F32)
    act = (sg * _sigmoid(sg) * su).astype(BF16)
    y = ALPHA * h + jnp.dot(act, wsd_ref[...], preferred_element_type=F32)
    ylo = y[:, :HALF]
    yhi = y[:, HALF:]

    slot = i % 2
    for k in range(TOPK):
        pltpu.make_async_copy(ys_ref.at[pl.ds(0, TM * ROW_TILE)], gbuf.at[slot, k], gsem.at[slot]).wait()
    w = w_ref[...]
    for k in range(TOPK):
        lo, hi = _unpack_pair(_load_tiles_as_rows(gbuf.at[slot, k], TM))
        wk = w[:, k:k + 1]
        ylo = ylo + wk * lo
        yhi = yhi + wk * hi
    out = _layer_norm(jnp.concatenate([ylo, yhi], axis=1), g_ref[...], b_ref[...])
    if not final:
        out_ref[...] = out
        return
    obuf[i % 2] = out
    for cond, cp in _final_copies(i, obuf, yp_ref, ys_out_ref, osem):
        @pl.when(cond)
        def _():
            cp.start()

    @pl.when(i == N_ROW_TILES - 1)
    def _():
        wait_step(i - 1)
        wait_step(i)


def _gather_call(final, pos_tiles, wsel, h, ys, wsg, wsu, wsd, g, b):
    vec = pl.BlockSpec((1, D), lambda i: (0, 0))
    once = dict(pipeline_mode=pl.Buffered(1))
    anyspec = pl.BlockSpec(memory_space=pl.ANY)
    scratch = [pltpu.VMEM((2, TOPK, TM * ROW_TILE, LANES), U32), pltpu.SemaphoreType.DMA((2,))]
    if final:
        out_specs = [anyspec, anyspec]
        out_shape = [jax.ShapeDtypeStruct((NB * SEQ, D), F32), jax.ShapeDtypeStruct((DEC, D), F32)]
        scratch += [pltpu.VMEM((2, TM, D), F32), pltpu.SemaphoreType.DMA((2,))]
    else:
        out_specs = pl.BlockSpec((TM, D), lambda i: (i, 0))
        out_shape = jax.ShapeDtypeStruct((T_ALL, D), F32)
    smem_tile = lambda f: pl.BlockSpec((1, 1, TM * TOPK), f, memory_space=pltpu.SMEM)
    return pl.pallas_call(
        functools.partial(_gather_kernel, final),
        grid=(N_ROW_TILES,),
        in_specs=[
            smem_tile(lambda i: (i, 0, 0)),
            smem_tile(lambda i: (jnp.minimum(i + 1, N_ROW_TILES - 1), 0, 0)),
            pl.BlockSpec((TM, TOPK), lambda i: (i, 0)),
            pl.BlockSpec((TM, D), lambda i: (i, 0)),
            anyspec,
            pl.BlockSpec((D, DE), lambda i: (0, 0), **once),
            pl.BlockSpec((D, DE), lambda i: (0, 0), **once),
            pl.BlockSpec((DE, D), lambda i: (0, 0), **once),
            vec,
            vec,
        ],
        out_specs=out_specs,
        out_shape=out_shape,
        scratch_shapes=scratch,
        compiler_params=_cparams(("arbitrary",)),
        name="moe_combine_final" if final else "moe_combine",
    )(pos_tiles, pos_tiles, wsel, h, ys, wsg, wsu, wsd, g, b)


def _moe_layer(layer, final, h, hp, w_router, router_bias, w_exp_gate, w_exp_up, w_exp_down,
               w_sh_gate, w_sh_up, w_sh_down, ln_g, ln_b):
    eidx_t, wsel, rank_t, counts = _router_call(h, w_router[layer], router_bias[layer])
    cnt = counts[:, 0].astype(I32)
    pos_t, te, gi, ne, nact, off = _plan_call(cnt, eidx_t, rank_t)
    pos_tiles = (pos_t * ROW_TILE).reshape(TOPK, N_ROW_TILES, TM).transpose(1, 0, 2).reshape(
        N_ROW_TILES, 1, TOPK * TM)
    xs = _dispatch_call(cnt, off, nact, pos_tiles, hp)
    ys = _experts_call(layer, te, gi, ne, nact, xs, w_exp_gate, w_exp_up, w_exp_down)
    return _gather_call(
        final, pos_tiles, wsel, h, ys,
        w_sh_gate[layer].astype(BF16), w_sh_up[layer].astype(BF16), w_sh_down[layer].astype(BF16),
        ln_g[layer][None, :], ln_b[layer][None, :])


def _ssm_prep_kernel(lr_ref, li_ref, ldt_ref, br_ref, bi_ref, abr_ref, abi_ref, bbr_ref, bbi_ref):
    lr = lr_ref[...]
    li = li_ref[...]
    dt = jnp.exp(ldt_ref[...])
    mag = jnp.exp(lr * dt)
    ab_re = mag * jnp.cos(li * dt)
    ab_im = mag * jnp.sin(li * dt)
    den = lr * lr + li * li
    nr = ab_re - 1.0
    ni = ab_im
    cr = (nr * lr + ni * li) / den
    ci = (ni * lr - nr * li) / den
    br = br_ref[...]
    bi = bi_ref[...]
    abr_ref[...] = ab_re
    abi_ref[...] = ab_im
    bbr_ref[...] = cr * br - ci * bi
    bbi_ref[...] = cr * bi + ci * br


def _ssm_prep_call(lam_re, lam_im, log_dt, b_re, b_im):
    wide = (NG, GC * NS)
    lr = jnp.tile(lam_re, (1, GC))
    li = jnp.tile(lam_im, (1, GC))
    ldt = jnp.broadcast_to(log_dt[:, None], wide)
    br = jnp.transpose(b_re, (0, 2, 1)).reshape(wide)
    bi = jnp.transpose(b_im, (0, 2, 1)).reshape(wide)
    sds = jax.ShapeDtypeStruct(wide, F32)
    return pl.pallas_call(
        _ssm_prep_kernel, out_shape=[sds, sds, sds, sds], name="ssm_prep",
        compiler_params=pltpu.CompilerParams(vmem_limit_bytes=VMEM_LIMIT),
    )(lr, li, ldt, br, bi)


def _cmul(ar, ai, xr, xi):
    return ar * xr - ai * xi, ar * xi + ai * xr


NSEG = 8
SEGL = LP // NSEG
STEP_UNROLL = 4
MOVE_UNROLL = 8
assert LP == NSEG * SEGL and SEGL % STEP_UNROLL == 0 and SEGL % MOVE_UNROLL == 0 and PAD_FRONT < SEGL
assert STEP_UNROLL % 2 == 0


def _ssm_p_kernel(u_ref, wb_ref, wc_ref, a_ref, d_ref, z_ref, st_ref, u_scr, s_scr, sb_scr, y_scr):
    def interleave(it, carry):
        for q in range(MOVE_UNROLL):
            t = it * MOVE_UNROLL + q
            u_scr[pl.ds(pl.multiple_of(t * NSEG, NSEG), NSEG), :] = u_ref[pl.ds(t, NSEG, stride=SEGL), :]
        return carry

    lax.fori_loop(0, SEGL // MOVE_UNROLL, interleave, 0)
    row = lax.broadcasted_iota(I32, (LP, LANES), 0)
    is_pad = jnp.logical_and(row % NSEG == 0, row // NSEG < PAD_FRONT)
    u = jnp.where(is_pad, 0.0, u_scr[...])
    s_scr[...] = jnp.dot(u.astype(BF16), wb_ref[0], preferred_element_type=F32)

    ar = a_ref[0, 0:1, :]
    ai = a_ref[0, 1:2, :]

    def group(t):
        r0 = pl.multiple_of(t * NSEG, NSEG)
        return pl.ds(r0, NSEG)

    def local_scan(it, carry):
        sr, si = carry
        for q in range(STEP_UNROLL):
            g = group(it * STEP_UNROLL + q)
            tr, ti = _cmul(ar, ai, sr, si)
            sr = tr + s_scr[g, 0:CH_STATE]
            si = ti + s_scr[g, CH_STATE:]
            s_scr[g, 0:CH_STATE] = sr
            s_scr[g, CH_STATE:] = si
        return sr, si

    zero8 = jnp.zeros((NSEG, CH_STATE), F32)
    er, ei = lax.fori_loop(0, SEGL // STEP_UNROLL, local_scan, (zero8, zero8))

    pr, pi = ar, ai
    acc = None
    bits = SEGL
    while bits:
        if bits & 1:
            acc = (pr, pi) if acc is None else _cmul(pr, pi, *acc)
        bits >>= 1
        if bits:
            pr, pi = _cmul(pr, pi, pr, pi)
    alr, ali = acc
    cr = jnp.zeros((1, CH_STATE), F32)
    ci = jnp.zeros((1, CH_STATE), F32)
    ins_r, ins_i = [], []
    for j in range(NSEG):
        ins_r.append(cr)
        ins_i.append(ci)
        tr, ti = _cmul(alr, ali, cr, ci)
        cr = tr + er[j:j + 1, :]
        ci = ti + ei[j:j + 1, :]
    st_ref[0, 0, :, 0:CH_STATE] = cr
    st_ref[0, 0, :, CH_STATE:] = ci

    def fixup(it, carry):
        dr, di = carry
        for q in range(0, STEP_UNROLL, 2):
            t0 = it * STEP_UNROLL + q
            rows_r, rows_i = [], []
            for t in (t0, t0 + 1):
                g = group(t)
                dr, di = _cmul(ar, ai, dr, di)
                rows_r.append(s_scr[g, 0:CH_STATE] + dr)
                rows_i.append(s_scr[g, CH_STATE:] + di)
            pair = pl.ds(pl.multiple_of(t0 * NSEG, 2 * NSEG), 2 * NSEG)
            sb_scr[pair, 0:CH_STATE] = jnp.concatenate(rows_r, axis=0).astype(BF16)
            sb_scr[pair, CH_STATE:] = jnp.concatenate(rows_i, axis=0).astype(BF16)
        return dr, di

    lax.fori_loop(0, SEGL // STEP_UNROLL, fixup,
                  (jnp.concatenate(ins_r, axis=0), jnp.concatenate(ins_i, axis=0)))

    y_scr[...] = jnp.dot(sb_scr[...], wc_ref[0], preferred_element_type=F32) + d_ref[0] * u

    def deinterleave(it, carry):
        for q in range(MOVE_UNROLL):
            t = it * MOVE_UNROLL + q
            u_scr[pl.ds(t, NSEG, stride=SEGL), :] = y_scr[pl.ds(pl.multiple_of(t * NSEG, NSEG), NSEG), :]
        return carry

    lax.fori_loop(0, SEGL // MOVE_UNROLL, deinterleave, 0)
    z_ref[...] = jax.nn.gelu(u_scr[...]).astype(BF16)


def _ssm_p_call(h, wb_bf, wc_bf, a_tab, d_tab):
    return pl.pallas_call(
        _ssm_p_kernel,
        grid=(NB, NCHUNK),
        in_specs=[
            pl.BlockSpec((LP, LANES), lambda b, k: (b, k)),
            pl.BlockSpec((1, LANES, 2 * CH_STATE), lambda b, k: (k, 0, 0)),
            pl.BlockSpec((1, 2 * CH_STATE, LANES), lambda b, k: (k, 0, 0)),
            pl.BlockSpec((1, 2, CH_STATE), lambda b, k: (k, 0, 0)),
            pl.BlockSpec((1, 1, LANES), lambda b, k: (k, 0, 0)),
        ],
        out_specs=[
            pl.BlockSpec((LP, LANES), lambda b, k: (b, k)),
            pl.BlockSpec((1, 1, 1, 2 * CH_STATE), lambda b, k: (b, k, 0, 0)),
        ],
        out_shape=[
            jax.ShapeDtypeStruct((T_PROMPT, D), BF16),
            jax.ShapeDtypeStruct((NB, NCHUNK, 1, 2 * CH_STATE), F32),
        ],
        scratch_shapes=[pltpu.VMEM((LP, LANES), F32), pltpu.VMEM((LP, 2 * CH_STATE), F32),
                        pltpu.VMEM((LP, 2 * CH_STATE), BF16), pltpu.VMEM((LP, LANES), F32)],
        compiler_params=_cparams(("parallel", "parallel")),
        name="ssm_prompt",
    )(h, wb_bf, wc_bf, a_tab, d_tab)


def _ssm_s_kernel(u_ref, sr_ref, si_ref, wb_ref, wc_ref, a_ref, d_ref, z_ref, nr_ref, ni_ref):
    u = u_ref[...]
    bu = jnp.dot(u, wb_ref[0], preferred_element_type=F32, precision=lax.Precision.HIGHEST)
    ar = a_ref[0, 0:1, :]
    ai = a_ref[0, 1:2, :]
    tr, ti = _cmul(ar, ai, sr_ref[...], si_ref[...])
    nr = tr + bu[:, 0:CH_STATE]
    ni = ti + bu[:, CH_STATE:]
    nr_ref[...] = nr
    ni_ref[...] = ni
    s = jnp.concatenate([nr, ni], axis=1).astype(BF16)
    y = jnp.dot(s, wc_ref[0], preferred_element_type=F32) + d_ref[0] * u
    z_ref[...] = jax.nn.gelu(y).astype(BF16)


def _ssm_s_call(h, s0r, s0i, wb_f32, wc_bf, a_tab, d_tab):
    st = pl.BlockSpec((DEC, CH_STATE), lambda k: (0, k))
    return pl.pallas_call(
        _ssm_s_kernel,
        grid=(NCHUNK,),
        in_specs=[
            pl.BlockSpec((DEC, LANES), lambda k: (T_PROMPT // DEC, k)),
            st,
            st,
            pl.BlockSpec((1, LANES, 2 * CH_STATE), lambda k: (k, 0, 0)),
            pl.BlockSpec((1, 2 * CH_STATE, LANES), lambda k: (k, 0, 0)),
            pl.BlockSpec((1, 2, CH_STATE), lambda k: (k, 0, 0)),
            pl.BlockSpec((1, 1, LANES), lambda k: (k, 0, 0)),
        ],
        out_specs=[pl.BlockSpec((DEC, LANES), lambda k: (0, k)), st, st],
        out_shape=[
            jax.ShapeDtypeStruct((DEC, D), BF16),
            jax.ShapeDtypeStruct((DEC, NG * NS), F32),
            jax.ShapeDtypeStruct((DEC, NG * NS), F32),
        ],
        compiler_params=_cparams(("parallel",)),
        name="ssm_sample",
    )(h, s0r, s0i, wb_f32, wc_bf, a_tab, d_tab)


def _glu_ln_kernel(z_ref, zt_ref, w_ref, bg_ref, h_ref, g_ref, b_ref, out_ref, hp_ref):
    z = _prompt_or_tail(z_ref, zt_ref)
    acc = jnp.dot(z, w_ref[...], preferred_element_type=F32) + bg_ref[...]
    m = acc[:, :D] * _sigmoid(acc[:, D:])
    out = _layer_norm(ALPHA * h_ref[...] + m, g_ref[...], b_ref[...])
    out_ref[...] = out
    _store_rows_as_tiles(hp_ref, _pack_pair(out[:, :HALF], out[:, HALF:]))


def _glu_ln_call(z_p, z_tail, w_bf, bg, h, g, b):
    vec = pl.BlockSpec((1, D), lambda i: (0, 0))
    return pl.pallas_call(
        _glu_ln_kernel,
        grid=(N_ROW_TILES,),
        in_specs=[
            pl.BlockSpec((TM, D), _PROMPT_TILE),
            pl.BlockSpec((TM, D), lambda i: (0, 0)),
            pl.BlockSpec((D, 2 * D), lambda i: (0, 0), pipeline_mode=pl.Buffered(1)),
            pl.BlockSpec((1, 2 * D), lambda i: (0, 0)),
            pl.BlockSpec((TM, D), lambda i: (i, 0)),
            vec,
            vec,
        ],
        out_specs=[pl.BlockSpec((TM, D), lambda i: (i, 0)),
                   pl.BlockSpec((TM * ROW_TILE, LANES), lambda i: (i, 0))],
        out_shape=[jax.ShapeDtypeStruct((T_ALL, D), F32),
                   jax.ShapeDtypeStruct((T_ALL * ROW_TILE, LANES), U32)],
        compiler_params=_cparams(("parallel",)),
        name="glu_ln",
    )(z_p, z_tail, w_bf, bg, h, g, b)


GROUPS_PER_CHUNK = LANES // GC


def _block_diag_in(t):
    t3 = t.reshape(NCHUNK, LANES, NS)
    same = (np.arange(LANES)[:, None] // GC) == (np.arange(CH_STATE)[None, :] // NS)
    return jnp.tile(t3, (1, 1, GROUPS_PER_CHUNK)) * jnp.asarray(same, t.dtype)


def _block_diag_out(t):
    t3 = jnp.swapaxes(t, 1, 2).reshape(NCHUNK, CH_STATE, GC)
    same = (np.arange(CH_STATE)[:, None] // NS) == (np.arange(LANES)[None, :] // GC)
    return jnp.tile(t3, (1, 1, GROUPS_PER_CHUNK)) * jnp.asarray(same, t.dtype)


def kernel(x_prompt, x_sample, cache_k, cache_v, state_ssm_re, state_ssm_im, meta_tokens, w_qkv, b_qkv, attn_sinks, w_o, b_o, ssm_lam_re, ssm_lam_im, ssm_log_dt, ssm_b_re, ssm_b_im, ssm_c_re, ssm_c_im, ssm_d, w_glu, b_glu, ln_mix_g, ln_mix_b, w_router, router_bias, w_exp_gate, w_exp_up, w_exp_down, w_sh_gate, w_sh_up, w_sh_down, ln_ffn_g, ln_ffn_b):
    moe_w = (w_router, router_bias, w_exp_gate, w_exp_up, w_exp_down, w_sh_gate, w_sh_up, w_sh_down,
             ln_ffn_g, ln_ffn_b)

    front = jnp.concatenate([jnp.zeros((PAD_FRONT, D), F32), meta_tokens], axis=0)
    pieces = []
    for b in range(NB):
        pieces += [front, x_prompt[b]]
    h = jnp.concatenate(pieces + [x_sample.reshape(DEC, D)], axis=0)

    rc, rs1, rs2 = _rope_tables()
    q, k, v = _qkv_call(h, w_qkv[0].astype(BF16), b_qkv[0][None, :], rc, rs1, rs2)
    o_p = _attn_p_call(attn_sinks[0], q, k, v)
    q3 = q[T_PROMPT:].astype(F32).reshape(DEC, NH, HD)
    o_s, ck_new, cv_new = _attn_s_call(
        attn_sinks[0][:, None], q3, k, v,
        cache_k[0].reshape(DEC, BLK, NKV * HD), cache_v[0].reshape(DEC, BLK, NKV * HD))
    o_tail = _tail_tile(o_p, o_s.reshape(DEC, D).astype(BF16))
    h, hp = _oproj_ln_call(o_p, o_tail, w_o[0].astype(BF16), b_o[0][None, :], h,
                           ln_mix_g[0][None, :], ln_mix_b[0][None, :])
    h = _moe_layer(0, False, h, hp, *moe_w)

    def last_window(t):
        rows = [t[(b + 1) * LP - BLK:(b + 1) * LP] for b in range(NB)]
        return jnp.stack(rows).reshape(NB, BLK, NKV, HD)

    kp = last_window(k)
    vp = last_window(v)

    ab_re, ab_im, bb_re, bb_im = _ssm_prep_call(
        ssm_lam_re[0], ssm_lam_im[0], ssm_log_dt[0], ssm_b_re[0], ssm_b_im[0])
    wb = jnp.concatenate([_block_diag_in(bb_re), _block_diag_in(bb_im)], axis=2)
    wc = jnp.concatenate([_block_diag_out(ssm_c_re[0]), -_block_diag_out(ssm_c_im[0])], axis=1)
    wc_bf = wc.astype(BF16)
    a_tab = jnp.stack([ab_re[:, :NS].reshape(NCHUNK, CH_STATE),
                       ab_im[:, :NS].reshape(NCHUNK, CH_STATE)], axis=1)
    d_tab = ssm_d[0].reshape(NCHUNK, 1, LANES)
    z_p, st_p = _ssm_p_call(h, wb.astype(BF16), wc_bf, a_tab, d_tab)
    z_s, sr_new, si_new = _ssm_s_call(
        h, state_ssm_re[0].reshape(DEC, NG * NS), state_ssm_im[0].reshape(DEC, NG * NS),
        wb, wc_bf, a_tab, d_tab)
    h, hp = _glu_ln_call(z_p, _tail_tile(z_p, z_s), w_glu[0].astype(BF16), b_glu[0][None, :], h,
                         ln_mix_g[1][None, :], ln_mix_b[1][None, :])
    y_prompt, y_sample = _moe_layer(1, True, h, hp, *moe_w)
    y_prompt = y_prompt.reshape(NB, SEQ, D)
    y_sample = y_sample.reshape(DEC, 1, D)
    st_p = st_p.reshape(NB, NCHUNK, 2, 8, NS)
    rp = st_p[:, :, 0].reshape(NB, NG, NS)
    ip = st_p[:, :, 1].reshape(NB, NG, NS)
    return (y_prompt, y_sample,
            kp[None], vp[None],
            ck_new.reshape(1, DEC, BLK, NKV, HD), cv_new.reshape(1, DEC, BLK, NKV, HD),
            rp[None], ip[None],
            sr_new.reshape(1, DEC, NG, NS), si_new.reshape(1, DEC, NG, NS))
```

```python
import functools
import math

import jax
import jax.numpy as jnp
import numpy as np
from jax import lax
from jax.experimental import pallas as pl
from jax.experimental.pallas import tpu as pltpu

F32 = jnp.float32
BF16 = jnp.bfloat16
I32 = jnp.int32
U32 = jnp.uint32

D = 2048
HALF = D // 2
NB = 4
N_META = 16
SEQ = 2048
L = N_META + SEQ
BLK = 128
PAD_FRONT = (-L) % BLK
LP = L + PAD_FRONT
NBLK = LP // BLK
T_PROMPT = NB * LP
DEC = 128
T_ALL = T_PROMPT + DEC
PAST_LEN = 8192
HD = 64
NH = 32
NKV = 4
QPK = NH // NKV
QKV = (NH + 2 * NKV) * HD
QK_COLS = (NH + NKV) * HD
ROT = HD // 4
ROT_HALF = ROT // 2
ROPE_THETA = 500000.0
NG = 128
GC = 16
NS = 64
NCHUNK = 16
CH_STATE = 8 * NS
NE = 64
TOPK = 8
NEG = 8
PER_GRP = NE // NEG
TOPG = 4
DE = 512
ROUTED_SCALE = 2.5
DEPTH = 2
ALPHA = (2 * DEPTH) ** 0.25
LN_EPS = 1e-5

V7X_VMEM_BYTES = 64 * 1024 * 1024
VMEM_LIMIT = 56 * 1024 * 1024
LANES = 128

TM = 384
N_ROW_TILES = T_ALL // TM
TAIL_START = (N_ROW_TILES - 1) * TM
TM_E = 256
N_PAIRS = T_ALL * TOPK
NT_E = -(-N_PAIRS // TM_E) + NE
R_ROWS = NT_E * TM_E
assert T_ALL % TM == 0 and TAIL_START <= T_PROMPT


def _cparams(sem):
    return pltpu.CompilerParams(dimension_semantics=sem, vmem_limit_bytes=VMEM_LIMIT)


def _sigmoid(x):
    return 1.0 / (1.0 + jnp.exp(-x))


def _layer_norm(y, g, b):
    mu = jnp.mean(y, axis=-1, keepdims=True)
    yc = y - mu
    var = jnp.mean(yc * yc, axis=-1, keepdims=True)
    return yc * lax.rsqrt(var + LN_EPS) * g + b


def _pack_pair(lo, hi):
    lo_b = lax.bitcast_convert_type(lo.astype(BF16).astype(F32), U32) >> 16
    hi_b = lax.bitcast_convert_type(hi.astype(BF16).astype(F32), U32) & jnp.uint32(0xFFFF0000)
    return lo_b | hi_b


def _unpack_pair(w):
    lo = lax.bitcast_convert_type(w << 16, F32)
    hi = lax.bitcast_convert_type(w & jnp.uint32(0xFFFF0000), F32)
    return lo, hi


ROW_TILE = HALF // LANES
assert ROW_TILE == 8


def _store_rows_as_tiles(ref, rows):
    n = rows.shape[0]
    for j in range(ROW_TILE):
        ref[pl.ds(j, n, stride=ROW_TILE), :] = rows[:, j * LANES:(j + 1) * LANES]


def _load_tiles_as_rows(ref, n):
    return jnp.concatenate([ref[pl.ds(j, n, stride=ROW_TILE), :] for j in range(ROW_TILE)], axis=1)


def _qkv_kernel(x_ref, w_ref, b_ref, c_ref, s1_ref, s2_ref, q_ref, k_ref, v_ref):
    xb = x_ref[...].astype(BF16)
    acc = jnp.dot(xb, w_ref[...], preferred_element_type=F32) + b_ref[...]
    c = c_ref[...]
    s1 = s1_ref[...]
    s2 = s2_ref[...]
    for j in range(QK_COLS // LANES):
        blk = acc[:, j * LANES:(j + 1) * LANES]
        r = (blk * c + pltpu.roll(blk, LANES - ROT_HALF, axis=1) * s1
             + pltpu.roll(blk, ROT_HALF, axis=1) * s2)
        if j < D // LANES:
            q_ref[:, j * LANES:(j + 1) * LANES] = (r * (1.0 / math.sqrt(HD))).astype(BF16)
        else:
            jj = j - D // LANES
            k_ref[:, jj * LANES:(jj + 1) * LANES] = r
    v_ref[...] = acc[:, QK_COLS:]


def _qkv_call(x, w_bf, b, rc, rs1, rs2):
    return pl.pallas_call(
        _qkv_kernel,
        grid=(T_ALL // TM,),
        in_specs=[
            pl.BlockSpec((TM, D), lambda i: (i, 0)),
            pl.BlockSpec((D, QKV), lambda i: (0, 0)),
            pl.BlockSpec((1, QKV), lambda i: (0, 0)),
            pl.BlockSpec((TM, LANES), lambda i: (i, 0)),
            pl.BlockSpec((TM, LANES), lambda i: (i, 0)),
            pl.BlockSpec((TM, LANES), lambda i: (i, 0)),
        ],
        out_specs=[
            pl.BlockSpec((TM, D), lambda i: (i, 0)),
            pl.BlockSpec((TM, NKV * HD), lambda i: (i, 0)),
            pl.BlockSpec((TM, NKV * HD), lambda i: (i, 0)),
        ],
        out_shape=[
            jax.ShapeDtypeStruct((T_ALL, D), BF16),
            jax.ShapeDtypeStruct((T_ALL, NKV * HD), F32),
            jax.ShapeDtypeStruct((T_ALL, NKV * HD), F32),
        ],
        compiler_params=_cparams(("parallel",)),
        name="qkv_rope",
    )(x, w_bf, b, rc, rs1, rs2)


def _rope_tables():
    pos_p = jnp.maximum(jnp.arange(LP, dtype=I32) - PAD_FRONT, 0)
    pos = jnp.concatenate([pos_p, jnp.full((8,), PAST_LEN, I32)]).astype(F32)
    inv_freq = ROPE_THETA ** (-jnp.arange(0, ROT, 2, dtype=F32) / ROT)
    ang = pos[:, None] * inv_freq[None, :]
    cos = jnp.cos(ang)
    sin = jnp.sin(ang)
    lane = np.arange(LANES) % HD
    freq = np.arange(ROT_HALF)[:, None]
    first = ((lane[None, :] == freq) & (lane[None, :] < ROT_HALF)).astype(np.float32)
    second = ((lane[None, :] - ROT_HALF == freq) & (lane[None, :] < ROT)).astype(np.float32)
    rest = (lane >= ROT).astype(np.float32)[None, :]
    place = functools.partial(jnp.dot, precision=lax.Precision.HIGHEST)
    c = place(cos, jnp.asarray(first + second)) + jnp.asarray(rest)
    s1 = place(sin, jnp.asarray(-first))
    s2 = place(sin, jnp.asarray(second))

    def all_rows(t):
        return jnp.concatenate([t[:LP]] * NB + [t[LP:]] * (DEC // 8), axis=0)

    return all_rows(c), all_rows(s1), all_rows(s2)


def _attn_p_kernel(sink_ref, q_ref, kp_ref, kc_ref, vp_ref, vc_ref, o_ref):
    j = pl.program_id(0) % NBLK
    keys = jnp.concatenate([kp_ref[...], kc_ref[...]], axis=0).astype(BF16)
    vals = jnp.concatenate([vp_ref[...], vc_ref[...]], axis=0).astype(BF16)
    r = lax.broadcasted_iota(I32, (BLK, 2 * BLK), 0)
    c = lax.broadcasted_iota(I32, (BLK, 2 * BLK), 1)
    dist = BLK + r - c
    kpos = (j - 1) * BLK - PAD_FRONT + c
    mask = (dist >= 0) & (dist <= BLK) & (kpos >= 0)
    for g in range(NKV):
        kg = keys[:, g * HD:(g + 1) * HD]
        vg = vals[:, g * HD:(g + 1) * HD]
        heads = range(g * QPK, (g + 1) * QPK)
        scores = [lax.dot_general(q_ref[:, h * HD:(h + 1) * HD], kg, (((1,), (1,)), ((), ())),
                                  preferred_element_type=F32) for h in heads]
        probs, rdens = [], []
        for h, s in zip(heads, scores):
            s = jnp.where(mask, s, -jnp.inf)
            sk = sink_ref[h]
            m = jnp.maximum(jnp.max(s, axis=1, keepdims=True), sk)
            p = jnp.exp(s - m)
            rdens.append(1.0 / (jnp.sum(p, axis=1, keepdims=True) + jnp.exp(sk - m)))
            probs.append(p.astype(BF16))
        for h, p, rden in zip(heads, probs, rdens):
            oh = jnp.dot(p, vg, preferred_element_type=F32) * rden
            o_ref[:, h * HD:(h + 1) * HD] = oh.astype(BF16)


def _attn_p_call(sinks, q, k, v):
    prev = lambda i: (jnp.where(i % NBLK == 0, i, i - 1), 0)
    cur = lambda i: (i, 0)
    return pl.pallas_call(
        _attn_p_kernel,
        grid=(NB * NBLK,),
        in_specs=[
            pl.BlockSpec(memory_space=pltpu.SMEM),
            pl.BlockSpec((BLK, D), cur),
            pl.BlockSpec((BLK, NKV * HD), prev),
            pl.BlockSpec((BLK, NKV * HD), cur),
            pl.BlockSpec((BLK, NKV * HD), prev),
            pl.BlockSpec((BLK, NKV * HD), cur),
        ],
        out_specs=pl.BlockSpec((BLK, D), cur),
        out_shape=jax.ShapeDtypeStruct((T_PROMPT, D), BF16),
        compiler_params=_cparams(("parallel",)),
        name="attn_prompt",
    )(sinks, q, k, k, v, v)


SEQ_PER_STEP = 16
SEQ_UNROLL = 4


def _attn_s_kernel(sink_ref, q_ref, kn_ref, vn_ref, ck_ref, cv_ref, o_ref, cko_ref, cvo_ref):
    row = lax.broadcasted_iota(I32, (BLK, NKV * HD), 0)
    hrow = lax.broadcasted_iota(I32, (NH, NKV * HD), 0) // QPK
    hlane = lax.broadcasted_iota(I32, (NH, NKV * HD), 1) // HD
    own = hrow == hlane
    sk = sink_ref[...]

    def score_stage(s):
        kn = kn_ref[pl.ds(s, 1), :]
        knr = kn.astype(BF16).astype(F32)
        q = q_ref[s].astype(BF16)
        qe = jnp.where(own, jnp.concatenate([q] * NKV, axis=1), jnp.zeros((), BF16))
        sc = lax.dot_general(qe, ck_ref[s].astype(BF16), (((1,), (1,)), ((), ())),
                             preferred_element_type=F32)
        sn = jnp.sum(qe.astype(F32) * knr, axis=1, keepdims=True)
        return sc, sn

    def softmax_stage(sc, sn):
        m = jnp.maximum(jnp.maximum(jnp.max(sc, axis=1, keepdims=True), sn), sk)
        p = jnp.exp(sc - m)
        pn = jnp.exp(sn - m)
        rden = 1.0 / (jnp.sum(p, axis=1, keepdims=True) + pn + jnp.exp(sk - m))
        return p.astype(BF16), pn.astype(BF16).astype(F32), rden

    def value_stage(s, p, pn, rden):
        vn = vn_ref[pl.ds(s, 1), :]
        vnr = vn.astype(BF16).astype(F32)
        of = jnp.dot(p, cv_ref[s].astype(BF16), preferred_element_type=F32)
        of = jnp.where(own, of + pn * vnr, 0.0)
        og = of[:, 0:HD]
        for g in range(1, NKV):
            og = og + of[:, g * HD:(g + 1) * HD]
        o_ref[s] = og * rden
        cko_ref[s] = jnp.where(row == BLK - 1, kn_ref[pl.ds(s, 1), :], pltpu.roll(ck_ref[s], BLK - 1, axis=0))
        cvo_ref[s] = jnp.where(row == BLK - 1, vn, pltpu.roll(cv_ref[s], BLK - 1, axis=0))

    def body(it, carry):
        seqs = [it * SEQ_UNROLL + u for u in range(SEQ_UNROLL)]
        staged = [score_stage(s) for s in seqs]
        soft = [softmax_stage(*st) for st in staged]
        for s, sm in zip(seqs, soft):
            value_stage(s, *sm)
        return carry

    lax.fori_loop(0, SEQ_PER_STEP // SEQ_UNROLL, body, 0)


def _attn_s_call(sinks_col, q3, k, v, cache_k, cache_v):
    sp = SEQ_PER_STEP
    kv_off = T_PROMPT // sp
    return pl.pallas_call(
        _attn_s_kernel,
        grid=(DEC // sp,),
        in_specs=[
            pl.BlockSpec((NH, 1), lambda i: (0, 0)),
            pl.BlockSpec((sp, NH, HD), lambda i: (i, 0, 0)),
            pl.BlockSpec((sp, NKV * HD), lambda i: (kv_off + i, 0)),
            pl.BlockSpec((sp, NKV * HD), lambda i: (kv_off + i, 0)),
            pl.BlockSpec((sp, BLK, NKV * HD), lambda i: (i, 0, 0)),
            pl.BlockSpec((sp, BLK, NKV * HD), lambda i: (i, 0, 0)),
        ],
        out_specs=[
            pl.BlockSpec((sp, NH, HD), lambda i: (i, 0, 0)),
            pl.BlockSpec((sp, BLK, NKV * HD), lambda i: (i, 0, 0)),
            pl.BlockSpec((sp, BLK, NKV * HD), lambda i: (i, 0, 0)),
        ],
        out_shape=[
            jax.ShapeDtypeStruct((DEC, NH, HD), F32),
            jax.ShapeDtypeStruct((DEC, BLK, NKV * HD), F32),
            jax.ShapeDtypeStruct((DEC, BLK, NKV * HD), F32),
        ],
        compiler_params=_cparams(("parallel",)),
        name="attn_sample",
    )(sinks_col, q3, k, v, cache_k, cache_v)


def _prompt_or_tail(x_ref, tail_ref):
    return jnp.where(pl.program_id(0) == N_ROW_TILES - 1, tail_ref[...], x_ref[...])


def _tail_tile(x_prompt_rows, x_sample_rows):
    return jnp.concatenate([x_prompt_rows[TAIL_START:], x_sample_rows], axis=0)


_PROMPT_TILE = lambda i: (jnp.minimum(i, N_ROW_TILES - 2), 0)


def _oproj_ln_kernel(o_ref, ot_ref, w_ref, bo_ref, h_ref, g_ref, b_ref, out_ref, hp_ref):
    o = _prompt_or_tail(o_ref, ot_ref)
    m = jnp.dot(o, w_ref[...], preferred_element_type=F32) + bo_ref[...]
    out = _layer_norm(ALPHA * h_ref[...] + m, g_ref[...], b_ref[...])
    out_ref[...] = out
    _store_rows_as_tiles(hp_ref, _pack_pair(out[:, :HALF], out[:, HALF:]))


def _oproj_ln_call(o_p, o_tail, w_bf, bo, h, g, b):
    vec = pl.BlockSpec((1, D), lambda i: (0, 0))
    return pl.pallas_call(
        _oproj_ln_kernel,
        grid=(N_ROW_TILES,),
        in_specs=[
            pl.BlockSpec((TM, D), _PROMPT_TILE),
            pl.BlockSpec((TM, D), lambda i: (0, 0)),
            pl.BlockSpec((D, D), lambda i: (0, 0)),
            vec,
            pl.BlockSpec((TM, D), lambda i: (i, 0)),
            vec,
            vec,
        ],
        out_specs=[pl.BlockSpec((TM, D), lambda i: (i, 0)),
                   pl.BlockSpec((TM * ROW_TILE, LANES), lambda i: (i, 0))],
        out_shape=[jax.ShapeDtypeStruct((T_ALL, D), F32),
                   jax.ShapeDtypeStruct((T_ALL * ROW_TILE, LANES), U32)],
        compiler_params=_cparams(("parallel",)),
        name="oproj_ln",
    )(o_p, o_tail, w_bf, bo, h, g, b)


def _router_kernel(h_ref, wr_ref, rb_ref, eidx_ref, wsel_ref, rank_ref, cnt_ref, carry_ref):
    i = pl.program_id(0)

    @pl.when(i == 0)
    def _():
        carry_ref[...] = jnp.zeros_like(carry_ref)

    h = h_ref[...]
    w = wr_ref[...]
    h_hi = h.astype(BF16)
    h_lo = (h - h_hi.astype(F32)).astype(BF16)
    w_hi = w.astype(BF16)
    w_lo = (w - w_hi.astype(F32)).astype(BF16)
    logits = (jnp.dot(h_hi, w_hi, preferred_element_type=F32)
              + (jnp.dot(h_hi, w_lo, preferred_element_type=F32)
                 + jnp.dot(h_lo, w_hi, preferred_element_type=F32)))
    scores = _sigmoid(logits.T[0:NE, :])
    biased = scores + rb_ref[...]
    ninf = -jnp.inf
    sub = lax.broadcasted_iota(I32, (PER_GRP, TM), 0).astype(F32)
    sc_g = [scores[g * PER_GRP:(g + 1) * PER_GRP, :] for g in range(NEG)]
    b_g = [biased[g * PER_GRP:(g + 1) * PER_GRP, :] for g in range(NEG)]
    e_g = [sub + float(g * PER_GRP) for g in range(NEG)]

    def smax(x):
        return jnp.max(x, axis=0, keepdims=True)

    def smin(x):
        return jnp.min(x, axis=0, keepdims=True)

    gs = []
    for g in range(NEG):
        m1 = smax(b_g[g])
        i1 = smin(jnp.where(b_g[g] == m1, sub, float(PER_GRP)))
        m2 = smax(jnp.where(sub == i1, ninf, b_g[g]))
        gs.append(m1 + m2)

    work = []
    for g in range(NEG):
        beaten = jnp.zeros((1, TM), F32)
        for o in range(NEG):
            if o != g:
                wins = (gs[o] >= gs[g]) if o < g else (gs[o] > gs[g])
                beaten = beaten + jnp.where(wins, 1.0, 0.0)
        work.append(jnp.where(beaten < float(TOPG), b_g[g], ninf))

    idx_rows, w_rows = [], []
    onehot = [jnp.zeros((PER_GRP, TM), F32) for _ in range(NEG)]
    for _ in range(TOPK):
        m = smax(work[0])
        for g in range(1, NEG):
            m = jnp.maximum(m, smax(work[g]))
        ik = smin(jnp.where(work[0] == m, e_g[0], float(NE)))
        for g in range(1, NEG):
            ik = jnp.minimum(ik, smin(jnp.where(work[g] == m, e_g[g], float(NE))))
        wk = jnp.zeros((1, TM), F32)
        for g in range(NEG):
            hit = e_g[g] == ik
            wk = wk + jnp.sum(jnp.where(hit, sc_g[g], 0.0), axis=0, keepdims=True)
            onehot[g] = jnp.where(hit, 1.0, onehot[g])
            work[g] = jnp.where(hit, ninf, work[g])
        idx_rows.append(ik)
        w_rows.append(wk)
    wsum = w_rows[0]
    for wk in w_rows[1:]:
        wsum = wsum + wk

    rr = lax.broadcasted_iota(I32, (TM, TM), 0)
    cc = lax.broadcasted_iota(I32, (TM, TM), 1)
    tri = jnp.where(rr < cc, 1.0, 0.0).astype(BF16)
    oh = jnp.concatenate(onehot, axis=0)
    prefix = jnp.dot(oh.astype(BF16), tri, preferred_element_type=F32) + carry_ref[...]
    carry_ref[...] = carry_ref[...] + jnp.sum(oh, axis=1, keepdims=True)
    cnt_ref[...] = carry_ref[...]

    rank_rows = []
    for k in range(TOPK):
        rk = jnp.zeros((1, TM), F32)
        for g in range(NEG):
            pg = prefix[g * PER_GRP:(g + 1) * PER_GRP, :]
            rk = rk + jnp.sum(jnp.where(e_g[g] == idx_rows[k], pg, 0.0), axis=0, keepdims=True)
        rank_rows.append(rk)
    eidx_ref[...] = jnp.concatenate(idx_rows, axis=0).astype(I32)
    rank_ref[...] = jnp.concatenate(rank_rows, axis=0).astype(I32)
    w_t = jnp.concatenate([wk / wsum * ROUTED_SCALE for wk in w_rows]
                          + [jnp.zeros((LANES - TOPK, TM), F32)], axis=0)
    wsel_ref[...] = w_t.T[:, 0:TOPK]


def _router_call(h, w_router, router_bias):
    tk = pl.BlockSpec((TM, TOPK), lambda i: (i, 0))
    kt = pl.BlockSpec((TOPK, TM), lambda i: (0, i))
    return pl.pallas_call(
        _router_kernel,
        grid=(T_ALL // TM,),
        in_specs=[
            pl.BlockSpec((TM, D), lambda i: (i, 0)),
            pl.BlockSpec((D, LANES), lambda i: (0, 0)),
            pl.BlockSpec((NE, 1), lambda i: (0, 0)),
        ],
        out_specs=[kt, tk, kt, pl.BlockSpec((NE, 1), lambda i: (0, 0))],
        out_shape=[
            jax.ShapeDtypeStruct((TOPK, T_ALL), I32),
            jax.ShapeDtypeStruct((T_ALL, TOPK), F32),
            jax.ShapeDtypeStruct((TOPK, T_ALL), I32),
            jax.ShapeDtypeStruct((NE, 1), F32),
        ],
        scratch_shapes=[pltpu.VMEM((NE, 1), F32)],
        compiler_params=_cparams(("arbitrary",)),
        name="router",
    )(h, jnp.pad(w_router, ((0, 0), (0, LANES - NE))), router_bias[:, None])


def _tiles_of(cnt):
    return (cnt + TM_E - 1) // TM_E


def _plan_kernel(cnt_ref, eidx_ref, rank_ref, pos_ref, te_ref, gi_ref, ne_ref, na_ref, off_s):
    def offsets(e, run):
        off_s[e] = run
        return run + _tiles_of(cnt_ref[e]) * TM_E

    total = lax.fori_loop(0, NE, offsets, 0)
    na = total // TM_E
    na_ref[0] = na

    def idle(t, carry):
        te_ref[t] = NE - 1
        gi_ref[t] = 0
        ne_ref[t] = -1
        return carry

    lax.fori_loop(na, NT_E, idle, 0)

    def forward(e, ordinal):
        t0 = off_s[e] // TM_E
        nt = _tiles_of(cnt_ref[e])

        def mark(t, carry):
            te_ref[t] = e
            gi_ref[t] = ordinal
            return carry

        lax.fori_loop(t0, t0 + nt, mark, 0)
        return ordinal + jnp.where(nt > 0, 1, 0)

    lax.fori_loop(0, NE, forward, 0)

    def backward(i, nxt):
        e = NE - 1 - i
        t0 = off_s[e] // TM_E
        nt = _tiles_of(cnt_ref[e])

        def mark(t, carry):
            ne_ref[t] = nxt
            return carry

        lax.fori_loop(t0, t0 + nt, mark, 0)
        return jnp.where(nt > 0, e, nxt)

    lax.fori_loop(0, NE, backward, -1)

    eidx = eidx_ref[...]
    pos = rank_ref[...]
    for e in range(NE):
        pos = pos + jnp.where(eidx == e, off_s[e], 0)
    pos_ref[...] = pos


def _plan_call(cnt, eidx_t, rank_t):
    smem = pl.BlockSpec(memory_space=pltpu.SMEM)
    vmem = pl.BlockSpec(memory_space=pltpu.VMEM)
    tiles = jax.ShapeDtypeStruct((NT_E,), I32)
    return pl.pallas_call(
        _plan_kernel,
        in_specs=[smem, vmem, vmem],
        out_specs=[vmem, smem, smem, smem, smem, smem],
        out_shape=[jax.ShapeDtypeStruct((TOPK, T_ALL), I32), tiles, tiles, tiles,
                   jax.ShapeDtypeStruct((1,), I32), jax.ShapeDtypeStruct((NE,), I32)],
        name="moe_plan",
    )(cnt, eidx_t, rank_t)


BLK_PER_TILE = TM // BLK


def _final_copies(step, obuf, yp_ref, ys_ref, sem):
    out = []
    slot = step % 2
    for m in range(BLK_PER_TILE):
        g = step * BLK_PER_TILE + m
        b = g // NBLK
        j = g % NBLK
        src = obuf.at[slot, pl.ds(m * BLK, BLK)]
        r0 = pl.multiple_of(jnp.maximum(b * SEQ + (j - 1) * BLK, 0), BLK)
        out.append((jnp.logical_and(g < NB * NBLK, j >= 1),
                    pltpu.make_async_copy(src, yp_ref.at[pl.ds(r0, BLK)], sem.at[slot])))
        out.append((g == NB * NBLK, pltpu.make_async_copy(src, ys_ref, sem.at[slot])))
    return out


HP_RING = 3


def _dispatch_kernel(cnt_ref, off_ref, na_ref, pos_ref, hp_hbm, xs_ref, hbuf, zero_ref, lsem, sem, zsem):
    i = pl.program_id(0)
    tile_rows = TM * ROW_TILE

    def load(step):
        r0 = pl.multiple_of(step * tile_rows, tile_rows)
        return pltpu.make_async_copy(hp_hbm.at[pl.ds(r0, tile_rows)], hbuf.at[step % HP_RING],
                                     lsem.at[step % HP_RING])

    def wait_rows(step):
        for _ in range(TOPK):
            pltpu.make_async_copy(hbuf.at[0], xs_ref.at[pl.ds(0, tile_rows)], sem.at[step % 2]).wait()

    @pl.when(i == 0)
    def _():
        load(0).start()
        if N_ROW_TILES > 1:
            load(1).start()

    load(i).wait()
    hp_ref = hbuf.at[i % HP_RING]

    def issue(t, carry):
        src = hp_ref.at[pl.ds(pl.multiple_of(t * ROW_TILE, ROW_TILE), ROW_TILE)]
        for k in range(TOPK):
            p = pl.multiple_of(pos_ref[0, 0, k * TM + t], ROW_TILE)
            pltpu.make_async_copy(src, xs_ref.at[pl.ds(p, ROW_TILE)], sem.at[i % 2]).start(priority=k % 2)
        return carry

    lax.fori_loop(0, TM, issue, 0)

    @pl.when(i >= 1)
    def _():
        wait_rows(i - 1)

    @pl.when(i + 2 < N_ROW_TILES)
    def _():
        load(i + 2).start()

    @pl.when(i == 0)
    def _():
        zero_ref[...] = jnp.zeros_like(zero_ref)

        def rows_copy(first_row, n_rows):
            r0 = pl.multiple_of(first_row * ROW_TILE, ROW_TILE)
            return pltpu.make_async_copy(zero_ref.at[pl.ds(0, n_rows * ROW_TILE)],
                                         xs_ref.at[pl.ds(r0, n_rows * ROW_TILE)], zsem)

        def row_copy(r):
            return rows_copy(r, 1)

        def group_copy(q):
            return rows_copy(q * 8, 8)

        def tile_copy(t):
            return rows_copy(t * TM_E, TM_E)

        def run(copy, lo, hi):
            def start(x, carry):
                copy(x).start()
                return carry

            def wait(x, carry):
                copy(x).wait()
                return carry

            lax.fori_loop(lo, hi, start, 0)
            lax.fori_loop(lo, hi, wait, 0)

        def per_expert(e, carry):
            lo = off_ref[e] + cnt_ref[e]
            hi = off_ref[e] + _tiles_of(cnt_ref[e]) * TM_E
            mid = jnp.minimum(((lo + 7) // 8) * 8, hi)
            run(row_copy, lo, mid)
            run(group_copy, mid // 8, hi // 8)
            return carry

        lax.fori_loop(0, NE, per_expert, 0)
        run(tile_copy, na_ref[0], NT_E)

    @pl.when(i == N_ROW_TILES - 1)
    def _():
        wait_rows(i)


def _dispatch_call(cnt, off, nact, pos_tiles, hp):
    return pl.pallas_call(
        _dispatch_kernel,
        grid_spec=pltpu.PrefetchScalarGridSpec(
            num_scalar_prefetch=3,
            grid=(N_ROW_TILES,),
            in_specs=[
                pl.BlockSpec((1, 1, TM * TOPK), lambda i, c, o, n: (i, 0, 0), memory_space=pltpu.SMEM),
                pl.BlockSpec(memory_space=pl.ANY),
            ],
            out_specs=pl.BlockSpec(memory_space=pl.ANY),
            scratch_shapes=[
                pltpu.VMEM((HP_RING, TM * ROW_TILE, LANES), U32),
                pltpu.VMEM((TM_E * ROW_TILE, LANES), U32),
                pltpu.SemaphoreType.DMA((HP_RING,)),
                pltpu.SemaphoreType.DMA((2,)),
                pltpu.SemaphoreType.DMA(()),
            ],
        ),
        out_shape=jax.ShapeDtypeStruct((R_ROWS * ROW_TILE, LANES), U32),
        compiler_params=_cparams(("arbitrary",)),
        name="moe_dispatch",
    )(cnt, off, nact, pos_tiles, hp)


def _experts_kernel(layer, te_ref, gi_ref, ne_ref, na_ref, xs_ref, wg_hbm, wu_hbm, wd_hbm, ys_ref,
                    wgf, wuf, wdf, wg_s, wu_s, wd_s, wsem):
    i = pl.program_id(0)
    na = na_ref[0]

    def weight_copies(e, slot):
        return (pltpu.make_async_copy(wg_hbm.at[layer, e], wgf.at[slot], wsem.at[slot, 0]),
                pltpu.make_async_copy(wu_hbm.at[layer, e], wuf.at[slot], wsem.at[slot, 1]),
                pltpu.make_async_copy(wd_hbm.at[layer, e], wdf.at[slot], wsem.at[slot, 2]))

    @pl.when(i == 0)
    def _():
        for cp in weight_copies(te_ref[0], 0):
            cp.start(priority=1)

    ic = jnp.minimum(i, na - 1)
    first = jnp.logical_or(i == 0, te_ref[ic] != te_ref[jnp.maximum(ic - 1, 0)])

    @pl.when(jnp.logical_and(i < na, first))
    def _():
        slot = gi_ref[ic] % 2
        nxt = ne_ref[ic]

        @pl.when(nxt >= 0)
        def _():
            for cp in weight_copies(nxt, 1 - slot):
                cp.start(priority=1)

        for cp in weight_copies(te_ref[ic], slot):
            cp.wait()
        wg_s[...] = wgf[slot].astype(BF16)
        wu_s[...] = wuf[slot].astype(BF16)
        wd_s[...] = wdf[slot].astype(BF16)

    @pl.when(i < na)
    def _():
        lo, hi = _unpack_pair(_load_tiles_as_rows(xs_ref, TM_E))
        lo = lo.astype(BF16)
        hi = hi.astype(BF16)
        hg = (jnp.dot(lo, wg_s[0:HALF, :], preferred_element_type=F32)
              + jnp.dot(hi, wg_s[HALF:D, :], preferred_element_type=F32))
        hu = (jnp.dot(lo, wu_s[0:HALF, :], preferred_element_type=F32)
              + jnp.dot(hi, wu_s[HALF:D, :], preferred_element_type=F32))
        act = (hg * _sigmoid(hg) * hu).astype(BF16)
        y = jnp.dot(act, wd_s[...], preferred_element_type=F32)
        _store_rows_as_tiles(ys_ref, _pack_pair(y[:, :HALF], y[:, HALF:]))

    @pl.when(i >= na)
    def _():
        ys_ref[...] = jnp.zeros_like(ys_ref)


def _experts_call(layer, te, gi, ne, nact, xs, w_gate, w_up, w_down):
    anyspec = pl.BlockSpec(memory_space=pl.ANY)
    return pl.pallas_call(
        functools.partial(_experts_kernel, layer),
        grid_spec=pltpu.PrefetchScalarGridSpec(
            num_scalar_prefetch=4,
            grid=(NT_E,),
            in_specs=[
                pl.BlockSpec((TM_E * ROW_TILE, LANES),
                             lambda i, te_r, gi_r, ne_r, na_r: (jnp.minimum(i, na_r[0] - 1), 0)),
                anyspec, anyspec, anyspec,
            ],
            out_specs=pl.BlockSpec((TM_E * ROW_TILE, LANES), lambda i, te_r, gi_r, ne_r, na_r: (i, 0)),
            scratch_shapes=[
                pltpu.VMEM((2, D, DE), F32),
                pltpu.VMEM((2, D, DE), F32),
                pltpu.VMEM((2, DE, D), F32),
                pltpu.VMEM((D, DE), BF16),
                pltpu.VMEM((D, DE), BF16),
                pltpu.VMEM((DE, D), BF16),
                pltpu.SemaphoreType.DMA((2, 3)),
            ],
        ),
        out_shape=jax.ShapeDtypeStruct((R_ROWS * ROW_TILE, LANES), U32),
        compiler_params=_cparams(("arbitrary",)),
        name="moe_experts",
    )(te, gi, ne, nact, xs, w_gate, w_up, w_down)


def _gather_kernel(final, pos_ref, posn_ref, w_ref, h_ref, ys_ref, wsg_ref, wsu_ref, wsd_ref,
                   g_ref, b_ref, *rest):
    i = pl.program_id(0)
    if final:
        yp_ref, ys_out_ref, gbuf, gsem, obuf, osem = rest

        def wait_step(step):
            for cond, cp in _final_copies(step, obuf, yp_ref, ys_out_ref, osem):
                @pl.when(cond)
                def _():
                    cp.wait()

        @pl.when(i >= 2)
        def _():
            wait_step(i - 2)
    else:
        out_ref, gbuf, gsem = rest

    def issue(idx_ref, slot):
        def body(t, carry):
            r0 = pl.multiple_of(t * ROW_TILE, ROW_TILE)
            for k in range(TOPK):
                p = pl.multiple_of(idx_ref[0, 0, k * TM + t], ROW_TILE)
                pltpu.make_async_copy(ys_ref.at[pl.ds(p, ROW_TILE)], gbuf.at[slot, k, pl.ds(r0, ROW_TILE)],
                                      gsem.at[slot]).start(priority=k % 2)
            return carry

        lax.fori_loop(0, TM, body, 0)

    @pl.when(i == 0)
    def _():
        issue(pos_ref, 0)

    @pl.when(i + 1 < N_ROW_TILES)
    def _():
        issue(posn_ref, (i + 1) % 2)

    h = h_ref[...]
    hb = h.astype(BF16)
    sg = jnp.dot(hb, wsg_ref[...], preferred_element_type=F32)
    su = jnp.dot(hb, wsu_ref[...], preferred_element_type=F32)
    act = (sg * _sigmoid(sg) * su).astype(BF16)
    y = ALPHA * h + jnp.dot(act, wsd_ref[...], preferred_element_type=F32)
    ylo = y[:, :HALF]
    yhi = y[:, HALF:]

    slot = i % 2
    for k in range(TOPK):
        pltpu.make_async_copy(ys_ref.at[pl.ds(0, TM * ROW_TILE)], gbuf.at[slot, k], gsem.at[slot]).wait()
    w = w_ref[...]
    for k in range(TOPK):
        lo, hi = _unpack_pair(_load_tiles_as_rows(gbuf.at[slot, k], TM))
        wk = w[:, k:k + 1]
        ylo = ylo + wk * lo
        yhi = yhi + wk * hi
    out = _layer_norm(jnp.concatenate([ylo, yhi], axis=1), g_ref[...], b_ref[...])
    if not final:
        out_ref[...] = out
        return
    obuf[i % 2] = out
    for cond, cp in _final_copies(i, obuf, yp_ref, ys_out_ref, osem):
        @pl.when(cond)
        def _():
            cp.start()

    @pl.when(i == N_ROW_TILES - 1)
    def _():
        wait_step(i - 1)
        wait_step(i)


def _gather_call(final, pos_tiles, wsel, h, ys, wsg, wsu, wsd, g, b):
    vec = pl.BlockSpec((1, D), lambda i: (0, 0))
    once = dict(pipeline_mode=pl.Buffered(1))
    anyspec = pl.BlockSpec(memory_space=pl.ANY)
    scratch = [pltpu.VMEM((2, TOPK, TM * ROW_TILE, LANES), U32), pltpu.SemaphoreType.DMA((2,))]
    if final:
        out_specs = [anyspec, anyspec]
        out_shape = [jax.ShapeDtypeStruct((NB * SEQ, D), F32), jax.ShapeDtypeStruct((DEC, D), F32)]
        scratch += [pltpu.VMEM((2, TM, D), F32), pltpu.SemaphoreType.DMA((2,))]
    else:
        out_specs = pl.BlockSpec((TM, D), lambda i: (i, 0))
        out_shape = jax.ShapeDtypeStruct((T_ALL, D), F32)
    smem_tile = lambda f: pl.BlockSpec((1, 1, TM * TOPK), f, memory_space=pltpu.SMEM)
    return pl.pallas_call(
        functools.partial(_gather_kernel, final),
        grid=(N_ROW_TILES,),
        in_specs=[
            smem_tile(lambda i: (i, 0, 0)),
            smem_tile(lambda i: (jnp.minimum(i + 1, N_ROW_TILES - 1), 0, 0)),
            pl.BlockSpec((TM, TOPK), lambda i: (i, 0)),
            pl.BlockSpec((TM, D), lambda i: (i, 0)),
            anyspec,
            pl.BlockSpec((D, DE), lambda i: (0, 0), **once),
            pl.BlockSpec((D, DE), lambda i: (0, 0), **once),
            pl.BlockSpec((DE, D), lambda i: (0, 0), **once),
            vec,
            vec,
        ],
        out_specs=out_specs,
        out_shape=out_shape,
        scratch_shapes=scratch,
        compiler_params=_cparams(("arbitrary",)),
        name="moe_combine_final" if final else "moe_combine",
    )(pos_tiles, pos_tiles, wsel, h, ys, wsg, wsu, wsd, g, b)


def _moe_layer(layer, final, h, hp, w_router, router_bias, w_exp_gate, w_exp_up, w_exp_down,
               w_sh_gate, w_sh_up, w_sh_down, ln_g, ln_b):
    eidx_t, wsel, rank_t, counts = _router_call(h, w_router[layer], router_bias[layer])
    cnt = counts[:, 0].astype(I32)
    pos_t, te, gi, ne, nact, off = _plan_call(cnt, eidx_t, rank_t)
    pos_tiles = (pos_t * ROW_TILE).reshape(TOPK, N_ROW_TILES, TM).transpose(1, 0, 2).reshape(
        N_ROW_TILES, 1, TOPK * TM)
    xs = _dispatch_call(cnt, off, nact, pos_tiles, hp)
    ys = _experts_call(layer, te, gi, ne, nact, xs, w_exp_gate, w_exp_up, w_exp_down)
    return _gather_call(
        final, pos_tiles, wsel, h, ys,
        w_sh_gate[layer].astype(BF16), w_sh_up[layer].astype(BF16), w_sh_down[layer].astype(BF16),
        ln_g[layer][None, :], ln_b[layer][None, :])


def _ssm_prep_kernel(lr_ref, li_ref, ldt_ref, br_ref, bi_ref, abr_ref, abi_ref, bbr_ref, bbi_ref):
    lr = lr_ref[...]
    li = li_ref[...]
    dt = jnp.exp(ldt_ref[...])
    mag = jnp.exp(lr * dt)
    ab_re = mag * jnp.cos(li * dt)
    ab_im = mag * jnp.sin(li * dt)
    den = lr * lr + li * li
    nr = ab_re - 1.0
    ni = ab_im
    cr = (nr * lr + ni * li) / den
    ci = (ni * lr - nr * li) / den
    br = br_ref[...]
    bi = bi_ref[...]
    abr_ref[...] = ab_re
    abi_ref[...] = ab_im
    bbr_ref[...] = cr * br - ci * bi
    bbi_ref[...] = cr * bi + ci * br


def _ssm_prep_call(lam_re, lam_im, log_dt, b_re, b_im):
    wide = (NG, GC * NS)
    lr = jnp.tile(lam_re, (1, GC))
    li = jnp.tile(lam_im, (1, GC))
    ldt = jnp.broadcast_to(log_dt[:, None], wide)
    br = jnp.transpose(b_re, (0, 2, 1)).reshape(wide)
    bi = jnp.transpose(b_im, (0, 2, 1)).reshape(wide)
    sds = jax.ShapeDtypeStruct(wide, F32)
    return pl.pallas_call(
        _ssm_prep_kernel, out_shape=[sds, sds, sds, sds], name="ssm_prep",
        compiler_params=pltpu.CompilerParams(vmem_limit_bytes=VMEM_LIMIT),
    )(lr, li, ldt, br, bi)


def _cmul(ar, ai, xr, xi):
    return ar * xr - ai * xi, ar * xi + ai * xr


NSEG = 8
SEGL = LP // NSEG
STEP_UNROLL = 4
MOVE_UNROLL = 8
assert LP == NSEG * SEGL and SEGL % STEP_UNROLL == 0 and SEGL % MOVE_UNROLL == 0 and PAD_FRONT < SEGL
assert STEP_UNROLL % 2 == 0


def _ssm_p_kernel(u_ref, wb_ref, wc_ref, a_ref, d_ref, z_ref, st_ref, u_scr, s_scr, sb_scr, y_scr):
    def interleave(it, carry):
        for q in range(MOVE_UNROLL):
            t = it * MOVE_UNROLL + q
            u_scr[pl.ds(pl.multiple_of(t * NSEG, NSEG), NSEG), :] = u_ref[pl.ds(t, NSEG, stride=SEGL), :]
        return carry

    lax.fori_loop(0, SEGL // MOVE_UNROLL, interleave, 0)
    row = lax.broadcasted_iota(I32, (LP, LANES), 0)
    is_pad = jnp.logical_and(row % NSEG == 0, row // NSEG < PAD_FRONT)
    u = jnp.where(is_pad, 0.0, u_scr[...])
    s_scr[...] = jnp.dot(u.astype(BF16), wb_ref[0], preferred_element_type=F32)

    ar = a_ref[0, 0:1, :]
    ai = a_ref[0, 1:2, :]

    def group(t):
        r0 = pl.multiple_of(t * NSEG, NSEG)
        return pl.ds(r0, NSEG)

    def local_scan(it, carry):
        sr, si = carry
        for q in range(STEP_UNROLL):
            g = group(it * STEP_UNROLL + q)
            tr, ti = _cmul(ar, ai, sr, si)
            sr = tr + s_scr[g, 0:CH_STATE]
            si = ti + s_scr[g, CH_STATE:]
            s_scr[g, 0:CH_STATE] = sr
            s_scr[g, CH_STATE:] = si
        return sr, si

    zero8 = jnp.zeros((NSEG, CH_STATE), F32)
    er, ei = lax.fori_loop(0, SEGL // STEP_UNROLL, local_scan, (zero8, zero8))

    pr, pi = ar, ai
    acc = None
    bits = SEGL
    while bits:
        if bits & 1:
            acc = (pr, pi) if acc is None else _cmul(pr, pi, *acc)
        bits >>= 1
        if bits:
            pr, pi = _cmul(pr, pi, pr, pi)
    alr, ali = acc
    cr = jnp.zeros((1, CH_STATE), F32)
    ci = jnp.zeros((1, CH_STATE), F32)
    ins_r, ins_i = [], []
    for j in range(NSEG):
        ins_r.append(cr)
        ins_i.append(ci)
        tr, ti = _cmul(alr, ali, cr, ci)
        cr = tr + er[j:j + 1, :]
        ci = ti + ei[j:j + 1, :]
    st_ref[0, 0, :, 0:CH_STATE] = cr
    st_ref[0, 0, :, CH_STATE:] = ci

    def fixup(it, carry):
        dr, di = carry
        for q in range(0, STEP_UNROLL, 2):
            t0 = it * STEP_UNROLL + q
            rows_r, rows_i = [], []
            for t in (t0, t0 + 1):
                g = group(t)
                dr, di = _cmul(ar, ai, dr, di)
                rows_r.append(s_scr[g, 0:CH_STATE] + dr)
                rows_i.append(s_scr[g, CH_STATE:] + di)
            pair = pl.ds(pl.multiple_of(t0 * NSEG, 2 * NSEG), 2 * NSEG)
            sb_scr[pair, 0:CH_STATE] = jnp.concatenate(rows_r, axis=0).astype(BF16)
            sb_scr[pair, CH_STATE:] = jnp.concatenate(rows_i, axis=0).astype(BF16)
        return dr, di

    lax.fori_loop(0, SEGL // STEP_UNROLL, fixup,
                  (jnp.concatenate(ins_r, axis=0), jnp.concatenate(ins_i, axis=0)))

    y_scr[...] = jnp.dot(sb_scr[...], wc_ref[0], preferred_element_type=F32) + d_ref[0] * u

    def deinterleave(it, carry):
        for q in range(MOVE_UNROLL):
            t = it * MOVE_UNROLL + q
            u_scr[pl.ds(t, NSEG, stride=SEGL), :] = y_scr[pl.ds(pl.multiple_of(t * NSEG, NSEG), NSEG), :]
        return carry

    lax.fori_loop(0, SEGL // MOVE_UNROLL, deinterleave, 0)
    z_ref[...] = jax.nn.gelu(u_scr[...]).astype(BF16)


def _ssm_p_call(h, wb_bf, wc_bf, a_tab, d_tab):
    return pl.pallas_call(
        _ssm_p_kernel,
        grid=(NB, NCHUNK),
        in_specs=[
            pl.BlockSpec((LP, LANES), lambda b, k: (b, k)),
            pl.BlockSpec((1, LANES, 2 * CH_STATE), lambda b, k: (k, 0, 0)),
            pl.BlockSpec((1, 2 * CH_STATE, LANES), lambda b, k: (k, 0, 0)),
            pl.BlockSpec((1, 2, CH_STATE), lambda b, k: (k, 0, 0)),
            pl.BlockSpec((1, 1, LANES), lambda b, k: (k, 0, 0)),
        ],
        out_specs=[
            pl.BlockSpec((LP, LANES), lambda b, k: (b, k)),
            pl.BlockSpec((1, 1, 1, 2 * CH_STATE), lambda b, k: (b, k, 0, 0)),
        ],
        out_shape=[
            jax.ShapeDtypeStruct((T_PROMPT, D), BF16),
            jax.ShapeDtypeStruct((NB, NCHUNK, 1, 2 * CH_STATE), F32),
        ],
        scratch_shapes=[pltpu.VMEM((LP, LANES), F32), pltpu.VMEM((LP, 2 * CH_STATE), F32),
                        pltpu.VMEM((LP, 2 * CH_STATE), BF16), pltpu.VMEM((LP, LANES), F32)],
        compiler_params=_cparams(("parallel", "parallel")),
        name="ssm_prompt",
    )(h, wb_bf, wc_bf, a_tab, d_tab)


def _ssm_s_kernel(u_ref, sr_ref, si_ref, wb_ref, wc_ref, a_ref, d_ref, z_ref, nr_ref, ni_ref):
    u = u_ref[...]
    bu = jnp.dot(u, wb_ref[0], preferred_element_type=F32, precision=lax.Precision.HIGHEST)
    ar = a_ref[0, 0:1, :]
    ai = a_ref[0, 1:2, :]
    tr, ti = _cmul(ar, ai, sr_ref[...], si_ref[...])
    nr = tr + bu[:, 0:CH_STATE]
    ni = ti + bu[:, CH_STATE:]
    nr_ref[...] = nr
    ni_ref[...] = ni
    s = jnp.concatenate([nr, ni], axis=1).astype(BF16)
    y = jnp.dot(s, wc_ref[0], preferred_element_type=F32) + d_ref[0] * u
    z_ref[...] = jax.nn.gelu(y).astype(BF16)


def _ssm_s_call(h, s0r, s0i, wb_f32, wc_bf, a_tab, d_tab):
    st = pl.BlockSpec((DEC, CH_STATE), lambda k: (0, k))
    return pl.pallas_call(
        _ssm_s_kernel,
        grid=(NCHUNK,),
        in_specs=[
            pl.BlockSpec((DEC, LANES), lambda k: (T_PROMPT // DEC, k)),
            st,
            st,
            pl.BlockSpec((1, LANES, 2 * CH_STATE), lambda k: (k, 0, 0)),
            pl.BlockSpec((1, 2 * CH_STATE, LANES), lambda k: (k, 0, 0)),
            pl.BlockSpec((1, 2, CH_STATE), lambda k: (k, 0, 0)),
            pl.BlockSpec((1, 1, LANES), lambda k: (k, 0, 0)),
        ],
        out_specs=[pl.BlockSpec((DEC, LANES), lambda k: (0, k)), st, st],
        out_shape=[
            jax.ShapeDtypeStruct((DEC, D), BF16),
            jax.ShapeDtypeStruct((DEC, NG * NS), F32),
            jax.ShapeDtypeStruct((DEC, NG * NS), F32),
        ],
        compiler_params=_cparams(("parallel",)),
        name="ssm_sample",
    )(h, s0r, s0i, wb_f32, wc_bf, a_tab, d_tab)


def _glu_ln_kernel(z_ref, zt_ref, w_ref, bg_ref, h_ref, g_ref, b_ref, out_ref, hp_ref):
    z = _prompt_or_tail(z_ref, zt_ref)
    acc = jnp.dot(z, w_ref[...], preferred_element_type=F32) + bg_ref[...]
    m = acc[:, :D] * _sigmoid(acc[:, D:])
    out = _layer_norm(ALPHA * h_ref[...] + m, g_ref[...], b_ref[...])
    out_ref[...] = out
    _store_rows_as_tiles(hp_ref, _pack_pair(out[:, :HALF], out[:, HALF:]))


def _glu_ln_call(z_p, z_tail, w_bf, bg, h, g, b):
    vec = pl.BlockSpec((1, D), lambda i: (0, 0))
    return pl.pallas_call(
        _glu_ln_kernel,
        grid=(N_ROW_TILES,),
        in_specs=[
            pl.BlockSpec((TM, D), _PROMPT_TILE),
            pl.BlockSpec((TM, D), lambda i: (0, 0)),
            pl.BlockSpec((D, 2 * D), lambda i: (0, 0), pipeline_mode=pl.Buffered(1)),
            pl.BlockSpec((1, 2 * D), lambda i: (0, 0)),
            pl.BlockSpec((TM, D), lambda i: (i, 0)),
            vec,
            vec,
        ],
        out_specs=[pl.BlockSpec((TM, D), lambda i: (i, 0)),
                   pl.BlockSpec((TM * ROW_TILE, LANES), lambda i: (i, 0))],
        out_shape=[jax.ShapeDtypeStruct((T_ALL, D), F32),
                   jax.ShapeDtypeStruct((T_ALL * ROW_TILE, LANES), U32)],
        compiler_params=_cparams(("parallel",)),
        name="glu_ln",
    )(z_p, z_tail, w_bf, bg, h, g, b)


GROUPS_PER_CHUNK = LANES // GC


def _block_diag_in(t):
    t3 = t.reshape(NCHUNK, LANES, NS)
    same = (np.arange(LANES)[:, None] // GC) == (np.arange(CH_STATE)[None, :] // NS)
    return jnp.tile(t3, (1, 1, GROUPS_PER_CHUNK)) * jnp.asarray(same, t.dtype)


def _block_diag_out(t):
    t3 = jnp.swapaxes(t, 1, 2).reshape(NCHUNK, CH_STATE, GC)
    same = (np.arange(CH_STATE)[:, None] // NS) == (np.arange(LANES)[None, :] // GC)
    return jnp.tile(t3, (1, 1, GROUPS_PER_CHUNK)) * jnp.asarray(same, t.dtype)


def kernel(x_prompt, x_sample, cache_k, cache_v, state_ssm_re, state_ssm_im, meta_tokens, w_qkv, b_qkv, attn_sinks, w_o, b_o, ssm_lam_re, ssm_lam_im, ssm_log_dt, ssm_b_re, ssm_b_im, ssm_c_re, ssm_c_im, ssm_d, w_glu, b_glu, ln_mix_g, ln_mix_b, w_router, router_bias, w_exp_gate, w_exp_up, w_exp_down, w_sh_gate, w_sh_up, w_sh_down, ln_ffn_g, ln_ffn_b):
    moe_w = (w_router, router_bias, w_exp_gate, w_exp_up, w_exp_down, w_sh_gate, w_sh_up, w_sh_down,
             ln_ffn_g, ln_ffn_b)

    front = jnp.concatenate([jnp.zeros((PAD_FRONT, D), F32), meta_tokens], axis=0)
    pieces = []
    for b in range(NB):
        pieces += [front, x_prompt[b]]
    h = jnp.concatenate(pieces + [x_sample.reshape(DEC, D)], axis=0)

    rc, rs1, rs2 = _rope_tables()
    q, k, v = _qkv_call(h, w_qkv[0].astype(BF16), b_qkv[0][None, :], rc, rs1, rs2)
    o_p = _attn_p_call(attn_sinks[0], q, k, v)
    q3 = q[T_PROMPT:].astype(F32).reshape(DEC, NH, HD)
    o_s, ck_new, cv_new = _attn_s_call(
        attn_sinks[0][:, None], q3, k, v,
        cache_k[0].reshape(DEC, BLK, NKV * HD), cache_v[0].reshape(DEC, BLK, NKV * HD))
    o_tail = _tail_tile(o_p, o_s.reshape(DEC, D).astype(BF16))
    h, hp = _oproj_ln_call(o_p, o_tail, w_o[0].astype(BF16), b_o[0][None, :], h,
                           ln_mix_g[0][None, :], ln_mix_b[0][None, :])
    h = _moe_layer(0, False, h, hp, *moe_w)

    def last_window(t):
        rows = [t[(b + 1) * LP - BLK:(b + 1) * LP] for b in range(NB)]
        return jnp.stack(rows).reshape(NB, BLK, NKV, HD)

    kp = last_window(k)
    vp = last_window(v)

    ab_re, ab_im, bb_re, bb_im = _ssm_prep_call(
        ssm_lam_re[0], ssm_lam_im[0], ssm_log_dt[0], ssm_b_re[0], ssm_b_im[0])
    wb = jnp.concatenate([_block_diag_in(bb_re), _block_diag_in(bb_im)], axis=2)
    wc = jnp.concatenate([_block_diag_out(ssm_c_re[0]), -_block_diag_out(ssm_c_im[0])], axis=1)
    wc_bf = wc.astype(BF16)
    a_tab = jnp.stack([ab_re[:, :NS].reshape(NCHUNK, CH_STATE),
                       ab_im[:, :NS].reshape(NCHUNK, CH_STATE)], axis=1)
    d_tab = ssm_d[0].reshape(NCHUNK, 1, LANES)
    z_p, st_p = _ssm_p_call(h, wb.astype(BF16), wc_bf, a_tab, d_tab)
    z_s, sr_new, si_new = _ssm_s_call(
        h, state_ssm_re[0].reshape(DEC, NG * NS), state_ssm_im[0].reshape(DEC, NG * NS),
        wb, wc_bf, a_tab, d_tab)
    h, hp = _glu_ln_call(z_p, _tail_tile(z_p, z_s), w_glu[0].astype(BF16), b_glu[0][None, :], h,
                         ln_mix_g[1][None, :], ln_mix_b[1][None, :])
    y_prompt, y_sample = _moe_layer(1, True, h, hp, *moe_w)
    y_prompt = y_prompt.reshape(NB, SEQ, D)
    y_sample = y_sample.reshape(DEC, 1, D)
    st_p = st_p.reshape(NB, NCHUNK, 2, 8, NS)
    rp = st_p[:, :, 0].reshape(NB, NG, NS)
    ip = st_p[:, :, 1].reshape(NB, NG, NS)
    return (y_prompt, y_sample,
            kp[None], vp[None],
            ck_new.reshape(1, DEC, BLK, NKV, HD), cv_new.reshape(1, DEC, BLK, NKV, HD),
            rp[None], ip[None],
            sr_new.reshape(1, DEC, NG, NS), si_new.reshape(1, DEC, NG, NS))
```
